```python
import math
import jax, jax.numpy as jnp
from jax import lax
import numpy as np

D_MODEL = 1024
BATCH = 16
SEQ = 4096
DEPTH = 4

CHUNK = 64
Q_BLOCK = 128
MEM_LEN = 256
BRANCH_W = D_MODEL // 2
HEAD_DIM = 64
D_RNN = BRANCH_W
RNN_BLOCKS = 8
RNN_BLOCK_W = D_RNN // RNN_BLOCKS
CONV_W = 4
RG_C = 8.0
DSA_HEADS = BRANCH_W // HEAD_DIM
DSA_LATENT = 128
IDX_HEADS = 8
IDX_DIM = 64
TOPK_MAX = 256
FOX_HEADS = BRANCH_W // HEAD_DIM
MEM_HEADS = 4
MEM_HEAD_DIM = 128
NUM_BUCKETS = 32
MAX_DISTANCE = 128
D_FF = 2816
N_BRANCH = 3
EPS = 1e-6
NEG = -1e30
IN_SPLITS = (D_RNN, D_RNN,
             DSA_HEADS * HEAD_DIM, DSA_LATENT, IDX_HEADS * IDX_DIM, IDX_DIM, IDX_HEADS,
             FOX_HEADS * HEAD_DIM, FOX_HEADS * HEAD_DIM, FOX_HEADS * HEAD_DIM, FOX_HEADS,
             D_MODEL, D_MODEL, D_MODEL)
D_IN = sum(IN_SPLITS)

kernel_name = 'hybrid_chunk_causal_rglru_dsa_fox_block'


def _split_points():
    return [int(v) for v in np.cumsum(IN_SPLITS)[:-1]]


def rms_norm(x, g):
    x32 = x.astype(jnp.float32)
    y = x32 * lax.rsqrt(jnp.mean(x32 * x32, axis=-1, keepdims=True) + EPS)
    return (y * g.astype(jnp.float32)).astype(x.dtype)


def swiglu_ffn(h, w_gu, w_down):
    g, u = jnp.split(h @ w_gu, 2, axis=-1)
    return (jax.nn.silu(g) * u) @ w_down


def t5_bucket(rel):
    nb = NUM_BUCKETS // 2
    max_exact = nb // 2
    base = jnp.where(rel > 0, nb, 0)
    n = jnp.abs(rel)
    n_f = jnp.maximum(n, 1).astype(jnp.float32)
    large = max_exact + (jnp.log(n_f / max_exact) / math.log(MAX_DISTANCE / max_exact)
                         * (nb - max_exact)).astype(jnp.int32)
    large = jnp.minimum(large, nb - 1)
    return base + jnp.where(n < max_exact, n, large)


def _to_blocks(a, n_blk):
    return jnp.moveaxis(a.reshape(a.shape[0], n_blk, Q_BLOCK, *a.shape[2:]), 1, 0)


def rglru_branch(x_in, gate_in, conv_w, conv_b, wa, ba, wx, bx, lam):
    b_, s_, _ = x_in.shape
    xc = lax.conv_general_dilated(x_in, conv_w[:, None, :].astype(x_in.dtype), window_strides=(1,),
                                  padding=[(CONV_W - 1, 0)], dimension_numbers=('NWC', 'WIO', 'NWC'),
                                  feature_group_count=D_RNN) + conv_b
    xb = xc.reshape(b_, s_, RNN_BLOCKS, RNN_BLOCK_W)
    r = jax.nn.sigmoid(jnp.einsum('bsnc,ncd->bsnd', xb, wa).reshape(b_, s_, D_RNN) + ba)
    i = jax.nn.sigmoid(jnp.einsum('bsnc,ncd->bsnd', xb, wx).reshape(b_, s_, D_RNN) + bx)
    log_a = -RG_C * r.astype(jnp.float32) * jax.nn.softplus(-lam.astype(jnp.float32))
    a = jnp.exp(log_a)
    u = jnp.sqrt(-jnp.expm1(2.0 * log_a)) * (i * xc).astype(jnp.float32)

    def combine(lhs, rhs):
        return (lhs[0] * rhs[0], rhs[0] * lhs[1] + rhs[1])

    _, h = lax.associative_scan(combine, (a, u), axis=1)
    return h.astype(x_in.dtype) * jax.nn.gelu(gate_in)


def dsa_branch(q, c_kv, q_idx, k_idx, w_idx, kv_norm, w_uk, w_uv, t5_table):
    b_, s_ = q.shape[:2]
    k_sel = min(TOPK_MAX, s_ // 4)
    n_blk = s_ // Q_BLOCK
    c = rms_norm(c_kv, kv_norm)
    q_lat = jnp.einsum('bshd,hdc->bshc', q, w_uk)
    w_idx = w_idx * IDX_HEADS ** -0.5
    key_chunk = jnp.arange(s_) // CHUNK

    def one_block(args):
        blk, ql, qi, wi = args
        t_pos = blk * Q_BLOCK + jnp.arange(Q_BLOCK)
        t_chunk = t_pos // CHUNK
        dots = jnp.einsum('bthd,bsd->btsh', qi, k_idx).astype(jnp.float32) * IDX_DIM ** -0.5
        score = jnp.einsum('btsh,bth->bts', jax.nn.relu(dots), wi.astype(jnp.float32))
        admissible = key_chunk[None, :] <= t_chunk[:, None]
        score = jnp.where(admissible[None], score, NEG)
        _, idx = lax.top_k(score, k_sel)
        valid = (idx // CHUNK) <= t_chunk[None, :, None]
        c_sel = jax.vmap(lambda cb, ib: cb[ib])(c, idx)
        logits = jnp.einsum('bthc,btkc->bhtk', ql, c_sel).astype(jnp.float32) * HEAD_DIM ** -0.5
        bias = t5_table[t5_bucket(idx - t_pos[None, :, None])]
        logits = logits + jnp.moveaxis(bias, -1, 1).astype(jnp.float32)
        logits = jnp.where(valid[:, None], logits, NEG)
        p = jax.nn.softmax(logits, axis=-1).astype(c.dtype)
        return jnp.einsum('bhtk,btkc->bthc', p, c_sel)

    o_lat = lax.map(one_block, (jnp.arange(n_blk), _to_blocks(q_lat, n_blk),
                                _to_blocks(q_idx, n_blk), _to_blocks(w_idx, n_blk)))
    o_lat = jnp.moveaxis(o_lat, 0, 1).reshape(b_, s_, DSA_HEADS, DSA_LATENT)
    o = jnp.einsum('bshc,hcd->bshd', o_lat, w_uv)
    return o.reshape(b_, s_, DSA_HEADS * HEAD_DIM)


def fox_branch(q, k, v, f_logit, f_bias):
    b_, s_ = q.shape[:2]
    n_blk = s_ // Q_BLOCK
    log_f = jax.nn.log_sigmoid(f_logit.astype(jnp.float32) + f_bias.astype(jnp.float32))
    cum = jnp.cumsum(log_f, axis=1)
    cum_k = jnp.moveaxis(cum, 2, 1)
    key_pos = jnp.arange(s_)

    def one_block(args):
        blk, qb, fb = args
        t_pos = blk * Q_BLOCK + jnp.arange(Q_BLOCK)
        logits = jnp.einsum('bthd,bshd->bhts', qb, k).astype(jnp.float32) * HEAD_DIM ** -0.5
        logits = logits + jnp.moveaxis(fb, 2, 1)[..., None] - cum_k[:, :, None, :]
        causal = key_pos[None, :] <= t_pos[:, None]
        logits = jnp.where(causal[None, None], logits, NEG)
        p = jax.nn.softmax(logits, axis=-1).astype(v.dtype)
        return jnp.einsum('bhts,bshd->bthd', p, v)

    o = lax.map(one_block, (jnp.arange(n_blk), _to_blocks(q, n_blk), _to_blocks(cum, n_blk)))
    return jnp.moveaxis(o, 0, 1).reshape(b_, s_, FOX_HEADS * HEAD_DIM)


def memory_cross_attention(h, mem_h, w_mq, w_mkv, w_mo):
    b_, s_ = h.shape[:2]
    q = (h @ w_mq).reshape(b_, s_, MEM_HEADS, MEM_HEAD_DIM)
    k, v = jnp.split(mem_h @ w_mkv, 2, axis=-1)
    k = k.reshape(b_, -1, MEM_HEADS, MEM_HEAD_DIM)
    v = v.reshape(b_, -1, MEM_HEADS, MEM_HEAD_DIM)
    logits = jnp.einsum('bthd,bmhd->bhtm', q, k).astype(jnp.float32) * MEM_HEAD_DIM ** -0.5
    p = jax.nn.softmax(logits, axis=-1).astype(v.dtype)
    o = jnp.einsum('bhtm,bmhd->bthd', p, v).reshape(b_, s_, MEM_HEADS * MEM_HEAD_DIM)
    return o @ w_mo


def setup_inputs(seed: int = 0) -> dict:
    key = jax.random.key(seed)
    keys = jax.random.split(key, 40)
    counter = [0]

    def nxt():
        k = keys[counter[0]]
        counter[0] += 1
        return k

    def nrm(shape, scale):
        return scale * jax.random.normal(nxt(), shape, jnp.float32)

    def gain(shape):
        return 1.0 + 0.05 * jax.random.normal(nxt(), shape, jnp.float32)

    L, D = DEPTH, D_MODEL
    x = nrm((BATCH, SEQ, D), 1.0)
    mem = nrm((BATCH, MEM_LEN, D), 1.0)
    ffn1_norm = gain((L, D))
    ffn1_w_gu = nrm((L, D, 2 * D_FF), D ** -0.5)
    ffn1_w_down = nrm((L, D_FF, D), D_FF ** -0.5)
    mix_norm = gain((L, D))
    w_in = nrm((L, D, D_IN), D ** -0.5)
    conv_w = nrm((L, CONV_W, D_RNN), CONV_W ** -0.5)
    conv_b = nrm((L, D_RNN), 0.02)
    rg_wa = nrm((L, RNN_BLOCKS, RNN_BLOCK_W, RNN_BLOCK_W), RNN_BLOCK_W ** -0.5)
    rg_ba = nrm((L, D_RNN), 0.02)
    rg_wx = nrm((L, RNN_BLOCKS, RNN_BLOCK_W, RNN_BLOCK_W), RNN_BLOCK_W ** -0.5)
    rg_bx = nrm((L, D_RNN), 0.02)
    a_pow = jax.random.uniform(nxt(), (L, D_RNN), jnp.float32, minval=0.9, maxval=0.999)
    a_base = a_pow ** (1.0 / RG_C)
    rg_lambda = jnp.log(a_base) - jnp.log1p(-a_base)
    kv_norm = gain((L, DSA_LATENT))
    w_uk = nrm((L, DSA_HEADS, HEAD_DIM, DSA_LATENT), HEAD_DIM ** -0.5)
    w_uv = nrm((L, DSA_HEADS, DSA_LATENT, HEAD_DIM), DSA_LATENT ** -0.5)
    forget_bias = 2.0 + 0.5 * jax.random.normal(nxt(), (L, FOX_HEADS), jnp.float32)
    w_branch = nrm((L, N_BRANCH, BRANCH_W, D), BRANCH_W ** -0.5)
    w_out = nrm((L, D, D), D ** -0.5)
    xattn_norm = gain((L, D))
    mem_norm = gain((L, D))
    w_mq = nrm((L, D, MEM_HEADS * MEM_HEAD_DIM), D ** -0.5)
    w_mkv = nrm((L, D, 2 * MEM_HEADS * MEM_HEAD_DIM), D ** -0.5)
    w_mo = nrm((L, MEM_HEADS * MEM_HEAD_DIM, D), (MEM_HEADS * MEM_HEAD_DIM) ** -0.5)
    ffn2_norm = gain((L, D))
    ffn2_w_gu = nrm((L, D, 2 * D_FF), D ** -0.5)
    ffn2_w_down = nrm((L, D_FF, D), D_FF ** -0.5)
    t5_bias = nrm((NUM_BUCKETS, DSA_HEADS), 0.5)
    final_norm = gain((D,))
    return {'x': x, 'mem': mem, 'ffn1_norm': ffn1_norm, 'ffn1_w_gu': ffn1_w_gu, 'ffn1_w_down': ffn1_w_down,
            'mix_norm': mix_norm, 'w_in': w_in, 'conv_w': conv_w, 'conv_b': conv_b,
            'rg_wa': rg_wa, 'rg_ba': rg_ba, 'rg_wx': rg_wx, 'rg_bx': rg_bx, 'rg_lambda': rg_lambda,
            'kv_norm': kv_norm, 'w_uk': w_uk, 'w_uv': w_uv, 'forget_bias': forget_bias,
            'w_branch': w_branch, 'w_out': w_out, 'xattn_norm': xattn_norm, 'mem_norm': mem_norm,
            'w_mq': w_mq, 'w_mkv': w_mkv, 'w_mo': w_mo, 'ffn2_norm': ffn2_norm, 'ffn2_w_gu': ffn2_w_gu,
            'ffn2_w_down': ffn2_w_down, 't5_bias': t5_bias, 'final_norm': final_norm}


def reference(x, mem, ffn1_norm, ffn1_w_gu, ffn1_w_down, mix_norm, w_in, conv_w, conv_b,
              rg_wa, rg_ba, rg_wx, rg_bx, rg_lambda, kv_norm, w_uk, w_uv, forget_bias,
              w_branch, w_out, xattn_norm, mem_norm, w_mq, w_mkv, w_mo, ffn2_norm, ffn2_w_gu,
              ffn2_w_down, t5_bias, final_norm):
    b_, s_, _ = x.shape
    for l in range(DEPTH):
        x = x + 0.5 * swiglu_ffn(rms_norm(x, ffn1_norm[l]), ffn1_w_gu[l], ffn1_w_down[l])
        h = rms_norm(x, mix_norm[l])
        (x_r, g_r, q_b, c_kv, q_i, k_i, w_i, q_c, k_c, v_c, f_c,
         gate_a, gate_b, gate_c) = jnp.split(h @ w_in[l], _split_points(), axis=-1)
        o_a = rglru_branch(x_r, g_r, conv_w[l], conv_b[l], rg_wa[l], rg_ba[l], rg_wx[l], rg_bx[l], rg_lambda[l])
        o_b = dsa_branch(q_b.reshape(b_, s_, DSA_HEADS, HEAD_DIM), c_kv,
                         q_i.reshape(b_, s_, IDX_HEADS, IDX_DIM), k_i, w_i,
                         kv_norm[l], w_uk[l], w_uv[l], t5_bias)
        o_c = fox_branch(q_c.reshape(b_, s_, FOX_HEADS, HEAD_DIM), k_c.reshape(b_, s_, FOX_HEADS, HEAD_DIM),
                         v_c.reshape(b_, s_, FOX_HEADS, HEAD_DIM), f_c, forget_bias[l])
        merged = (jax.nn.sigmoid(gate_a) * (o_a @ w_branch[l, 0])
                  + jax.nn.sigmoid(gate_b) * (o_b @ w_branch[l, 1])
                  + jax.nn.sigmoid(gate_c) * (o_c @ w_branch[l, 2]))
        x = x + merged @ w_out[l]
        x = x + memory_cross_attention(rms_norm(x, xattn_norm[l]), rms_norm(mem, mem_norm[l]),
                                       w_mq[l], w_mkv[l], w_mo[l])
        x = x + 0.5 * swiglu_ffn(rms_norm(x, ffn2_norm[l]), ffn2_w_gu[l], ffn2_w_down[l])
    return rms_norm(x, final_norm)
```

```python
import functools
import math

import numpy as np
import jax
import jax.numpy as jnp
from jax import lax
from jax.experimental import pallas as pl
from jax.experimental.pallas import tpu as pltpu

F32 = jnp.float32
BF16 = jnp.bfloat16
I32 = jnp.int32

LANES = 128
VMEM_LIMIT_BYTES = 56 * 1024 * 1024

EPS = 1e-6
NEG = -1e30
CHUNK = 64
HEAD_DIM = 64
RG_C = 8.0
CONV_W = 4
IDX_HEADS = 8
IDX_DIM = 64
TOPK_MAX = 256
MEM_HEADS = 4
MEM_HEAD_DIM = 128
NUM_BUCKETS = 32
MAX_DISTANCE = 128

TM_TOKENS = 512
FFN_CHUNK = 256
TS_SCAN = 256
TQ_FOX = 256
TQ_DSA = 128
TK_DSA = 512
DSA_PAD = TK_DSA - TQ_DSA


def _key_of_float(v):
    b = int(np.float32(v).view(np.int32))
    return b ^ ((b >> 31) & 0x7FFFFFFF)


KEY_NEG = _key_of_float(NEG)


def _params(*semantics):
    return pltpu.CompilerParams(dimension_semantics=semantics, vmem_limit_bytes=VMEM_LIMIT_BYTES)


def _const_spec(shape):
    nd = len(shape)
    return pl.BlockSpec(shape, lambda *_: (0,) * nd, pipeline_mode=pl.Buffered(1))


def _rms(x32, g):
    ms = jnp.mean(x32 * x32, axis=-1, keepdims=True)
    return x32 * lax.rsqrt(ms + EPS) * g


def _dot(a, b):
    return jnp.dot(a, b, preferred_element_type=F32)


def _dot_nt(a, b):
    return lax.dot_general(a, b, (((1,), (1,)), ((), ())), preferred_element_type=F32)


def _ffn_kernel(x_ref, g_ref, wg_ref, wu_ref, wd_ref, o_ref, a_ref):
    x = x_ref[...]
    h = _rms(x, g_ref[...]).astype(BF16)
    d_ff = wg_ref.shape[1]
    for f0 in range(0, d_ff, FFN_CHUNK):
        g = _dot(h, wg_ref[:, f0:f0 + FFN_CHUNK])
        u = _dot(h, wu_ref[:, f0:f0 + FFN_CHUNK])
        a_ref[:, f0:f0 + FFN_CHUNK] = (g * jax.nn.sigmoid(g) * u).astype(BF16)
    o_ref[...] = x + 0.5 * _dot(a_ref[...], wd_ref[...])


def _ffn(x, g, wg, wu, wd):
    n, d = x.shape
    d_ff = wg.shape[1]
    tm = min(TM_TOKENS, n)
    return pl.pallas_call(
        _ffn_kernel,
        grid=(n // tm,),
        in_specs=[pl.BlockSpec((tm, d), lambda i: (i, 0)),
                  _const_spec((1, d)), _const_spec((d, d_ff)), _const_spec((d, d_ff)),
                  _const_spec((d_ff, d))],
        out_specs=pl.BlockSpec((tm, d), lambda i: (i, 0)),
        out_shape=jax.ShapeDtypeStruct((n, d), F32),
        scratch_shapes=[pltpu.VMEM((tm, d_ff), BF16)],
        compiler_params=_params("parallel"),
        name="ffn",
    )(x, g, wg, wu, wd)


def _norm_proj_kernel(x_ref, g_ref, w_ref, *o_refs):
    h = _rms(x_ref[...].astype(F32), g_ref[...]).astype(BF16)
    c0 = 0
    for o_ref in o_refs:
        width = o_ref.shape[1]
        for s0 in range(0, width, 1024):
            s1 = min(s0 + 1024, width)
            o_ref[:, s0:s1] = _dot(h, w_ref[:, c0 + s0:c0 + s1]).astype(o_ref.dtype)
        c0 += width


def _norm_proj(x, g, w, widths, dtypes):
    n, d = x.shape
    tm = min(TM_TOKENS, n)
    assert sum(widths) == w.shape[1]
    return pl.pallas_call(
        _norm_proj_kernel,
        grid=(n // tm,),
        in_specs=[pl.BlockSpec((tm, d), lambda i: (i, 0)), _const_spec((1, d)),
                  _const_spec(w.shape)],
        out_specs=[pl.BlockSpec((tm, wd), lambda i: (i, 0)) for wd in widths],
        out_shape=[jax.ShapeDtypeStruct((n, wd), dt) for wd, dt in zip(widths, dtypes)],
        compiler_params=_params("parallel"),
        name="norm_proj",
    )(x, g, w)


def _gelu_tanh(x):
    return 0.5 * x * (1.0 + jnp.tanh(math.sqrt(2.0 / math.pi) * (x + 0.044715 * (x * x * x))))


def _softplus(x):
    return jnp.maximum(x, 0.0) + jnp.log1p(jnp.exp(-jnp.abs(x)))


def _rglru_kernel(rg_ref, cw_ref, cb_ref, wax_ref, bax_ref, lam_ref, o_ref, xbuf, hc):
    ts = rg_ref.shape[0]
    dr = o_ref.shape[1]

    @pl.when(pl.program_id(1) == 0)
    def _():
        xbuf[0:8, :] = jnp.zeros((8, dr), F32)
        hc[...] = jnp.zeros_like(hc)

    xr = rg_ref[:, :dr].astype(F32)
    gr = rg_ref[:, dr:].astype(F32)
    xbuf[8:, :] = xr
    xc = cb_ref[...] + cw_ref[3:4, :] * xr
    for j in range(CONV_W - 1):
        xc = xc + cw_ref[j:j + 1, :] * xbuf[pl.ds(5 + j, ts), :]
    xbuf[0:8, :] = xbuf[ts:ts + 8, :]

    ax = _dot(xc.astype(BF16), wax_ref[...]) + bax_ref[...]
    r = jax.nn.sigmoid(ax[:, :dr])
    gi = jax.nn.sigmoid(ax[:, dr:])
    log_a = (-RG_C) * r * _softplus(-lam_ref[...])
    a = jnp.exp(log_a)
    th = jnp.tanh(log_a)
    u = jnp.sqrt(-2.0 * th / (1.0 - th)) * (gi * xc)

    row = lax.broadcasted_iota(I32, (ts, dr), 0)
    d = 1
    while d < ts:
        a_sh = pltpu.roll(a, d, 0)
        u_sh = pltpu.roll(u, d, 0)
        keep = row >= d
        u = jnp.where(keep, a * u_sh + u, u)
        a = jnp.where(keep, a * a_sh, a)
        d *= 2
    h = u + a * hc[0:1, :]
    hc[...] = jnp.broadcast_to(h[ts - 1:ts, :], hc.shape)
    o_ref[...] = (h * _gelu_tanh(gr)).astype(o_ref.dtype)


def _rglru(rg, batch, conv_w, conv_b, wax, bax, lam):
    n, two_dr = rg.shape
    dr = two_dr // 2
    seq = n // batch
    ts = min(TS_SCAN, seq)
    ns = seq // ts
    return pl.pallas_call(
        _rglru_kernel,
        grid=(batch, ns),
        in_specs=[pl.BlockSpec((ts, two_dr), lambda b, s: (b * ns + s, 0)),
                  _const_spec(conv_w.shape), _const_spec(conv_b.shape), _const_spec(wax.shape),
                  _const_spec(bax.shape), _const_spec(lam.shape)],
        out_specs=pl.BlockSpec((ts, dr), lambda b, s: (b * ns + s, 0)),
        out_shape=jax.ShapeDtypeStruct((n, dr), BF16),
        scratch_shapes=[pltpu.VMEM((ts + 8, dr), F32), pltpu.VMEM((8, dr), F32)],
        compiler_params=_params("parallel", "arbitrary"),
        name="rglru",
    )(rg, conv_w, conv_b, wax, bax, lam)


def _forget_cumsum_kernel(f_ref, b_ref, o_ref):
    z = f_ref[...] + b_ref[...]
    x = -_softplus(-z)
    seq = x.shape[0]
    row = lax.broadcasted_iota(I32, x.shape, 0)
    d = 1
    while d < seq:
        x = jnp.where(row >= d, x + pltpu.roll(x, d, 0), x)
        d *= 2
    o_ref[...] = x


def _forget_cumsum(small, batch, fbias):
    n, w = small.shape
    seq = n // batch
    return pl.pallas_call(
        _forget_cumsum_kernel,
        grid=(batch,),
        in_specs=[pl.BlockSpec((seq, w), lambda b: (b, 0)), _const_spec((1, w))],
        out_specs=pl.BlockSpec((seq, w), lambda b: (b, 0)),
        out_shape=jax.ShapeDtypeStruct((n, w), F32),
        compiler_params=_params("parallel"),
        name="forget_cumsum",
    )(small, fbias)


def _fox_kernel(q_ref, k_ref, v_ref, cum_ref, o_ref):
    tq = q_ref.shape[0]
    n_pairs = q_ref.shape[1] // LANES
    i = pl.program_id(1)
    lane = lax.broadcasted_iota(I32, (1, LANES), 1)
    row = lax.broadcasted_iota(I32, (tq, tq), 0)
    col = lax.broadcasted_iota(I32, (tq, tq), 1)
    causal = col <= row

    for p in range(n_pairs):
        lanes = slice(p * LANES, (p + 1) * LANES)
        qp = q_ref[:, lanes]
        outs = []
        for e in range(2):
            head = 2 * p + e
            in_head = (lane >= e * HEAD_DIM) & (lane < (e + 1) * HEAD_DIM)
            qh = jnp.where(in_head, qp, jnp.zeros_like(qp)) * jnp.asarray(HEAD_DIM ** -0.5, BF16)

            def step(j, carry, masked):
                m, l, acc = carry
                rows = pl.ds(pl.multiple_of(j * tq, tq), tq)
                s = _dot_nt(qh, k_ref[rows, lanes]) - cum_ref[head, pl.ds(j, 1), :]
                if masked:
                    s = jnp.where(causal, s, NEG)
                m_new = jnp.maximum(m, jnp.max(s, axis=1, keepdims=True))
                alpha = jnp.exp(m - m_new)
                pr = jnp.exp(s - m_new)
                l = alpha * l + jnp.sum(pr, axis=1, keepdims=True)
                acc = alpha * acc + _dot(pr.astype(BF16), v_ref[rows, lanes])
                return m_new, l, acc

            carry = (jnp.full((tq, 1), NEG, F32), jnp.zeros((tq, 1), F32),
                     jnp.zeros((tq, LANES), F32))
            carry = lax.fori_loop(0, i, functools.partial(step, masked=False), carry)
            _, l, acc = step(i, carry, masked=True)
            outs.append(acc / l)
        o_ref[:, lanes] = jnp.where(lane < HEAD_DIM, outs[0], outs[1]).astype(o_ref.dtype)


def _fox(q, k, v, cum, batch):
    n, w = q.shape
    seq = n // batch
    tq = min(TQ_FOX, seq)
    nq = seq // tq
    heads = cum.shape[1]
    return pl.pallas_call(
        _fox_kernel,
        grid=(batch, nq),
        in_specs=[pl.BlockSpec((tq, w), lambda b, i: (b * nq + i, 0)),
                  pl.BlockSpec((seq, w), lambda b, i: (b, 0)),
                  pl.BlockSpec((seq, w), lambda b, i: (b, 0)),
                  pl.BlockSpec((None, heads, nq, tq), lambda b, i: (b, 0, 0, 0))],
        out_specs=pl.BlockSpec((tq, w), lambda b, i: (b * nq + i, 0)),
        out_shape=jax.ShapeDtypeStruct((n, w), BF16),
        compiler_params=_params("parallel", "arbitrary"),
        name="fox",
    )(q, k, v, cum)


def _float_key(x):
    b = lax.bitcast_convert_type(x, I32)
    return b ^ ((b >> 31) & 0x7FFFFFFF)


def _dsa_kernel(qb_ref, qi_ref, sm_ref, ckv_ref, ki_ref, kvn_ref, wuk_ref, wuv_ref, bias_ref,
                o_ref, c_s, ki_s, keys, wb, m_s, l_s, acc_s, *, k_sel):
    tq = qb_ref.shape[0]
    seq = ckv_ref.shape[0]
    heads = bias_ref.shape[0]
    tk = TK_DSA
    i = pl.program_id(1)

    @pl.when(i == 0)
    def _():
        c_s[0:DSA_PAD, :] = jnp.zeros((DSA_PAD, LANES), BF16)
        ki_s[0:DSA_PAD, :] = jnp.zeros((DSA_PAD, LANES), BF16)
        c_s[DSA_PAD:, :] = _rms(ckv_ref[...].astype(F32), kvn_ref[...]).astype(BF16)
        ki_s[DSA_PAD:, :] = ki_ref[...]

    n_valid = (i + 1) * tq
    n_tiles = (n_valid + tk - 1) // tk

    def tile_rows(k):
        return pl.ds(pl.multiple_of(n_valid - (k + 1) * tk + DSA_PAD, tq), tk)

    lane = lax.broadcasted_iota(I32, (1, LANES), 1)
    w_scale = IDX_HEADS ** -0.5 * IDX_DIM ** -0.5
    qm = []
    for h in range(IDX_HEADS):
        qp = qi_ref[:, (h // 2) * LANES:(h // 2 + 1) * LANES]
        e = h % 2
        in_head = (lane >= e * IDX_DIM) & (lane < (e + 1) * IDX_DIM)
        qm.append(jnp.where(in_head, qp, jnp.zeros_like(qp)))
        wb[h] = jnp.broadcast_to(sm_ref[:, h:h + 1] * w_scale, (tq, LANES))
    row_chunk = (i * tq + lax.broadcasted_iota(I32, (tq, tk), 0)) >> 6
    col_local = lax.broadcasted_iota(I32, (tq, tk), 1)

    def score_tile(k, _):
        kt = ki_s[tile_rows(k), :]
        sc = jnp.zeros((tq, tk), F32)
        for h in range(IDX_HEADS):
            d = jnp.maximum(_dot_nt(qm[h], kt), 0.0)
            sc = sc + d * jnp.concatenate([wb[h]] * (tk // LANES), axis=1)
        colg = col_local + (n_valid - (k + 1) * tk)
        sc = jnp.where(colg >= 0, jnp.where((colg >> 6) <= row_chunk, sc, NEG), NEG)
        keys[k] = _float_key(sc)
        return 0

    lax.fori_loop(0, n_tiles, score_tile, 0)

    def count_ge(thr):
        def body(k, cnt):
            ge = jnp.where(keys[k] >= thr, 1, 0)
            for c in range(tk // LANES):
                cnt = cnt + ge[:, c * LANES:(c + 1) * LANES]
            return cnt
        cnt = lax.fori_loop(0, n_tiles, body, jnp.zeros((tq, LANES), I32))
        return jnp.sum(cnt, axis=1, keepdims=True)

    sign = jnp.int32(-2 ** 31)

    def bit_step(b, u):
        cand = u | lax.shift_left(jnp.int32(1), 31 - b)
        return jnp.where(count_ge(cand ^ sign) >= k_sel, cand, u)

    u = lax.fori_loop(0, 32, bit_step, jnp.zeros((tq, 1), I32))
    thr = jnp.maximum(u ^ sign, KEY_NEG + 1)

    excess = count_ge(thr) - k_sel

    @pl.when(jnp.max(excess) > 0)
    def _():
        def count_eq_below(cut):
            def body(k, cnt):
                colg = col_local + (n_valid - (k + 1) * tk)
                hit = jnp.where(keys[k] == thr, jnp.where(colg < cut, 1, 0), 0)
                for c in range(tk // LANES):
                    cnt = cnt + hit[:, c * LANES:(c + 1) * LANES]
                return cnt
            cnt = lax.fori_loop(0, n_tiles, body, jnp.zeros((tq, LANES), I32))
            return jnp.sum(cnt, axis=1, keepdims=True)

        n_bits = max(1, (2 * seq - 1).bit_length())
        keep = count_eq_below(jnp.full((tq, 1), 2 ** n_bits, I32)) - jnp.maximum(excess, 0)

        def cut_step(b, cut):
            cand = cut | lax.shift_left(jnp.int32(1), n_bits - 1 - b)
            return jnp.where(count_eq_below(cand) <= keep, cand, cut)

        cut = lax.fori_loop(0, n_bits, cut_step, jnp.zeros((tq, 1), I32))

        def drop(k, _):
            colg = col_local + (n_valid - (k + 1) * tk)
            kk = keys[k]
            keys[k] = jnp.where(kk == thr, jnp.where(colg >= cut, KEY_NEG, kk), kk)
            return 0

        lax.fori_loop(0, n_tiles, drop, 0)

    ql = _dot(qb_ref[...], wuk_ref[...]) * (HEAD_DIM ** -0.5)
    ql = jnp.concatenate([ql[:, h * LANES:(h + 1) * LANES] for h in range(heads)],
                         axis=0).astype(BF16)
    m_s[...] = jnp.full(m_s.shape, NEG, F32)
    l_s[...] = jnp.zeros(l_s.shape, F32)
    acc_s[...] = jnp.zeros(acc_s.shape, F32)

    def attend(k, near):
        ct = c_s[tile_rows(k), :]
        s_all = _dot_nt(ql, ct)
        sel = keys[k] >= thr
        ps = []
        for h in range(heads):
            s = s_all[h * tq:(h + 1) * tq, :]
            if near:
                s = jnp.concatenate([s[:, :tk - 2 * tq], s[:, tk - 2 * tq:] + bias_ref[h]], axis=1)
            s = jnp.where(sel, s, NEG)
            m_old = m_s[h]
            m_new = jnp.maximum(m_old, jnp.max(s, axis=1, keepdims=True))
            alpha = jnp.exp(m_old - m_new)
            pr = jnp.exp(s - m_new)
            l_s[h] = alpha * l_s[h] + jnp.sum(pr, axis=1, keepdims=True)
            acc_s[h] = alpha * acc_s[h]
            m_s[h] = m_new
            ps.append(pr.astype(BF16))
        pv = _dot(jnp.concatenate(ps, axis=0), ct)
        for h in range(heads):
            acc_s[h] = acc_s[h] + pv[h * tq:(h + 1) * tq, :]

    attend(0, near=True)

    def far(k, _):
        attend(k, near=False)
        return 0

    lax.fori_loop(1, n_tiles, far, 0)

    o_lat = jnp.concatenate([acc_s[h] / l_s[h] for h in range(heads)], axis=1).astype(BF16)
    o_ref[...] = _dot(o_lat, wuv_ref[...]).astype(o_ref.dtype)


def _dsa(qb, qi, small, ckv, ki2, kvn, wuk_bd, wuv_bd, bias_near, batch):
    n, w = qb.shape
    seq = n // batch
    tq = TQ_DSA
    nq = seq // tq
    heads = bias_near.shape[0]
    k_sel = min(TOPK_MAX, seq // 4)
    blk = lambda width: pl.BlockSpec((tq, width), lambda b, i: (b * nq + i, 0))
    per_batch = pl.BlockSpec((seq, LANES), lambda b, i: (b, 0))
    return pl.pallas_call(
        functools.partial(_dsa_kernel, k_sel=k_sel),
        grid=(batch, nq),
        in_specs=[blk(w), blk(w), blk(LANES), per_batch, per_batch, _const_spec(kvn.shape),
                  _const_spec(wuk_bd.shape), _const_spec(wuv_bd.shape),
                  _const_spec(bias_near.shape)],
        out_specs=blk(w),
        out_shape=jax.ShapeDtypeStruct((n, w), BF16),
        scratch_shapes=[pltpu.VMEM((seq + DSA_PAD, LANES), BF16),
                        pltpu.VMEM((seq + DSA_PAD, LANES), BF16),
                        pltpu.VMEM((pl.cdiv(seq, TK_DSA) + 1, tq, TK_DSA), I32),
                        pltpu.VMEM((IDX_HEADS, tq, LANES), F32),
                        pltpu.VMEM((heads, tq, 1), F32),
                        pltpu.VMEM((heads, tq, 1), F32),
                        pltpu.VMEM((heads, tq, LANES), F32)],
        compiler_params=_params("parallel", "arbitrary"),
        name="dsa",
    )(qb, qi, small, ckv, ki2, kvn, wuk_bd, wuv_bd, bias_near)


def _merge_kernel(x_ref, oa_ref, ob_ref, oc_ref, gate_ref, wbr_ref, wout_ref, o_ref):
    d = x_ref.shape[1]
    merged = jnp.zeros(x_ref.shape, F32)
    for j, br_ref in enumerate((oa_ref, ob_ref, oc_ref)):
        gate = jax.nn.sigmoid(gate_ref[:, j * d:(j + 1) * d].astype(F32))
        merged = merged + gate * _dot(br_ref[...], wbr_ref[j])
    o_ref[...] = x_ref[...] + _dot(merged.astype(BF16), wout_ref[...])


def _merge(x, oa, ob, oc, gates, wbr, wout):
    n, d = x.shape
    tm = min(TM_TOKENS, n)
    bw = oa.shape[1]
    blk = lambda width: pl.BlockSpec((tm, width), lambda i: (i, 0))
    return pl.pallas_call(
        _merge_kernel,
        grid=(n // tm,),
        in_specs=[blk(d), blk(bw), blk(bw), blk(bw), blk(3 * d), _const_spec(wbr.shape),
                  _const_spec(wout.shape)],
        out_specs=blk(d),
        out_shape=jax.ShapeDtypeStruct((n, d), F32),
        compiler_params=_params("parallel"),
        name="merge",
    )(x, oa, ob, oc, gates, wbr, wout)


def _mem_attn_kernel(x_ref, g_ref, wq_ref, kv_ref, wo_ref, o_ref):
    x = x_ref[...]
    hw = MEM_HEADS * MEM_HEAD_DIM
    q = _dot(_rms(x, g_ref[...]).astype(BF16), wq_ref[...]).astype(BF16)
    outs = []
    for h in range(MEM_HEADS):
        lanes = slice(h * MEM_HEAD_DIM, (h + 1) * MEM_HEAD_DIM)
        s = _dot_nt(q[:, lanes], kv_ref[:, lanes]) * (MEM_HEAD_DIM ** -0.5)
        pr = jnp.exp(s - jnp.max(s, axis=1, keepdims=True))
        pr = pr / jnp.sum(pr, axis=1, keepdims=True)
        outs.append(_dot(pr.astype(BF16), kv_ref[:, hw + h * MEM_HEAD_DIM:hw + (h + 1) * MEM_HEAD_DIM]))
    o = jnp.concatenate(outs, axis=1).astype(BF16)
    o_ref[...] = x + _dot(o, wo_ref[...])


def _mem_attn(x, g, wq, kv, wo, batch):
    n, d = x.shape
    seq = n // batch
    tm = min(TM_TOKENS, seq)
    ns = seq // tm
    mem_len = kv.shape[0] // batch
    return pl.pallas_call(
        _mem_attn_kernel,
        grid=(batch, ns),
        in_specs=[pl.BlockSpec((tm, d), lambda b, s: (b * ns + s, 0)), _const_spec((1, d)),
                  _const_spec(wq.shape),
                  pl.BlockSpec((mem_len, kv.shape[1]), lambda b, s: (b, 0)),
                  _const_spec(wo.shape)],
        out_specs=pl.BlockSpec((tm, d), lambda b, s: (b * ns + s, 0)),
        out_shape=jax.ShapeDtypeStruct((n, d), F32),
        compiler_params=_params("parallel", "arbitrary"),
        name="mem_attn",
    )(x, g, wq, kv, wo)


def _final_norm_kernel(x_ref, g_ref, o_ref):
    o_ref[...] = _rms(x_ref[...], g_ref[...])


def _final_norm(x, g):
    n, d = x.shape
    tm = min(TM_TOKENS, n)
    return pl.pallas_call(
        _final_norm_kernel,
        grid=(n // tm,),
        in_specs=[pl.BlockSpec((tm, d), lambda i: (i, 0)), _const_spec((1, d))],
        out_specs=pl.BlockSpec((tm, d), lambda i: (i, 0)),
        out_shape=jax.ShapeDtypeStruct((n, d), F32),
        compiler_params=_params("parallel"),
        name="final_norm",
    )(x, g)


def _t5_bucket(rel):
    nb = NUM_BUCKETS // 2
    max_exact = nb // 2
    base = jnp.where(rel > 0, nb, 0)
    n = jnp.abs(rel)
    n_f = jnp.maximum(n, 1).astype(jnp.float32)
    large = max_exact + (jnp.log(n_f / max_exact) / math.log(MAX_DISTANCE / max_exact)
                         * (nb - max_exact)).astype(jnp.int32)
    large = jnp.minimum(large, nb - 1)
    return base + jnp.where(n < max_exact, n, large)


def _near_bias(t5_bias):
    t = jnp.arange(TQ_DSA)[:, None]
    s = jnp.arange(2 * TQ_DSA)[None, :] - TQ_DSA
    tbl = t5_bias[_t5_bucket(s - t)] - t5_bias[NUM_BUCKETS // 2 - 1]
    return jnp.moveaxis(tbl, -1, 0).astype(F32)


def _block_diag(w):
    nb, a, b = w.shape
    eye = jnp.eye(nb, dtype=w.dtype)
    return (eye[:, None, :, None] * w[:, :, None, :]).reshape(nb * a, nb * b)


def _pad_cols(w, width):
    return jnp.pad(w, ((0, 0), (0, width - w.shape[1])))


def kernel(x, mem, ffn1_norm, ffn1_w_gu, ffn1_w_down, mix_norm, w_in, conv_w, conv_b, rg_wa, rg_ba, rg_wx, rg_bx, rg_lambda, kv_norm, w_uk, w_uv, forget_bias, w_branch, w_out, xattn_norm, mem_norm, w_mq, w_mkv, w_mo, ffn2_norm, ffn2_w_gu, ffn2_w_down, t5_bias, final_norm):
    batch, seq, d = x.shape
    depth = w_in.shape[0]
    d_ff = ffn1_w_down.shape[1]
    dr = conv_w.shape[2]
    heads = w_uk.shape[1]
    d_lat = w_uk.shape[3]
    n = batch * seq
    assert seq % TK_DSA == 0 and d_lat == LANES and 2 * IDX_DIM == LANES and 2 * HEAD_DIM == LANES

    widths = (dr, dr, heads * HEAD_DIM, d_lat, IDX_HEADS * IDX_DIM, IDX_DIM, IDX_HEADS,
              heads * HEAD_DIM, heads * HEAD_DIM, heads * HEAD_DIM, heads, d, d, d)
    offs = np.concatenate([[0], np.cumsum(widths)])
    col = lambda w, j: w[:, int(offs[j]):int(offs[j + 1])]

    bias_near = _near_bias(t5_bias)
    row = lambda v: v.reshape(1, -1).astype(F32)

    xf = x.reshape(n, d)
    memf = mem.reshape(batch * mem.shape[1], d)
    for l in range(depth):
        wl = w_in[l]
        small_w = _pad_cols(jnp.concatenate([col(wl, 6), col(wl, 10)], axis=1), LANES)
        w_all = jnp.concatenate(
            [col(wl, 0), col(wl, 1), col(wl, 2), col(wl, 3), col(wl, 4), col(wl, 5), col(wl, 5),
             col(wl, 7), col(wl, 8), col(wl, 9), col(wl, 11), col(wl, 12), col(wl, 13)],
            axis=1).astype(BF16)
        w_all = jnp.concatenate([w_all, small_w.astype(BF16)], axis=1)
        out_widths = (2 * dr, heads * HEAD_DIM, d_lat, IDX_HEADS * IDX_DIM, 2 * IDX_DIM,
                      heads * HEAD_DIM, heads * HEAD_DIM, heads * HEAD_DIM, 3 * d, LANES)
        out_dtypes = (BF16,) * 9 + (F32,)

        xf = _ffn(xf, row(ffn1_norm[l]), ffn1_w_gu[l][:, :d_ff].astype(BF16),
                  ffn1_w_gu[l][:, d_ff:].astype(BF16), ffn1_w_down[l].astype(BF16))

        rg, qb, ckv, qi, ki2, qc, kc, vc, gates, small = _norm_proj(
            xf, row(mix_norm[l]), w_all, out_widths, out_dtypes)

        wax = jnp.concatenate([_block_diag(rg_wa[l]), _block_diag(rg_wx[l])], axis=1).astype(BF16)
        bax = jnp.concatenate([rg_ba[l], rg_bx[l]]).reshape(1, -1)
        o_a = _rglru(rg, batch, conv_w[l], row(conv_b[l]), wax, bax, row(rg_lambda[l]))

        fbias = _pad_cols(jnp.concatenate([jnp.zeros((IDX_HEADS,), F32), forget_bias[l]])[None], LANES)
        cum = _forget_cumsum(small, batch, fbias)
        nq = seq // min(TQ_FOX, seq)
        cum = jnp.transpose(cum.reshape(batch, seq, LANES)[:, :, IDX_HEADS:IDX_HEADS + heads],
                            (0, 2, 1)).reshape(batch, heads, nq, seq // nq)
        o_c = _fox(qc, kc, vc, cum, batch)

        o_b = _dsa(qb, qi, small, ckv, ki2, row(kv_norm[l]), _block_diag(w_uk[l]).astype(BF16),
                   _block_diag(w_uv[l]).astype(BF16), bias_near, batch)

        xf = _merge(xf, o_a, o_b, o_c, gates, w_branch[l].astype(BF16), w_out[l].astype(BF16))

        (kv,) = _norm_proj(memf, row(mem_norm[l]), w_mkv[l].astype(BF16),
                           (w_mkv.shape[2],), (BF16,))
        xf = _mem_attn(xf, row(xattn_norm[l]), w_mq[l].astype(BF16), kv, w_mo[l].astype(BF16), batch)

        xf = _ffn(xf, row(ffn2_norm[l]), ffn2_w_gu[l][:, :d_ff].astype(BF16),
                  ffn2_w_gu[l][:, d_ff:].astype(BF16), ffn2_w_down[l].astype(BF16))
    return _final_norm(xf, row(final_norm)).reshape(batch, seq, d)
```

```python
import functools
import math

import numpy as np
import jax
import jax.numpy as jnp
from jax import lax
from jax.experimental import pallas as pl
from jax.experimental.pallas import tpu as pltpu

F32 = jnp.float32
BF16 = jnp.bfloat16
I32 = jnp.int32

LANES = 128
VMEM_LIMIT_BYTES = 56 * 1024 * 1024

EPS = 1e-6
NEG = -1e30
CHUNK = 64
CHUNK_SHIFT = CHUNK.bit_length() - 1
HEAD_DIM = 64
RG_C = 8.0
CONV_W = 4
IDX_HEADS = 8
IDX_DIM = 64
TOPK_MAX = 256
MEM_HEADS = 4
MEM_HEAD_DIM = 128
NUM_BUCKETS = 32
MAX_DISTANCE = 128

TM_TOKENS = 512
FFN_CHUNK = 256
TS_SCAN = 256
TQ_FOX = 512
CUM_PARTS = 3
TQ_DSA = 128
TK_DSA = 512
DSA_HEAD_GROUP = 2
DSA_PAD = TK_DSA - TQ_DSA


def _key_of_float(v):
    b = int(np.float32(v).view(np.int32))
    return b ^ ((b >> 31) & 0x7FFFFFFF)


KEY_NEG = _key_of_float(NEG)


def _params(*semantics):
    return pltpu.CompilerParams(dimension_semantics=semantics, vmem_limit_bytes=VMEM_LIMIT_BYTES)


def _const_spec(shape):
    nd = len(shape)
    return pl.BlockSpec(shape, lambda *_: (0,) * nd, pipeline_mode=pl.Buffered(1))


def _rms(x32, g):
    ms = jnp.mean(x32 * x32, axis=-1, keepdims=True)
    return x32 * lax.rsqrt(ms + EPS) * g


def _dot(a, b):
    return jnp.dot(a, b, preferred_element_type=F32)


def _dot_nt(a, b):
    return lax.dot_general(a, b, (((1,), (1,)), ((), ())), preferred_element_type=F32)


def _dot_tn(a, b):
    return lax.dot_general(a, b, (((0,), (0,)), ((), ())), preferred_element_type=F32)


def _ffn_kernel(x_ref, g_ref, wg_ref, wu_ref, wd_ref, o_ref, a_ref):
    x = x_ref[...]
    h = _rms(x, g_ref[...]).astype(BF16)
    d_ff = wg_ref.shape[1]
    for f0 in range(0, d_ff, FFN_CHUNK):
        g = _dot(h, wg_ref[:, f0:f0 + FFN_CHUNK])
        u = _dot(h, wu_ref[:, f0:f0 + FFN_CHUNK])
        a_ref[:, f0:f0 + FFN_CHUNK] = (g * jax.nn.sigmoid(g) * u).astype(BF16)
    o_ref[...] = x + 0.5 * _dot(a_ref[...], wd_ref[...])


def _ffn(x, g, wg, wu, wd):
    n, d = x.shape
    d_ff = wg.shape[1]
    tm = min(TM_TOKENS, n)
    return pl.pallas_call(
        _ffn_kernel,
        grid=(n // tm,),
        in_specs=[pl.BlockSpec((tm, d), lambda i: (i, 0)),
                  _const_spec((1, d)), _const_spec((d, d_ff)), _const_spec((d, d_ff)),
                  _const_spec((d_ff, d))],
        out_specs=pl.BlockSpec((tm, d), lambda i: (i, 0)),
        out_shape=jax.ShapeDtypeStruct((n, d), F32),
        scratch_shapes=[pltpu.VMEM((tm, d_ff), BF16)],
        compiler_params=_params("parallel"),
        name="ffn",
    )(x, g, wg, wu, wd)


def _norm_proj_kernel(x_ref, g_ref, w_ref, *o_refs):
    h = _rms(x_ref[...].astype(F32), g_ref[...]).astype(BF16)
    c0 = 0
    for o_ref in o_refs:
        width = o_ref.shape[1]
        for s0 in range(0, width, 1024):
            s1 = min(s0 + 1024, width)
            o_ref[:, s0:s1] = _dot(h, w_ref[:, c0 + s0:c0 + s1]).astype(o_ref.dtype)
        c0 += width


def _norm_proj(x, g, w, widths, dtypes):
    n, d = x.shape
    tm = min(TM_TOKENS, n)
    assert sum(widths) == w.shape[1]
    return pl.pallas_call(
        _norm_proj_kernel,
        grid=(n // tm,),
        in_specs=[pl.BlockSpec((tm, d), lambda i: (i, 0)), _const_spec((1, d)),
                  _const_spec(w.shape)],
        out_specs=[pl.BlockSpec((tm, wd), lambda i: (i, 0)) for wd in widths],
        out_shape=[jax.ShapeDtypeStruct((n, wd), dt) for wd, dt in zip(widths, dtypes)],
        compiler_params=_params("parallel"),
        name="norm_proj",
    )(x, g, w)


def _gelu_tanh(x):
    return 0.5 * x * (1.0 + jnp.tanh(math.sqrt(2.0 / math.pi) * (x + 0.044715 * (x * x * x))))


def _softplus(x):
    return jnp.maximum(x, 0.0) + jnp.log1p(jnp.exp(-jnp.abs(x)))


def _rglru_kernel(rg_ref, cw_ref, cb_ref, wax_ref, bax_ref, lam_ref, o_ref, xbuf, hc):
    ts = rg_ref.shape[0]
    dr = o_ref.shape[1]

    @pl.when(pl.program_id(1) == 0)
    def _():
        xbuf[0:8, :] = jnp.zeros((8, dr), F32)
        hc[...] = jnp.zeros_like(hc)

    xr = rg_ref[:, :dr].astype(F32)
    gr = rg_ref[:, dr:].astype(F32)
    xbuf[8:, :] = xr
    xc = cb_ref[...] + cw_ref[3:4, :] * xr
    for j in range(CONV_W - 1):
        xc = xc + cw_ref[j:j + 1, :] * xbuf[pl.ds(5 + j, ts), :]
    xbuf[0:8, :] = xbuf[ts:ts + 8, :]

    ax = _dot(xc.astype(BF16), wax_ref[...]) + bax_ref[...]
    r = jax.nn.sigmoid(ax[:, :dr])
    gi = jax.nn.sigmoid(ax[:, dr:])
    log_a = (-RG_C) * r * _softplus(-lam_ref[...])
    a = jnp.exp(log_a)
    th = jnp.tanh(log_a)
    u = jnp.sqrt(-2.0 * th / (1.0 - th)) * (gi * xc)

    row = lax.broadcasted_iota(I32, (ts, dr), 0)
    d = 1
    while d < ts:
        a_sh = pltpu.roll(a, d, 0)
        u_sh = pltpu.roll(u, d, 0)
        keep = row >= d
        u = jnp.where(keep, a * u_sh + u, u)
        a = jnp.where(keep, a * a_sh, a)
        d *= 2
    h = u + a * hc[0:1, :]
    hc[...] = jnp.broadcast_to(h[ts - 1:ts, :], hc.shape)
    o_ref[...] = (h * _gelu_tanh(gr)).astype(o_ref.dtype)


def _rglru(rg, batch, conv_w, conv_b, wax, bax, lam):
    n, two_dr = rg.shape
    dr = two_dr // 2
    seq = n // batch
    ts = min(TS_SCAN, seq)
    ns = seq // ts
    return pl.pallas_call(
        _rglru_kernel,
        grid=(batch, ns),
        in_specs=[pl.BlockSpec((ts, two_dr), lambda b, s: (b * ns + s, 0)),
                  _const_spec(conv_w.shape), _const_spec(conv_b.shape), _const_spec(wax.shape),
                  _const_spec(bax.shape), _const_spec(lam.shape)],
        out_specs=pl.BlockSpec((ts, dr), lambda b, s: (b * ns + s, 0)),
        out_shape=jax.ShapeDtypeStruct((n, dr), BF16),
        scratch_shapes=[pltpu.VMEM((ts + 8, dr), F32), pltpu.VMEM((8, dr), F32)],
        compiler_params=_params("parallel", "arbitrary"),
        name="rglru",
    )(rg, conv_w, conv_b, wax, bax, lam)


def _forget_cumsum_kernel(f_ref, b_ref, o_ref):
    z = f_ref[...] + b_ref[...]
    x = -_softplus(-z)
    seq = x.shape[0]
    row = lax.broadcasted_iota(I32, x.shape, 0)
    d = 1
    while d < seq:
        x = jnp.where(row >= d, x + pltpu.roll(x, d, 0), x)
        d *= 2
    for j in range(CUM_PARTS):
        part = x.astype(BF16)
        o_ref[j] = part
        x = x - part.astype(F32)


def _forget_cumsum(small, batch, fbias):
    n, w = small.shape
    seq = n // batch
    return pl.pallas_call(
        _forget_cumsum_kernel,
        grid=(batch,),
        in_specs=[pl.BlockSpec((seq, w), lambda b: (b, 0)), _const_spec((1, w))],
        out_specs=pl.BlockSpec((CUM_PARTS, seq, w), lambda b: (0, b, 0)),
        out_shape=jax.ShapeDtypeStruct((CUM_PARTS, n, w), BF16),
        compiler_params=_params("parallel"),
        name="forget_cumsum",
    )(small, fbias)


def _fox_kernel(q_ref, k_ref, v_ref, cp_ref, sel_ref, o_ref, kaug, qaug, m_s, l_s, acc_s):
    tq = q_ref.shape[0]
    n_pairs = q_ref.shape[1] // LANES
    i = pl.program_id(1)
    lane = lax.broadcasted_iota(I32, (1, LANES), 1)

    @pl.when(i == 0)
    def _():
        for p in range(n_pairs):
            kaug[:, 2 * p * LANES:(2 * p + 1) * LANES] = k_ref[:, p * LANES:(p + 1) * LANES]
            extra = _dot(cp_ref[0], sel_ref[p, 0])
            for j in range(1, CUM_PARTS):
                extra = extra + _dot(cp_ref[j], sel_ref[p, j])
            kaug[:, (2 * p + 1) * LANES:(2 * p + 2) * LANES] = extra.astype(BF16)

    for p in range(n_pairs):
        qp = q_ref[:, p * LANES:(p + 1) * LANES]
        for e in range(2):
            rows = slice(e * tq, (e + 1) * tq)
            in_head = (lane >= e * HEAD_DIM) & (lane < (e + 1) * HEAD_DIM)
            qaug[p, rows, 0:LANES] = (jnp.where(in_head, qp, jnp.zeros_like(qp))
                                      * jnp.asarray(HEAD_DIM ** -0.5, BF16))
            ones = jnp.where((lane >= CUM_PARTS * e) & (lane < CUM_PARTS * (e + 1)), 1.0, 0.0)
            qaug[p, rows, LANES:2 * LANES] = jnp.broadcast_to(ones, (tq, LANES)).astype(BF16)

    m_s[...] = jnp.full(m_s.shape, NEG, F32)
    l_s[...] = jnp.zeros(l_s.shape, F32)
    acc_s[...] = jnp.zeros(acc_s.shape, F32)
    key_row = lax.broadcasted_iota(I32, (tq, 2 * tq), 0)
    query = lax.broadcasted_iota(I32, (tq, 2 * tq), 1) & (tq - 1)
    causal = key_row <= query

    def step(j, masked):
        rows = pl.ds(pl.multiple_of(j * tq, tq), tq)
        logits = lambda p: _dot_nt(kaug[rows, 2 * p * LANES:(2 * p + 2) * LANES], qaug[p])
        s_next = logits(0)
        for p in range(n_pairs):
            s = s_next
            if p + 1 < n_pairs:
                s_next = logits(p + 1)
            if masked:
                s = jnp.where(causal, s, NEG)
            m_old = m_s[p]
            m_new = jnp.maximum(m_old, jnp.max(s, axis=0, keepdims=True))
            alpha = jnp.exp(m_old - m_new)
            pr = jnp.exp(s - m_new)
            l_s[p] = alpha * l_s[p] + jnp.sum(pr, axis=0, keepdims=True)
            m_s[p] = m_new
            acc_s[p] = alpha * acc_s[p] + _dot_tn(v_ref[rows, p * LANES:(p + 1) * LANES],
                                                  pr.astype(BF16))

    def body(j, _):
        step(j, masked=False)
        return 0

    lax.fori_loop(0, i, body, 0)
    step(i, masked=True)

    feat = lax.broadcasted_iota(I32, (LANES, tq), 0)
    for p in range(n_pairs):
        o_t = acc_s[p] / l_s[p]
        o_t = jnp.where(feat < HEAD_DIM, o_t[:, :tq], o_t[:, tq:])
        o_ref[:, p * LANES:(p + 1) * LANES] = o_t.T.astype(o_ref.dtype)


def _fox(q, k, v, cum_parts, sel, batch):
    n, w = q.shape
    seq = n // batch
    tq = min(TQ_FOX, seq)
    nq = seq // tq
    n_pairs = w // LANES
    return pl.pallas_call(
        _fox_kernel,
        grid=(batch, nq),
        in_specs=[pl.BlockSpec((tq, w), lambda b, i: (b * nq + i, 0)),
                  pl.BlockSpec((seq, w), lambda b, i: (b, 0)),
                  pl.BlockSpec((seq, w), lambda b, i: (b, 0)),
                  pl.BlockSpec((CUM_PARTS, seq, LANES), lambda b, i: (0, b, 0)),
                  _const_spec(sel.shape)],
        out_specs=pl.BlockSpec((tq, w), lambda b, i: (b * nq + i, 0)),
        out_shape=jax.ShapeDtypeStruct((n, w), BF16),
        scratch_shapes=[pltpu.VMEM((seq, 2 * w), BF16),
                        pltpu.VMEM((n_pairs, 2 * tq, 2 * LANES), BF16),
                        pltpu.VMEM((n_pairs, 1, 2 * tq), F32),
                        pltpu.VMEM((n_pairs, 1, 2 * tq), F32),
                        pltpu.VMEM((n_pairs, LANES, 2 * tq), F32)],
        compiler_params=_params("parallel", "arbitrary"),
        name="fox",
    )(q, k, v, cum_parts, sel)


def _cum_selector(n_pairs, first_lane):
    sel = np.zeros((n_pairs, CUM_PARTS, LANES, LANES), np.float32)
    for p in range(n_pairs):
        for j in range(CUM_PARTS):
            for e in range(2):
                sel[p, j, first_lane + 2 * p + e, CUM_PARTS * e + j] = -1.0
    return jnp.asarray(sel, BF16)


def _float_key(x):
    b = lax.bitcast_convert_type(x, I32)
    return b ^ ((b >> 31) & 0x7FFFFFFF)


def _dsa_kernel(qb_ref, qi_ref, sm_ref, ckv_ref, ki_ref, kvn_ref, wuk_ref, wuv_ref, bias_ref,
                o_ref, c_s, ki_s, keys, qim, m_s, l_s, acc_s, *, k_sel):
    tq = qb_ref.shape[0]
    seq = ckv_ref.shape[0]
    heads = bias_ref.shape[0]
    tk = TK_DSA
    i = pl.program_id(1)

    @pl.when(i == 0)
    def _():
        c_s[0:DSA_PAD, :] = jnp.zeros((DSA_PAD, LANES), BF16)
        ki_s[0:DSA_PAD, :] = jnp.zeros((DSA_PAD, LANES), BF16)
        c_s[DSA_PAD:, :] = _rms(ckv_ref[...].astype(F32), kvn_ref[...]).astype(BF16)
        ki_s[DSA_PAD:, :] = ki_ref[...]

    n_valid = (i + 1) * tq
    n_tiles = (n_valid + tk - 1) // tk

    def tile_rows(k):
        return pl.ds(pl.multiple_of(n_valid - (k + 1) * tk + DSA_PAD, tq), tk)

    lane = lax.broadcasted_iota(I32, (1, LANES), 1)
    for h in range(IDX_HEADS):
        qp = qi_ref[:, (h // 2) * LANES:(h // 2 + 1) * LANES]
        e = h % 2
        in_head = (lane >= e * IDX_DIM) & (lane < (e + 1) * IDX_DIM)
        qim[h * tq:(h + 1) * tq, :] = jnp.where(in_head, qp, jnp.zeros_like(qp))
    w_t = sm_ref[...].T[0:IDX_HEADS, :] * (IDX_HEADS ** -0.5 * IDX_DIM ** -0.5)
    q_chunk = (i * tq + lax.broadcasted_iota(I32, (tk, tq), 1)) >> CHUNK_SHIFT
    key_local = lax.broadcasted_iota(I32, (tk, tq), 0)

    def score_tile(k, _):
        d_all = _dot_nt(ki_s[tile_rows(k), :], qim[...])
        sc = jnp.zeros((tk, tq), F32)
        for h in range(IDX_HEADS):
            sc = sc + jnp.maximum(d_all[:, h * tq:(h + 1) * tq], 0.0) * w_t[h:h + 1, :]
        pos = key_local + (n_valid - (k + 1) * tk)
        sc = jnp.where(pos >= 0, jnp.where((pos >> CHUNK_SHIFT) <= q_chunk, sc, NEG), NEG)
        keys[k] = _float_key(sc)
        return 0

    lax.fori_loop(0, n_tiles, score_tile, 0)

    def count_rows(hit_fn):
        def body(k, cnt):
            return cnt + jnp.sum(hit_fn(k).reshape(tk // 8, 8, tq), axis=0)
        cnt = lax.fori_loop(0, n_tiles, body, jnp.zeros((8, tq), I32))
        return jnp.sum(cnt, axis=0, keepdims=True)

    def count_ge(thr):
        return count_rows(lambda k: jnp.where(keys[k] >= thr, 1, 0))

    sign = jnp.int32(-2 ** 31)

    def bit_step(b, u):
        cand = u | lax.shift_left(jnp.int32(1), 31 - b)
        return jnp.where(count_ge(cand ^ sign) >= k_sel, cand, u)

    u = lax.fori_loop(0, 32, bit_step, jnp.zeros((1, tq), I32))
    thr = jnp.maximum(u ^ sign, KEY_NEG + 1)

    excess = count_ge(thr) - k_sel

    @pl.when(jnp.max(excess) > 0)
    def _():
        def count_eq_below(cut):
            def hit(k):
                pos = key_local + (n_valid - (k + 1) * tk)
                return jnp.where(keys[k] == thr, jnp.where(pos < cut, 1, 0), 0)
            return count_rows(hit)

        n_bits = max(1, (2 * seq - 1).bit_length())
        keep = count_eq_below(jnp.full((1, tq), 2 ** n_bits, I32)) - jnp.maximum(excess, 0)

        def cut_step(b, cut):
            cand = cut | lax.shift_left(jnp.int32(1), n_bits - 1 - b)
            return jnp.where(count_eq_below(cand) <= keep, cand, cut)

        cut = lax.fori_loop(0, n_bits, cut_step, jnp.zeros((1, tq), I32))

        def drop(k, _):
            pos = key_local + (n_valid - (k + 1) * tk)
            kk = keys[k]
            keys[k] = jnp.where(kk == thr, jnp.where(pos >= cut, KEY_NEG, kk), kk)
            return 0

        lax.fori_loop(0, n_tiles, drop, 0)

    ql_t = (_dot_nt(wuk_ref[...], qb_ref[...]) * (HEAD_DIM ** -0.5)).astype(BF16)
    n_groups = heads // DSA_HEAD_GROUP
    gw = DSA_HEAD_GROUP * tq
    ql_g = [jnp.concatenate([ql_t[h * LANES:(h + 1) * LANES, :]
                             for h in range(g * DSA_HEAD_GROUP, (g + 1) * DSA_HEAD_GROUP)], axis=1)
            for g in range(n_groups)]
    m_s[...] = jnp.full(m_s.shape, NEG, F32)
    l_s[...] = jnp.zeros(l_s.shape, F32)
    acc_s[...] = jnp.zeros(acc_s.shape, F32)

    def attend(k, near):
        ct = c_s[tile_rows(k), :]
        sel = keys[k] >= thr
        s_next = _dot(ct, ql_g[0])
        for g in range(n_groups):
            s_g = s_next
            if g + 1 < n_groups:
                s_next = _dot(ct, ql_g[g + 1])
            parts = []
            for hh in range(DSA_HEAD_GROUP):
                s = s_g[:, hh * tq:(hh + 1) * tq]
                if near:
                    s = jnp.concatenate([s[:tk - 2 * tq], s[tk - 2 * tq:]
                                         + bias_ref[g * DSA_HEAD_GROUP + hh]], axis=0)
                parts.append(jnp.where(sel, s, NEG))
            s = jnp.concatenate(parts, axis=1)
            m_old = m_s[g]
            m_new = jnp.maximum(m_old, jnp.max(s, axis=0, keepdims=True))
            alpha = jnp.exp(m_old - m_new)
            pr = jnp.exp(s - m_new)
            l_s[g] = alpha * l_s[g] + jnp.sum(pr, axis=0, keepdims=True)
            m_s[g] = m_new
            acc_s[g] = alpha * acc_s[g] + _dot_tn(ct, pr.astype(BF16))

    attend(0, near=True)

    def far(k, _):
        attend(k, near=False)
        return 0

    lax.fori_loop(1, n_tiles, far, 0)

    o_parts = []
    for g in range(n_groups):
        o_g = acc_s[g] / l_s[g]
        o_parts += [o_g[:, hh * tq:(hh + 1) * tq] for hh in range(DSA_HEAD_GROUP)]
    o_lat_t = jnp.concatenate(o_parts, axis=0).astype(BF16)
    o_ref[...] = _dot(wuv_ref[...], o_lat_t).T.astype(o_ref.dtype)


def _dsa(qb, qi, small, ckv, ki2, kvn, wuk_bd, wuv_bd, bias_near, batch):
    n, w = qb.shape
    seq = n // batch
    tq = TQ_DSA
    nq = seq // tq
    heads = bias_near.shape[0]
    n_groups = heads // DSA_HEAD_GROUP
    gw = DSA_HEAD_GROUP * tq
    k_sel = min(TOPK_MAX, seq // 4)
    blk = lambda width: pl.BlockSpec((tq, width), lambda b, i: (b * nq + i, 0))
    per_batch = pl.BlockSpec((seq, LANES), lambda b, i: (b, 0))
    return pl.pallas_call(
        functools.partial(_dsa_kernel, k_sel=k_sel),
        grid=(batch, nq),
        in_specs=[blk(w), blk(w), blk(LANES), per_batch, per_batch, _const_spec(kvn.shape),
                  _const_spec(wuk_bd.shape), _const_spec(wuv_bd.shape),
                  _const_spec(bias_near.shape)],
        out_specs=blk(w),
        out_shape=jax.ShapeDtypeStruct((n, w), BF16),
        scratch_shapes=[pltpu.VMEM((seq + DSA_PAD, LANES), BF16),
                        pltpu.VMEM((seq + DSA_PAD, LANES), BF16),
                        pltpu.VMEM((pl.cdiv(seq, TK_DSA), TK_DSA, tq), I32),
                        pltpu.VMEM((IDX_HEADS * tq, LANES), BF16),
                        pltpu.VMEM((n_groups, 1, gw), F32),
                        pltpu.VMEM((n_groups, 1, gw), F32),
                        pltpu.VMEM((n_groups, LANES, gw), F32)],
        compiler_params=_params("parallel", "arbitrary"),
        name="dsa",
    )(qb, qi, small, ckv, ki2, kvn, wuk_bd, wuv_bd, bias_near)


def _merge_kernel(x_ref, oa_ref, ob_ref, oc_ref, gate_ref, wbr_ref, wout_ref, o_ref):
    d = x_ref.shape[1]
    merged = jnp.zeros(x_ref.shape, F32)
    for j, br_ref in enumerate((oa_ref, ob_ref, oc_ref)):
        gate = jax.nn.sigmoid(gate_ref[:, j * d:(j + 1) * d].astype(F32))
        merged = merged + gate * _dot(br_ref[...], wbr_ref[j])
    o_ref[...] = x_ref[...] + _dot(merged.astype(BF16), wout_ref[...])


def _merge(x, oa, ob, oc, gates, wbr, wout):
    n, d = x.shape
    tm = min(TM_TOKENS, n)
    bw = oa.shape[1]
    blk = lambda width: pl.BlockSpec((tm, width), lambda i: (i, 0))
    return pl.pallas_call(
        _merge_kernel,
        grid=(n // tm,),
        in_specs=[blk(d), blk(bw), blk(bw), blk(bw), blk(3 * d), _const_spec(wbr.shape),
                  _const_spec(wout.shape)],
        out_specs=blk(d),
        out_shape=jax.ShapeDtypeStruct((n, d), F32),
        compiler_params=_params("parallel"),
        name="merge",
    )(x, oa, ob, oc, gates, wbr, wout)


def _mem_attn_kernel(x_ref, g_ref, wq_ref, kv_ref, wo_ref, o_ref):
    x = x_ref[...]
    hw = MEM_HEADS * MEM_HEAD_DIM
    q = _dot(_rms(x, g_ref[...]).astype(BF16), wq_ref[...]).astype(BF16)
    outs = []
    for h in range(MEM_HEADS):
        lanes = slice(h * MEM_HEAD_DIM, (h + 1) * MEM_HEAD_DIM)
        s = _dot_nt(q[:, lanes], kv_ref[:, lanes]) * (MEM_HEAD_DIM ** -0.5)
        pr = jnp.exp(s - jnp.max(s, axis=1, keepdims=True))
        pr = pr / jnp.sum(pr, axis=1, keepdims=True)
        outs.append(_dot(pr.astype(BF16), kv_ref[:, hw + h * MEM_HEAD_DIM:hw + (h + 1) * MEM_HEAD_DIM]))
    o = jnp.concatenate(outs, axis=1).astype(BF16)
    o_ref[...] = x + _dot(o, wo_ref[...])


def _mem_attn(x, g, wq, kv, wo, batch):
    n, d = x.shape
    seq = n // batch
    tm = min(TM_TOKENS, seq)
    ns = seq // tm
    mem_len = kv.shape[0] // batch
    return pl.pallas_call(
        _mem_attn_kernel,
        grid=(batch, ns),
        in_specs=[pl.BlockSpec((tm, d), lambda b, s: (b * ns + s, 0)), _const_spec((1, d)),
                  _const_spec(wq.shape),
                  pl.BlockSpec((mem_len, kv.shape[1]), lambda b, s: (b, 0)),
                  _const_spec(wo.shape)],
        out_specs=pl.BlockSpec((tm, d), lambda b, s: (b * ns + s, 0)),
        out_shape=jax.ShapeDtypeStruct((n, d), F32),
        compiler_params=_params("parallel", "arbitrary"),
        name="mem_attn",
    )(x, g, wq, kv, wo)


def _final_norm_kernel(x_ref, g_ref, o_ref):
    o_ref[...] = _rms(x_ref[...], g_ref[...])


def _final_norm(x, g):
    n, d = x.shape
    tm = min(TM_TOKENS, n)
    return pl.pallas_call(
        _final_norm_kernel,
        grid=(n // tm,),
        in_specs=[pl.BlockSpec((tm, d), lambda i: (i, 0)), _const_spec((1, d))],
        out_specs=pl.BlockSpec((tm, d), lambda i: (i, 0)),
        out_shape=jax.ShapeDtypeStruct((n, d), F32),
        compiler_params=_params("parallel"),
        name="final_norm",
    )(x, g)


def _t5_bucket(rel):
    nb = NUM_BUCKETS // 2
    max_exact = nb // 2
    base = jnp.where(rel > 0, nb, 0)
    n = jnp.abs(rel)
    n_f = jnp.maximum(n, 1).astype(jnp.float32)
    large = max_exact + (jnp.log(n_f / max_exact) / math.log(MAX_DISTANCE / max_exact)
                         * (nb - max_exact)).astype(jnp.int32)
    large = jnp.minimum(large, nb - 1)
    return base + jnp.where(n < max_exact, n, large)


def _near_bias(t5_bias):
    t = jnp.arange(TQ_DSA)[:, None]
    s = jnp.arange(2 * TQ_DSA)[None, :] - TQ_DSA
    tbl = t5_bias[_t5_bucket(s - t)] - t5_bias[NUM_BUCKETS // 2 - 1]
    return jnp.transpose(tbl, (2, 1, 0)).astype(F32)


def _block_diag(w):
    nb, a, b = w.shape
    eye = jnp.eye(nb, dtype=w.dtype)
    return (eye[:, None, :, None] * w[:, :, None, :]).reshape(nb * a, nb * b)


def _pad_cols(w, width):
    return jnp.pad(w, ((0, 0), (0, width - w.shape[1])))


def kernel(x, mem, ffn1_norm, ffn1_w_gu, ffn1_w_down, mix_norm, w_in, conv_w, conv_b, rg_wa, rg_ba, rg_wx, rg_bx, rg_lambda, kv_norm, w_uk, w_uv, forget_bias, w_branch, w_out, xattn_norm, mem_norm, w_mq, w_mkv, w_mo, ffn2_norm, ffn2_w_gu, ffn2_w_down, t5_bias, final_norm):
    batch, seq, d = x.shape
    depth = w_in.shape[0]
    d_ff = ffn1_w_down.shape[1]
    dr = conv_w.shape[2]
    heads = w_uk.shape[1]
    d_lat = w_uk.shape[3]
    n = batch * seq
    assert seq % TK_DSA == 0 and d_lat == LANES and 2 * IDX_DIM == LANES and 2 * HEAD_DIM == LANES

    widths = (dr, dr, heads * HEAD_DIM, d_lat, IDX_HEADS * IDX_DIM, IDX_DIM, IDX_HEADS,
              heads * HEAD_DIM, heads * HEAD_DIM, heads * HEAD_DIM, heads, d, d, d)
    offs = np.concatenate([[0], np.cumsum(widths)])
    col = lambda w, j: w[:, int(offs[j]):int(offs[j + 1])]

    bias_near = _near_bias(t5_bias)
    cum_sel = _cum_selector(heads * HEAD_DIM // LANES, IDX_HEADS)
    row = lambda v: v.reshape(1, -1).astype(F32)

    xf = x.reshape(n, d)
    memf = mem.reshape(batch * mem.shape[1], d)
    for l in range(depth):
        wl = w_in[l]
        small_w = _pad_cols(jnp.concatenate([col(wl, 6), col(wl, 10)], axis=1), LANES)
        w_all = jnp.concatenate(
            [col(wl, 0), col(wl, 1), col(wl, 2), col(wl, 3), col(wl, 4), col(wl, 5), col(wl, 5),
             col(wl, 7), col(wl, 8), col(wl, 9), col(wl, 11), col(wl, 12), col(wl, 13)],
            axis=1).astype(BF16)
        w_all = jnp.concatenate([w_all, small_w.astype(BF16)], axis=1)
        out_widths = (2 * dr, heads * HEAD_DIM, d_lat, IDX_HEADS * IDX_DIM, 2 * IDX_DIM,
                      heads * HEAD_DIM, heads * HEAD_DIM, heads * HEAD_DIM, 3 * d, LANES)
        out_dtypes = (BF16,) * 9 + (F32,)

        xf = _ffn(xf, row(ffn1_norm[l]), ffn1_w_gu[l][:, :d_ff].astype(BF16),
                  ffn1_w_gu[l][:, d_ff:].astype(BF16), ffn1_w_down[l].astype(BF16))

        rg, qb, ckv, qi, ki2, qc, kc, vc, gates, small = _norm_proj(
            xf, row(mix_norm[l]), w_all, out_widths, out_dtypes)

        wax = jnp.concatenate([_block_diag(rg_wa[l]), _block_diag(rg_wx[l])], axis=1).astype(BF16)
        bax = jnp.concatenate([rg_ba[l], rg_bx[l]]).reshape(1, -1)
        o_a = _rglru(rg, batch, conv_w[l], row(conv_b[l]), wax, bax, row(rg_lambda[l]))

        fbias = _pad_cols(jnp.concatenate([jnp.zeros((IDX_HEADS,), F32), forget_bias[l]])[None], LANES)
        o_c = _fox(qc, kc, vc, _forget_cumsum(small, batch, fbias), cum_sel, batch)

        o_b = _dsa(qb, qi, small, ckv, ki2, row(kv_norm[l]), _block_diag(w_uk[l]).T.astype(BF16),
                   _block_diag(w_uv[l]).T.astype(BF16), bias_near, batch)

        xf = _merge(xf, o_a, o_b, o_c, gates, w_branch[l].astype(BF16), w_out[l].astype(BF16))

        (kv,) = _norm_proj(memf, row(mem_norm[l]), w_mkv[l].astype(BF16),
                           (w_mkv.shape[2],), (BF16,))
        xf = _mem_attn(xf, row(xattn_norm[l]), w_mq[l].astype(BF16), kv, w_mo[l].astype(BF16), batch)

        xf = _ffn(xf, row(ffn2_norm[l]), ffn2_w_gu[l][:, :d_ff].astype(BF16),
                  ffn2_w_gu[l][:, d_ff:].astype(BF16), ffn2_w_down[l].astype(BF16))
    return _final_norm(xf, row(final_norm)).reshape(batch, seq, d)
```

```python
import functools
import math

import numpy as np
import jax
import jax.numpy as jnp
from jax import lax
from jax.experimental import pallas as pl
from jax.experimental.pallas import tpu as pltpu

F32 = jnp.float32
BF16 = jnp.bfloat16
I32 = jnp.int32
I16 = jnp.int16

LANES = 128
SUBLANES = 8
VMEM_LIMIT_BYTES = 56 * 1024 * 1024

EPS = 1e-6
NEG = -1e30
LOG2E = math.log2(math.e)
CHUNK = 64
CHUNK_SHIFT = CHUNK.bit_length() - 1
HEAD_DIM = 64
RG_C = 8.0
CONV_W = 4
IDX_HEADS = 8
IDX_DIM = 64
TOPK_MAX = 256
MEM_HEADS = 4
MEM_HEAD_DIM = 128
NUM_BUCKETS = 32
MAX_DISTANCE = 128

TM_TOKENS = 512
FFN_CHUNK = 256
PROJ_CHUNK = 1024
TS_SCAN = 256
TQ_FOX = 512
CUM_PARTS = 3
TQ_DSA = 128
TK_DSA = 512
DSA_HEAD_GROUP = 2
DSA_PAD = TK_DSA - TQ_DSA


def _key_of_float(v):
    b = int(np.float32(v).view(np.int32))
    return b ^ ((b >> 31) & 0x7FFFFFFF)


KEY_NEG = _key_of_float(NEG)


def _params(*semantics):
    return pltpu.CompilerParams(dimension_semantics=semantics, vmem_limit_bytes=VMEM_LIMIT_BYTES)


def _const_spec(shape):
    nd = len(shape)
    return pl.BlockSpec(shape, lambda *_: (0,) * nd, pipeline_mode=pl.Buffered(1))


def _rms(x32, g):
    ms = jnp.mean(x32 * x32, axis=-1, keepdims=True)
    return x32 * lax.rsqrt(ms + EPS) * g


def _dot(a, b):
    return jnp.dot(a, b, preferred_element_type=F32)


def _dot_nt(a, b):
    return lax.dot_general(a, b, (((1,), (1,)), ((), ())), preferred_element_type=F32)


def _tree_sum(xs):
    while len(xs) > 1:
        xs = [xs[j] + xs[j + 1] for j in range(0, len(xs) - 1, 2)] + ([xs[-1]] if len(xs) % 2 else [])
    return xs[0]


def _dot_tn(a, b):
    return lax.dot_general(a, b, (((0,), (0,)), ((), ())), preferred_element_type=F32)


def _ffn_kernel(x_ref, g_ref, wg_ref, wu_ref, wd_ref, o_ref, a_ref):
    x = x_ref[...]
    h = _rms(x, g_ref[...]).astype(BF16)
    d_ff = wg_ref.shape[1]
    for f0 in range(0, d_ff, FFN_CHUNK):
        g = _dot(h, wg_ref[:, f0:f0 + FFN_CHUNK])
        u = _dot(h, wu_ref[:, f0:f0 + FFN_CHUNK])
        a_ref[:, f0:f0 + FFN_CHUNK] = (g * jax.nn.sigmoid(g) * u).astype(BF16)
    o_ref[...] = x + 0.5 * _dot(a_ref[...], wd_ref[...])


def _ffn(x, g, wg, wu, wd):
    n, d = x.shape
    d_ff = wg.shape[1]
    tm = min(TM_TOKENS, n)
    return pl.pallas_call(
        _ffn_kernel,
        grid=(n // tm,),
        in_specs=[pl.BlockSpec((tm, d), lambda i: (i, 0)),
                  _const_spec((1, d)), _const_spec((d, d_ff)), _const_spec((d, d_ff)),
                  _const_spec((d_ff, d))],
        out_specs=pl.BlockSpec((tm, d), lambda i: (i, 0)),
        out_shape=jax.ShapeDtypeStruct((n, d), F32),
        scratch_shapes=[pltpu.VMEM((tm, d_ff), BF16)],
        compiler_params=_params("parallel"),
        name="ffn",
    )(x, g, wg, wu, wd)


def _norm_proj_kernel(x_ref, g_ref, w_ref, *o_refs):
    h = _rms(x_ref[...].astype(F32), g_ref[...]).astype(BF16)
    c0 = 0
    for o_ref in o_refs:
        width = o_ref.shape[1]
        for s0 in range(0, width, PROJ_CHUNK):
            s1 = min(s0 + PROJ_CHUNK, width)
            o_ref[:, s0:s1] = _dot(h, w_ref[:, c0 + s0:c0 + s1]).astype(o_ref.dtype)
        c0 += width


def _norm_proj(x, g, w, widths, dtypes):
    n, d = x.shape
    tm = min(TM_TOKENS, n)
    assert sum(widths) == w.shape[1]
    return pl.pallas_call(
        _norm_proj_kernel,
        grid=(n // tm,),
        in_specs=[pl.BlockSpec((tm, d), lambda i: (i, 0)), _const_spec((1, d)),
                  _const_spec(w.shape)],
        out_specs=[pl.BlockSpec((tm, wd), lambda i: (i, 0)) for wd in widths],
        out_shape=[jax.ShapeDtypeStruct((n, wd), dt) for wd, dt in zip(widths, dtypes)],
        compiler_params=_params("parallel"),
        name="norm_proj",
    )(x, g, w)


def _gelu_tanh(x):
    return 0.5 * x * (1.0 + jnp.tanh(math.sqrt(2.0 / math.pi) * (x + 0.044715 * (x * x * x))))


def _softplus(x):
    return jnp.maximum(x, 0.0) + jnp.log1p(jnp.exp(-jnp.abs(x)))


def _rglru_kernel(rg_ref, cw_ref, cb_ref, wax_ref, bax_ref, lam_ref, o_ref, xbuf, hc):
    ts = rg_ref.shape[0]
    dr = o_ref.shape[1]

    @pl.when(pl.program_id(1) == 0)
    def _():
        xbuf[0:8, :] = jnp.zeros((8, dr), F32)
        hc[...] = jnp.zeros_like(hc)

    xr = rg_ref[:, :dr].astype(F32)
    gr = rg_ref[:, dr:].astype(F32)
    xbuf[8:, :] = xr
    xc = cb_ref[...] + cw_ref[3:4, :] * xr
    for j in range(CONV_W - 1):
        xc = xc + cw_ref[j:j + 1, :] * xbuf[pl.ds(5 + j, ts), :]
    xbuf[0:8, :] = xbuf[ts:ts + 8, :]

    ax = _dot(xc.astype(BF16), wax_ref[...]) + bax_ref[...]
    r = jax.nn.sigmoid(ax[:, :dr])
    gi = jax.nn.sigmoid(ax[:, dr:])
    log_a = (-RG_C) * r * _softplus(-lam_ref[...])
    a = jnp.exp(log_a)
    th = jnp.tanh(log_a)
    u = jnp.sqrt(-2.0 * th / (1.0 - th)) * (gi * xc)

    row = lax.broadcasted_iota(I32, (ts, dr), 0)
    d = 1
    while d < ts:
        a_sh = pltpu.roll(a, d, 0)
        u_sh = pltpu.roll(u, d, 0)
        keep = row >= d
        u = jnp.where(keep, a * u_sh + u, u)
        a = jnp.where(keep, a * a_sh, a)
        d *= 2
    h = u + a * hc[0:1, :]
    hc[...] = jnp.broadcast_to(h[ts - 1:ts, :], hc.shape)
    o_ref[...] = (h * _gelu_tanh(gr)).astype(o_ref.dtype)


def _rglru(rg, batch, conv_w, conv_b, wax, bax, lam):
    n, two_dr = rg.shape
    dr = two_dr // 2
    seq = n // batch
    ts = min(TS_SCAN, seq)
    ns = seq // ts
    return pl.pallas_call(
        _rglru_kernel,
        grid=(batch, ns),
        in_specs=[pl.BlockSpec((ts, two_dr), lambda b, s: (b * ns + s, 0)),
                  _const_spec(conv_w.shape), _const_spec(conv_b.shape), _const_spec(wax.shape),
                  _const_spec(bax.shape), _const_spec(lam.shape)],
        out_specs=pl.BlockSpec((ts, dr), lambda b, s: (b * ns + s, 0)),
        out_shape=jax.ShapeDtypeStruct((n, dr), BF16),
        scratch_shapes=[pltpu.VMEM((ts + 8, dr), F32), pltpu.VMEM((8, dr), F32)],
        compiler_params=_params("parallel", "arbitrary"),
        name="rglru",
    )(rg, conv_w, conv_b, wax, bax, lam)


def _forget_cumsum_kernel(f_ref, b_ref, o_ref):
    z = f_ref[...] + b_ref[...]
    x = -_softplus(-z)
    seq = x.shape[0]
    row = lax.broadcasted_iota(I32, x.shape, 0)
    d = 1
    while d < seq:
        x = jnp.where(row >= d, x + pltpu.roll(x, d, 0), x)
        d *= 2
    x = x * LOG2E
    for j in range(CUM_PARTS):
        part = x.astype(BF16)
        o_ref[j] = part
        x = x - part.astype(F32)


def _forget_cumsum(small, batch, fbias):
    n, w = small.shape
    seq = n // batch
    return pl.pallas_call(
        _forget_cumsum_kernel,
        grid=(batch,),
        in_specs=[pl.BlockSpec((seq, w), lambda b: (b, 0)), _const_spec((1, w))],
        out_specs=pl.BlockSpec((CUM_PARTS, seq, w), lambda b: (0, b, 0)),
        out_shape=jax.ShapeDtypeStruct((CUM_PARTS, n, w), BF16),
        compiler_params=_params("parallel"),
        name="forget_cumsum",
    )(small, fbias)


def _fox_kernel(q_ref, k_ref, v_ref, cp_ref, sel_ref, o_ref, kaug, qaug, s0, m_s, l_s, acc_s):
    tq = q_ref.shape[0]
    n_pairs = q_ref.shape[1] // LANES
    i = pl.program_id(1)
    lane = lax.broadcasted_iota(I32, (1, LANES), 1)

    @pl.when(i == 0)
    def _():
        for p in range(n_pairs):
            kaug[:, 2 * p * LANES:(2 * p + 1) * LANES] = k_ref[:, p * LANES:(p + 1) * LANES]
            extra = _dot(cp_ref[0], sel_ref[p, 0])
            for j in range(1, CUM_PARTS):
                extra = extra + _dot(cp_ref[j], sel_ref[p, j])
            kaug[:, (2 * p + 1) * LANES:(2 * p + 2) * LANES] = extra.astype(BF16)

    for p in range(n_pairs):
        qp = q_ref[:, p * LANES:(p + 1) * LANES]
        for e in range(2):
            rows = slice(e * tq, (e + 1) * tq)
            in_head = (lane >= e * HEAD_DIM) & (lane < (e + 1) * HEAD_DIM)
            qaug[p, rows, 0:LANES] = jnp.where(in_head, qp, jnp.zeros_like(qp))
            ones = jnp.where((lane >= CUM_PARTS * e) & (lane < CUM_PARTS * (e + 1)), 1.0, 0.0)
            qaug[p, rows, LANES:2 * LANES] = jnp.broadcast_to(ones, (tq, LANES)).astype(BF16)

    m_s[...] = jnp.full(m_s.shape, NEG, F32)
    l_s[...] = jnp.zeros(l_s.shape, F32)
    acc_s[...] = jnp.zeros(acc_s.shape, F32)
    key_row = lax.broadcasted_iota(I32, (tq, 2 * tq), 0)
    query = lax.broadcasted_iota(I32, (tq, 2 * tq), 1) & (tq - 1)
    causal = key_row <= query

    def logits(j, p):
        rows = pl.ds(pl.multiple_of(j * tq, tq), tq)
        return _dot_nt(kaug[rows, 2 * p * LANES:(2 * p + 2) * LANES], qaug[p])

    s0[...] = logits(0, 0)

    def step(j, masked):
        rows = pl.ds(pl.multiple_of(j * tq, tq), tq)
        s_next = s0[...]
        for p in range(n_pairs):
            s = s_next
            if p + 1 < n_pairs:
                s_next = logits(j, p + 1)
            else:
                s0[...] = logits(jnp.minimum(j + 1, i), 0)
            if masked:
                s = jnp.where(causal, s, NEG)
            m_old = m_s[p]
            m_new = jnp.maximum(m_old, jnp.max(s, axis=0, keepdims=True))
            alpha = jnp.exp2(m_old - m_new)
            pr = jnp.exp2(s - m_new)
            l_s[p] = alpha * l_s[p] + jnp.sum(pr, axis=0, keepdims=True)
            m_s[p] = m_new
            acc_s[p] = alpha * acc_s[p] + _dot_tn(v_ref[rows, p * LANES:(p + 1) * LANES],
                                                  pr.astype(BF16))

    def body(j, _):
        step(j, masked=False)
        return 0

    lax.fori_loop(0, i, body, 0)
    step(i, masked=True)

    feat = lax.broadcasted_iota(I32, (LANES, tq), 0)
    for p in range(n_pairs):
        o_t = acc_s[p] / l_s[p]
        o_t = jnp.where(feat < HEAD_DIM, o_t[:, :tq], o_t[:, tq:])
        o_ref[:, p * LANES:(p + 1) * LANES] = o_t.T.astype(o_ref.dtype)


def _fox(q, k, v, cum_parts, sel, batch):
    n, w = q.shape
    seq = n // batch
    tq = min(TQ_FOX, seq)
    nq = seq // tq
    n_pairs = w // LANES
    return pl.pallas_call(
        _fox_kernel,
        grid=(batch, nq),
        in_specs=[pl.BlockSpec((tq, w), lambda b, i: (b * nq + i, 0)),
                  pl.BlockSpec((seq, w), lambda b, i: (b, 0)),
                  pl.BlockSpec((seq, w), lambda b, i: (b, 0)),
                  pl.BlockSpec((CUM_PARTS, seq, LANES), lambda b, i: (0, b, 0)),
                  _const_spec(sel.shape)],
        out_specs=pl.BlockSpec((tq, w), lambda b, i: (b * nq + i, 0)),
        out_shape=jax.ShapeDtypeStruct((n, w), BF16),
        scratch_shapes=[pltpu.VMEM((seq, 2 * w), BF16),
                        pltpu.VMEM((n_pairs, 2 * tq, 2 * LANES), BF16),
                        pltpu.VMEM((tq, 2 * tq), F32),
                        pltpu.VMEM((n_pairs, 1, 2 * tq), F32),
                        pltpu.VMEM((n_pairs, 1, 2 * tq), F32),
                        pltpu.VMEM((n_pairs, LANES, 2 * tq), F32)],
        compiler_params=_params("parallel", "arbitrary"),
        name="fox",
    )(q, k, v, cum_parts, sel)


def _cum_selector(n_pairs, first_lane):
    sel = np.zeros((n_pairs, CUM_PARTS, LANES, LANES), np.float32)
    for p in range(n_pairs):
        for j in range(CUM_PARTS):
            for e in range(2):
                sel[p, j, first_lane + 2 * p + e, CUM_PARTS * e + j] = -1.0
    return jnp.asarray(sel, BF16)


def _float_key(x):
    b = lax.bitcast_convert_type(x, I32)
    return b ^ ((b >> 31) & 0x7FFFFFFF)


def _dsa_kernel(qb_ref, qi_ref, sm_ref, ckv_ref, ki_ref, kvn_ref, wuk_ref, wuv_ref, bias_ref,
                o_ref, c_s, ki_s, keys, keys16, qim, d0, s0, m_s, l_s, acc_s, *, k_sel):
    tq = qb_ref.shape[0]
    seq = ckv_ref.shape[0]
    heads = bias_ref.shape[0]
    tk = TK_DSA
    i = pl.program_id(1)

    @pl.when(i == 0)
    def _():
        c_s[0:DSA_PAD, :] = jnp.zeros((DSA_PAD, LANES), BF16)
        ki_s[0:DSA_PAD, :] = jnp.zeros((DSA_PAD, LANES), BF16)
        c_s[DSA_PAD:, :] = _rms(ckv_ref[...].astype(F32), kvn_ref[...]).astype(BF16)
        ki_s[DSA_PAD:, :] = ki_ref[...]

    n_valid = (i + 1) * tq
    n_tiles = (n_valid + tk - 1) // tk

    def tile_rows(k):
        return pl.ds(pl.multiple_of(n_valid - (k + 1) * tk + DSA_PAD, tq), tk)

    lane = lax.broadcasted_iota(I32, (1, LANES), 1)
    for h in range(IDX_HEADS):
        qp = qi_ref[:, (h // 2) * LANES:(h // 2 + 1) * LANES]
        e = h % 2
        in_head = (lane >= e * IDX_DIM) & (lane < (e + 1) * IDX_DIM)
        qim[h * tq:(h + 1) * tq, :] = jnp.where(in_head, qp, jnp.zeros_like(qp))
    w_t = sm_ref[...].T[0:IDX_HEADS, :] * (IDX_HEADS ** -0.5 * IDX_DIM ** -0.5)
    q_chunk = (i * tq + lax.broadcasted_iota(I32, (tk, tq), 1)) >> CHUNK_SHIFT
    key_local = lax.broadcasted_iota(I32, (tk, tq), 0)

    def pair_dots(k, p):
        return _dot_nt(ki_s[tile_rows(k), :], qim[2 * p * tq:(2 * p + 2) * tq, :])

    d0[...] = pair_dots(0, 0)

    def score_tile(k, _):
        d_next = d0[...]
        sc = jnp.zeros((tk, tq), F32)
        for p in range(IDX_HEADS // 2):
            d = d_next
            if p + 1 < IDX_HEADS // 2:
                d_next = pair_dots(k, p + 1)
            else:
                d0[...] = pair_dots(jnp.minimum(k + 1, n_tiles - 1), 0)
            for e in range(2):
                sc = sc + jnp.maximum(d[:, e * tq:(e + 1) * tq], 0.0) * w_t[2 * p + e:2 * p + e + 1, :]
        pos = key_local + (n_valid - (k + 1) * tk)
        sc = jnp.where(pos >= 0, jnp.where((pos >> CHUNK_SHIFT) <= q_chunk, sc, NEG), NEG)
        key = _float_key(sc)
        keys[k] = key
        keys16[k] = (key >> 16).astype(I16)
        return 0

    lax.fori_loop(0, n_tiles, score_tile, 0)

    def count_rows(hit_fn):
        def body(k, cnt):
            return cnt + jnp.sum(hit_fn(k).reshape(tk // SUBLANES, SUBLANES, tq), axis=0)
        cnt = lax.fori_loop(0, n_tiles, body, jnp.zeros((SUBLANES, tq), I32))
        return jnp.sum(cnt, axis=0, keepdims=True)

    def count_ge(thr):
        return count_rows(lambda k: jnp.where(keys[k] >= thr, 1, 0))

    def count_ge16(thr16):
        rows16 = 2 * SUBLANES
        def body(k, cnt):
            ge = jnp.where(keys16[k] >= thr16, jnp.int16(1), jnp.int16(0))
            return cnt + _tree_sum([ge[c * rows16:(c + 1) * rows16, :] for c in range(tk // rows16)])
        cnt = lax.fori_loop(0, n_tiles, body, jnp.zeros((rows16, tq), I16))
        return jnp.sum(cnt.astype(I32), axis=0, keepdims=True)

    half = 2 ** 15

    def kth_largest16(k_need):
        def step(b, u):
            cand = u | lax.shift_left(jnp.int32(1), 15 - b)
            return jnp.where(count_ge16((cand - half).astype(I16)) >= k_need, cand, u)
        return lax.fori_loop(0, 16, step, jnp.zeros((1, tq), I32)) - half

    t_hi = kth_largest16(k_sel)
    n_above = jnp.where(t_hi >= half - 1, 0,
                        count_ge16((jnp.minimum(t_hi, half - 2) + 1).astype(I16)))

    def low_halves(k, _):
        key = keys[k]
        lo = (key & 0xFFFF) - half
        keys16[k] = jnp.where((key >> 16) == t_hi, lo, -half).astype(I16)
        return 0

    lax.fori_loop(0, n_tiles, low_halves, 0)
    t_lo = kth_largest16(k_sel - n_above)
    thr = jnp.maximum(lax.shift_left(t_hi, 16) + (t_lo + half), KEY_NEG + 1)

    excess = count_ge(thr) - k_sel

    @pl.when(jnp.max(excess) > 0)
    def _():
        def count_eq_below(cut):
            def hit(k):
                pos = key_local + (n_valid - (k + 1) * tk)
                return jnp.where(keys[k] == thr, jnp.where(pos < cut, 1, 0), 0)
            return count_rows(hit)

        n_bits = max(1, (2 * seq - 1).bit_length())
        keep = count_eq_below(jnp.full((1, tq), 2 ** n_bits, I32)) - jnp.maximum(excess, 0)

        def cut_step(b, cut):
            cand = cut | lax.shift_left(jnp.int32(1), n_bits - 1 - b)
            return jnp.where(count_eq_below(cand) <= keep, cand, cut)

        cut = lax.fori_loop(0, n_bits, cut_step, jnp.zeros((1, tq), I32))

        def drop(k, _):
            pos = key_local + (n_valid - (k + 1) * tk)
            kk = keys[k]
            keys[k] = jnp.where(kk == thr, jnp.where(pos >= cut, KEY_NEG, kk), kk)
            return 0

        lax.fori_loop(0, n_tiles, drop, 0)

    ql_t = (_dot_nt(wuk_ref[...], qb_ref[...]) * (HEAD_DIM ** -0.5 * LOG2E)).astype(BF16)
    n_groups = heads // DSA_HEAD_GROUP
    gw = DSA_HEAD_GROUP * tq
    ql_g = [jnp.concatenate([ql_t[h * LANES:(h + 1) * LANES, :]
                             for h in range(g * DSA_HEAD_GROUP, (g + 1) * DSA_HEAD_GROUP)], axis=1)
            for g in range(n_groups)]
    m_s[...] = jnp.full(m_s.shape, NEG, F32)
    l_s[...] = jnp.zeros(l_s.shape, F32)
    acc_s[...] = jnp.zeros(acc_s.shape, F32)

    def attend(k, near):
        ct = c_s[tile_rows(k), :]
        sel = keys[k] >= thr
        s_next = s0[...]
        for g in range(n_groups):
            s_g = s_next
            if g + 1 < n_groups:
                s_next = _dot(ct, ql_g[g + 1])
            else:
                s0[...] = _dot(c_s[tile_rows(jnp.minimum(k + 1, n_tiles - 1)), :], ql_g[0])
            parts = []
            for hh in range(DSA_HEAD_GROUP):
                s = s_g[:, hh * tq:(hh + 1) * tq]
                if near:
                    s = jnp.concatenate([s[:tk - 2 * tq], s[tk - 2 * tq:]
                                         + bias_ref[g * DSA_HEAD_GROUP + hh]], axis=0)
                parts.append(jnp.where(sel, s, NEG))
            s = jnp.concatenate(parts, axis=1)
            m_old = m_s[g]
            m_new = jnp.maximum(m_old, jnp.max(s, axis=0, keepdims=True))
            alpha = jnp.exp2(m_old - m_new)
            pr = jnp.exp2(s - m_new)
            l_s[g] = alpha * l_s[g] + jnp.sum(pr, axis=0, keepdims=True)
            m_s[g] = m_new
            acc_s[g] = alpha * acc_s[g] + _dot_tn(ct, pr.astype(BF16))

    s0[...] = _dot(c_s[tile_rows(0), :], ql_g[0])
    attend(0, near=True)

    def far(k, _):
        attend(k, near=False)
        return 0

    lax.fori_loop(1, n_tiles, far, 0)

    o_parts = []
    for g in range(n_groups):
        o_g = acc_s[g] / l_s[g]
        o_parts += [o_g[:, hh * tq:(hh + 1) * tq] for hh in range(DSA_HEAD_GROUP)]
    o_lat_t = jnp.concatenate(o_parts, axis=0).astype(BF16)
    o_ref[...] = _dot(wuv_ref[...], o_lat_t).T.astype(o_ref.dtype)


def _dsa(qb, qi, small, ckv, ki2, kvn, wuk_bd, wuv_bd, bias_near, batch):
    n, w = qb.shape
    seq = n // batch
    tq = TQ_DSA
    nq = seq // tq
    heads = bias_near.shape[0]
    n_groups = heads // DSA_HEAD_GROUP
    gw = DSA_HEAD_GROUP * tq
    k_sel = min(TOPK_MAX, seq // 4)
    blk = lambda width: pl.BlockSpec((tq, width), lambda b, i: (b * nq + i, 0))
    per_batch = pl.BlockSpec((seq, LANES), lambda b, i: (b, 0))
    return pl.pallas_call(
        functools.partial(_dsa_kernel, k_sel=k_sel),
        grid=(batch, nq),
        in_specs=[blk(w), blk(w), blk(LANES), per_batch, per_batch, _const_spec(kvn.shape),
                  _const_spec(wuk_bd.shape), _const_spec(wuv_bd.shape),
                  _const_spec(bias_near.shape)],
        out_specs=blk(w),
        out_shape=jax.ShapeDtypeStruct((n, w), BF16),
        scratch_shapes=[pltpu.VMEM((seq + DSA_PAD, LANES), BF16),
                        pltpu.VMEM((seq + DSA_PAD, LANES), BF16),
                        pltpu.VMEM((pl.cdiv(seq, TK_DSA), TK_DSA, tq), I32),
                        pltpu.VMEM((pl.cdiv(seq, TK_DSA), TK_DSA, tq), I16),
                        pltpu.VMEM((IDX_HEADS * tq, LANES), BF16),
                        pltpu.VMEM((TK_DSA, 2 * tq), F32),
                        pltpu.VMEM((TK_DSA, gw), F32),
                        pltpu.VMEM((n_groups, 1, gw), F32),
                        pltpu.VMEM((n_groups, 1, gw), F32),
                        pltpu.VMEM((n_groups, LANES, gw), F32)],
        compiler_params=_params("parallel", "arbitrary"),
        name="dsa",
    )(qb, qi, small, ckv, ki2, kvn, wuk_bd, wuv_bd, bias_near)


def _merge_kernel(x_ref, oa_ref, ob_ref, oc_ref, gate_ref, wbr_ref, wout_ref, o_ref):
    d = x_ref.shape[1]
    merged = jnp.zeros(x_ref.shape, F32)
    for j, br_ref in enumerate((oa_ref, ob_ref, oc_ref)):
        gate = jax.nn.sigmoid(gate_ref[:, j * d:(j + 1) * d].astype(F32))
        merged = merged + gate * _dot(br_ref[...], wbr_ref[j])
    o_ref[...] = x_ref[...] + _dot(merged.astype(BF16), wout_ref[...])


def _merge(x, oa, ob, oc, gates, wbr, wout):
    n, d = x.shape
    tm = min(TM_TOKENS, n)
    bw = oa.shape[1]
    blk = lambda width: pl.BlockSpec((tm, width), lambda i: (i, 0))
    return pl.pallas_call(
        _merge_kernel,
        grid=(n // tm,),
        in_specs=[blk(d), blk(bw), blk(bw), blk(bw), blk(3 * d), _const_spec(wbr.shape),
                  _const_spec(wout.shape)],
        out_specs=blk(d),
        out_shape=jax.ShapeDtypeStruct((n, d), F32),
        compiler_params=_params("parallel"),
        name="merge",
    )(x, oa, ob, oc, gates, wbr, wout)


def _mem_attn_kernel(x_ref, g_ref, wq_ref, kv_ref, wo_ref, o_ref):
    x = x_ref[...]
    hw = MEM_HEADS * MEM_HEAD_DIM
    q = _dot(_rms(x, g_ref[...]).astype(BF16), wq_ref[...]).astype(BF16)
    outs = []
    for h in range(MEM_HEADS):
        lanes = slice(h * MEM_HEAD_DIM, (h + 1) * MEM_HEAD_DIM)
        s = _dot_nt(q[:, lanes], kv_ref[:, lanes]) * (MEM_HEAD_DIM ** -0.5)
        pr = jnp.exp(s - jnp.max(s, axis=1, keepdims=True))
        pr = pr / jnp.sum(pr, axis=1, keepdims=True)
        outs.append(_dot(pr.astype(BF16), kv_ref[:, hw + h * MEM_HEAD_DIM:hw + (h + 1) * MEM_HEAD_DIM]))
    o = jnp.concatenate(outs, axis=1).astype(BF16)
    o_ref[...] = x + _dot(o, wo_ref[...])


def _mem_attn(x, g, wq, kv, wo, batch):
    n, d = x.shape
    seq = n // batch
    tm = min(TM_TOKENS, seq)
    ns = seq // tm
    mem_len = kv.shape[0] // batch
    return pl.pallas_call(
        _mem_attn_kernel,
        grid=(batch, ns),
        in_specs=[pl.BlockSpec((tm, d), lambda b, s: (b * ns + s, 0)), _const_spec((1, d)),
                  _const_spec(wq.shape),
                  pl.BlockSpec((mem_len, kv.shape[1]), lambda b, s: (b, 0)),
                  _const_spec(wo.shape)],
        out_specs=pl.BlockSpec((tm, d), lambda b, s: (b * ns + s, 0)),
        out_shape=jax.ShapeDtypeStruct((n, d), F32),
        compiler_params=_params("parallel", "arbitrary"),
        name="mem_attn",
    )(x, g, wq, kv, wo)


def _final_norm_kernel(x_ref, g_ref, o_ref):
    o_ref[...] = _rms(x_ref[...], g_ref[...])


def _final_norm(x, g):
    n, d = x.shape
    tm = min(TM_TOKENS, n)
    return pl.pallas_call(
        _final_norm_kernel,
        grid=(n // tm,),
        in_specs=[pl.BlockSpec((tm, d), lambda i: (i, 0)), _const_spec((1, d))],
        out_specs=pl.BlockSpec((tm, d), lambda i: (i, 0)),
        out_shape=jax.ShapeDtypeStruct((n, d), F32),
        compiler_params=_params("parallel"),
        name="final_norm",
    )(x, g)


def _t5_bucket(rel):
    nb = NUM_BUCKETS // 2
    max_exact = nb // 2
    base = jnp.where(rel > 0, nb, 0)
    n = jnp.abs(rel)
    n_f = jnp.maximum(n, 1).astype(jnp.float32)
    large = max_exact + (jnp.log(n_f / max_exact) / math.log(MAX_DISTANCE / max_exact)
                         * (nb - max_exact)).astype(jnp.int32)
    large = jnp.minimum(large, nb - 1)
    return base + jnp.where(n < max_exact, n, large)


def _near_bias(t5_bias):
    t = jnp.arange(TQ_DSA)[:, None]
    s = jnp.arange(2 * TQ_DSA)[None, :] - TQ_DSA
    tbl = t5_bias[_t5_bucket(s - t)] - t5_bias[NUM_BUCKETS // 2 - 1]
    return jnp.transpose(tbl * LOG2E, (2, 1, 0)).astype(F32)


def _block_diag(w):
    nb, a, b = w.shape
    eye = jnp.eye(nb, dtype=w.dtype)
    return (eye[:, None, :, None] * w[:, :, None, :]).reshape(nb * a, nb * b)


def _pad_cols(w, width):
    return jnp.pad(w, ((0, 0), (0, width - w.shape[1])))


def kernel(x, mem, ffn1_norm, ffn1_w_gu, ffn1_w_down, mix_norm, w_in, conv_w, conv_b, rg_wa, rg_ba, rg_wx, rg_bx, rg_lambda, kv_norm, w_uk, w_uv, forget_bias, w_branch, w_out, xattn_norm, mem_norm, w_mq, w_mkv, w_mo, ffn2_norm, ffn2_w_gu, ffn2_w_down, t5_bias, final_norm):
    batch, seq, d = x.shape
    depth = w_in.shape[0]
    d_ff = ffn1_w_down.shape[1]
    dr = conv_w.shape[2]
    heads = w_uk.shape[1]
    d_lat = w_uk.shape[3]
    n = batch * seq
    assert seq % TK_DSA == 0 and d_lat == LANES and 2 * IDX_DIM == LANES and 2 * HEAD_DIM == LANES

    widths = (dr, dr, heads * HEAD_DIM, d_lat, IDX_HEADS * IDX_DIM, IDX_DIM, IDX_HEADS,
              heads * HEAD_DIM, heads * HEAD_DIM, heads * HEAD_DIM, heads, d, d, d)
    offs = np.concatenate([[0], np.cumsum(widths)])
    col = lambda w, j: w[:, int(offs[j]):int(offs[j + 1])]

    bias_near = _near_bias(t5_bias)
    cum_sel = _cum_selector(heads * HEAD_DIM // LANES, IDX_HEADS)
    row = lambda v: v.reshape(1, -1).astype(F32)

    xf = x.reshape(n, d)
    memf = mem.reshape(batch * mem.shape[1], d)
    for l in range(depth):
        wl = w_in[l]
        small_w = _pad_cols(jnp.concatenate([col(wl, 6), col(wl, 10)], axis=1), LANES)
        w_all = jnp.concatenate(
            [col(wl, 0), col(wl, 1), col(wl, 2), col(wl, 3), col(wl, 4), col(wl, 5), col(wl, 5),
             col(wl, 7) * (HEAD_DIM ** -0.5 * LOG2E),
             col(wl, 8), col(wl, 9), col(wl, 11), col(wl, 12), col(wl, 13)],
            axis=1).astype(BF16)
        w_all = jnp.concatenate([w_all, small_w.astype(BF16)], axis=1)
        out_widths = (2 * dr, heads * HEAD_DIM, d_lat, IDX_HEADS * IDX_DIM, 2 * IDX_DIM,
                      heads * HEAD_DIM, heads * HEAD_DIM, heads * HEAD_DIM, 3 * d, LANES)
        out_dtypes = (BF16,) * 9 + (F32,)

        xf = _ffn(xf, row(ffn1_norm[l]), ffn1_w_gu[l][:, :d_ff].astype(BF16),
                  ffn1_w_gu[l][:, d_ff:].astype(BF16), ffn1_w_down[l].astype(BF16))

        rg, qb, ckv, qi, ki2, qc, kc, vc, gates, small = _norm_proj(
            xf, row(mix_norm[l]), w_all, out_widths, out_dtypes)

        wax = jnp.concatenate([_block_diag(rg_wa[l]), _block_diag(rg_wx[l])], axis=1).astype(BF16)
        bax = jnp.concatenate([rg_ba[l], rg_bx[l]]).reshape(1, -1)
        o_a = _rglru(rg, batch, conv_w[l], row(conv_b[l]), wax, bax, row(rg_lambda[l]))

        fbias = _pad_cols(jnp.concatenate([jnp.zeros((IDX_HEADS,), F32), forget_bias[l]])[None], LANES)
        o_c = _fox(qc, kc, vc, _forget_cumsum(small, batch, fbias), cum_sel, batch)

        o_b = _dsa(qb, qi, small, ckv, ki2, row(kv_norm[l]), _block_diag(w_uk[l]).T.astype(BF16),
                   _block_diag(w_uv[l]).T.astype(BF16), bias_near, batch)

        xf = _merge(xf, o_a, o_b, o_c, gates, w_branch[l].astype(BF16), w_out[l].astype(BF16))

        (kv,) = _norm_proj(memf, row(mem_norm[l]), w_mkv[l].astype(BF16),
                           (w_mkv.shape[2],), (BF16,))
        xf = _mem_attn(xf, row(xattn_norm[l]), w_mq[l].astype(BF16), kv, w_mo[l].astype(BF16), batch)

        xf = _ffn(xf, row(ffn2_norm[l]), ffn2_w_gu[l][:, :d_ff].astype(BF16),
                  ffn2_w_gu[l][:, d_ff:].astype(BF16), ffn2_w_down[l].astype(BF16))
    return _final_norm(xf, row(final_norm)).reshape(batch, seq, d)
```

```python
import functools
import math

import numpy as np
import jax
import jax.numpy as jnp
from jax import lax
from jax.experimental import pallas as pl
from jax.experimental.pallas import tpu as pltpu

F32 = jnp.float32
BF16 = jnp.bfloat16
I32 = jnp.int32

LANES = 128
SUBLANES = 8
VMEM_LIMIT_BYTES = 56 * 1024 * 1024

EPS = 1e-6
NEG = -1e30
LOG2E = math.log2(math.e)
CHUNK = 64
CHUNK_SHIFT = CHUNK.bit_length() - 1
HEAD_DIM = 64
RG_C = 8.0
CONV_W = 4
IDX_HEADS = 8
IDX_DIM = 64
TOPK_MAX = 256
MEM_HEADS = 4
MEM_HEAD_DIM = 128
NUM_BUCKETS = 32
MAX_DISTANCE = 128

TM_TOKENS = 512
FFN_CHUNK = 256
PROJ_CHUNK = 1024
TS_SCAN = 256
TQ_FOX = 512
CUM_PARTS = 3
TQ_DSA = 256
DSA_NEAR_KEYS = TQ_DSA + MAX_DISTANCE
TK_DSA = 512
DSA_HEAD_GROUP = 2
DSA_PAD = TK_DSA - TQ_DSA


def _key_of_float(v):
    b = int(np.float32(v).view(np.int32))
    return b ^ ((b >> 31) & 0x7FFFFFFF)


KEY_NEG = _key_of_float(NEG)


def _params(*semantics):
    return pltpu.CompilerParams(dimension_semantics=semantics, vmem_limit_bytes=VMEM_LIMIT_BYTES)


def _const_spec(shape):
    nd = len(shape)
    return pl.BlockSpec(shape, lambda *_: (0,) * nd, pipeline_mode=pl.Buffered(1))


def _rms(x32, g):
    ms = jnp.mean(x32 * x32, axis=-1, keepdims=True)
    return x32 * lax.rsqrt(ms + EPS) * g


def _dot(a, b):
    return jnp.dot(a, b, preferred_element_type=F32)


def _dot_nt(a, b):
    return lax.dot_general(a, b, (((1,), (1,)), ((), ())), preferred_element_type=F32)


def _tree_sum(xs):
    while len(xs) > 1:
        xs = [xs[j] + xs[j + 1] for j in range(0, len(xs) - 1, 2)] + ([xs[-1]] if len(xs) % 2 else [])
    return xs[0]


def _dot_tn(a, b):
    return lax.dot_general(a, b, (((0,), (0,)), ((), ())), preferred_element_type=F32)


def _ffn_kernel(x_ref, g_ref, wg_ref, wu_ref, wd_ref, o_ref, a_ref):
    x = x_ref[...]
    h = _rms(x, g_ref[...]).astype(BF16)
    d_ff = wg_ref.shape[1]
    for f0 in range(0, d_ff, FFN_CHUNK):
        g = _dot(h, wg_ref[:, f0:f0 + FFN_CHUNK])
        u = _dot(h, wu_ref[:, f0:f0 + FFN_CHUNK])
        a_ref[:, f0:f0 + FFN_CHUNK] = (g * jax.nn.sigmoid(g) * u).astype(BF16)
    o_ref[...] = x + 0.5 * _dot(a_ref[...], wd_ref[...])


def _ffn(x, g, wg, wu, wd):
    n, d = x.shape
    d_ff = wg.shape[1]
    tm = min(TM_TOKENS, n)
    return pl.pallas_call(
        _ffn_kernel,
        grid=(n // tm,),
        in_specs=[pl.BlockSpec((tm, d), lambda i: (i, 0)),
                  _const_spec((1, d)), _const_spec((d, d_ff)), _const_spec((d, d_ff)),
                  _const_spec((d_ff, d))],
        out_specs=pl.BlockSpec((tm, d), lambda i: (i, 0)),
        out_shape=jax.ShapeDtypeStruct((n, d), F32),
        scratch_shapes=[pltpu.VMEM((tm, d_ff), BF16)],
        compiler_params=_params("parallel"),
        name="ffn",
    )(x, g, wg, wu, wd)


def _norm_proj_kernel(x_ref, g_ref, w_ref, *o_refs):
    h = _rms(x_ref[...].astype(F32), g_ref[...]).astype(BF16)
    c0 = 0
    for o_ref in o_refs:
        width = o_ref.shape[1]
        for s0 in range(0, width, PROJ_CHUNK):
            s1 = min(s0 + PROJ_CHUNK, width)
            o_ref[:, s0:s1] = _dot(h, w_ref[:, c0 + s0:c0 + s1]).astype(o_ref.dtype)
        c0 += width


def _norm_proj(x, g, w, widths, dtypes):
    n, d = x.shape
    tm = min(TM_TOKENS, n)
    assert sum(widths) == w.shape[1]
    return pl.pallas_call(
        _norm_proj_kernel,
        grid=(n // tm,),
        in_specs=[pl.BlockSpec((tm, d), lambda i: (i, 0)), _const_spec((1, d)),
                  _const_spec(w.shape)],
        out_specs=[pl.BlockSpec((tm, wd), lambda i: (i, 0)) for wd in widths],
        out_shape=[jax.ShapeDtypeStruct((n, wd), dt) for wd, dt in zip(widths, dtypes)],
        compiler_params=_params("parallel"),
        name="norm_proj",
    )(x, g, w)


def _gelu_tanh(x):
    return 0.5 * x * (1.0 + jnp.tanh(math.sqrt(2.0 / math.pi) * (x + 0.044715 * (x * x * x))))


def _softplus(x):
    return jnp.maximum(x, 0.0) + jnp.log1p(jnp.exp(-jnp.abs(x)))


def _rglru_kernel(rg_ref, cw_ref, cb_ref, wax_ref, bax_ref, lam_ref, o_ref, xbuf, hc):
    ts = rg_ref.shape[0]
    dr = o_ref.shape[1]

    @pl.when(pl.program_id(1) == 0)
    def _():
        xbuf[0:8, :] = jnp.zeros((8, dr), F32)
        hc[...] = jnp.zeros_like(hc)

    xr = rg_ref[:, :dr].astype(F32)
    gr = rg_ref[:, dr:].astype(F32)
    xbuf[8:, :] = xr
    xc = cb_ref[...] + cw_ref[3:4, :] * xr
    for j in range(CONV_W - 1):
        xc = xc + cw_ref[j:j + 1, :] * xbuf[pl.ds(5 + j, ts), :]
    xbuf[0:8, :] = xbuf[ts:ts + 8, :]

    ax = _dot(xc.astype(BF16), wax_ref[...]) + bax_ref[...]
    r = jax.nn.sigmoid(ax[:, :dr])
    gi = jax.nn.sigmoid(ax[:, dr:])
    log_a = (-RG_C) * r * _softplus(-lam_ref[...])
    a = jnp.exp(log_a)
    th = jnp.tanh(log_a)
    u = jnp.sqrt(-2.0 * th / (1.0 - th)) * (gi * xc)

    row = lax.broadcasted_iota(I32, (ts, dr), 0)
    d = 1
    while d < ts:
        a_sh = pltpu.roll(a, d, 0)
        u_sh = pltpu.roll(u, d, 0)
        keep = row >= d
        u = jnp.where(keep, a * u_sh + u, u)
        a = jnp.where(keep, a * a_sh, a)
        d *= 2
    h = u + a * hc[0:1, :]
    hc[...] = jnp.broadcast_to(h[ts - 1:ts, :], hc.shape)
    o_ref[...] = (h * _gelu_tanh(gr)).astype(o_ref.dtype)


def _rglru(rg, batch, conv_w, conv_b, wax, bax, lam):
    n, two_dr = rg.shape
    dr = two_dr // 2
    seq = n // batch
    ts = min(TS_SCAN, seq)
    ns = seq // ts
    return pl.pallas_call(
        _rglru_kernel,
        grid=(batch, ns),
        in_specs=[pl.BlockSpec((ts, two_dr), lambda b, s: (b * ns + s, 0)),
                  _const_spec(conv_w.shape), _const_spec(conv_b.shape), _const_spec(wax.shape),
                  _const_spec(bax.shape), _const_spec(lam.shape)],
        out_specs=pl.BlockSpec((ts, dr), lambda b, s: (b * ns + s, 0)),
        out_shape=jax.ShapeDtypeStruct((n, dr), BF16),
        scratch_shapes=[pltpu.VMEM((ts + 8, dr), F32), pltpu.VMEM((8, dr), F32)],
        compiler_params=_params("parallel", "arbitrary"),
        name="rglru",
    )(rg, conv_w, conv_b, wax, bax, lam)


def _forget_cumsum_kernel(f_ref, b_ref, o_ref):
    z = f_ref[...] + b_ref[...]
    x = -_softplus(-z)
    seq = x.shape[0]
    row = lax.broadcasted_iota(I32, x.shape, 0)
    d = 1
    while d < seq:
        x = jnp.where(row >= d, x + pltpu.roll(x, d, 0), x)
        d *= 2
    x = x * LOG2E
    for j in range(CUM_PARTS):
        part = x.astype(BF16)
        o_ref[j] = part
        x = x - part.astype(F32)


def _forget_cumsum(small, batch, fbias):
    n, w = small.shape
    seq = n // batch
    return pl.pallas_call(
        _forget_cumsum_kernel,
        grid=(batch,),
        in_specs=[pl.BlockSpec((seq, w), lambda b: (b, 0)), _const_spec((1, w))],
        out_specs=pl.BlockSpec((CUM_PARTS, seq, w), lambda b: (0, b, 0)),
        out_shape=jax.ShapeDtypeStruct((CUM_PARTS, n, w), BF16),
        compiler_params=_params("parallel"),
        name="forget_cumsum",
    )(small, fbias)


def _fox_kernel(q_ref, k_ref, v_ref, cp_ref, sel_ref, o_ref, kaug, qaug, s0, m_s, l_s, acc_s):
    tq = q_ref.shape[0]
    n_pairs = q_ref.shape[1] // LANES
    i = pl.program_id(1)
    lane = lax.broadcasted_iota(I32, (1, LANES), 1)

    @pl.when(i == 0)
    def _():
        for p in range(n_pairs):
            kaug[:, 2 * p * LANES:(2 * p + 1) * LANES] = k_ref[:, p * LANES:(p + 1) * LANES]
            extra = _dot(cp_ref[0], sel_ref[p, 0])
            for j in range(1, CUM_PARTS):
                extra = extra + _dot(cp_ref[j], sel_ref[p, j])
            kaug[:, (2 * p + 1) * LANES:(2 * p + 2) * LANES] = extra.astype(BF16)

    for p in range(n_pairs):
        qp = q_ref[:, p * LANES:(p + 1) * LANES]
        for e in range(2):
            rows = slice(e * tq, (e + 1) * tq)
            in_head = (lane >= e * HEAD_DIM) & (lane < (e + 1) * HEAD_DIM)
            qaug[p, rows, 0:LANES] = jnp.where(in_head, qp, jnp.zeros_like(qp))
            ones = jnp.where((lane >= CUM_PARTS * e) & (lane < CUM_PARTS * (e + 1)), 1.0, 0.0)
            qaug[p, rows, LANES:2 * LANES] = jnp.broadcast_to(ones, (tq, LANES)).astype(BF16)

    m_s[...] = jnp.full(m_s.shape, NEG, F32)
    l_s[...] = jnp.zeros(l_s.shape, F32)
    acc_s[...] = jnp.zeros(acc_s.shape, F32)
    key_row = lax.broadcasted_iota(I32, (tq, 2 * tq), 0)
    query = lax.broadcasted_iota(I32, (tq, 2 * tq), 1) & (tq - 1)
    causal = key_row <= query

    def logits(j, p):
        rows = pl.ds(pl.multiple_of(j * tq, tq), tq)
        return _dot_nt(kaug[rows, 2 * p * LANES:(2 * p + 2) * LANES], qaug[p])

    s0[...] = logits(0, 0)

    def step(j, masked):
        rows = pl.ds(pl.multiple_of(j * tq, tq), tq)
        s_next = s0[...]
        for p in range(n_pairs):
            s = s_next
            if p + 1 < n_pairs:
                s_next = logits(j, p + 1)
            else:
                s0[...] = logits(jnp.minimum(j + 1, i), 0)
            if masked:
                s = jnp.where(causal, s, NEG)
            m_old = m_s[p]
            m_new = jnp.maximum(m_old, jnp.max(s, axis=0, keepdims=True))
            alpha = jnp.exp2(m_old - m_new)
            pr = jnp.exp2(s - m_new)
            l_s[p] = alpha * l_s[p] + jnp.sum(pr, axis=0, keepdims=True)
            m_s[p] = m_new
            acc_s[p] = alpha * acc_s[p] + _dot_tn(v_ref[rows, p * LANES:(p + 1) * LANES],
                                                  pr.astype(BF16))

    def body(j, _):
        step(j, masked=False)
        return 0

    lax.fori_loop(0, i, body, 0)
    step(i, masked=True)

    feat = lax.broadcasted_iota(I32, (LANES, tq), 0)
    for p in range(n_pairs):
        o_t = acc_s[p] / l_s[p]
        o_t = jnp.where(feat < HEAD_DIM, o_t[:, :tq], o_t[:, tq:])
        o_ref[:, p * LANES:(p + 1) * LANES] = o_t.T.astype(o_ref.dtype)


def _fox(q, k, v, cum_parts, sel, batch):
    n, w = q.shape
    seq = n // batch
    tq = min(TQ_FOX, seq)
    nq = seq // tq
    n_pairs = w // LANES
    return pl.pallas_call(
        _fox_kernel,
        grid=(batch, nq),
        in_specs=[pl.BlockSpec((tq, w), lambda b, i: (b * nq + i, 0)),
                  pl.BlockSpec((seq, w), lambda b, i: (b, 0)),
                  pl.BlockSpec((seq, w), lambda b, i: (b, 0)),
                  pl.BlockSpec((CUM_PARTS, seq, LANES), lambda b, i: (0, b, 0)),
                  _const_spec(sel.shape)],
        out_specs=pl.BlockSpec((tq, w), lambda b, i: (b * nq + i, 0)),
        out_shape=jax.ShapeDtypeStruct((n, w), BF16),
        scratch_shapes=[pltpu.VMEM((seq, 2 * w), BF16),
                        pltpu.VMEM((n_pairs, 2 * tq, 2 * LANES), BF16),
                        pltpu.VMEM((tq, 2 * tq), F32),
                        pltpu.VMEM((n_pairs, 1, 2 * tq), F32),
                        pltpu.VMEM((n_pairs, 1, 2 * tq), F32),
                        pltpu.VMEM((n_pairs, LANES, 2 * tq), F32)],
        compiler_params=_params("parallel", "arbitrary"),
        name="fox",
    )(q, k, v, cum_parts, sel)


def _cum_selector(n_pairs, first_lane):
    sel = np.zeros((n_pairs, CUM_PARTS, LANES, LANES), np.float32)
    for p in range(n_pairs):
        for j in range(CUM_PARTS):
            for e in range(2):
                sel[p, j, first_lane + 2 * p + e, CUM_PARTS * e + j] = -1.0
    return jnp.asarray(sel, BF16)


def _float_key(x):
    b = lax.bitcast_convert_type(x, I32)
    return b ^ ((b >> 31) & 0x7FFFFFFF)


def _dsa_kernel(qb_ref, qi_ref, sm_ref, ckv_ref, ki_ref, kvn_ref, wuk_ref, wuv_ref, bias_ref,
                o_ref, c_s, ki_s, keys, qim, d0, s0, m_s, l_s, acc_s, *, k_sel):
    tq = qb_ref.shape[0]
    seq = ckv_ref.shape[0]
    heads = bias_ref.shape[0]
    tk = TK_DSA
    i = pl.program_id(1)

    @pl.when(i == 0)
    def _():
        c_s[0:DSA_PAD, :] = jnp.zeros((DSA_PAD, LANES), BF16)
        ki_s[0:DSA_PAD, :] = jnp.zeros((DSA_PAD, LANES), BF16)
        c_s[DSA_PAD:, :] = _rms(ckv_ref[...].astype(F32), kvn_ref[...]).astype(BF16)
        ki_s[DSA_PAD:, :] = ki_ref[...]

    n_valid = (i + 1) * tq
    n_tiles = (n_valid + tk - 1) // tk

    def tile_rows(k):
        return pl.ds(pl.multiple_of(n_valid - (k + 1) * tk + DSA_PAD, tq), tk)

    lane = lax.broadcasted_iota(I32, (1, LANES), 1)
    for h in range(IDX_HEADS):
        qp = qi_ref[:, (h // 2) * LANES:(h // 2 + 1) * LANES]
        e = h % 2
        in_head = (lane >= e * IDX_DIM) & (lane < (e + 1) * IDX_DIM)
        qim[h * tq:(h + 1) * tq, :] = jnp.where(in_head, qp, jnp.zeros_like(qp))
    w_t = sm_ref[...].T[0:IDX_HEADS, :] * (IDX_HEADS ** -0.5 * IDX_DIM ** -0.5)
    q_chunk = (i * tq + lax.broadcasted_iota(I32, (tk, tq), 1)) >> CHUNK_SHIFT
    key_local = lax.broadcasted_iota(I32, (tk, tq), 0)

    def pair_dots(k, p):
        return _dot_nt(ki_s[tile_rows(k), :], qim[2 * p * tq:(2 * p + 2) * tq, :])

    d0[...] = pair_dots(0, 0)

    def score_tile(k, _):
        d_next = d0[...]
        sc = jnp.zeros((tk, tq), F32)
        for p in range(IDX_HEADS // 2):
            d = d_next
            if p + 1 < IDX_HEADS // 2:
                d_next = pair_dots(k, p + 1)
            else:
                d0[...] = pair_dots(jnp.minimum(k + 1, n_tiles - 1), 0)
            for e in range(2):
                sc = sc + jnp.maximum(d[:, e * tq:(e + 1) * tq], 0.0) * w_t[2 * p + e:2 * p + e + 1, :]
        pos = key_local + (n_valid - (k + 1) * tk)
        sc = jnp.where(pos >= 0, jnp.where((pos >> CHUNK_SHIFT) <= q_chunk, sc, NEG), NEG)
        keys[k] = _float_key(sc)
        return 0

    lax.fori_loop(0, n_tiles, score_tile, 0)

    def count_rows(hit_fn):
        def body(k, cnt):
            return cnt + jnp.sum(hit_fn(k).reshape(tk // SUBLANES, SUBLANES, tq), axis=0)
        cnt = lax.fori_loop(0, n_tiles, body, jnp.zeros((SUBLANES, tq), I32))
        return jnp.sum(cnt, axis=0, keepdims=True)

    def count_ge(thr):
        return count_rows(lambda k: jnp.where(keys[k] >= thr, 1, 0))

    sign = jnp.int32(-2 ** 31)

    def bit_step(b, u):
        cand = u | lax.shift_left(jnp.int32(1), 31 - b)
        return jnp.where(count_ge(cand ^ sign) >= k_sel, cand, u)

    u = lax.fori_loop(0, 32, bit_step, jnp.zeros((1, tq), I32))
    thr = jnp.maximum(u ^ sign, KEY_NEG + 1)

    excess = count_ge(thr) - k_sel

    @pl.when(jnp.max(excess) > 0)
    def _():
        def count_eq_below(cut):
            def hit(k):
                pos = key_local + (n_valid - (k + 1) * tk)
                return jnp.where(keys[k] == thr, jnp.where(pos < cut, 1, 0), 0)
            return count_rows(hit)

        n_bits = max(1, (2 * seq - 1).bit_length())
        keep = count_eq_below(jnp.full((1, tq), 2 ** n_bits, I32)) - jnp.maximum(excess, 0)

        def cut_step(b, cut):
            cand = cut | lax.shift_left(jnp.int32(1), n_bits - 1 - b)
            return jnp.where(count_eq_below(cand) <= keep, cand, cut)

        cut = lax.fori_loop(0, n_bits, cut_step, jnp.zeros((1, tq), I32))

        def drop(k, _):
            pos = key_local + (n_valid - (k + 1) * tk)
            kk = keys[k]
            keys[k] = jnp.where(kk == thr, jnp.where(pos >= cut, KEY_NEG, kk), kk)
            return 0

        lax.fori_loop(0, n_tiles, drop, 0)

    ql_t = (_dot_nt(wuk_ref[...], qb_ref[...]) * (HEAD_DIM ** -0.5 * LOG2E)).astype(BF16)
    n_groups = heads // DSA_HEAD_GROUP
    gw = DSA_HEAD_GROUP * tq
    ql_g = [jnp.concatenate([ql_t[h * LANES:(h + 1) * LANES, :]
                             for h in range(g * DSA_HEAD_GROUP, (g + 1) * DSA_HEAD_GROUP)], axis=1)
            for g in range(n_groups)]
    m_s[...] = jnp.full(m_s.shape, NEG, F32)
    l_s[...] = jnp.zeros(l_s.shape, F32)
    acc_s[...] = jnp.zeros(acc_s.shape, F32)

    def attend(k, near):
        ct = c_s[tile_rows(k), :]
        sel = keys[k] >= thr
        s_next = s0[...]
        for g in range(n_groups):
            s_g = s_next
            if g + 1 < n_groups:
                s_next = _dot(ct, ql_g[g + 1])
            else:
                s0[...] = _dot(c_s[tile_rows(jnp.minimum(k + 1, n_tiles - 1)), :], ql_g[0])
            parts = []
            for hh in range(DSA_HEAD_GROUP):
                s = s_g[:, hh * tq:(hh + 1) * tq]
                if near:
                    s = jnp.concatenate([s[:tk - DSA_NEAR_KEYS], s[tk - DSA_NEAR_KEYS:]
                                         + bias_ref[g * DSA_HEAD_GROUP + hh]], axis=0)
                parts.append(jnp.where(sel, s, NEG))
            s = jnp.concatenate(parts, axis=1)
            m_old = m_s[g]
            m_new = jnp.maximum(m_old, jnp.max(s, axis=0, keepdims=True))
            alpha = jnp.exp2(m_old - m_new)
            pr = jnp.exp2(s - m_new)
            l_s[g] = alpha * l_s[g] + jnp.sum(pr, axis=0, keepdims=True)
            m_s[g] = m_new
            acc_s[g] = alpha * acc_s[g] + _dot_tn(ct, pr.astype(BF16))

    s0[...] = _dot(c_s[tile_rows(0), :], ql_g[0])
    attend(0, near=True)

    def far(k, _):
        attend(k, near=False)
        return 0

    lax.fori_loop(1, n_tiles, far, 0)

    o_parts = []
    for g in range(n_groups):
        o_g = acc_s[g] / l_s[g]
        o_parts += [o_g[:, hh * tq:(hh + 1) * tq] for hh in range(DSA_HEAD_GROUP)]
    o_lat_t = jnp.concatenate(o_parts, axis=0).astype(BF16)
    o_ref[...] = _dot(wuv_ref[...], o_lat_t).T.astype(o_ref.dtype)


def _dsa(qb, qi, small, ckv, ki2, kvn, wuk_bd, wuv_bd, bias_near, batch):
    n, w = qb.shape
    seq = n // batch
    tq = TQ_DSA
    nq = seq // tq
    heads = bias_near.shape[0]
    n_groups = heads // DSA_HEAD_GROUP
    gw = DSA_HEAD_GROUP * tq
    k_sel = min(TOPK_MAX, seq // 4)
    blk = lambda width: pl.BlockSpec((tq, width), lambda b, i: (b * nq + i, 0))
    per_batch = pl.BlockSpec((seq, LANES), lambda b, i: (b, 0))
    return pl.pallas_call(
        functools.partial(_dsa_kernel, k_sel=k_sel),
        grid=(batch, nq),
        in_specs=[blk(w), blk(w), blk(LANES), per_batch, per_batch, _const_spec(kvn.shape),
                  _const_spec(wuk_bd.shape), _const_spec(wuv_bd.shape),
                  _const_spec(bias_near.shape)],
        out_specs=blk(w),
        out_shape=jax.ShapeDtypeStruct((n, w), BF16),
        scratch_shapes=[pltpu.VMEM((seq + DSA_PAD, LANES), BF16),
                        pltpu.VMEM((seq + DSA_PAD, LANES), BF16),
                        pltpu.VMEM((pl.cdiv(seq, TK_DSA), TK_DSA, tq), I32),
                        pltpu.VMEM((IDX_HEADS * tq, LANES), BF16),
                        pltpu.VMEM((TK_DSA, 2 * tq), F32),
                        pltpu.VMEM((TK_DSA, gw), F32),
                        pltpu.VMEM((n_groups, 1, gw), F32),
                        pltpu.VMEM((n_groups, 1, gw), F32),
                        pltpu.VMEM((n_groups, LANES, gw), F32)],
        compiler_params=_params("parallel", "arbitrary"),
        name="dsa",
    )(qb, qi, small, ckv, ki2, kvn, wuk_bd, wuv_bd, bias_near)


def _merge_kernel(x_ref, oa_ref, ob_ref, oc_ref, gate_ref, wbr_ref, wout_ref, o_ref):
    d = x_ref.shape[1]
    merged = jnp.zeros(x_ref.shape, F32)
    for j, br_ref in enumerate((oa_ref, ob_ref, oc_ref)):
        gate = jax.nn.sigmoid(gate_ref[:, j * d:(j + 1) * d].astype(F32))
        merged = merged + gate * _dot(br_ref[...], wbr_ref[j])
    o_ref[...] = x_ref[...] + _dot(merged.astype(BF16), wout_ref[...])


def _merge(x, oa, ob, oc, gates, wbr, wout):
    n, d = x.shape
    tm = min(TM_TOKENS, n)
    bw = oa.shape[1]
    blk = lambda width: pl.BlockSpec((tm, width), lambda i: (i, 0))
    return pl.pallas_call(
        _merge_kernel,
        grid=(n // tm,),
        in_specs=[blk(d), blk(bw), blk(bw), blk(bw), blk(3 * d), _const_spec(wbr.shape),
                  _const_spec(wout.shape)],
        out_specs=blk(d),
        out_shape=jax.ShapeDtypeStruct((n, d), F32),
        compiler_params=_params("parallel"),
        name="merge",
    )(x, oa, ob, oc, gates, wbr, wout)


def _mem_attn_kernel(x_ref, g_ref, wq_ref, kv_ref, wo_ref, o_ref):
    x = x_ref[...]
    hw = MEM_HEADS * MEM_HEAD_DIM
    q = _dot(_rms(x, g_ref[...]).astype(BF16), wq_ref[...]).astype(BF16)
    outs = []
    for h in range(MEM_HEADS):
        lanes = slice(h * MEM_HEAD_DIM, (h + 1) * MEM_HEAD_DIM)
        s = _dot_nt(q[:, lanes], kv_ref[:, lanes]) * (MEM_HEAD_DIM ** -0.5)
        pr = jnp.exp(s - jnp.max(s, axis=1, keepdims=True))
        pr = pr / jnp.sum(pr, axis=1, keepdims=True)
        outs.append(_dot(pr.astype(BF16), kv_ref[:, hw + h * MEM_HEAD_DIM:hw + (h + 1) * MEM_HEAD_DIM]))
    o = jnp.concatenate(outs, axis=1).astype(BF16)
    o_ref[...] = x + _dot(o, wo_ref[...])


def _mem_attn(x, g, wq, kv, wo, batch):
    n, d = x.shape
    seq = n // batch
    tm = min(TM_TOKENS, seq)
    ns = seq // tm
    mem_len = kv.shape[0] // batch
    return pl.pallas_call(
        _mem_attn_kernel,
        grid=(batch, ns),
        in_specs=[pl.BlockSpec((tm, d), lambda b, s: (b * ns + s, 0)), _const_spec((1, d)),
                  _const_spec(wq.shape),
                  pl.BlockSpec((mem_len, kv.shape[1]), lambda b, s: (b, 0)),
                  _const_spec(wo.shape)],
        out_specs=pl.BlockSpec((tm, d), lambda b, s: (b * ns + s, 0)),
        out_shape=jax.ShapeDtypeStruct((n, d), F32),
        compiler_params=_params("parallel", "arbitrary"),
        name="mem_attn",
    )(x, g, wq, kv, wo)


def _final_norm_kernel(x_ref, g_ref, o_ref):
    o_ref[...] = _rms(x_ref[...], g_ref[...])


def _final_norm(x, g):
    n, d = x.shape
    tm = min(TM_TOKENS, n)
    return pl.pallas_call(
        _final_norm_kernel,
        grid=(n // tm,),
        in_specs=[pl.BlockSpec((tm, d), lambda i: (i, 0)), _const_spec((1, d))],
        out_specs=pl.BlockSpec((tm, d), lambda i: (i, 0)),
        out_shape=jax.ShapeDtypeStruct((n, d), F32),
        compiler_params=_params("parallel"),
        name="final_norm",
    )(x, g)


def _t5_bucket(rel):
    nb = NUM_BUCKETS // 2
    max_exact = nb // 2
    base = jnp.where(rel > 0, nb, 0)
    n = jnp.abs(rel)
    n_f = jnp.maximum(n, 1).astype(jnp.float32)
    large = max_exact + (jnp.log(n_f / max_exact) / math.log(MAX_DISTANCE / max_exact)
                         * (nb - max_exact)).astype(jnp.int32)
    large = jnp.minimum(large, nb - 1)
    return base + jnp.where(n < max_exact, n, large)


def _near_bias(t5_bias):
    t = jnp.arange(TQ_DSA)[:, None]
    s = jnp.arange(DSA_NEAR_KEYS)[None, :] - MAX_DISTANCE
    tbl = t5_bias[_t5_bucket(s - t)] - t5_bias[NUM_BUCKETS // 2 - 1]
    return jnp.transpose(tbl * LOG2E, (2, 1, 0)).astype(F32)


def _block_diag(w):
    nb, a, b = w.shape
    eye = jnp.eye(nb, dtype=w.dtype)
    return (eye[:, None, :, None] * w[:, :, None, :]).reshape(nb * a, nb * b)


def _pad_cols(w, width):
    return jnp.pad(w, ((0, 0), (0, width - w.shape[1])))


def kernel(x, mem, ffn1_norm, ffn1_w_gu, ffn1_w_down, mix_norm, w_in, conv_w, conv_b, rg_wa, rg_ba, rg_wx, rg_bx, rg_lambda, kv_norm, w_uk, w_uv, forget_bias, w_branch, w_out, xattn_norm, mem_norm, w_mq, w_mkv, w_mo, ffn2_norm, ffn2_w_gu, ffn2_w_down, t5_bias, final_norm):
    batch, seq, d = x.shape
    depth = w_in.shape[0]
    d_ff = ffn1_w_down.shape[1]
    dr = conv_w.shape[2]
    heads = w_uk.shape[1]
    d_lat = w_uk.shape[3]
    n = batch * seq
    assert seq % TK_DSA == 0 and d_lat == LANES and 2 * IDX_DIM == LANES and 2 * HEAD_DIM == LANES

    widths = (dr, dr, heads * HEAD_DIM, d_lat, IDX_HEADS * IDX_DIM, IDX_DIM, IDX_HEADS,
              heads * HEAD_DIM, heads * HEAD_DIM, heads * HEAD_DIM, heads, d, d, d)
    offs = np.concatenate([[0], np.cumsum(widths)])
    col = lambda w, j: w[:, int(offs[j]):int(offs[j + 1])]

    bias_near = _near_bias(t5_bias)
    cum_sel = _cum_selector(heads * HEAD_DIM // LANES, IDX_HEADS)
    row = lambda v: v.reshape(1, -1).astype(F32)

    xf = x.reshape(n, d)
    memf = mem.reshape(batch * mem.shape[1], d)
    for l in range(depth):
        wl = w_in[l]
        small_w = _pad_cols(jnp.concatenate([col(wl, 6), col(wl, 10)], axis=1), LANES)
        w_all = jnp.concatenate(
            [col(wl, 0), col(wl, 1), col(wl, 2), col(wl, 3), col(wl, 4), col(wl, 5), col(wl, 5),
             col(wl, 7) * (HEAD_DIM ** -0.5 * LOG2E),
             col(wl, 8), col(wl, 9), col(wl, 11), col(wl, 12), col(wl, 13)],
            axis=1).astype(BF16)
        w_all = jnp.concatenate([w_all, small_w.astype(BF16)], axis=1)
        out_widths = (2 * dr, heads * HEAD_DIM, d_lat, IDX_HEADS * IDX_DIM, 2 * IDX_DIM,
                      heads * HEAD_DIM, heads * HEAD_DIM, heads * HEAD_DIM, 3 * d, LANES)
        out_dtypes = (BF16,) * 9 + (F32,)

        xf = _ffn(xf, row(ffn1_norm[l]), ffn1_w_gu[l][:, :d_ff].astype(BF16),
                  ffn1_w_gu[l][:, d_ff:].astype(BF16), ffn1_w_down[l].astype(BF16))

        rg, qb, ckv, qi, ki2, qc, kc, vc, gates, small = _norm_proj(
            xf, row(mix_norm[l]), w_all, out_widths, out_dtypes)

        wax = jnp.concatenate([_block_diag(rg_wa[l]), _block_diag(rg_wx[l])], axis=1).astype(BF16)
        bax = jnp.concatenate([rg_ba[l], rg_bx[l]]).reshape(1, -1)
        o_a = _rglru(rg, batch, conv_w[l], row(conv_b[l]), wax, bax, row(rg_lambda[l]))

        fbias = _pad_cols(jnp.concatenate([jnp.zeros((IDX_HEADS,), F32), forget_bias[l]])[None], LANES)
        o_c = _fox(qc, kc, vc, _forget_cumsum(small, batch, fbias), cum_sel, batch)

        o_b = _dsa(qb, qi, small, ckv, ki2, row(kv_norm[l]), _block_diag(w_uk[l]).T.astype(BF16),
                   _block_diag(w_uv[l]).T.astype(BF16), bias_near, batch)

        xf = _merge(xf, o_a, o_b, o_c, gates, w_branch[l].astype(BF16), w_out[l].astype(BF16))

        (kv,) = _norm_proj(memf, row(mem_norm[l]), w_mkv[l].astype(BF16),
                           (w_mkv.shape[2],), (BF16,))
        xf = _mem_attn(xf, row(xattn_norm[l]), w_mq[l].astype(BF16), kv, w_mo[l].astype(BF16), batch)

        xf = _ffn(xf, row(ffn2_norm[l]), ffn2_w_gu[l][:, :d_ff].astype(BF16),
                  ffn2_w_gu[l][:, d_ff:].astype(BF16), ffn2_w_down[l].astype(BF16))
    return _final_norm(xf, row(final_norm)).reshape(batch, seq, d)
```

```python
import functools
import math

import numpy as np
import jax
import jax.numpy as jnp
from jax import lax
from jax.experimental import pallas as pl
from jax.experimental.pallas import tpu as pltpu

F32 = jnp.float32
BF16 = jnp.bfloat16
I32 = jnp.int32
I16 = jnp.int16

LANES = 128
SUBLANES = 8
VMEM_LIMIT_BYTES = 56 * 1024 * 1024

EPS = 1e-6
NEG = -1e30
LOG2E = math.log2(math.e)
CHUNK = 64
CHUNK_SHIFT = CHUNK.bit_length() - 1
HEAD_DIM = 64
RG_C = 8.0
CONV_W = 4
IDX_HEADS = 8
IDX_DIM = 64
TOPK_MAX = 256
MEM_HEADS = 4
MEM_HEAD_DIM = 128
NUM_BUCKETS = 32
MAX_DISTANCE = 128

TM_TOKENS = 512
FFN_CHUNK = 256
PROJ_CHUNK = 1024
TS_SCAN = 256
TQ_FOX = 512
CUM_PARTS = 3
TQ_DSA = 256
DSA_NEAR_KEYS = TQ_DSA + MAX_DISTANCE
TK_DSA = 512
DSA_HEAD_GROUP = 2
DSA_PAD = TK_DSA - TQ_DSA


def _key_of_float(v):
    b = int(np.float32(v).view(np.int32))
    return b ^ ((b >> 31) & 0x7FFFFFFF)


KEY_NEG = _key_of_float(NEG)


def _params(*semantics):
    return pltpu.CompilerParams(dimension_semantics=semantics, vmem_limit_bytes=VMEM_LIMIT_BYTES)


def _const_spec(shape):
    nd = len(shape)
    return pl.BlockSpec(shape, lambda *_: (0,) * nd, pipeline_mode=pl.Buffered(1))


def _rms(x32, g):
    ms = jnp.mean(x32 * x32, axis=-1, keepdims=True)
    return x32 * lax.rsqrt(ms + EPS) * g


def _dot(a, b):
    return jnp.dot(a, b, preferred_element_type=F32)


def _dot_nt(a, b):
    return lax.dot_general(a, b, (((1,), (1,)), ((), ())), preferred_element_type=F32)


def _tree_sum(xs):
    while len(xs) > 1:
        xs = [xs[j] + xs[j + 1] for j in range(0, len(xs) - 1, 2)] + ([xs[-1]] if len(xs) % 2 else [])
    return xs[0]


def _dot_tn(a, b):
    return lax.dot_general(a, b, (((0,), (0,)), ((), ())), preferred_element_type=F32)


def _ffn_kernel(x_ref, g_ref, wg_ref, wu_ref, wd_ref, o_ref, a_ref):
    x = x_ref[...]
    h = _rms(x, g_ref[...]).astype(BF16)
    d_ff = wg_ref.shape[1]
    for f0 in range(0, d_ff, FFN_CHUNK):
        g = _dot(h, wg_ref[:, f0:f0 + FFN_CHUNK])
        u = _dot(h, wu_ref[:, f0:f0 + FFN_CHUNK])
        a_ref[:, f0:f0 + FFN_CHUNK] = (g * jax.nn.sigmoid(g) * u).astype(BF16)
    o_ref[...] = x + 0.5 * _dot(a_ref[...], wd_ref[...])


def _ffn(x, g, wg, wu, wd):
    n, d = x.shape
    d_ff = wg.shape[1]
    tm = min(TM_TOKENS, n)
    return pl.pallas_call(
        _ffn_kernel,
        grid=(n // tm,),
        in_specs=[pl.BlockSpec((tm, d), lambda i: (i, 0)),
                  _const_spec((1, d)), _const_spec((d, d_ff)), _const_spec((d, d_ff)),
                  _const_spec((d_ff, d))],
        out_specs=pl.BlockSpec((tm, d), lambda i: (i, 0)),
        out_shape=jax.ShapeDtypeStruct((n, d), F32),
        scratch_shapes=[pltpu.VMEM((tm, d_ff), BF16)],
        compiler_params=_params("parallel"),
        name="ffn",
    )(x, g, wg, wu, wd)


def _norm_proj_kernel(x_ref, g_ref, w_ref, *o_refs):
    h = _rms(x_ref[...].astype(F32), g_ref[...]).astype(BF16)
    c0 = 0
    for o_ref in o_refs:
        width = o_ref.shape[1]
        for s0 in range(0, width, PROJ_CHUNK):
            s1 = min(s0 + PROJ_CHUNK, width)
            o_ref[:, s0:s1] = _dot(h, w_ref[:, c0 + s0:c0 + s1]).astype(o_ref.dtype)
        c0 += width


def _norm_proj(x, g, w, widths, dtypes):
    n, d = x.shape
    tm = min(TM_TOKENS, n)
    assert sum(widths) == w.shape[1]
    return pl.pallas_call(
        _norm_proj_kernel,
        grid=(n // tm,),
        in_specs=[pl.BlockSpec((tm, d), lambda i: (i, 0)), _const_spec((1, d)),
                  _const_spec(w.shape)],
        out_specs=[pl.BlockSpec((tm, wd), lambda i: (i, 0)) for wd in widths],
        out_shape=[jax.ShapeDtypeStruct((n, wd), dt) for wd, dt in zip(widths, dtypes)],
        compiler_params=_params("parallel"),
        name="norm_proj",
    )(x, g, w)


def _gelu_tanh(x):
    return 0.5 * x * (1.0 + jnp.tanh(math.sqrt(2.0 / math.pi) * (x + 0.044715 * (x * x * x))))


def _softplus(x):
    return jnp.maximum(x, 0.0) + jnp.log1p(jnp.exp(-jnp.abs(x)))


def _rglru_kernel(rg_ref, cw_ref, cb_ref, wax_ref, bax_ref, lam_ref, o_ref, xbuf, hc):
    ts = rg_ref.shape[0]
    dr = o_ref.shape[1]

    @pl.when(pl.program_id(1) == 0)
    def _():
        xbuf[0:8, :] = jnp.zeros((8, dr), F32)
        hc[...] = jnp.zeros_like(hc)

    xr = rg_ref[:, :dr].astype(F32)
    gr = rg_ref[:, dr:].astype(F32)
    xbuf[8:, :] = xr
    xc = cb_ref[...] + cw_ref[3:4, :] * xr
    for j in range(CONV_W - 1):
        xc = xc + cw_ref[j:j + 1, :] * xbuf[pl.ds(5 + j, ts), :]
    xbuf[0:8, :] = xbuf[ts:ts + 8, :]

    ax = _dot(xc.astype(BF16), wax_ref[...]) + bax_ref[...]
    r = jax.nn.sigmoid(ax[:, :dr])
    gi = jax.nn.sigmoid(ax[:, dr:])
    log_a = (-RG_C) * r * _softplus(-lam_ref[...])
    a = jnp.exp(log_a)
    th = jnp.tanh(log_a)
    u = jnp.sqrt(-2.0 * th / (1.0 - th)) * (gi * xc)

    row = lax.broadcasted_iota(I32, (ts, dr), 0)
    d = 1
    while d < ts:
        a_sh = pltpu.roll(a, d, 0)
        u_sh = pltpu.roll(u, d, 0)
        keep = row >= d
        u = jnp.where(keep, a * u_sh + u, u)
        a = jnp.where(keep, a * a_sh, a)
        d *= 2
    h = u + a * hc[0:1, :]
    hc[...] = jnp.broadcast_to(h[ts - 1:ts, :], hc.shape)
    o_ref[...] = (h * _gelu_tanh(gr)).astype(o_ref.dtype)


def _rglru(rg, batch, conv_w, conv_b, wax, bax, lam):
    n, two_dr = rg.shape
    dr = two_dr // 2
    seq = n // batch
    ts = min(TS_SCAN, seq)
    ns = seq // ts
    return pl.pallas_call(
        _rglru_kernel,
        grid=(batch, ns),
        in_specs=[pl.BlockSpec((ts, two_dr), lambda b, s: (b * ns + s, 0)),
                  _const_spec(conv_w.shape), _const_spec(conv_b.shape), _const_spec(wax.shape),
                  _const_spec(bax.shape), _const_spec(lam.shape)],
        out_specs=pl.BlockSpec((ts, dr), lambda b, s: (b * ns + s, 0)),
        out_shape=jax.ShapeDtypeStruct((n, dr), BF16),
        scratch_shapes=[pltpu.VMEM((ts + 8, dr), F32), pltpu.VMEM((8, dr), F32)],
        compiler_params=_params("parallel", "arbitrary"),
        name="rglru",
    )(rg, conv_w, conv_b, wax, bax, lam)


def _forget_cumsum_kernel(f_ref, b_ref, o_ref):
    z = f_ref[...] + b_ref[...]
    x = -_softplus(-z)
    seq = x.shape[0]
    row = lax.broadcasted_iota(I32, x.shape, 0)
    d = 1
    while d < seq:
        x = jnp.where(row >= d, x + pltpu.roll(x, d, 0), x)
        d *= 2
    x = x * LOG2E
    for j in range(CUM_PARTS):
        part = x.astype(BF16)
        o_ref[j] = part
        x = x - part.astype(F32)


def _forget_cumsum(small, batch, fbias):
    n, w = small.shape
    seq = n // batch
    return pl.pallas_call(
        _forget_cumsum_kernel,
        grid=(batch,),
        in_specs=[pl.BlockSpec((seq, w), lambda b: (b, 0)), _const_spec((1, w))],
        out_specs=pl.BlockSpec((CUM_PARTS, seq, w), lambda b: (0, b, 0)),
        out_shape=jax.ShapeDtypeStruct((CUM_PARTS, n, w), BF16),
        compiler_params=_params("parallel"),
        name="forget_cumsum",
    )(small, fbias)


def _fox_kernel(q_ref, k_ref, v_ref, cp_ref, sel_ref, o_ref, kaug, qaug, s0, m_s, l_s, acc_s):
    tq = q_ref.shape[0]
    n_pairs = q_ref.shape[1] // LANES
    i = pl.program_id(1)
    lane = lax.broadcasted_iota(I32, (1, LANES), 1)

    @pl.when(i == 0)
    def _():
        for p in range(n_pairs):
            kaug[:, 2 * p * LANES:(2 * p + 1) * LANES] = k_ref[:, p * LANES:(p + 1) * LANES]
            extra = _dot(cp_ref[0], sel_ref[p, 0])
            for j in range(1, CUM_PARTS):
                extra = extra + _dot(cp_ref[j], sel_ref[p, j])
            kaug[:, (2 * p + 1) * LANES:(2 * p + 2) * LANES] = extra.astype(BF16)

    for p in range(n_pairs):
        qp = q_ref[:, p * LANES:(p + 1) * LANES]
        for e in range(2):
            rows = slice(e * tq, (e + 1) * tq)
            in_head = (lane >= e * HEAD_DIM) & (lane < (e + 1) * HEAD_DIM)
            qaug[p, rows, 0:LANES] = jnp.where(in_head, qp, jnp.zeros_like(qp))
            ones = jnp.where((lane >= CUM_PARTS * e) & (lane < CUM_PARTS * (e + 1)), 1.0, 0.0)
            qaug[p, rows, LANES:2 * LANES] = jnp.broadcast_to(ones, (tq, LANES)).astype(BF16)

    m_s[...] = jnp.full(m_s.shape, NEG, F32)
    l_s[...] = jnp.zeros(l_s.shape, F32)
    acc_s[...] = jnp.zeros(acc_s.shape, F32)
    key_row = lax.broadcasted_iota(I32, (tq, 2 * tq), 0)
    query = lax.broadcasted_iota(I32, (tq, 2 * tq), 1) & (tq - 1)
    causal = key_row <= query

    def logits(j, p):
        rows = pl.ds(pl.multiple_of(j * tq, tq), tq)
        return _dot_nt(kaug[rows, 2 * p * LANES:(2 * p + 2) * LANES], qaug[p])

    s0[...] = logits(0, 0)

    def step(j, masked):
        rows = pl.ds(pl.multiple_of(j * tq, tq), tq)
        s_next = s0[...]
        for p in range(n_pairs):
            s = s_next
            if p + 1 < n_pairs:
                s_next = logits(j, p + 1)
            else:
                s0[...] = logits(jnp.minimum(j + 1, i), 0)
            if masked:
                s = jnp.where(causal, s, NEG)
            m_old = m_s[p]
            m_new = jnp.maximum(m_old, jnp.max(s, axis=0, keepdims=True))
            alpha = jnp.exp2(m_old - m_new)
            pr = jnp.exp2(s - m_new)
            l_s[p] = alpha * l_s[p] + jnp.sum(pr, axis=0, keepdims=True)
            m_s[p] = m_new
            acc_s[p] = alpha * acc_s[p] + _dot_tn(v_ref[rows, p * LANES:(p + 1) * LANES],
                                                  pr.astype(BF16))

    def body(j, _):
        step(j, masked=False)
        return 0

    lax.fori_loop(0, i, body, 0)
    step(i, masked=True)

    feat = lax.broadcasted_iota(I32, (LANES, tq), 0)
    for p in range(n_pairs):
        o_t = acc_s[p] / l_s[p]
        o_t = jnp.where(feat < HEAD_DIM, o_t[:, :tq], o_t[:, tq:])
        o_ref[:, p * LANES:(p + 1) * LANES] = o_t.T.astype(o_ref.dtype)


def _fox(q, k, v, cum_parts, sel, batch):
    n, w = q.shape
    seq = n // batch
    tq = min(TQ_FOX, seq)
    nq = seq // tq
    n_pairs = w // LANES
    return pl.pallas_call(
        _fox_kernel,
        grid=(batch, nq),
        in_specs=[pl.BlockSpec((tq, w), lambda b, i: (b * nq + i, 0)),
                  pl.BlockSpec((seq, w), lambda b, i: (b, 0)),
                  pl.BlockSpec((seq, w), lambda b, i: (b, 0)),
                  pl.BlockSpec((CUM_PARTS, seq, LANES), lambda b, i: (0, b, 0)),
                  _const_spec(sel.shape)],
        out_specs=pl.BlockSpec((tq, w), lambda b, i: (b * nq + i, 0)),
        out_shape=jax.ShapeDtypeStruct((n, w), BF16),
        scratch_shapes=[pltpu.VMEM((seq, 2 * w), BF16),
                        pltpu.VMEM((n_pairs, 2 * tq, 2 * LANES), BF16),
                        pltpu.VMEM((tq, 2 * tq), F32),
                        pltpu.VMEM((n_pairs, 1, 2 * tq), F32),
                        pltpu.VMEM((n_pairs, 1, 2 * tq), F32),
                        pltpu.VMEM((n_pairs, LANES, 2 * tq), F32)],
        compiler_params=_params("parallel", "arbitrary"),
        name="fox",
    )(q, k, v, cum_parts, sel)


def _cum_selector(n_pairs, first_lane):
    sel = np.zeros((n_pairs, CUM_PARTS, LANES, LANES), np.float32)
    for p in range(n_pairs):
        for j in range(CUM_PARTS):
            for e in range(2):
                sel[p, j, first_lane + 2 * p + e, CUM_PARTS * e + j] = -1.0
    return jnp.asarray(sel, BF16)


def _float_key(x):
    b = lax.bitcast_convert_type(x, I32)
    return b ^ ((b >> 31) & 0x7FFFFFFF)


def _dsa_kernel(qb_ref, qi_ref, sm_ref, ckv_ref, ki_ref, kvn_ref, wuk_ref, wuv_ref, bias_ref,
                o_ref, c_s, ki_s, keys, keys16, qim, d0, s0, m_s, l_s, acc_s, *, k_sel):
    tq = qb_ref.shape[0]
    seq = ckv_ref.shape[0]
    heads = bias_ref.shape[0]
    tk = TK_DSA
    i = pl.program_id(1)

    @pl.when(i == 0)
    def _():
        c_s[0:DSA_PAD, :] = jnp.zeros((DSA_PAD, LANES), BF16)
        ki_s[0:DSA_PAD, :] = jnp.zeros((DSA_PAD, LANES), BF16)
        c_s[DSA_PAD:, :] = _rms(ckv_ref[...].astype(F32), kvn_ref[...]).astype(BF16)
        ki_s[DSA_PAD:, :] = ki_ref[...]

    n_valid = (i + 1) * tq
    n_tiles = (n_valid + tk - 1) // tk

    def tile_rows(k):
        return pl.ds(pl.multiple_of(n_valid - (k + 1) * tk + DSA_PAD, tq), tk)

    lane = lax.broadcasted_iota(I32, (1, LANES), 1)
    for h in range(IDX_HEADS):
        qp = qi_ref[:, (h // 2) * LANES:(h // 2 + 1) * LANES]
        e = h % 2
        in_head = (lane >= e * IDX_DIM) & (lane < (e + 1) * IDX_DIM)
        qim[h * tq:(h + 1) * tq, :] = jnp.where(in_head, qp, jnp.zeros_like(qp))
    w_t = sm_ref[...].T[0:IDX_HEADS, :] * (IDX_HEADS ** -0.5 * IDX_DIM ** -0.5)
    q_chunk = (i * tq + lax.broadcasted_iota(I32, (tk, tq), 1)) >> CHUNK_SHIFT
    key_local = lax.broadcasted_iota(I32, (tk, tq), 0)

    def pair_dots(k, p):
        return _dot_nt(ki_s[tile_rows(k), :], qim[2 * p * tq:(2 * p + 2) * tq, :])

    d0[...] = pair_dots(0, 0)

    def score_tile(k, _):
        d_next = d0[...]
        sc = jnp.zeros((tk, tq), F32)
        for p in range(IDX_HEADS // 2):
            d = d_next
            if p + 1 < IDX_HEADS // 2:
                d_next = pair_dots(k, p + 1)
            else:
                d0[...] = pair_dots(jnp.minimum(k + 1, n_tiles - 1), 0)
            for e in range(2):
                sc = sc + jnp.maximum(d[:, e * tq:(e + 1) * tq], 0.0) * w_t[2 * p + e:2 * p + e + 1, :]
        pos = key_local + (n_valid - (k + 1) * tk)
        sc = jnp.where(pos >= 0, jnp.where((pos >> CHUNK_SHIFT) <= q_chunk, sc, NEG), NEG)
        key = _float_key(sc)
        keys[k] = key
        keys16[k] = (key >> 16).astype(I16)
        return 0

    lax.fori_loop(0, n_tiles, score_tile, 0)

    def count_rows(hit_fn):
        def body(k, cnt):
            return cnt + jnp.sum(hit_fn(k).reshape(tk // SUBLANES, SUBLANES, tq), axis=0)
        cnt = lax.fori_loop(0, n_tiles, body, jnp.zeros((SUBLANES, tq), I32))
        return jnp.sum(cnt, axis=0, keepdims=True)

    def count_ge(thr):
        return count_rows(lambda k: jnp.where(keys[k] >= thr, 1, 0))

    def count_ge16(thr16):
        rows16 = 2 * SUBLANES
        def body(k, cnt):
            ge = jnp.where(keys16[k] >= thr16, jnp.ones((), I16), jnp.zeros((), I16))
            return cnt + _tree_sum([ge[c * rows16:(c + 1) * rows16, :] for c in range(tk // rows16)])
        cnt = lax.fori_loop(0, n_tiles, body, jnp.zeros((rows16, tq), I16))
        return jnp.sum(cnt.astype(I32), axis=0, keepdims=True)

    half = 2 ** 15

    def kth_largest16(k_need):
        def step(b, u):
            cand = u | lax.shift_left(jnp.int32(1), 15 - b)
            return jnp.where(count_ge16((cand - half).astype(I16)) >= k_need, cand, u)
        return lax.fori_loop(0, 16, step, jnp.zeros((1, tq), I32)) - half

    t_hi = kth_largest16(k_sel)
    n_above = jnp.where(t_hi >= half - 1, 0,
                        count_ge16((jnp.minimum(t_hi, half - 2) + 1).astype(I16)))

    def low_halves(k, _):
        key = keys[k]
        keys16[k] = jnp.where((key >> 16) == t_hi, (key & 0xFFFF) - half, -half).astype(I16)
        return 0

    lax.fori_loop(0, n_tiles, low_halves, 0)
    t_lo = kth_largest16(k_sel - n_above)
    thr = jnp.maximum(lax.shift_left(t_hi, 16) + (t_lo + half), KEY_NEG + 1)

    excess = count_ge(thr) - k_sel

    @pl.when(jnp.max(excess) > 0)
    def _():
        def count_eq_below(cut):
            def hit(k):
                pos = key_local + (n_valid - (k + 1) * tk)
                return jnp.where(keys[k] == thr, jnp.where(pos < cut, 1, 0), 0)
            return count_rows(hit)

        n_bits = max(1, (2 * seq - 1).bit_length())
        keep = count_eq_below(jnp.full((1, tq), 2 ** n_bits, I32)) - jnp.maximum(excess, 0)

        def cut_step(b, cut):
            cand = cut | lax.shift_left(jnp.int32(1), n_bits - 1 - b)
            return jnp.where(count_eq_below(cand) <= keep, cand, cut)

        cut = lax.fori_loop(0, n_bits, cut_step, jnp.zeros((1, tq), I32))

        def drop(k, _):
            pos = key_local + (n_valid - (k + 1) * tk)
            kk = keys[k]
            keys[k] = jnp.where(kk == thr, jnp.where(pos >= cut, KEY_NEG, kk), kk)
            return 0

        lax.fori_loop(0, n_tiles, drop, 0)

    ql_t = (_dot_nt(wuk_ref[...], qb_ref[...]) * (HEAD_DIM ** -0.5 * LOG2E)).astype(BF16)
    n_groups = heads // DSA_HEAD_GROUP
    gw = DSA_HEAD_GROUP * tq
    ql_g = [jnp.concatenate([ql_t[h * LANES:(h + 1) * LANES, :]
                             for h in range(g * DSA_HEAD_GROUP, (g + 1) * DSA_HEAD_GROUP)], axis=1)
            for g in range(n_groups)]
    m_s[...] = jnp.full(m_s.shape, NEG, F32)
    l_s[...] = jnp.zeros(l_s.shape, F32)
    acc_s[...] = jnp.zeros(acc_s.shape, F32)

    def attend(k, near):
        ct = c_s[tile_rows(k), :]
        sel = keys[k] >= thr
        s_next = s0[...]
        for g in range(n_groups):
            s_g = s_next
            if g + 1 < n_groups:
                s_next = _dot(ct, ql_g[g + 1])
            else:
                s0[...] = _dot(c_s[tile_rows(jnp.minimum(k + 1, n_tiles - 1)), :], ql_g[0])
            parts = []
            for hh in range(DSA_HEAD_GROUP):
                s = s_g[:, hh * tq:(hh + 1) * tq]
                if near:
                    s = jnp.concatenate([s[:tk - DSA_NEAR_KEYS], s[tk - DSA_NEAR_KEYS:]
                                         + bias_ref[g * DSA_HEAD_GROUP + hh]], axis=0)
                parts.append(jnp.where(sel, s, NEG))
            s = jnp.concatenate(parts, axis=1)
            m_old = m_s[g]
            m_new = jnp.maximum(m_old, jnp.max(s, axis=0, keepdims=True))
            alpha = jnp.exp2(m_old - m_new)
            pr = jnp.exp2(s - m_new)
            l_s[g] = alpha * l_s[g] + jnp.sum(pr, axis=0, keepdims=True)
            m_s[g] = m_new
            acc_s[g] = alpha * acc_s[g] + _dot_tn(ct, pr.astype(BF16))

    s0[...] = _dot(c_s[tile_rows(0), :], ql_g[0])
    attend(0, near=True)

    def far(k, _):
        attend(k, near=False)
        return 0

    lax.fori_loop(1, n_tiles, far, 0)

    o_parts = []
    for g in range(n_groups):
        o_g = acc_s[g] / l_s[g]
        o_parts += [o_g[:, hh * tq:(hh + 1) * tq] for hh in range(DSA_HEAD_GROUP)]
    o_lat_t = jnp.concatenate(o_parts, axis=0).astype(BF16)
    o_ref[...] = _dot(wuv_ref[...], o_lat_t).T.astype(o_ref.dtype)


def _dsa(qb, qi, small, ckv, ki2, kvn, wuk_bd, wuv_bd, bias_near, batch):
    n, w = qb.shape
    seq = n // batch
    tq = TQ_DSA
    nq = seq // tq
    heads = bias_near.shape[0]
    n_groups = heads // DSA_HEAD_GROUP
    gw = DSA_HEAD_GROUP * tq
    k_sel = min(TOPK_MAX, seq // 4)
    blk = lambda width: pl.BlockSpec((tq, width), lambda b, i: (b * nq + i, 0))
    per_batch = pl.BlockSpec((seq, LANES), lambda b, i: (b, 0))
    return pl.pallas_call(
        functools.partial(_dsa_kernel, k_sel=k_sel),
        grid=(batch, nq),
        in_specs=[blk(w), blk(w), blk(LANES), per_batch, per_batch, _const_spec(kvn.shape),
                  _const_spec(wuk_bd.shape), _const_spec(wuv_bd.shape),
                  _const_spec(bias_near.shape)],
        out_specs=blk(w),
        out_shape=jax.ShapeDtypeStruct((n, w), BF16),
        scratch_shapes=[pltpu.VMEM((seq + DSA_PAD, LANES), BF16),
                        pltpu.VMEM((seq + DSA_PAD, LANES), BF16),
                        pltpu.VMEM((pl.cdiv(seq, TK_DSA), TK_DSA, tq), I32),
                        pltpu.VMEM((pl.cdiv(seq, TK_DSA), TK_DSA, tq), I16),
                        pltpu.VMEM((IDX_HEADS * tq, LANES), BF16),
                        pltpu.VMEM((TK_DSA, 2 * tq), F32),
                        pltpu.VMEM((TK_DSA, gw), F32),
                        pltpu.VMEM((n_groups, 1, gw), F32),
                        pltpu.VMEM((n_groups, 1, gw), F32),
                        pltpu.VMEM((n_groups, LANES, gw), F32)],
        compiler_params=_params("parallel", "arbitrary"),
        name="dsa",
    )(qb, qi, small, ckv, ki2, kvn, wuk_bd, wuv_bd, bias_near)


def _merge_kernel(x_ref, oa_ref, ob_ref, oc_ref, gate_ref, wbr_ref, wout_ref, o_ref):
    d = x_ref.shape[1]
    merged = jnp.zeros(x_ref.shape, F32)
    for j, br_ref in enumerate((oa_ref, ob_ref, oc_ref)):
        gate = jax.nn.sigmoid(gate_ref[:, j * d:(j + 1) * d].astype(F32))
        merged = merged + gate * _dot(br_ref[...], wbr_ref[j])
    o_ref[...] = x_ref[...] + _dot(merged.astype(BF16), wout_ref[...])


def _merge(x, oa, ob, oc, gates, wbr, wout):
    n, d = x.shape
    tm = min(TM_TOKENS, n)
    bw = oa.shape[1]
    blk = lambda width: pl.BlockSpec((tm, width), lambda i: (i, 0))
    return pl.pallas_call(
        _merge_kernel,
        grid=(n // tm,),
        in_specs=[blk(d), blk(bw), blk(bw), blk(bw), blk(3 * d), _const_spec(wbr.shape),
                  _const_spec(wout.shape)],
        out_specs=blk(d),
        out_shape=jax.ShapeDtypeStruct((n, d), F32),
        compiler_params=_params("parallel"),
        name="merge",
    )(x, oa, ob, oc, gates, wbr, wout)


def _mem_attn_kernel(x_ref, g_ref, wq_ref, kv_ref, wo_ref, o_ref):
    x = x_ref[...]
    hw = MEM_HEADS * MEM_HEAD_DIM
    q = _dot(_rms(x, g_ref[...]).astype(BF16), wq_ref[...]).astype(BF16)
    outs = []
    for h in range(MEM_HEADS):
        lanes = slice(h * MEM_HEAD_DIM, (h + 1) * MEM_HEAD_DIM)
        s = _dot_nt(q[:, lanes], kv_ref[:, lanes]) * (MEM_HEAD_DIM ** -0.5)
        pr = jnp.exp(s - jnp.max(s, axis=1, keepdims=True))
        pr = pr / jnp.sum(pr, axis=1, keepdims=True)
        outs.append(_dot(pr.astype(BF16), kv_ref[:, hw + h * MEM_HEAD_DIM:hw + (h + 1) * MEM_HEAD_DIM]))
    o = jnp.concatenate(outs, axis=1).astype(BF16)
    o_ref[...] = x + _dot(o, wo_ref[...])


def _mem_attn(x, g, wq, kv, wo, batch):
    n, d = x.shape
    seq = n // batch
    tm = min(TM_TOKENS, seq)
    ns = seq // tm
    mem_len = kv.shape[0] // batch
    return pl.pallas_call(
        _mem_attn_kernel,
        grid=(batch, ns),
        in_specs=[pl.BlockSpec((tm, d), lambda b, s: (b * ns + s, 0)), _const_spec((1, d)),
                  _const_spec(wq.shape),
                  pl.BlockSpec((mem_len, kv.shape[1]), lambda b, s: (b, 0)),
                  _const_spec(wo.shape)],
        out_specs=pl.BlockSpec((tm, d), lambda b, s: (b * ns + s, 0)),
        out_shape=jax.ShapeDtypeStruct((n, d), F32),
        compiler_params=_params("parallel", "arbitrary"),
        name="mem_attn",
    )(x, g, wq, kv, wo)


def _final_norm_kernel(x_ref, g_ref, o_ref):
    o_ref[...] = _rms(x_ref[...], g_ref[...])


def _final_norm(x, g):
    n, d = x.shape
    tm = min(TM_TOKENS, n)
    return pl.pallas_call(
        _final_norm_kernel,
        grid=(n // tm,),
        in_specs=[pl.BlockSpec((tm, d), lambda i: (i, 0)), _const_spec((1, d))],
        out_specs=pl.BlockSpec((tm, d), lambda i: (i, 0)),
        out_shape=jax.ShapeDtypeStruct((n, d), F32),
        compiler_params=_params("parallel"),
        name="final_norm",
    )(x, g)


def _t5_bucket(rel):
    nb = NUM_BUCKETS // 2
    max_exact = nb // 2
    base = jnp.where(rel > 0, nb, 0)
    n = jnp.abs(rel)
    n_f = jnp.maximum(n, 1).astype(jnp.float32)
    large = max_exact + (jnp.log(n_f / max_exact) / math.log(MAX_DISTANCE / max_exact)
                         * (nb - max_exact)).astype(jnp.int32)
    large = jnp.minimum(large, nb - 1)
    return base + jnp.where(n < max_exact, n, large)


def _near_bias(t5_bias):
    t = jnp.arange(TQ_DSA)[:, None]
    s = jnp.arange(DSA_NEAR_KEYS)[None, :] - MAX_DISTANCE
    tbl = t5_bias[_t5_bucket(s - t)] - t5_bias[NUM_BUCKETS // 2 - 1]
    return jnp.transpose(tbl * LOG2E, (2, 1, 0)).astype(F32)


def _block_diag(w):
    nb, a, b = w.shape
    eye = jnp.eye(nb, dtype=w.dtype)
    return (eye[:, None, :, None] * w[:, :, None, :]).reshape(nb * a, nb * b)


def _pad_cols(w, width):
    return jnp.pad(w, ((0, 0), (0, width - w.shape[1])))


def kernel(x, mem, ffn1_norm, ffn1_w_gu, ffn1_w_down, mix_norm, w_in, conv_w, conv_b, rg_wa, rg_ba, rg_wx, rg_bx, rg_lambda, kv_norm, w_uk, w_uv, forget_bias, w_branch, w_out, xattn_norm, mem_norm, w_mq, w_mkv, w_mo, ffn2_norm, ffn2_w_gu, ffn2_w_down, t5_bias, final_norm):
    batch, seq, d = x.shape
    depth = w_in.shape[0]
    d_ff = ffn1_w_down.shape[1]
    dr = conv_w.shape[2]
    heads = w_uk.shape[1]
    d_lat = w_uk.shape[3]
    n = batch * seq
    assert seq % TK_DSA == 0 and d_lat == LANES and 2 * IDX_DIM == LANES and 2 * HEAD_DIM == LANES

    widths = (dr, dr, heads * HEAD_DIM, d_lat, IDX_HEADS * IDX_DIM, IDX_DIM, IDX_HEADS,
              heads * HEAD_DIM, heads * HEAD_DIM, heads * HEAD_DIM, heads, d, d, d)
    offs = np.concatenate([[0], np.cumsum(widths)])
    col = lambda w, j: w[:, int(offs[j]):int(offs[j + 1])]

    bias_near = _near_bias(t5_bias)
    cum_sel = _cum_selector(heads * HEAD_DIM // LANES, IDX_HEADS)
    row = lambda v: v.reshape(1, -1).astype(F32)

    xf = x.reshape(n, d)
    memf = mem.reshape(batch * mem.shape[1], d)
    for l in range(depth):
        wl = w_in[l]
        small_w = _pad_cols(jnp.concatenate([col(wl, 6), col(wl, 10)], axis=1), LANES)
        w_all = jnp.concatenate(
            [col(wl, 0), col(wl, 1), col(wl, 2), col(wl, 3), col(wl, 4), col(wl, 5), col(wl, 5),
             col(wl, 7) * (HEAD_DIM ** -0.5 * LOG2E),
             col(wl, 8), col(wl, 9), col(wl, 11), col(wl, 12), col(wl, 13)],
            axis=1).astype(BF16)
        w_all = jnp.concatenate([w_all, small_w.astype(BF16)], axis=1)
        out_widths = (2 * dr, heads * HEAD_DIM, d_lat, IDX_HEADS * IDX_DIM, 2 * IDX_DIM,
                      heads * HEAD_DIM, heads * HEAD_DIM, heads * HEAD_DIM, 3 * d, LANES)
        out_dtypes = (BF16,) * 9 + (F32,)

        xf = _ffn(xf, row(ffn1_norm[l]), ffn1_w_gu[l][:, :d_ff].astype(BF16),
                  ffn1_w_gu[l][:, d_ff:].astype(BF16), ffn1_w_down[l].astype(BF16))

        rg, qb, ckv, qi, ki2, qc, kc, vc, gates, small = _norm_proj(
            xf, row(mix_norm[l]), w_all, out_widths, out_dtypes)

        wax = jnp.concatenate([_block_diag(rg_wa[l]), _block_diag(rg_wx[l])], axis=1).astype(BF16)
        bax = jnp.concatenate([rg_ba[l], rg_bx[l]]).reshape(1, -1)
        o_a = _rglru(rg, batch, conv_w[l], row(conv_b[l]), wax, bax, row(rg_lambda[l]))

        fbias = _pad_cols(jnp.concatenate([jnp.zeros((IDX_HEADS,), F32), forget_bias[l]])[None], LANES)
        o_c = _fox(qc, kc, vc, _forget_cumsum(small, batch, fbias), cum_sel, batch)

        o_b = _dsa(qb, qi, small, ckv, ki2, row(kv_norm[l]), _block_diag(w_uk[l]).T.astype(BF16),
                   _block_diag(w_uv[l]).T.astype(BF16), bias_near, batch)

        xf = _merge(xf, o_a, o_b, o_c, gates, w_branch[l].astype(BF16), w_out[l].astype(BF16))

        (kv,) = _norm_proj(memf, row(mem_norm[l]), w_mkv[l].astype(BF16),
                           (w_mkv.shape[2],), (BF16,))
        xf = _mem_attn(xf, row(xattn_norm[l]), w_mq[l].astype(BF16), kv, w_mo[l].astype(BF16), batch)

        xf = _ffn(xf, row(ffn2_norm[l]), ffn2_w_gu[l][:, :d_ff].astype(BF16),
                  ffn2_w_gu[l][:, d_ff:].astype(BF16), ffn2_w_down[l].astype(BF16))
    return _final_norm(xf, row(final_norm)).reshape(batch, seq, d)
```

```python
import functools
import math

import numpy as np
import jax
import jax.numpy as jnp
from jax import lax
from jax.experimental import pallas as pl
from jax.experimental.pallas import tpu as pltpu

F32 = jnp.float32
BF16 = jnp.bfloat16
I32 = jnp.int32
I16 = jnp.int16

LANES = 128
SUBLANES = 8
VMEM_LIMIT_BYTES = 56 * 1024 * 1024

EPS = 1e-6
NEG = -1e30
LOG2E = math.log2(math.e)
CHUNK = 64
CHUNK_SHIFT = CHUNK.bit_length() - 1
HEAD_DIM = 64
RG_C = 8.0
CONV_W = 4
IDX_HEADS = 8
IDX_DIM = 64
TOPK_MAX = 256
MEM_HEADS = 4
MEM_HEAD_DIM = 128
NUM_BUCKETS = 32
MAX_DISTANCE = 128

TM_TOKENS = 512
FFN_CHUNK = 256
PROJ_CHUNK = 1024
TS_SCAN = 256
TQ_FOX = 512
CUM_PARTS = 3
TQ_DSA = 256
DSA_NEAR_KEYS = TQ_DSA + MAX_DISTANCE
TK_DSA = 512
DSA_HEAD_GROUP = 2
DSA_PAD = TK_DSA - TQ_DSA


def _key_of_float(v):
    b = int(np.float32(v).view(np.int32))
    return b ^ ((b >> 31) & 0x7FFFFFFF)


KEY_NEG = _key_of_float(NEG)


def _params(*semantics):
    return pltpu.CompilerParams(dimension_semantics=semantics, vmem_limit_bytes=VMEM_LIMIT_BYTES)


def _const_spec(shape):
    nd = len(shape)
    return pl.BlockSpec(shape, lambda *_: (0,) * nd, pipeline_mode=pl.Buffered(1))


def _rms(x32, g):
    ms = jnp.mean(x32 * x32, axis=-1, keepdims=True)
    return x32 * lax.rsqrt(ms + EPS) * g


def _dot(a, b):
    return jnp.dot(a, b, preferred_element_type=F32)


def _dot_nt(a, b):
    return lax.dot_general(a, b, (((1,), (1,)), ((), ())), preferred_element_type=F32)


def _tree_sum(xs):
    while len(xs) > 1:
        xs = [xs[j] + xs[j + 1] for j in range(0, len(xs) - 1, 2)] + ([xs[-1]] if len(xs) % 2 else [])
    return xs[0]


def _dot_tn(a, b):
    return lax.dot_general(a, b, (((0,), (0,)), ((), ())), preferred_element_type=F32)


def _ffn_kernel(x_ref, g_ref, wg_ref, wu_ref, wd_ref, gout_ref, o_ref, a_ref, *, norm_out):
    x = x_ref[...]
    h = _rms(x, g_ref[...]).astype(BF16)
    d_ff = wg_ref.shape[1]
    for f0 in range(0, d_ff, FFN_CHUNK):
        g = _dot(h, wg_ref[:, f0:f0 + FFN_CHUNK])
        u = _dot(h, wu_ref[:, f0:f0 + FFN_CHUNK])
        a_ref[:, f0:f0 + FFN_CHUNK] = (g * jax.nn.sigmoid(g) * u).astype(BF16)
    y = x + 0.5 * _dot(a_ref[...], wd_ref[...])
    o_ref[...] = _rms(y, gout_ref[...]) if norm_out else y


def _ffn(x, g, wg, wu, wd, g_out, norm_out):
    n, d = x.shape
    d_ff = wg.shape[1]
    tm = min(TM_TOKENS, n)
    return pl.pallas_call(
        functools.partial(_ffn_kernel, norm_out=norm_out),
        grid=(n // tm,),
        in_specs=[pl.BlockSpec((tm, d), lambda i: (i, 0)),
                  _const_spec((1, d)), _const_spec((d, d_ff)), _const_spec((d, d_ff)),
                  _const_spec((d_ff, d)), _const_spec((1, d))],
        out_specs=pl.BlockSpec((tm, d), lambda i: (i, 0)),
        out_shape=jax.ShapeDtypeStruct((n, d), F32),
        scratch_shapes=[pltpu.VMEM((tm, d_ff), BF16)],
        compiler_params=_params("parallel"),
        name="ffn",
    )(x, g, wg, wu, wd, g_out)


def _norm_proj_kernel(x_ref, g_ref, w_ref, *o_refs):
    h = _rms(x_ref[...].astype(F32), g_ref[...]).astype(BF16)
    c0 = 0
    for o_ref in o_refs:
        width = o_ref.shape[1]
        for s0 in range(0, width, PROJ_CHUNK):
            s1 = min(s0 + PROJ_CHUNK, width)
            o_ref[:, s0:s1] = _dot(h, w_ref[:, c0 + s0:c0 + s1]).astype(o_ref.dtype)
        c0 += width


def _norm_proj(x, g, w, widths, dtypes):
    n, d = x.shape
    tm = min(TM_TOKENS, n)
    assert sum(widths) == w.shape[1]
    return pl.pallas_call(
        _norm_proj_kernel,
        grid=(n // tm,),
        in_specs=[pl.BlockSpec((tm, d), lambda i: (i, 0)), _const_spec((1, d)),
                  _const_spec(w.shape)],
        out_specs=[pl.BlockSpec((tm, wd), lambda i: (i, 0)) for wd in widths],
        out_shape=[jax.ShapeDtypeStruct((n, wd), dt) for wd, dt in zip(widths, dtypes)],
        compiler_params=_params("parallel"),
        name="norm_proj",
    )(x, g, w)


def _gelu_tanh(x):
    return 0.5 * x * (1.0 + jnp.tanh(math.sqrt(2.0 / math.pi) * (x + 0.044715 * (x * x * x))))


def _softplus(x):
    return jnp.maximum(x, 0.0) + jnp.log1p(jnp.exp(-jnp.abs(x)))


def _rglru_kernel(rg_ref, cw_ref, cb_ref, wax_ref, bax_ref, lam_ref, o_ref, xbuf, hc):
    ts = rg_ref.shape[0]
    dr = o_ref.shape[1]

    @pl.when(pl.program_id(1) == 0)
    def _():
        xbuf[0:8, :] = jnp.zeros((8, dr), F32)
        hc[...] = jnp.zeros_like(hc)

    xr = rg_ref[:, :dr].astype(F32)
    gr = rg_ref[:, dr:].astype(F32)
    xbuf[8:, :] = xr
    xc = cb_ref[...] + cw_ref[3:4, :] * xr
    for j in range(CONV_W - 1):
        xc = xc + cw_ref[j:j + 1, :] * xbuf[pl.ds(5 + j, ts), :]
    xbuf[0:8, :] = xbuf[ts:ts + 8, :]

    ax = _dot(xc.astype(BF16), wax_ref[...]) + bax_ref[...]
    r = jax.nn.sigmoid(ax[:, :dr])
    gi = jax.nn.sigmoid(ax[:, dr:])
    log_a = (-RG_C) * r * _softplus(-lam_ref[...])
    a = jnp.exp(log_a)
    th = jnp.tanh(log_a)
    u = jnp.sqrt(-2.0 * th / (1.0 - th)) * (gi * xc)

    row = lax.broadcasted_iota(I32, (ts, dr), 0)
    d = 1
    while d < ts:
        a_sh = pltpu.roll(a, d, 0)
        u_sh = pltpu.roll(u, d, 0)
        keep = row >= d
        u = jnp.where(keep, a * u_sh + u, u)
        a = jnp.where(keep, a * a_sh, a)
        d *= 2
    h = u + a * hc[0:1, :]
    hc[...] = jnp.broadcast_to(h[ts - 1:ts, :], hc.shape)
    o_ref[...] = (h * _gelu_tanh(gr)).astype(o_ref.dtype)


def _rglru(rg, batch, conv_w, conv_b, wax, bax, lam):
    n, two_dr = rg.shape
    dr = two_dr // 2
    seq = n // batch
    ts = min(TS_SCAN, seq)
    ns = seq // ts
    return pl.pallas_call(
        _rglru_kernel,
        grid=(batch, ns),
        in_specs=[pl.BlockSpec((ts, two_dr), lambda b, s: (b * ns + s, 0)),
                  _const_spec(conv_w.shape), _const_spec(conv_b.shape), _const_spec(wax.shape),
                  _const_spec(bax.shape), _const_spec(lam.shape)],
        out_specs=pl.BlockSpec((ts, dr), lambda b, s: (b * ns + s, 0)),
        out_shape=jax.ShapeDtypeStruct((n, dr), BF16),
        scratch_shapes=[pltpu.VMEM((ts + 8, dr), F32), pltpu.VMEM((8, dr), F32)],
        compiler_params=_params("parallel", "arbitrary"),
        name="rglru",
    )(rg, conv_w, conv_b, wax, bax, lam)


def _forget_cumsum_kernel(f_ref, b_ref, o_ref):
    z = f_ref[...] + b_ref[...]
    x = -_softplus(-z)
    seq = x.shape[0]
    row = lax.broadcasted_iota(I32, x.shape, 0)
    d = 1
    while d < seq:
        x = jnp.where(row >= d, x + pltpu.roll(x, d, 0), x)
        d *= 2
    x = x * LOG2E
    for j in range(CUM_PARTS):
        part = x.astype(BF16)
        o_ref[j] = part
        x = x - part.astype(F32)


def _forget_cumsum(small, batch, fbias):
    n, w = small.shape
    seq = n // batch
    return pl.pallas_call(
        _forget_cumsum_kernel,
        grid=(batch,),
        in_specs=[pl.BlockSpec((seq, w), lambda b: (b, 0)), _const_spec((1, w))],
        out_specs=pl.BlockSpec((CUM_PARTS, seq, w), lambda b: (0, b, 0)),
        out_shape=jax.ShapeDtypeStruct((CUM_PARTS, n, w), BF16),
        compiler_params=_params("parallel"),
        name="forget_cumsum",
    )(small, fbias)


def _fox_kernel(q_ref, k_ref, v_ref, cp_ref, sel_ref, o_ref, kaug, qaug, s0, m_s, l_s, acc_s):
    tq = q_ref.shape[0]
    n_pairs = q_ref.shape[1] // LANES
    i = pl.program_id(1)
    lane = lax.broadcasted_iota(I32, (1, LANES), 1)

    @pl.when(i == 0)
    def _():
        for p in range(n_pairs):
            kaug[:, 2 * p * LANES:(2 * p + 1) * LANES] = k_ref[:, p * LANES:(p + 1) * LANES]
            extra = _dot(cp_ref[0], sel_ref[p, 0])
            for j in range(1, CUM_PARTS):
                extra = extra + _dot(cp_ref[j], sel_ref[p, j])
            kaug[:, (2 * p + 1) * LANES:(2 * p + 2) * LANES] = extra.astype(BF16)

    for p in range(n_pairs):
        qp = q_ref[:, p * LANES:(p + 1) * LANES]
        for e in range(2):
            rows = slice(e * tq, (e + 1) * tq)
            in_head = (lane >= e * HEAD_DIM) & (lane < (e + 1) * HEAD_DIM)
            qaug[p, rows, 0:LANES] = jnp.where(in_head, qp, jnp.zeros_like(qp))
            ones = jnp.where((lane >= CUM_PARTS * e) & (lane < CUM_PARTS * (e + 1)), 1.0, 0.0)
            qaug[p, rows, LANES:2 * LANES] = jnp.broadcast_to(ones, (tq, LANES)).astype(BF16)

    m_s[...] = jnp.full(m_s.shape, NEG, F32)
    l_s[...] = jnp.zeros(l_s.shape, F32)
    acc_s[...] = jnp.zeros(acc_s.shape, F32)
    key_row = lax.broadcasted_iota(I32, (tq, 2 * tq), 0)
    query = lax.broadcasted_iota(I32, (tq, 2 * tq), 1) & (tq - 1)
    causal = key_row <= query

    def logits(j, p):
        rows = pl.ds(pl.multiple_of(j * tq, tq), tq)
        return _dot_nt(kaug[rows, 2 * p * LANES:(2 * p + 2) * LANES], qaug[p])

    s0[...] = logits(0, 0)

    def step(j, masked):
        rows = pl.ds(pl.multiple_of(j * tq, tq), tq)
        s_next = s0[...]
        for p in range(n_pairs):
            s = s_next
            if p + 1 < n_pairs:
                s_next = logits(j, p + 1)
            else:
                s0[...] = logits(jnp.minimum(j + 1, i), 0)
            if masked:
                s = jnp.where(causal, s, NEG)
            m_old = m_s[p]
            m_new = jnp.maximum(m_old, jnp.max(s, axis=0, keepdims=True))
            alpha = jnp.exp2(m_old - m_new)
            pr = jnp.exp2(s - m_new)
            l_s[p] = alpha * l_s[p] + jnp.sum(pr, axis=0, keepdims=True)
            m_s[p] = m_new
            acc_s[p] = alpha * acc_s[p] + _dot_tn(v_ref[rows, p * LANES:(p + 1) * LANES],
                                                  pr.astype(BF16))

    def body(j, _):
        step(j, masked=False)
        return 0

    lax.fori_loop(0, i, body, 0)
    step(i, masked=True)

    feat = lax.broadcasted_iota(I32, (LANES, tq), 0)
    for p in range(n_pairs):
        o_t = acc_s[p] / l_s[p]
        o_t = jnp.where(feat < HEAD_DIM, o_t[:, :tq], o_t[:, tq:])
        o_ref[:, p * LANES:(p + 1) * LANES] = o_t.T.astype(o_ref.dtype)


def _fox(q, k, v, cum_parts, sel, batch):
    n, w = q.shape
    seq = n // batch
    tq = min(TQ_FOX, seq)
    nq = seq // tq
    n_pairs = w // LANES
    return pl.pallas_call(
        _fox_kernel,
        grid=(batch, nq),
        in_specs=[pl.BlockSpec((tq, w), lambda b, i: (b * nq + i, 0)),
                  pl.BlockSpec((seq, w), lambda b, i: (b, 0)),
                  pl.BlockSpec((seq, w), lambda b, i: (b, 0)),
                  pl.BlockSpec((CUM_PARTS, seq, LANES), lambda b, i: (0, b, 0)),
                  _const_spec(sel.shape)],
        out_specs=pl.BlockSpec((tq, w), lambda b, i: (b * nq + i, 0)),
        out_shape=jax.ShapeDtypeStruct((n, w), BF16),
        scratch_shapes=[pltpu.VMEM((seq, 2 * w), BF16),
                        pltpu.VMEM((n_pairs, 2 * tq, 2 * LANES), BF16),
                        pltpu.VMEM((tq, 2 * tq), F32),
                        pltpu.VMEM((n_pairs, 1, 2 * tq), F32),
                        pltpu.VMEM((n_pairs, 1, 2 * tq), F32),
                        pltpu.VMEM((n_pairs, LANES, 2 * tq), F32)],
        compiler_params=_params("parallel", "arbitrary"),
        name="fox",
    )(q, k, v, cum_parts, sel)


def _cum_selector(n_pairs, first_lane):
    sel = np.zeros((n_pairs, CUM_PARTS, LANES, LANES), np.float32)
    for p in range(n_pairs):
        for j in range(CUM_PARTS):
            for e in range(2):
                sel[p, j, first_lane + 2 * p + e, CUM_PARTS * e + j] = -1.0
    return jnp.asarray(sel, BF16)


def _float_key(x):
    b = lax.bitcast_convert_type(x, I32)
    return b ^ ((b >> 31) & 0x7FFFFFFF)


def _dsa_kernel(qb_ref, qi_ref, sm_ref, ckv_ref, ki_ref, kvn_ref, wuk_ref, wuv_ref, bias_ref,
                o_ref, c_s, ki_s, keys, keys16, qim, d0, s0, m_s, l_s, acc_s, *, k_sel):
    tq = qb_ref.shape[0]
    seq = ckv_ref.shape[0]
    heads = bias_ref.shape[0]
    tk = TK_DSA
    i = pl.program_id(1)

    @pl.when(i == 0)
    def _():
        c_s[0:DSA_PAD, :] = jnp.zeros((DSA_PAD, LANES), BF16)
        ki_s[0:DSA_PAD, :] = jnp.zeros((DSA_PAD, LANES), BF16)
        c_s[DSA_PAD:, :] = _rms(ckv_ref[...].astype(F32), kvn_ref[...]).astype(BF16)
        ki_s[DSA_PAD:, :] = ki_ref[...]

    n_valid = (i + 1) * tq
    n_tiles = (n_valid + tk - 1) // tk

    def tile_rows(k):
        return pl.ds(pl.multiple_of(n_valid - (k + 1) * tk + DSA_PAD, tq), tk)

    lane = lax.broadcasted_iota(I32, (1, LANES), 1)
    for h in range(IDX_HEADS):
        qp = qi_ref[:, (h // 2) * LANES:(h // 2 + 1) * LANES]
        e = h % 2
        in_head = (lane >= e * IDX_DIM) & (lane < (e + 1) * IDX_DIM)
        qim[h * tq:(h + 1) * tq, :] = jnp.where(in_head, qp, jnp.zeros_like(qp))
    w_t = sm_ref[...].T[0:IDX_HEADS, :] * (IDX_HEADS ** -0.5 * IDX_DIM ** -0.5)
    q_chunk = (i * tq + lax.broadcasted_iota(I32, (tk, tq), 1)) >> CHUNK_SHIFT
    key_local = lax.broadcasted_iota(I32, (tk, tq), 0)

    def pair_dots(k, p):
        return _dot_nt(ki_s[tile_rows(k), :], qim[2 * p * tq:(2 * p + 2) * tq, :])

    d0[...] = pair_dots(0, 0)

    def score_tile(k, _):
        d_next = d0[...]
        sc = jnp.zeros((tk, tq), F32)
        for p in range(IDX_HEADS // 2):
            d = d_next
            if p + 1 < IDX_HEADS // 2:
                d_next = pair_dots(k, p + 1)
            else:
                d0[...] = pair_dots(jnp.minimum(k + 1, n_tiles - 1), 0)
            for e in range(2):
                sc = sc + jnp.maximum(d[:, e * tq:(e + 1) * tq], 0.0) * w_t[2 * p + e:2 * p + e + 1, :]
        pos = key_local + (n_valid - (k + 1) * tk)
        sc = jnp.where(pos >= 0, jnp.where((pos >> CHUNK_SHIFT) <= q_chunk, sc, NEG), NEG)
        key = _float_key(sc)
        keys[k] = key
        keys16[k] = (key >> 16).astype(I16)
        return 0

    lax.fori_loop(0, n_tiles, score_tile, 0)

    def count_rows(hit_fn):
        def body(k, cnt):
            return cnt + jnp.sum(hit_fn(k).reshape(tk // SUBLANES, SUBLANES, tq), axis=0)
        cnt = lax.fori_loop(0, n_tiles, body, jnp.zeros((SUBLANES, tq), I32))
        return jnp.sum(cnt, axis=0, keepdims=True)

    def count_ge(thr):
        return count_rows(lambda k: jnp.where(keys[k] >= thr, 1, 0))

    def count_ge16(thr16):
        rows16 = 2 * SUBLANES
        def body(k, cnt):
            ge = jnp.where(keys16[k] >= thr16, jnp.ones((), I16), jnp.zeros((), I16))
            return cnt + _tree_sum([ge[c * rows16:(c + 1) * rows16, :] for c in range(tk // rows16)])
        cnt = lax.fori_loop(0, n_tiles, body, jnp.zeros((rows16, tq), I16))
        return jnp.sum(cnt.astype(I32), axis=0, keepdims=True)

    half = 2 ** 15

    def kth_largest16(k_need):
        def step(b, u):
            cand = u | lax.shift_left(jnp.int32(1), 15 - b)
            return jnp.where(count_ge16((cand - half).astype(I16)) >= k_need, cand, u)
        return lax.fori_loop(0, 16, step, jnp.zeros((1, tq), I32)) - half

    t_hi = kth_largest16(k_sel)
    n_above = jnp.where(t_hi >= half - 1, 0,
                        count_ge16((jnp.minimum(t_hi, half - 2) + 1).astype(I16)))

    def low_halves(k, _):
        key = keys[k]
        keys16[k] = jnp.where((key >> 16) == t_hi, (key & 0xFFFF) - half, -half).astype(I16)
        return 0

    lax.fori_loop(0, n_tiles, low_halves, 0)
    t_lo = kth_largest16(k_sel - n_above)
    thr = jnp.maximum(lax.shift_left(t_hi, 16) + (t_lo + half), KEY_NEG + 1)

    excess = count_ge(thr) - k_sel

    @pl.when(jnp.max(excess) > 0)
    def _():
        def count_eq_below(cut):
            def hit(k):
                pos = key_local + (n_valid - (k + 1) * tk)
                return jnp.where(keys[k] == thr, jnp.where(pos < cut, 1, 0), 0)
            return count_rows(hit)

        n_bits = max(1, (2 * seq - 1).bit_length())
        keep = count_eq_below(jnp.full((1, tq), 2 ** n_bits, I32)) - jnp.maximum(excess, 0)

        def cut_step(b, cut):
            cand = cut | lax.shift_left(jnp.int32(1), n_bits - 1 - b)
            return jnp.where(count_eq_below(cand) <= keep, cand, cut)

        cut = lax.fori_loop(0, n_bits, cut_step, jnp.zeros((1, tq), I32))

        def drop(k, _):
            pos = key_local + (n_valid - (k + 1) * tk)
            kk = keys[k]
            keys[k] = jnp.where(kk == thr, jnp.where(pos >= cut, KEY_NEG, kk), kk)
            return 0

        lax.fori_loop(0, n_tiles, drop, 0)

    ql_t = (_dot_nt(wuk_ref[...], qb_ref[...]) * (HEAD_DIM ** -0.5 * LOG2E)).astype(BF16)
    n_groups = heads // DSA_HEAD_GROUP
    gw = DSA_HEAD_GROUP * tq
    ql_g = [jnp.concatenate([ql_t[h * LANES:(h + 1) * LANES, :]
                             for h in range(g * DSA_HEAD_GROUP, (g + 1) * DSA_HEAD_GROUP)], axis=1)
            for g in range(n_groups)]
    m_s[...] = jnp.full(m_s.shape, NEG, F32)
    l_s[...] = jnp.zeros(l_s.shape, F32)
    acc_s[...] = jnp.zeros(acc_s.shape, F32)

    def attend(k, near):
        ct = c_s[tile_rows(k), :]
        mask = jnp.where(keys[k] >= thr, 0.0, NEG)
        s_next = s0[...]
        for g in range(n_groups):
            s_g = s_next
            if g + 1 < n_groups:
                s_next = _dot(ct, ql_g[g + 1])
            else:
                s0[...] = _dot(c_s[tile_rows(jnp.minimum(k + 1, n_tiles - 1)), :], ql_g[0])
            parts = []
            for hh in range(DSA_HEAD_GROUP):
                s = s_g[:, hh * tq:(hh + 1) * tq]
                if near:
                    s = jnp.concatenate([s[:tk - DSA_NEAR_KEYS], s[tk - DSA_NEAR_KEYS:]
                                         + bias_ref[g * DSA_HEAD_GROUP + hh]], axis=0)
                parts.append(s + mask)
            s = jnp.concatenate(parts, axis=1)
            m_old = m_s[g]
            m_new = jnp.maximum(m_old, jnp.max(s, axis=0, keepdims=True))
            alpha = jnp.exp2(m_old - m_new)
            pr = jnp.exp2(s - m_new)
            l_s[g] = alpha * l_s[g] + jnp.sum(pr, axis=0, keepdims=True)
            m_s[g] = m_new
            acc_s[g] = alpha * acc_s[g] + _dot_tn(ct, pr.astype(BF16))

    s0[...] = _dot(c_s[tile_rows(0), :], ql_g[0])
    attend(0, near=True)

    def far(k, _):
        attend(k, near=False)
        return 0

    lax.fori_loop(1, n_tiles, far, 0)

    o_parts = []
    for g in range(n_groups):
        o_g = acc_s[g] / l_s[g]
        o_parts += [o_g[:, hh * tq:(hh + 1) * tq] for hh in range(DSA_HEAD_GROUP)]
    o_lat_t = jnp.concatenate(o_parts, axis=0).astype(BF16)
    o_ref[...] = _dot(wuv_ref[...], o_lat_t).T.astype(o_ref.dtype)


def _dsa(qb, qi, small, ckv, ki2, kvn, wuk_bd, wuv_bd, bias_near, batch):
    n, w = qb.shape
    seq = n // batch
    tq = TQ_DSA
    nq = seq // tq
    heads = bias_near.shape[0]
    n_groups = heads // DSA_HEAD_GROUP
    gw = DSA_HEAD_GROUP * tq
    k_sel = min(TOPK_MAX, seq // 4)
    blk = lambda width: pl.BlockSpec((tq, width), lambda b, i: (b * nq + i, 0))
    per_batch = pl.BlockSpec((seq, LANES), lambda b, i: (b, 0))
    return pl.pallas_call(
        functools.partial(_dsa_kernel, k_sel=k_sel),
        grid=(batch, nq),
        in_specs=[blk(w), blk(w), blk(LANES), per_batch, per_batch, _const_spec(kvn.shape),
                  _const_spec(wuk_bd.shape), _const_spec(wuv_bd.shape),
                  _const_spec(bias_near.shape)],
        out_specs=blk(w),
        out_shape=jax.ShapeDtypeStruct((n, w), BF16),
        scratch_shapes=[pltpu.VMEM((seq + DSA_PAD, LANES), BF16),
                        pltpu.VMEM((seq + DSA_PAD, LANES), BF16),
                        pltpu.VMEM((pl.cdiv(seq, TK_DSA), TK_DSA, tq), I32),
                        pltpu.VMEM((pl.cdiv(seq, TK_DSA), TK_DSA, tq), I16),
                        pltpu.VMEM((IDX_HEADS * tq, LANES), BF16),
                        pltpu.VMEM((TK_DSA, 2 * tq), F32),
                        pltpu.VMEM((TK_DSA, gw), F32),
                        pltpu.VMEM((n_groups, 1, gw), F32),
                        pltpu.VMEM((n_groups, 1, gw), F32),
                        pltpu.VMEM((n_groups, LANES, gw), F32)],
        compiler_params=_params("parallel", "arbitrary"),
        name="dsa",
    )(qb, qi, small, ckv, ki2, kvn, wuk_bd, wuv_bd, bias_near)


def _merge_kernel(x_ref, oa_ref, ob_ref, oc_ref, gate_ref, wbr_ref, wout_ref, o_ref):
    d = x_ref.shape[1]
    merged = jnp.zeros(x_ref.shape, F32)
    for j, br_ref in enumerate((oa_ref, ob_ref, oc_ref)):
        gate = jax.nn.sigmoid(gate_ref[:, j * d:(j + 1) * d].astype(F32))
        merged = merged + gate * _dot(br_ref[...], wbr_ref[j])
    o_ref[...] = x_ref[...] + _dot(merged.astype(BF16), wout_ref[...])


def _merge(x, oa, ob, oc, gates, wbr, wout):
    n, d = x.shape
    tm = min(TM_TOKENS, n)
    bw = oa.shape[1]
    blk = lambda width: pl.BlockSpec((tm, width), lambda i: (i, 0))
    return pl.pallas_call(
        _merge_kernel,
        grid=(n // tm,),
        in_specs=[blk(d), blk(bw), blk(bw), blk(bw), blk(3 * d), _const_spec(wbr.shape),
                  _const_spec(wout.shape)],
        out_specs=blk(d),
        out_shape=jax.ShapeDtypeStruct((n, d), F32),
        compiler_params=_params("parallel"),
        name="merge",
    )(x, oa, ob, oc, gates, wbr, wout)


def _mem_attn_kernel(x_ref, g_ref, wq_ref, kv_ref, wo_ref, o_ref):
    x = x_ref[...]
    hw = MEM_HEADS * MEM_HEAD_DIM
    q = _dot(_rms(x, g_ref[...]).astype(BF16), wq_ref[...]).astype(BF16)
    outs = []
    for h in range(MEM_HEADS):
        lanes = slice(h * MEM_HEAD_DIM, (h + 1) * MEM_HEAD_DIM)
        s = _dot_nt(q[:, lanes], kv_ref[:, lanes]) * (MEM_HEAD_DIM ** -0.5)
        pr = jnp.exp(s - jnp.max(s, axis=1, keepdims=True))
        pr = pr / jnp.sum(pr, axis=1, keepdims=True)
        outs.append(_dot(pr.astype(BF16), kv_ref[:, hw + h * MEM_HEAD_DIM:hw + (h + 1) * MEM_HEAD_DIM]))
    o = jnp.concatenate(outs, axis=1).astype(BF16)
    o_ref[...] = x + _dot(o, wo_ref[...])


def _mem_attn(x, g, wq, kv, wo, batch):
    n, d = x.shape
    seq = n // batch
    tm = min(TM_TOKENS, seq)
    ns = seq // tm
    mem_len = kv.shape[0] // batch
    return pl.pallas_call(
        _mem_attn_kernel,
        grid=(batch, ns),
        in_specs=[pl.BlockSpec((tm, d), lambda b, s: (b * ns + s, 0)), _const_spec((1, d)),
                  _const_spec(wq.shape),
                  pl.BlockSpec((mem_len, kv.shape[1]), lambda b, s: (b, 0)),
                  _const_spec(wo.shape)],
        out_specs=pl.BlockSpec((tm, d), lambda b, s: (b * ns + s, 0)),
        out_shape=jax.ShapeDtypeStruct((n, d), F32),
        compiler_params=_params("parallel", "arbitrary"),
        name="mem_attn",
    )(x, g, wq, kv, wo)


def _t5_bucket(rel):
    nb = NUM_BUCKETS // 2
    max_exact = nb // 2
    base = jnp.where(rel > 0, nb, 0)
    n = jnp.abs(rel)
    n_f = jnp.maximum(n, 1).astype(jnp.float32)
    large = max_exact + (jnp.log(n_f / max_exact) / math.log(MAX_DISTANCE / max_exact)
                         * (nb - max_exact)).astype(jnp.int32)
    large = jnp.minimum(large, nb - 1)
    return base + jnp.where(n < max_exact, n, large)


def _near_bias(t5_bias):
    t = jnp.arange(TQ_DSA)[:, None]
    s = jnp.arange(DSA_NEAR_KEYS)[None, :] - MAX_DISTANCE
    onehot = jax.nn.one_hot(_t5_bucket(s - t), NUM_BUCKETS, dtype=F32)
    tbl = jnp.einsum("tsb,bh->hst", onehot, (t5_bias - t5_bias[NUM_BUCKETS // 2 - 1]) * LOG2E,
                     precision=lax.Precision.HIGHEST)
    return tbl.astype(F32)


def _block_diag(w):
    nb, a, b = w.shape
    eye = jnp.eye(nb, dtype=w.dtype)
    return (eye[:, None, :, None] * w[:, :, None, :]).reshape(nb * a, nb * b)


def _pad_cols(w, width):
    return jnp.pad(w, ((0, 0), (0, width - w.shape[1])))


def kernel(x, mem, ffn1_norm, ffn1_w_gu, ffn1_w_down, mix_norm, w_in, conv_w, conv_b, rg_wa, rg_ba, rg_wx, rg_bx, rg_lambda, kv_norm, w_uk, w_uv, forget_bias, w_branch, w_out, xattn_norm, mem_norm, w_mq, w_mkv, w_mo, ffn2_norm, ffn2_w_gu, ffn2_w_down, t5_bias, final_norm):
    batch, seq, d = x.shape
    depth = w_in.shape[0]
    d_ff = ffn1_w_down.shape[1]
    dr = conv_w.shape[2]
    heads = w_uk.shape[1]
    d_lat = w_uk.shape[3]
    n = batch * seq
    assert seq % TK_DSA == 0 and d_lat == LANES and 2 * IDX_DIM == LANES and 2 * HEAD_DIM == LANES

    widths = (dr, dr, heads * HEAD_DIM, d_lat, IDX_HEADS * IDX_DIM, IDX_DIM, IDX_HEADS,
              heads * HEAD_DIM, heads * HEAD_DIM, heads * HEAD_DIM, heads, d, d, d)
    offs = np.concatenate([[0], np.cumsum(widths)])
    col = lambda w, j: w[:, int(offs[j]):int(offs[j + 1])]

    bias_near = _near_bias(t5_bias)
    cum_sel = _cum_selector(heads * HEAD_DIM // LANES, IDX_HEADS)
    row = lambda v: v.reshape(1, -1).astype(F32)

    xf = x.reshape(n, d)
    memf = mem.reshape(batch * mem.shape[1], d)
    for l in range(depth):
        wl = w_in[l]
        small_w = _pad_cols(jnp.concatenate([col(wl, 6), col(wl, 10)], axis=1), LANES)
        w_all = jnp.concatenate(
            [col(wl, 0), col(wl, 1), col(wl, 2), col(wl, 3), col(wl, 4), col(wl, 5), col(wl, 5),
             col(wl, 7) * (HEAD_DIM ** -0.5 * LOG2E),
             col(wl, 8), col(wl, 9), col(wl, 11), col(wl, 12), col(wl, 13)],
            axis=1).astype(BF16)
        w_all = jnp.concatenate([w_all, small_w.astype(BF16)], axis=1)
        out_widths = (2 * dr, heads * HEAD_DIM, d_lat, IDX_HEADS * IDX_DIM, 2 * IDX_DIM,
                      heads * HEAD_DIM, heads * HEAD_DIM, heads * HEAD_DIM, 3 * d, LANES)
        out_dtypes = (BF16,) * 9 + (F32,)

        xf = _ffn(xf, row(ffn1_norm[l]), ffn1_w_gu[l][:, :d_ff].astype(BF16),
                  ffn1_w_gu[l][:, d_ff:].astype(BF16), ffn1_w_down[l].astype(BF16),
                  row(final_norm), norm_out=False)

        rg, qb, ckv, qi, ki2, qc, kc, vc, gates, small = _norm_proj(
            xf, row(mix_norm[l]), w_all, out_widths, out_dtypes)

        wax = jnp.concatenate([_block_diag(rg_wa[l]), _block_diag(rg_wx[l])], axis=1).astype(BF16)
        bax = jnp.concatenate([rg_ba[l], rg_bx[l]]).reshape(1, -1)
        o_a = _rglru(rg, batch, conv_w[l], row(conv_b[l]), wax, bax, row(rg_lambda[l]))

        fbias = _pad_cols(jnp.concatenate([jnp.zeros((IDX_HEADS,), F32), forget_bias[l]])[None], LANES)
        o_c = _fox(qc, kc, vc, _forget_cumsum(small, batch, fbias), cum_sel, batch)

        o_b = _dsa(qb, qi, small, ckv, ki2, row(kv_norm[l]), _block_diag(w_uk[l]).T.astype(BF16),
                   _block_diag(w_uv[l]).T.astype(BF16), bias_near, batch)

        xf = _merge(xf, o_a, o_b, o_c, gates, w_branch[l].astype(BF16), w_out[l].astype(BF16))

        (kv,) = _norm_proj(memf, row(mem_norm[l]), w_mkv[l].astype(BF16),
                           (w_mkv.shape[2],), (BF16,))
        xf = _mem_attn(xf, row(xattn_norm[l]), w_mq[l].astype(BF16), kv, w_mo[l].astype(BF16), batch)

        xf = _ffn(xf, row(ffn2_norm[l]), ffn2_w_gu[l][:, :d_ff].astype(BF16),
                  ffn2_w_gu[l][:, d_ff:].astype(BF16), ffn2_w_down[l].astype(BF16),
                  row(final_norm), norm_out=(l == depth - 1))
    return xf.reshape(batch, seq, d)
```

```python
import functools
import math

import numpy as np
import jax
import jax.numpy as jnp
from jax import lax
from jax.experimental import pallas as pl
from jax.experimental.pallas import tpu as pltpu

F32 = jnp.float32
BF16 = jnp.bfloat16
I32 = jnp.int32
I16 = jnp.int16

LANES = 128
SUBLANES = 8
VMEM_LIMIT_BYTES = 56 * 1024 * 1024

EPS = 1e-6
NEG = -1e30
LOG2E = math.log2(math.e)
FAST_GAP = 100.0
CHUNK = 64
CHUNK_SHIFT = CHUNK.bit_length() - 1
HEAD_DIM = 64
RG_C = 8.0
CONV_W = 4
IDX_HEADS = 8
IDX_DIM = 64
TOPK_MAX = 256
MEM_HEADS = 4
MEM_HEAD_DIM = 128
NUM_BUCKETS = 32
MAX_DISTANCE = 128

TM_TOKENS = 512
FFN_CHUNK = 256
PROJ_CHUNK = 1024
TS_SCAN = 256
TQ_FOX = 512
CUM_PARTS = 3
TQ_DSA = 256
DSA_NEAR_KEYS = TQ_DSA + MAX_DISTANCE
TK_DSA = 512
DSA_HEAD_GROUP = 2
DSA_PAD = TK_DSA - TQ_DSA


def _key_of_float(v):
    b = int(np.float32(v).view(np.int32))
    return b ^ ((b >> 31) & 0x7FFFFFFF)


KEY_NEG = _key_of_float(NEG)


def _params(*semantics):
    return pltpu.CompilerParams(dimension_semantics=semantics, vmem_limit_bytes=VMEM_LIMIT_BYTES)


def _const_spec(shape):
    nd = len(shape)
    return pl.BlockSpec(shape, lambda *_: (0,) * nd, pipeline_mode=pl.Buffered(1))


def _rms(x32, g):
    ms = jnp.mean(x32 * x32, axis=-1, keepdims=True)
    return x32 * lax.rsqrt(ms + EPS) * g


def _dot(a, b):
    return jnp.dot(a, b, preferred_element_type=F32)


def _dot_nt(a, b):
    return lax.dot_general(a, b, (((1,), (1,)), ((), ())), preferred_element_type=F32)


def _tree_sum(xs):
    while len(xs) > 1:
        xs = [xs[j] + xs[j + 1] for j in range(0, len(xs) - 1, 2)] + ([xs[-1]] if len(xs) % 2 else [])
    return xs[0]


def _dot_tn(a, b):
    return lax.dot_general(a, b, (((0,), (0,)), ((), ())), preferred_element_type=F32)


def _ffn_kernel(x_ref, g_ref, wg_ref, wu_ref, wd_ref, gout_ref, o_ref, a_ref, *, norm_out):
    x = x_ref[...]
    h = _rms(x, g_ref[...]).astype(BF16)
    d_ff = wg_ref.shape[1]
    for f0 in range(0, d_ff, FFN_CHUNK):
        g = _dot(h, wg_ref[:, f0:f0 + FFN_CHUNK])
        u = _dot(h, wu_ref[:, f0:f0 + FFN_CHUNK])
        a_ref[:, f0:f0 + FFN_CHUNK] = (g * jax.nn.sigmoid(g) * u).astype(BF16)
    y = x + 0.5 * _dot(a_ref[...], wd_ref[...])
    o_ref[...] = _rms(y, gout_ref[...]) if norm_out else y


def _ffn(x, g, wg, wu, wd, g_out, norm_out):
    n, d = x.shape
    d_ff = wg.shape[1]
    tm = min(TM_TOKENS, n)
    return pl.pallas_call(
        functools.partial(_ffn_kernel, norm_out=norm_out),
        grid=(n // tm,),
        in_specs=[pl.BlockSpec((tm, d), lambda i: (i, 0)),
                  _const_spec((1, d)), _const_spec((d, d_ff)), _const_spec((d, d_ff)),
                  _const_spec((d_ff, d)), _const_spec((1, d))],
        out_specs=pl.BlockSpec((tm, d), lambda i: (i, 0)),
        out_shape=jax.ShapeDtypeStruct((n, d), F32),
        scratch_shapes=[pltpu.VMEM((tm, d_ff), BF16)],
        compiler_params=_params("parallel"),
        name="ffn",
    )(x, g, wg, wu, wd, g_out)


def _norm_proj_kernel(x_ref, g_ref, w_ref, *o_refs):
    h = _rms(x_ref[...].astype(F32), g_ref[...]).astype(BF16)
    c0 = 0
    for o_ref in o_refs:
        width = o_ref.shape[1]
        for s0 in range(0, width, PROJ_CHUNK):
            s1 = min(s0 + PROJ_CHUNK, width)
            o_ref[:, s0:s1] = _dot(h, w_ref[:, c0 + s0:c0 + s1]).astype(o_ref.dtype)
        c0 += width


def _norm_proj(x, g, w, widths, dtypes):
    n, d = x.shape
    tm = min(TM_TOKENS, n)
    assert sum(widths) == w.shape[1]
    return pl.pallas_call(
        _norm_proj_kernel,
        grid=(n // tm,),
        in_specs=[pl.BlockSpec((tm, d), lambda i: (i, 0)), _const_spec((1, d)),
                  _const_spec(w.shape)],
        out_specs=[pl.BlockSpec((tm, wd), lambda i: (i, 0)) for wd in widths],
        out_shape=[jax.ShapeDtypeStruct((n, wd), dt) for wd, dt in zip(widths, dtypes)],
        compiler_params=_params("parallel"),
        name="norm_proj",
    )(x, g, w)


def _gelu_tanh(x):
    return 0.5 * x * (1.0 + jnp.tanh(math.sqrt(2.0 / math.pi) * (x + 0.044715 * (x * x * x))))


def _softplus(x):
    return jnp.maximum(x, 0.0) + jnp.log1p(jnp.exp(-jnp.abs(x)))


def _rglru_kernel(rg_ref, cw_ref, cb_ref, wax_ref, bax_ref, lam_ref, o_ref, xbuf, hc):
    ts = rg_ref.shape[0]
    dr = o_ref.shape[1]

    @pl.when(pl.program_id(1) == 0)
    def _():
        xbuf[0:8, :] = jnp.zeros((8, dr), F32)
        hc[...] = jnp.zeros_like(hc)

    xr = rg_ref[:, :dr].astype(F32)
    gr = rg_ref[:, dr:].astype(F32)
    xbuf[8:, :] = xr
    xc = cb_ref[...] + cw_ref[3:4, :] * xr
    for j in range(CONV_W - 1):
        xc = xc + cw_ref[j:j + 1, :] * xbuf[pl.ds(5 + j, ts), :]
    xbuf[0:8, :] = xbuf[ts:ts + 8, :]

    ax = _dot(xc.astype(BF16), wax_ref[...]) + bax_ref[...]
    r = jax.nn.sigmoid(ax[:, :dr])
    gi = jax.nn.sigmoid(ax[:, dr:])
    log_a = (-RG_C) * r * _softplus(-lam_ref[...])
    a = jnp.exp(log_a)
    th = jnp.tanh(log_a)
    u = jnp.sqrt(-2.0 * th / (1.0 - th)) * (gi * xc)

    row = lax.broadcasted_iota(I32, (ts, dr), 0)
    d = 1
    while d < ts:
        a_sh = pltpu.roll(a, d, 0)
        u_sh = pltpu.roll(u, d, 0)
        keep = row >= d
        u = jnp.where(keep, a * u_sh + u, u)
        a = jnp.where(keep, a * a_sh, a)
        d *= 2
    h = u + a * hc[0:1, :]
    hc[...] = jnp.broadcast_to(h[ts - 1:ts, :], hc.shape)
    o_ref[...] = (h * _gelu_tanh(gr)).astype(o_ref.dtype)


def _rglru(rg, batch, conv_w, conv_b, wax, bax, lam):
    n, two_dr = rg.shape
    dr = two_dr // 2
    seq = n // batch
    ts = min(TS_SCAN, seq)
    ns = seq // ts
    return pl.pallas_call(
        _rglru_kernel,
        grid=(batch, ns),
        in_specs=[pl.BlockSpec((ts, two_dr), lambda b, s: (b * ns + s, 0)),
                  _const_spec(conv_w.shape), _const_spec(conv_b.shape), _const_spec(wax.shape),
                  _const_spec(bax.shape), _const_spec(lam.shape)],
        out_specs=pl.BlockSpec((ts, dr), lambda b, s: (b * ns + s, 0)),
        out_shape=jax.ShapeDtypeStruct((n, dr), BF16),
        scratch_shapes=[pltpu.VMEM((ts + 8, dr), F32), pltpu.VMEM((8, dr), F32)],
        compiler_params=_params("parallel", "arbitrary"),
        name="rglru",
    )(rg, conv_w, conv_b, wax, bax, lam)


def _forget_cumsum_kernel(f_ref, b_ref, o_ref):
    z = f_ref[...] + b_ref[...]
    x = -_softplus(-z)
    seq = x.shape[0]
    row = lax.broadcasted_iota(I32, x.shape, 0)
    d = 1
    while d < seq:
        x = jnp.where(row >= d, x + pltpu.roll(x, d, 0), x)
        d *= 2
    x = x * LOG2E
    for j in range(CUM_PARTS):
        part = x.astype(BF16)
        o_ref[j] = part
        x = x - part.astype(F32)


def _forget_cumsum(small, batch, fbias):
    n, w = small.shape
    seq = n // batch
    return pl.pallas_call(
        _forget_cumsum_kernel,
        grid=(batch,),
        in_specs=[pl.BlockSpec((seq, w), lambda b: (b, 0)), _const_spec((1, w))],
        out_specs=pl.BlockSpec((CUM_PARTS, seq, w), lambda b: (0, b, 0)),
        out_shape=jax.ShapeDtypeStruct((CUM_PARTS, n, w), BF16),
        compiler_params=_params("parallel"),
        name="forget_cumsum",
    )(small, fbias)


def _split_bf16(x, n):
    parts = []
    for _ in range(n):
        part = x.astype(BF16).astype(F32)
        parts.append(part)
        x = x - part
    return parts


def _fox_kernel(q_ref, k_ref, v_ref, cp_ref, sel_ref, o_ref, kaug, qaug, s0, kmax, m_s, l_s, acc_s):
    tq = q_ref.shape[0]
    n_pairs = q_ref.shape[1] // LANES
    i = pl.program_id(1)
    lane = lax.broadcasted_iota(I32, (1, LANES), 1)
    bound_lanes = 2 * CUM_PARTS

    @pl.when(i == 0)
    def _():
        ones = jnp.where((lane >= bound_lanes) & (lane < bound_lanes + CUM_PARTS), 1.0, 0.0)
        for p in range(n_pairs):
            kp = k_ref[:, p * LANES:(p + 1) * LANES]
            kaug[:, 2 * p * LANES:(2 * p + 1) * LANES] = kp
            extra = _dot(cp_ref[0], sel_ref[p, 0])
            for j in range(1, CUM_PARTS):
                extra = extra + _dot(cp_ref[j], sel_ref[p, j])
            kaug[:, (2 * p + 1) * LANES:(2 * p + 2) * LANES] = (extra + ones).astype(BF16)
            k2 = kp.astype(F32) * kp.astype(F32)
            for e in range(2):
                in_head = (lane >= e * HEAD_DIM) & (lane < (e + 1) * HEAD_DIM)
                n2 = jnp.sum(jnp.where(in_head, k2, 0.0), axis=1, keepdims=True)
                kmax[2 * p + e] = jnp.broadcast_to(jnp.sqrt(jnp.max(n2, axis=0, keepdims=True)),
                                                   (1, LANES))

    q_rows = pl.ds(pl.multiple_of(i * tq, tq), tq)
    f_t = cp_ref[0, q_rows, :].astype(F32)
    for j in range(1, CUM_PARTS):
        f_t = f_t + cp_ref[j, q_rows, :].astype(F32)

    slack = jnp.zeros((tq, 1), F32)
    for p in range(n_pairs):
        qp = q_ref[:, p * LANES:(p + 1) * LANES]
        for e in range(2):
            h = 2 * p + e
            rows = slice(e * tq, (e + 1) * tq)
            in_head = (lane >= e * HEAD_DIM) & (lane < (e + 1) * HEAD_DIM)
            qh = jnp.where(in_head, qp, jnp.zeros_like(qp))
            qaug[p, rows, 0:LANES] = qh
            qf = qh.astype(F32)
            reach = jnp.sqrt(jnp.sum(qf * qf, axis=1, keepdims=True)) * kmax[h][:, 0:1]
            slack = jnp.maximum(slack, reach)
            bound = reach - f_t[:, IDX_HEADS + h:IDX_HEADS + h + 1]
            extra = jnp.where((lane >= CUM_PARTS * e) & (lane < CUM_PARTS * (e + 1)), 1.0, 0.0)
            for j, part in enumerate(_split_bf16(-bound, CUM_PARTS)):
                extra = jnp.where(lane == bound_lanes + j, part, extra)
            qaug[p, rows, LANES:2 * LANES] = extra.astype(BF16)
    fast = 2.0 * jnp.max(slack) < FAST_GAP

    l_s[...] = jnp.zeros(l_s.shape, F32)
    acc_s[...] = jnp.zeros(acc_s.shape, F32)
    key_row = lax.broadcasted_iota(I32, (tq, 2 * tq), 0)
    query = lax.broadcasted_iota(I32, (tq, 2 * tq), 1) & (tq - 1)
    causal = key_row <= query

    def logits(j, p):
        rows = pl.ds(pl.multiple_of(j * tq, tq), tq)
        return _dot_nt(kaug[rows, 2 * p * LANES:(2 * p + 2) * LANES], qaug[p])

    s0[...] = logits(0, 0)

    def step(j, masked, online):
        rows = pl.ds(pl.multiple_of(j * tq, tq), tq)
        s_next = s0[...]
        for p in range(n_pairs):
            s = s_next
            if p + 1 < n_pairs:
                s_next = logits(j, p + 1)
            else:
                s0[...] = logits(jnp.minimum(j + 1, i), 0)
            if masked:
                s = jnp.where(causal, s, NEG)
            vt = v_ref[rows, p * LANES:(p + 1) * LANES]
            if online:
                m_old = m_s[p]
                m_new = jnp.maximum(m_old, jnp.max(s, axis=0, keepdims=True))
                alpha = jnp.exp2(m_old - m_new)
                pr = jnp.exp2(s - m_new)
                l_s[p] = alpha * l_s[p] + jnp.sum(pr, axis=0, keepdims=True)
                m_s[p] = m_new
                acc_s[p] = alpha * acc_s[p] + _dot_tn(vt, pr.astype(BF16))
            else:
                pr = jnp.exp2(s)
                l_s[p] = l_s[p] + jnp.sum(pr, axis=0, keepdims=True)
                acc_s[p] = acc_s[p] + _dot_tn(vt, pr.astype(BF16))

    def run(online):
        def body(j, _):
            step(j, masked=False, online=online)
            return 0
        lax.fori_loop(0, i, body, 0)
        step(i, masked=True, online=online)

    @pl.when(fast)
    def _():
        run(online=False)

    @pl.when(jnp.logical_not(fast))
    def _():
        m_s[...] = jnp.full(m_s.shape, NEG, F32)
        run(online=True)

    feat = lax.broadcasted_iota(I32, (LANES, tq), 0)
    for p in range(n_pairs):
        o_t = acc_s[p] / l_s[p]
        o_t = jnp.where(feat < HEAD_DIM, o_t[:, :tq], o_t[:, tq:])
        o_ref[:, p * LANES:(p + 1) * LANES] = o_t.T.astype(o_ref.dtype)


def _fox(q, k, v, cum_parts, sel, batch):
    n, w = q.shape
    seq = n // batch
    tq = min(TQ_FOX, seq)
    nq = seq // tq
    n_pairs = w // LANES
    return pl.pallas_call(
        _fox_kernel,
        grid=(batch, nq),
        in_specs=[pl.BlockSpec((tq, w), lambda b, i: (b * nq + i, 0)),
                  pl.BlockSpec((seq, w), lambda b, i: (b, 0)),
                  pl.BlockSpec((seq, w), lambda b, i: (b, 0)),
                  pl.BlockSpec((CUM_PARTS, seq, LANES), lambda b, i: (0, b, 0)),
                  _const_spec(sel.shape)],
        out_specs=pl.BlockSpec((tq, w), lambda b, i: (b * nq + i, 0)),
        out_shape=jax.ShapeDtypeStruct((n, w), BF16),
        scratch_shapes=[pltpu.VMEM((seq, 2 * w), BF16),
                        pltpu.VMEM((n_pairs, 2 * tq, 2 * LANES), BF16),
                        pltpu.VMEM((tq, 2 * tq), F32),
                        pltpu.VMEM((2 * n_pairs, 1, LANES), F32),
                        pltpu.VMEM((n_pairs, 1, 2 * tq), F32),
                        pltpu.VMEM((n_pairs, 1, 2 * tq), F32),
                        pltpu.VMEM((n_pairs, LANES, 2 * tq), F32)],
        compiler_params=_params("parallel", "arbitrary"),
        name="fox",
    )(q, k, v, cum_parts, sel)


def _cum_selector(n_pairs, first_lane):
    sel = np.zeros((n_pairs, CUM_PARTS, LANES, LANES), np.float32)
    for p in range(n_pairs):
        for j in range(CUM_PARTS):
            for e in range(2):
                sel[p, j, first_lane + 2 * p + e, CUM_PARTS * e + j] = -1.0
    return jnp.asarray(sel, BF16)


def _float_key(x):
    b = lax.bitcast_convert_type(x, I32)
    return b ^ ((b >> 31) & 0x7FFFFFFF)


def _dsa_kernel(qb_ref, qi_ref, sm_ref, ckv_ref, ki_ref, kvn_ref, wuk_ref, wuv_ref, bias_ref, brng_ref,
                o_ref, c_s, ki_s, keys, keys16, qim, d0, s0, cmax, m_s, l_s, acc_s, *, k_sel):
    tq = qb_ref.shape[0]
    seq = ckv_ref.shape[0]
    heads = bias_ref.shape[0]
    tk = TK_DSA
    i = pl.program_id(1)
    lane = lax.broadcasted_iota(I32, (1, LANES), 1)

    @pl.when(i == 0)
    def _():
        c_s[0:DSA_PAD, :] = jnp.zeros((DSA_PAD, 2 * LANES), BF16)
        ki_s[0:DSA_PAD, :] = jnp.zeros((DSA_PAD, LANES), BF16)
        c = _rms(ckv_ref[...].astype(F32), kvn_ref[...]).astype(BF16)
        c_s[DSA_PAD:, 0:LANES] = c
        c_s[DSA_PAD:, LANES:] = jnp.broadcast_to(jnp.where(lane < CUM_PARTS, 1.0, 0.0),
                                                 (seq, LANES)).astype(BF16)
        ki_s[DSA_PAD:, :] = ki_ref[...]
        cf = c.astype(F32)
        n2 = jnp.max(jnp.sum(cf * cf, axis=1, keepdims=True), axis=0, keepdims=True)
        cmax[...] = jnp.broadcast_to(jnp.sqrt(n2), cmax.shape)

    n_valid = (i + 1) * tq
    n_tiles = (n_valid + tk - 1) // tk

    def tile_rows(k):
        return pl.ds(pl.multiple_of(n_valid - (k + 1) * tk + DSA_PAD, tq), tk)

    lane = lax.broadcasted_iota(I32, (1, LANES), 1)
    for h in range(IDX_HEADS):
        qp = qi_ref[:, (h // 2) * LANES:(h // 2 + 1) * LANES]
        e = h % 2
        in_head = (lane >= e * IDX_DIM) & (lane < (e + 1) * IDX_DIM)
        qim[h * tq:(h + 1) * tq, :] = jnp.where(in_head, qp, jnp.zeros_like(qp))
    w_t = sm_ref[...].T[0:IDX_HEADS, :] * (IDX_HEADS ** -0.5 * IDX_DIM ** -0.5)
    q_chunk = (i * tq + lax.broadcasted_iota(I32, (tk, tq), 1)) >> CHUNK_SHIFT
    key_local = lax.broadcasted_iota(I32, (tk, tq), 0)

    def pair_dots(k, p):
        return _dot_nt(ki_s[tile_rows(k), :], qim[2 * p * tq:(2 * p + 2) * tq, :])

    d0[...] = pair_dots(0, 0)

    def score_tile(k, _):
        d_next = d0[...]
        sc = jnp.zeros((tk, tq), F32)
        for p in range(IDX_HEADS // 2):
            d = d_next
            if p + 1 < IDX_HEADS // 2:
                d_next = pair_dots(k, p + 1)
            else:
                d0[...] = pair_dots(jnp.minimum(k + 1, n_tiles - 1), 0)
            for e in range(2):
                sc = sc + jnp.maximum(d[:, e * tq:(e + 1) * tq], 0.0) * w_t[2 * p + e:2 * p + e + 1, :]
        pos = key_local + (n_valid - (k + 1) * tk)
        sc = jnp.where(pos >= 0, jnp.where((pos >> CHUNK_SHIFT) <= q_chunk, sc, NEG), NEG)
        key = _float_key(sc)
        keys[k] = key
        keys16[k] = (key >> 16).astype(I16)
        return 0

    lax.fori_loop(0, n_tiles, score_tile, 0)

    def count_rows(hit_fn):
        def body(k, cnt):
            return cnt + jnp.sum(hit_fn(k).reshape(tk // SUBLANES, SUBLANES, tq), axis=0)
        cnt = lax.fori_loop(0, n_tiles, body, jnp.zeros((SUBLANES, tq), I32))
        return jnp.sum(cnt, axis=0, keepdims=True)

    def count_ge(thr):
        return count_rows(lambda k: jnp.where(keys[k] >= thr, 1, 0))

    def count_ge16(thr16):
        rows16 = 2 * SUBLANES
        def body(k, cnt):
            ge = jnp.where(keys16[k] >= thr16, jnp.ones((), I16), jnp.zeros((), I16))
            return cnt + _tree_sum([ge[c * rows16:(c + 1) * rows16, :] for c in range(tk // rows16)])
        cnt = lax.fori_loop(0, n_tiles, body, jnp.zeros((rows16, tq), I16))
        return jnp.sum(cnt.astype(I32), axis=0, keepdims=True)

    half = 2 ** 15

    def kth_largest16(k_need):
        def step(b, u):
            cand = u | lax.shift_left(jnp.int32(1), 15 - b)
            return jnp.where(count_ge16((cand - half).astype(I16)) >= k_need, cand, u)
        return lax.fori_loop(0, 16, step, jnp.zeros((1, tq), I32)) - half

    t_hi = kth_largest16(k_sel)
    n_above = jnp.where(t_hi >= half - 1, 0,
                        count_ge16((jnp.minimum(t_hi, half - 2) + 1).astype(I16)))

    def low_halves(k, _):
        key = keys[k]
        keys16[k] = jnp.where((key >> 16) == t_hi, (key & 0xFFFF) - half, -half).astype(I16)
        return 0

    lax.fori_loop(0, n_tiles, low_halves, 0)
    t_lo = kth_largest16(k_sel - n_above)
    thr = jnp.maximum(lax.shift_left(t_hi, 16) + (t_lo + half), KEY_NEG + 1)

    excess = count_ge(thr) - k_sel

    @pl.when(jnp.max(excess) > 0)
    def _():
        def count_eq_below(cut):
            def hit(k):
                pos = key_local + (n_valid - (k + 1) * tk)
                return jnp.where(keys[k] == thr, jnp.where(pos < cut, 1, 0), 0)
            return count_rows(hit)

        n_bits = max(1, (2 * seq - 1).bit_length())
        keep = count_eq_below(jnp.full((1, tq), 2 ** n_bits, I32)) - jnp.maximum(excess, 0)

        def cut_step(b, cut):
            cand = cut | lax.shift_left(jnp.int32(1), n_bits - 1 - b)
            return jnp.where(count_eq_below(cand) <= keep, cand, cut)

        cut = lax.fori_loop(0, n_bits, cut_step, jnp.zeros((1, tq), I32))

        def drop(k, _):
            pos = key_local + (n_valid - (k + 1) * tk)
            kk = keys[k]
            keys[k] = jnp.where(kk == thr, jnp.where(pos >= cut, KEY_NEG, kk), kk)
            return 0

        lax.fori_loop(0, n_tiles, drop, 0)

    ql_t = (_dot_nt(wuk_ref[...], qb_ref[...]) * (HEAD_DIM ** -0.5 * LOG2E)).astype(BF16)
    n_groups = heads // DSA_HEAD_GROUP
    gw = DSA_HEAD_GROUP * tq
    slack = jnp.zeros((1, tq), F32)
    row_id = lax.broadcasted_iota(I32, (LANES, tq), 0)
    ql_aug = []
    for h in range(heads):
        ql_h = ql_t[h * LANES:(h + 1) * LANES, :]
        qf = ql_h.astype(F32)
        reach = jnp.sqrt(jnp.sum(qf * qf, axis=0, keepdims=True)) * cmax[:, 0:1]
        slack = jnp.maximum(slack, 2.0 * reach + (brng_ref[0, h][:, 0:1] - brng_ref[1, h][:, 0:1]))
        bound_rows = jnp.zeros((LANES, tq), F32)
        for j, part in enumerate(_split_bf16(-(reach + brng_ref[0, h][:, 0:1]), CUM_PARTS)):
            bound_rows = jnp.where(row_id == j, part, bound_rows)
        ql_aug.append(jnp.concatenate([ql_h, bound_rows.astype(BF16)], axis=0))
    ql_g = [jnp.concatenate(ql_aug[g * DSA_HEAD_GROUP:(g + 1) * DSA_HEAD_GROUP], axis=1)
            for g in range(n_groups)]
    fast = jnp.max(slack) < FAST_GAP
    l_s[...] = jnp.zeros(l_s.shape, F32)
    acc_s[...] = jnp.zeros(acc_s.shape, F32)

    def attend(k, near, online):
        ct_aug = c_s[tile_rows(k), :]
        ct = c_s[tile_rows(k), 0:LANES]
        mask = jnp.where(keys[k] >= thr, 0.0, NEG)
        s_next = s0[...]
        for g in range(n_groups):
            s_g = s_next
            if g + 1 < n_groups:
                s_next = _dot(ct_aug, ql_g[g + 1])
            else:
                s0[...] = _dot(c_s[tile_rows(jnp.minimum(k + 1, n_tiles - 1)), :], ql_g[0])
            parts = []
            for hh in range(DSA_HEAD_GROUP):
                s = s_g[:, hh * tq:(hh + 1) * tq]
                if near:
                    s = jnp.concatenate([s[:tk - DSA_NEAR_KEYS], s[tk - DSA_NEAR_KEYS:]
                                         + bias_ref[g * DSA_HEAD_GROUP + hh]], axis=0)
                parts.append(s + mask)
            s = jnp.concatenate(parts, axis=1)
            if online:
                m_old = m_s[g]
                m_new = jnp.maximum(m_old, jnp.max(s, axis=0, keepdims=True))
                alpha = jnp.exp2(m_old - m_new)
                pr = jnp.exp2(s - m_new)
                l_s[g] = alpha * l_s[g] + jnp.sum(pr, axis=0, keepdims=True)
                m_s[g] = m_new
                acc_s[g] = alpha * acc_s[g] + _dot_tn(ct, pr.astype(BF16))
            else:
                pr = jnp.exp2(s)
                l_s[g] = l_s[g] + jnp.sum(pr, axis=0, keepdims=True)
                acc_s[g] = acc_s[g] + _dot_tn(ct, pr.astype(BF16))

    s0[...] = _dot(c_s[tile_rows(0), :], ql_g[0])

    def run(online):
        attend(0, near=True, online=online)

        def far(k, _):
            attend(k, near=False, online=online)
            return 0

        lax.fori_loop(1, n_tiles, far, 0)

    @pl.when(fast)
    def _():
        run(online=False)

    @pl.when(jnp.logical_not(fast))
    def _():
        m_s[...] = jnp.full(m_s.shape, NEG, F32)
        run(online=True)

    o_parts = []
    for g in range(n_groups):
        o_g = acc_s[g] / l_s[g]
        o_parts += [o_g[:, hh * tq:(hh + 1) * tq] for hh in range(DSA_HEAD_GROUP)]
    o_lat_t = jnp.concatenate(o_parts, axis=0).astype(BF16)
    o_ref[...] = _dot(wuv_ref[...], o_lat_t).T.astype(o_ref.dtype)


def _dsa(qb, qi, small, ckv, ki2, kvn, wuk_bd, wuv_bd, bias_near, batch):
    n, w = qb.shape
    seq = n // batch
    tq = TQ_DSA
    nq = seq // tq
    heads = bias_near.shape[0]
    n_groups = heads // DSA_HEAD_GROUP
    gw = DSA_HEAD_GROUP * tq
    k_sel = min(TOPK_MAX, seq // 4)
    blk = lambda width: pl.BlockSpec((tq, width), lambda b, i: (b * nq + i, 0))
    per_batch = pl.BlockSpec((seq, LANES), lambda b, i: (b, 0))
    bias_rng = jnp.stack([jnp.maximum(jnp.max(bias_near, axis=(1, 2)), 0.0),
                          jnp.minimum(jnp.min(bias_near, axis=(1, 2)), 0.0)])
    bias_rng = jnp.broadcast_to(bias_rng[:, :, None, None], (2, heads, 1, LANES))
    return pl.pallas_call(
        functools.partial(_dsa_kernel, k_sel=k_sel),
        grid=(batch, nq),
        in_specs=[blk(w), blk(w), blk(LANES), per_batch, per_batch, _const_spec(kvn.shape),
                  _const_spec(wuk_bd.shape), _const_spec(wuv_bd.shape),
                  _const_spec(bias_near.shape), _const_spec(bias_rng.shape)],
        out_specs=blk(w),
        out_shape=jax.ShapeDtypeStruct((n, w), BF16),
        scratch_shapes=[pltpu.VMEM((seq + DSA_PAD, 2 * LANES), BF16),
                        pltpu.VMEM((seq + DSA_PAD, LANES), BF16),
                        pltpu.VMEM((pl.cdiv(seq, TK_DSA), TK_DSA, tq), I32),
                        pltpu.VMEM((pl.cdiv(seq, TK_DSA), TK_DSA, tq), I16),
                        pltpu.VMEM((IDX_HEADS * tq, LANES), BF16),
                        pltpu.VMEM((TK_DSA, 2 * tq), F32),
                        pltpu.VMEM((TK_DSA, gw), F32),
                        pltpu.VMEM((1, LANES), F32),
                        pltpu.VMEM((n_groups, 1, gw), F32),
                        pltpu.VMEM((n_groups, 1, gw), F32),
                        pltpu.VMEM((n_groups, LANES, gw), F32)],
        compiler_params=_params("parallel", "arbitrary"),
        name="dsa",
    )(qb, qi, small, ckv, ki2, kvn, wuk_bd, wuv_bd, bias_near, bias_rng)


def _merge_kernel(x_ref, oa_ref, ob_ref, oc_ref, gate_ref, wbr_ref, wout_ref, o_ref):
    d = x_ref.shape[1]
    merged = jnp.zeros(x_ref.shape, F32)
    for j, br_ref in enumerate((oa_ref, ob_ref, oc_ref)):
        gate = jax.nn.sigmoid(gate_ref[:, j * d:(j + 1) * d].astype(F32))
        merged = merged + gate * _dot(br_ref[...], wbr_ref[j])
    o_ref[...] = x_ref[...] + _dot(merged.astype(BF16), wout_ref[...])


def _merge(x, oa, ob, oc, gates, wbr, wout):
    n, d = x.shape
    tm = min(TM_TOKENS, n)
    bw = oa.shape[1]
    blk = lambda width: pl.BlockSpec((tm, width), lambda i: (i, 0))
    return pl.pallas_call(
        _merge_kernel,
        grid=(n // tm,),
        in_specs=[blk(d), blk(bw), blk(bw), blk(bw), blk(3 * d), _const_spec(wbr.shape),
                  _const_spec(wout.shape)],
        out_specs=blk(d),
        out_shape=jax.ShapeDtypeStruct((n, d), F32),
        compiler_params=_params("parallel"),
        name="merge",
    )(x, oa, ob, oc, gates, wbr, wout)


def _mem_attn_kernel(x_ref, g_ref, wq_ref, kv_ref, wo_ref, o_ref):
    x = x_ref[...]
    hw = MEM_HEADS * MEM_HEAD_DIM
    q = _dot(_rms(x, g_ref[...]).astype(BF16), wq_ref[...]).astype(BF16)
    outs = []
    for h in range(MEM_HEADS):
        lanes = slice(h * MEM_HEAD_DIM, (h + 1) * MEM_HEAD_DIM)
        s = _dot_nt(q[:, lanes], kv_ref[:, lanes]) * (MEM_HEAD_DIM ** -0.5)
        pr = jnp.exp(s - jnp.max(s, axis=1, keepdims=True))
        pr = pr / jnp.sum(pr, axis=1, keepdims=True)
        outs.append(_dot(pr.astype(BF16), kv_ref[:, hw + h * MEM_HEAD_DIM:hw + (h + 1) * MEM_HEAD_DIM]))
    o = jnp.concatenate(outs, axis=1).astype(BF16)
    o_ref[...] = x + _dot(o, wo_ref[...])


def _mem_attn(x, g, wq, kv, wo, batch):
    n, d = x.shape
    seq = n // batch
    tm = min(TM_TOKENS, seq)
    ns = seq // tm
    mem_len = kv.shape[0] // batch
    return pl.pallas_call(
        _mem_attn_kernel,
        grid=(batch, ns),
        in_specs=[pl.BlockSpec((tm, d), lambda b, s: (b * ns + s, 0)), _const_spec((1, d)),
                  _const_spec(wq.shape),
                  pl.BlockSpec((mem_len, kv.shape[1]), lambda b, s: (b, 0)),
                  _const_spec(wo.shape)],
        out_specs=pl.BlockSpec((tm, d), lambda b, s: (b * ns + s, 0)),
        out_shape=jax.ShapeDtypeStruct((n, d), F32),
        compiler_params=_params("parallel", "arbitrary"),
        name="mem_attn",
    )(x, g, wq, kv, wo)


def _t5_bucket(rel):
    nb = NUM_BUCKETS // 2
    max_exact = nb // 2
    base = jnp.where(rel > 0, nb, 0)
    n = jnp.abs(rel)
    n_f = jnp.maximum(n, 1).astype(jnp.float32)
    large = max_exact + (jnp.log(n_f / max_exact) / math.log(MAX_DISTANCE / max_exact)
                         * (nb - max_exact)).astype(jnp.int32)
    large = jnp.minimum(large, nb - 1)
    return base + jnp.where(n < max_exact, n, large)


def _near_bias(t5_bias):
    t = jnp.arange(TQ_DSA)[:, None]
    s = jnp.arange(DSA_NEAR_KEYS)[None, :] - MAX_DISTANCE
    onehot = jax.nn.one_hot(_t5_bucket(s - t), NUM_BUCKETS, dtype=F32)
    tbl = jnp.einsum("tsb,bh->hst", onehot, (t5_bias - t5_bias[NUM_BUCKETS // 2 - 1]) * LOG2E,
                     precision=lax.Precision.HIGHEST)
    return tbl.astype(F32)


def _block_diag(w):
    nb, a, b = w.shape
    eye = jnp.eye(nb, dtype=w.dtype)
    return (eye[:, None, :, None] * w[:, :, None, :]).reshape(nb * a, nb * b)


def _pad_cols(w, width):
    return jnp.pad(w, ((0, 0), (0, width - w.shape[1])))


def kernel(x, mem, ffn1_norm, ffn1_w_gu, ffn1_w_down, mix_norm, w_in, conv_w, conv_b, rg_wa, rg_ba, rg_wx, rg_bx, rg_lambda, kv_norm, w_uk, w_uv, forget_bias, w_branch, w_out, xattn_norm, mem_norm, w_mq, w_mkv, w_mo, ffn2_norm, ffn2_w_gu, ffn2_w_down, t5_bias, final_norm):
    batch, seq, d = x.shape
    depth = w_in.shape[0]
    d_ff = ffn1_w_down.shape[1]
    dr = conv_w.shape[2]
    heads = w_uk.shape[1]
    d_lat = w_uk.shape[3]
    n = batch * seq
    assert seq % TK_DSA == 0 and d_lat == LANES and 2 * IDX_DIM == LANES and 2 * HEAD_DIM == LANES

    widths = (dr, dr, heads * HEAD_DIM, d_lat, IDX_HEADS * IDX_DIM, IDX_DIM, IDX_HEADS,
              heads * HEAD_DIM, heads * HEAD_DIM, heads * HEAD_DIM, heads, d, d, d)
    offs = np.concatenate([[0], np.cumsum(widths)])
    col = lambda w, j: w[:, int(offs[j]):int(offs[j + 1])]

    bias_near = _near_bias(t5_bias)
    cum_sel = _cum_selector(heads * HEAD_DIM // LANES, IDX_HEADS)
    row = lambda v: v.reshape(1, -1).astype(F32)

    xf = x.reshape(n, d)
    memf = mem.reshape(batch * mem.shape[1], d)
    for l in range(depth):
        wl = w_in[l]
        small_w = _pad_cols(jnp.concatenate([col(wl, 6), col(wl, 10)], axis=1), LANES)
        w_all = jnp.concatenate(
            [col(wl, 0), col(wl, 1), col(wl, 2), col(wl, 3), col(wl, 4), col(wl, 5), col(wl, 5),
             col(wl, 7) * (HEAD_DIM ** -0.5 * LOG2E),
             col(wl, 8), col(wl, 9), col(wl, 11), col(wl, 12), col(wl, 13)],
            axis=1).astype(BF16)
        w_all = jnp.concatenate([w_all, small_w.astype(BF16)], axis=1)
        out_widths = (2 * dr, heads * HEAD_DIM, d_lat, IDX_HEADS * IDX_DIM, 2 * IDX_DIM,
                      heads * HEAD_DIM, heads * HEAD_DIM, heads * HEAD_DIM, 3 * d, LANES)
        out_dtypes = (BF16,) * 9 + (F32,)

        xf = _ffn(xf, row(ffn1_norm[l]), ffn1_w_gu[l][:, :d_ff].astype(BF16),
                  ffn1_w_gu[l][:, d_ff:].astype(BF16), ffn1_w_down[l].astype(BF16),
                  row(final_norm), norm_out=False)

        rg, qb, ckv, qi, ki2, qc, kc, vc, gates, small = _norm_proj(
            xf, row(mix_norm[l]), w_all, out_widths, out_dtypes)

        wax = jnp.concatenate([_block_diag(rg_wa[l]), _block_diag(rg_wx[l])], axis=1).astype(BF16)
        bax = jnp.concatenate([rg_ba[l], rg_bx[l]]).reshape(1, -1)
        o_a = _rglru(rg, batch, conv_w[l], row(conv_b[l]), wax, bax, row(rg_lambda[l]))

        fbias = _pad_cols(jnp.concatenate([jnp.zeros((IDX_HEADS,), F32), forget_bias[l]])[None], LANES)
        o_c = _fox(qc, kc, vc, _forget_cumsum(small, batch, fbias), cum_sel, batch)

        o_b = _dsa(qb, qi, small, ckv, ki2, row(kv_norm[l]), _block_diag(w_uk[l]).T.astype(BF16),
                   _block_diag(w_uv[l]).T.astype(BF16), bias_near, batch)

        xf = _merge(xf, o_a, o_b, o_c, gates, w_branch[l].astype(BF16), w_out[l].astype(BF16))

        (kv,) = _norm_proj(memf, row(mem_norm[l]), w_mkv[l].astype(BF16),
                           (w_mkv.shape[2],), (BF16,))
        xf = _mem_attn(xf, row(xattn_norm[l]), w_mq[l].astype(BF16), kv, w_mo[l].astype(BF16), batch)

        xf = _ffn(xf, row(ffn2_norm[l]), ffn2_w_gu[l][:, :d_ff].astype(BF16),
                  ffn2_w_gu[l][:, d_ff:].astype(BF16), ffn2_w_down[l].astype(BF16),
                  row(final_norm), norm_out=(l == depth - 1))
    return xf.reshape(batch, seq, d)
```

```python
import functools
import math

import numpy as np
import jax
import jax.numpy as jnp
from jax import lax
from jax.experimental import pallas as pl
from jax.experimental.pallas import tpu as pltpu

F32 = jnp.float32
BF16 = jnp.bfloat16
I32 = jnp.int32
I16 = jnp.int16

LANES = 128
SUBLANES = 8
VMEM_LIMIT_BYTES = 56 * 1024 * 1024

EPS = 1e-6
NEG = -1e30
LOG2E = math.log2(math.e)
FAST_GAP = 100.0
CHUNK = 64
CHUNK_SHIFT = CHUNK.bit_length() - 1
HEAD_DIM = 64
RG_C = 8.0
CONV_W = 4
IDX_HEADS = 8
IDX_DIM = 64
TOPK_MAX = 256
MEM_HEADS = 4
MEM_HEAD_DIM = 128
NUM_BUCKETS = 32
MAX_DISTANCE = 128

TM_TOKENS = 512
FFN_CHUNK = 256
PROJ_CHUNK = 1024
TS_SCAN = 256
TQ_FOX = 512
CUM_PARTS = 3
TQ_DSA = 256
DSA_NEAR_KEYS = TQ_DSA + MAX_DISTANCE
TK_DSA = 512
DSA_HEAD_GROUP = 2
DSA_PAD = TK_DSA - TQ_DSA


def _key_of_float(v):
    b = int(np.float32(v).view(np.int32))
    return b ^ ((b >> 31) & 0x7FFFFFFF)


KEY_NEG = _key_of_float(NEG)


def _params(*semantics):
    return pltpu.CompilerParams(dimension_semantics=semantics, vmem_limit_bytes=VMEM_LIMIT_BYTES)


def _const_spec(shape):
    nd = len(shape)
    return pl.BlockSpec(shape, lambda *_: (0,) * nd, pipeline_mode=pl.Buffered(1))


def _rms(x32, g):
    ms = jnp.mean(x32 * x32, axis=-1, keepdims=True)
    return x32 * lax.rsqrt(ms + EPS) * g


def _dot(a, b):
    return jnp.dot(a, b, preferred_element_type=F32)


def _dot_nt(a, b):
    return lax.dot_general(a, b, (((1,), (1,)), ((), ())), preferred_element_type=F32)


def _tree_sum(xs):
    while len(xs) > 1:
        xs = [xs[j] + xs[j + 1] for j in range(0, len(xs) - 1, 2)] + ([xs[-1]] if len(xs) % 2 else [])
    return xs[0]


def _dot_tn(a, b):
    return lax.dot_general(a, b, (((0,), (0,)), ((), ())), preferred_element_type=F32)


def _ffn_kernel(x_ref, g_ref, wg_ref, wu_ref, wd_ref, gout_ref, o_ref, a_ref, *, norm_out):
    x = x_ref[...]
    h = _rms(x, g_ref[...]).astype(BF16)
    d_ff = wg_ref.shape[1]
    for f0 in range(0, d_ff, FFN_CHUNK):
        g = _dot(h, wg_ref[:, f0:f0 + FFN_CHUNK])
        u = _dot(h, wu_ref[:, f0:f0 + FFN_CHUNK])
        a_ref[:, f0:f0 + FFN_CHUNK] = (g * jax.nn.sigmoid(g) * u).astype(BF16)
    y = x + 0.5 * _dot(a_ref[...], wd_ref[...])
    o_ref[...] = _rms(y, gout_ref[...]) if norm_out else y


def _ffn(x, g, wg, wu, wd, g_out, norm_out):
    n, d = x.shape
    d_ff = wg.shape[1]
    tm = min(TM_TOKENS, n)
    return pl.pallas_call(
        functools.partial(_ffn_kernel, norm_out=norm_out),
        grid=(n // tm,),
        in_specs=[pl.BlockSpec((tm, d), lambda i: (i, 0)),
                  _const_spec((1, d)), _const_spec((d, d_ff)), _const_spec((d, d_ff)),
                  _const_spec((d_ff, d)), _const_spec((1, d))],
        out_specs=pl.BlockSpec((tm, d), lambda i: (i, 0)),
        out_shape=jax.ShapeDtypeStruct((n, d), F32),
        scratch_shapes=[pltpu.VMEM((tm, d_ff), BF16)],
        compiler_params=_params("parallel"),
        name="ffn",
    )(x, g, wg, wu, wd, g_out)


def _norm_proj_kernel(x_ref, g_ref, w_ref, *o_refs):
    h = _rms(x_ref[...].astype(F32), g_ref[...]).astype(BF16)
    c0 = 0
    for o_ref in o_refs:
        width = o_ref.shape[1]
        for s0 in range(0, width, PROJ_CHUNK):
            s1 = min(s0 + PROJ_CHUNK, width)
            o_ref[:, s0:s1] = _dot(h, w_ref[:, c0 + s0:c0 + s1]).astype(o_ref.dtype)
        c0 += width


def _norm_proj(x, g, w, widths, dtypes):
    n, d = x.shape
    tm = min(TM_TOKENS, n)
    assert sum(widths) == w.shape[1]
    return pl.pallas_call(
        _norm_proj_kernel,
        grid=(n // tm,),
        in_specs=[pl.BlockSpec((tm, d), lambda i: (i, 0)), _const_spec((1, d)),
                  _const_spec(w.shape)],
        out_specs=[pl.BlockSpec((tm, wd), lambda i: (i, 0)) for wd in widths],
        out_shape=[jax.ShapeDtypeStruct((n, wd), dt) for wd, dt in zip(widths, dtypes)],
        compiler_params=_params("parallel"),
        name="norm_proj",
    )(x, g, w)


def _gelu_tanh(x):
    return 0.5 * x * (1.0 + jnp.tanh(math.sqrt(2.0 / math.pi) * (x + 0.044715 * (x * x * x))))


def _softplus(x):
    return jnp.maximum(x, 0.0) + jnp.log1p(jnp.exp(-jnp.abs(x)))


def _rglru_kernel(rg_ref, cw_ref, cb_ref, wax_ref, bax_ref, lam_ref, o_ref, xbuf, hc):
    ts = rg_ref.shape[0]
    dr = o_ref.shape[1]

    @pl.when(pl.program_id(1) == 0)
    def _():
        xbuf[0:8, :] = jnp.zeros((8, dr), F32)
        hc[...] = jnp.zeros_like(hc)

    xr = rg_ref[:, :dr].astype(F32)
    gr = rg_ref[:, dr:].astype(F32)
    xbuf[8:, :] = xr
    xc = cb_ref[...] + cw_ref[3:4, :] * xr
    for j in range(CONV_W - 1):
        xc = xc + cw_ref[j:j + 1, :] * xbuf[pl.ds(5 + j, ts), :]
    xbuf[0:8, :] = xbuf[ts:ts + 8, :]

    ax = _dot(xc.astype(BF16), wax_ref[...]) + bax_ref[...]
    r = jax.nn.sigmoid(ax[:, :dr])
    gi = jax.nn.sigmoid(ax[:, dr:])
    log_a = (-RG_C) * r * _softplus(-lam_ref[...])
    a = jnp.exp(log_a)
    th = jnp.tanh(log_a)
    num = -2.0 * th
    u = num * lax.rsqrt(jnp.maximum(num * (1.0 - th), jnp.finfo(F32).tiny)) * (gi * xc)

    row = lax.broadcasted_iota(I32, (ts, dr), 0) & (SUBLANES - 1)
    d = 1
    while d < SUBLANES:
        a_sh = pltpu.roll(a, d, 0)
        u_sh = pltpu.roll(u, d, 0)
        keep = row >= d
        u = jnp.where(keep, a * u_sh + u, u)
        a = jnp.where(keep, a * a_sh, a)
        d *= 2
    carry = hc[0:1, :]
    groups = []
    for g in range(ts // SUBLANES):
        rows = slice(g * SUBLANES, (g + 1) * SUBLANES)
        h_g = u[rows] + a[rows] * carry
        groups.append(h_g)
        carry = h_g[SUBLANES - 1:SUBLANES, :]
    h = jnp.concatenate(groups, axis=0)
    hc[...] = jnp.broadcast_to(carry, hc.shape)
    o_ref[...] = (h * _gelu_tanh(gr)).astype(o_ref.dtype)


def _rglru(rg, batch, conv_w, conv_b, wax, bax, lam):
    n, two_dr = rg.shape
    dr = two_dr // 2
    seq = n // batch
    ts = min(TS_SCAN, seq)
    ns = seq // ts
    return pl.pallas_call(
        _rglru_kernel,
        grid=(batch, ns),
        in_specs=[pl.BlockSpec((ts, two_dr), lambda b, s: (b * ns + s, 0)),
                  _const_spec(conv_w.shape), _const_spec(conv_b.shape), _const_spec(wax.shape),
                  _const_spec(bax.shape), _const_spec(lam.shape)],
        out_specs=pl.BlockSpec((ts, dr), lambda b, s: (b * ns + s, 0)),
        out_shape=jax.ShapeDtypeStruct((n, dr), BF16),
        scratch_shapes=[pltpu.VMEM((ts + 8, dr), F32), pltpu.VMEM((8, dr), F32)],
        compiler_params=_params("parallel", "arbitrary"),
        name="rglru",
    )(rg, conv_w, conv_b, wax, bax, lam)


def _forget_cumsum_kernel(f_ref, b_ref, o_ref):
    z = f_ref[...] + b_ref[...]
    x = -_softplus(-z)
    seq = x.shape[0]
    row = lax.broadcasted_iota(I32, x.shape, 0)
    d = 1
    while d < seq:
        x = jnp.where(row >= d, x + pltpu.roll(x, d, 0), x)
        d *= 2
    x = x * LOG2E
    for j in range(CUM_PARTS):
        part = x.astype(BF16)
        o_ref[j] = part
        x = x - part.astype(F32)


def _forget_cumsum(small, batch, fbias):
    n, w = small.shape
    seq = n // batch
    return pl.pallas_call(
        _forget_cumsum_kernel,
        grid=(batch,),
        in_specs=[pl.BlockSpec((seq, w), lambda b: (b, 0)), _const_spec((1, w))],
        out_specs=pl.BlockSpec((CUM_PARTS, seq, w), lambda b: (0, b, 0)),
        out_shape=jax.ShapeDtypeStruct((CUM_PARTS, n, w), BF16),
        compiler_params=_params("parallel"),
        name="forget_cumsum",
    )(small, fbias)


def _split_bf16(x, n):
    parts = []
    for _ in range(n):
        part = x.astype(BF16).astype(F32)
        parts.append(part)
        x = x - part
    return parts


def _fox_kernel(q_ref, k_ref, v_ref, cp_ref, sel_ref, o_ref, kaug, qaug, s0, kmax, m_s, l_s, acc_s):
    tq = q_ref.shape[0]
    n_pairs = q_ref.shape[1] // LANES
    i = pl.program_id(1)
    lane = lax.broadcasted_iota(I32, (1, LANES), 1)
    bound_lanes = 2 * CUM_PARTS

    @pl.when(i == 0)
    def _():
        ones = jnp.where((lane >= bound_lanes) & (lane < bound_lanes + CUM_PARTS), 1.0, 0.0)
        for p in range(n_pairs):
            kp = k_ref[:, p * LANES:(p + 1) * LANES]
            kaug[:, 2 * p * LANES:(2 * p + 1) * LANES] = kp
            extra = _dot(cp_ref[0], sel_ref[p, 0])
            for j in range(1, CUM_PARTS):
                extra = extra + _dot(cp_ref[j], sel_ref[p, j])
            kaug[:, (2 * p + 1) * LANES:(2 * p + 2) * LANES] = (extra + ones).astype(BF16)
            k2 = kp.astype(F32) * kp.astype(F32)
            for e in range(2):
                in_head = (lane >= e * HEAD_DIM) & (lane < (e + 1) * HEAD_DIM)
                n2 = jnp.sum(jnp.where(in_head, k2, 0.0), axis=1, keepdims=True)
                kmax[2 * p + e] = jnp.broadcast_to(jnp.sqrt(jnp.max(n2, axis=0, keepdims=True)),
                                                   (1, LANES))

    q_rows = pl.ds(pl.multiple_of(i * tq, tq), tq)
    f_t = cp_ref[0, q_rows, :].astype(F32)
    for j in range(1, CUM_PARTS):
        f_t = f_t + cp_ref[j, q_rows, :].astype(F32)

    slack = jnp.zeros((tq, 1), F32)
    for p in range(n_pairs):
        qp = q_ref[:, p * LANES:(p + 1) * LANES]
        for e in range(2):
            h = 2 * p + e
            rows = slice(e * tq, (e + 1) * tq)
            in_head = (lane >= e * HEAD_DIM) & (lane < (e + 1) * HEAD_DIM)
            qh = jnp.where(in_head, qp, jnp.zeros_like(qp))
            qaug[p, rows, 0:LANES] = qh
            qf = qh.astype(F32)
            reach = jnp.sqrt(jnp.sum(qf * qf, axis=1, keepdims=True)) * kmax[h][:, 0:1]
            slack = jnp.maximum(slack, reach)
            bound = reach - f_t[:, IDX_HEADS + h:IDX_HEADS + h + 1]
            extra = jnp.where((lane >= CUM_PARTS * e) & (lane < CUM_PARTS * (e + 1)), 1.0, 0.0)
            for j, part in enumerate(_split_bf16(-bound, CUM_PARTS)):
                extra = jnp.where(lane == bound_lanes + j, part, extra)
            qaug[p, rows, LANES:2 * LANES] = extra.astype(BF16)
    fast = 2.0 * jnp.max(slack) < FAST_GAP

    l_s[...] = jnp.zeros(l_s.shape, F32)
    acc_s[...] = jnp.zeros(acc_s.shape, F32)
    key_row = lax.broadcasted_iota(I32, (tq, 2 * tq), 0)
    query = lax.broadcasted_iota(I32, (tq, 2 * tq), 1) & (tq - 1)
    causal = key_row <= query

    def logits(j, p):
        rows = pl.ds(pl.multiple_of(j * tq, tq), tq)
        return _dot_nt(kaug[rows, 2 * p * LANES:(2 * p + 2) * LANES], qaug[p])

    s0[...] = logits(0, 0)

    def step(j, masked, online):
        rows = pl.ds(pl.multiple_of(j * tq, tq), tq)
        s_next = s0[...]
        for p in range(n_pairs):
            s = s_next
            if p + 1 < n_pairs:
                s_next = logits(j, p + 1)
            else:
                s0[...] = logits(jnp.minimum(j + 1, i), 0)
            if masked:
                s = jnp.where(causal, s, NEG)
            vt = v_ref[rows, p * LANES:(p + 1) * LANES]
            if online:
                m_old = m_s[p]
                m_new = jnp.maximum(m_old, jnp.max(s, axis=0, keepdims=True))
                alpha = jnp.exp2(m_old - m_new)
                pr = jnp.exp2(s - m_new)
                l_s[p] = alpha * l_s[p] + jnp.sum(pr, axis=0, keepdims=True)
                m_s[p] = m_new
                acc_s[p] = alpha * acc_s[p] + _dot_tn(vt, pr.astype(BF16))
            else:
                pr = jnp.exp2(s)
                l_s[p] = l_s[p] + jnp.sum(pr, axis=0, keepdims=True)
                acc_s[p] = acc_s[p] + _dot_tn(vt, pr.astype(BF16))

    def run(online):
        def body(j, _):
            step(j, masked=False, online=online)
            return 0
        lax.fori_loop(0, i, body, 0)
        step(i, masked=True, online=online)

    @pl.when(fast)
    def _():
        run(online=False)

    @pl.when(jnp.logical_not(fast))
    def _():
        m_s[...] = jnp.full(m_s.shape, NEG, F32)
        run(online=True)

    feat = lax.broadcasted_iota(I32, (LANES, tq), 0)
    for p in range(n_pairs):
        o_t = acc_s[p] / l_s[p]
        o_t = jnp.where(feat < HEAD_DIM, o_t[:, :tq], o_t[:, tq:])
        o_ref[:, p * LANES:(p + 1) * LANES] = o_t.T.astype(o_ref.dtype)


def _fox(q, k, v, cum_parts, sel, batch):
    n, w = q.shape
    seq = n // batch
    tq = min(TQ_FOX, seq)
    nq = seq // tq
    n_pairs = w // LANES
    return pl.pallas_call(
        _fox_kernel,
        grid=(batch, nq),
        in_specs=[pl.BlockSpec((tq, w), lambda b, i: (b * nq + i, 0)),
                  pl.BlockSpec((seq, w), lambda b, i: (b, 0)),
                  pl.BlockSpec((seq, w), lambda b, i: (b, 0)),
                  pl.BlockSpec((CUM_PARTS, seq, LANES), lambda b, i: (0, b, 0)),
                  _const_spec(sel.shape)],
        out_specs=pl.BlockSpec((tq, w), lambda b, i: (b * nq + i, 0)),
        out_shape=jax.ShapeDtypeStruct((n, w), BF16),
        scratch_shapes=[pltpu.VMEM((seq, 2 * w), BF16),
                        pltpu.VMEM((n_pairs, 2 * tq, 2 * LANES), BF16),
                        pltpu.VMEM((tq, 2 * tq), F32),
                        pltpu.VMEM((2 * n_pairs, 1, LANES), F32),
                        pltpu.VMEM((n_pairs, 1, 2 * tq), F32),
                        pltpu.VMEM((n_pairs, 1, 2 * tq), F32),
                        pltpu.VMEM((n_pairs, LANES, 2 * tq), F32)],
        compiler_params=_params("parallel", "arbitrary"),
        name="fox",
    )(q, k, v, cum_parts, sel)


def _cum_selector(n_pairs, first_lane):
    sel = np.zeros((n_pairs, CUM_PARTS, LANES, LANES), np.float32)
    for p in range(n_pairs):
        for j in range(CUM_PARTS):
            for e in range(2):
                sel[p, j, first_lane + 2 * p + e, CUM_PARTS * e + j] = -1.0
    return jnp.asarray(sel, BF16)


def _float_key(x):
    b = lax.bitcast_convert_type(x, I32)
    return b ^ ((b >> 31) & 0x7FFFFFFF)


def _dsa_kernel(qb_ref, qi_ref, sm_ref, ckv_ref, ki_ref, kvn_ref, wuk_ref, wuv_ref, bias_ref, brng_ref,
                o_ref, c_s, ki_s, keys, keys16, qim, d0, s0, cmax, m_s, l_s, acc_s, *, k_sel):
    tq = qb_ref.shape[0]
    seq = ckv_ref.shape[0]
    heads = bias_ref.shape[0]
    tk = TK_DSA
    i = pl.program_id(1)
    lane = lax.broadcasted_iota(I32, (1, LANES), 1)

    @pl.when(i == 0)
    def _():
        c_s[0:DSA_PAD, :] = jnp.zeros((DSA_PAD, 2 * LANES), BF16)
        ki_s[0:DSA_PAD, :] = jnp.zeros((DSA_PAD, LANES), BF16)
        c = _rms(ckv_ref[...].astype(F32), kvn_ref[...]).astype(BF16)
        c_s[DSA_PAD:, 0:LANES] = c
        c_s[DSA_PAD:, LANES:] = jnp.broadcast_to(jnp.where(lane < CUM_PARTS, 1.0, 0.0),
                                                 (seq, LANES)).astype(BF16)
        ki_s[DSA_PAD:, :] = ki_ref[...]
        cf = c.astype(F32)
        n2 = jnp.max(jnp.sum(cf * cf, axis=1, keepdims=True), axis=0, keepdims=True)
        cmax[...] = jnp.broadcast_to(jnp.sqrt(n2), cmax.shape)

    n_valid = (i + 1) * tq
    n_tiles = (n_valid + tk - 1) // tk
    max_tiles = pl.cdiv(seq, tk)

    def tile_rows(k):
        return pl.ds(pl.multiple_of(n_valid - (k + 1) * tk + DSA_PAD, tq), tk)

    lane = lax.broadcasted_iota(I32, (1, LANES), 1)
    for h in range(IDX_HEADS):
        qp = qi_ref[:, (h // 2) * LANES:(h // 2 + 1) * LANES]
        e = h % 2
        in_head = (lane >= e * IDX_DIM) & (lane < (e + 1) * IDX_DIM)
        qim[h * tq:(h + 1) * tq, :] = jnp.where(in_head, qp, jnp.zeros_like(qp))
    w_t = sm_ref[...].T[0:IDX_HEADS, :] * (IDX_HEADS ** -0.5 * IDX_DIM ** -0.5)
    q_chunk = (i * tq + lax.broadcasted_iota(I32, (tk, tq), 1)) >> CHUNK_SHIFT
    key_local = lax.broadcasted_iota(I32, (tk, tq), 0)

    def pair_dots(k, p):
        return _dot_nt(ki_s[tile_rows(k), :], qim[2 * p * tq:(2 * p + 2) * tq, :])

    d0[...] = pair_dots(0, 0)

    def score_tile(k, _):
        d_next = d0[...]
        sc = jnp.zeros((tk, tq), F32)
        for p in range(IDX_HEADS // 2):
            d = d_next
            if p + 1 < IDX_HEADS // 2:
                d_next = pair_dots(k, p + 1)
            else:
                d0[...] = pair_dots(jnp.minimum(k + 1, n_tiles - 1), 0)
            for e in range(2):
                sc = sc + jnp.maximum(d[:, e * tq:(e + 1) * tq], 0.0) * w_t[2 * p + e:2 * p + e + 1, :]
        pos = key_local + (n_valid - (k + 1) * tk)
        sc = jnp.where(pos >= 0, jnp.where((pos >> CHUNK_SHIFT) <= q_chunk, sc, NEG), NEG)
        key = _float_key(sc)
        keys[k] = key
        keys16[k] = (key >> 16).astype(I16)
        return 0

    lax.fori_loop(0, n_tiles, score_tile, 0)

    def count_rows(hit_fn):
        def body(k, cnt):
            return cnt + jnp.sum(hit_fn(k).reshape(tk // SUBLANES, SUBLANES, tq), axis=0)
        cnt = lax.fori_loop(0, n_tiles, body, jnp.zeros((SUBLANES, tq), I32))
        return jnp.sum(cnt, axis=0, keepdims=True)

    def count_ge(thr):
        return count_rows(lambda k: jnp.where(keys[k] >= thr, 1, 0))

    def count_ge16(thr16):
        rows16 = 2 * SUBLANES
        def tile(k):
            ge = jnp.where(keys16[k] >= thr16, jnp.ones((), I16), jnp.zeros((), I16))
            return _tree_sum([ge[c * rows16:(c + 1) * rows16, :] for c in range(tk // rows16)])
        branches = [functools.partial(lambda t: _tree_sum([tile(k) for k in range(t)]), t)
                    for t in range(1, max_tiles + 1)]
        cnt = lax.switch(n_tiles - 1, branches)
        return jnp.sum(cnt.astype(I32), axis=0, keepdims=True)

    half = 2 ** 15

    def kth_largest16(k_need):
        def step(b, u):
            cand = u | lax.shift_left(jnp.int32(1), 15 - b)
            return jnp.where(count_ge16((cand - half).astype(I16)) >= k_need, cand, u)
        return lax.fori_loop(0, 16, step, jnp.zeros((1, tq), I32)) - half

    t_hi = kth_largest16(k_sel)
    n_above = jnp.where(t_hi >= half - 1, 0,
                        count_ge16((jnp.minimum(t_hi, half - 2) + 1).astype(I16)))

    def low_halves(k, _):
        key = keys[k]
        keys16[k] = jnp.where((key >> 16) == t_hi, (key & 0xFFFF) - half, -half).astype(I16)
        return 0

    lax.fori_loop(0, n_tiles, low_halves, 0)
    t_lo = kth_largest16(k_sel - n_above)
    thr = jnp.maximum(lax.shift_left(t_hi, 16) + (t_lo + half), KEY_NEG + 1)

    excess = count_ge(thr) - k_sel

    @pl.when(jnp.max(excess) > 0)
    def _():
        def count_eq_below(cut):
            def hit(k):
                pos = key_local + (n_valid - (k + 1) * tk)
                return jnp.where(keys[k] == thr, jnp.where(pos < cut, 1, 0), 0)
            return count_rows(hit)

        n_bits = max(1, (2 * seq - 1).bit_length())
        keep = count_eq_below(jnp.full((1, tq), 2 ** n_bits, I32)) - jnp.maximum(excess, 0)

        def cut_step(b, cut):
            cand = cut | lax.shift_left(jnp.int32(1), n_bits - 1 - b)
            return jnp.where(count_eq_below(cand) <= keep, cand, cut)

        cut = lax.fori_loop(0, n_bits, cut_step, jnp.zeros((1, tq), I32))

        def drop(k, _):
            pos = key_local + (n_valid - (k + 1) * tk)
            kk = keys[k]
            keys[k] = jnp.where(kk == thr, jnp.where(pos >= cut, KEY_NEG, kk), kk)
            return 0

        lax.fori_loop(0, n_tiles, drop, 0)

    ql_t = (_dot_nt(wuk_ref[...], qb_ref[...]) * (HEAD_DIM ** -0.5 * LOG2E)).astype(BF16)
    n_groups = heads // DSA_HEAD_GROUP
    gw = DSA_HEAD_GROUP * tq
    slack = jnp.zeros((1, tq), F32)
    row_id = lax.broadcasted_iota(I32, (LANES, tq), 0)
    ql_aug = []
    for h in range(heads):
        ql_h = ql_t[h * LANES:(h + 1) * LANES, :]
        qf = ql_h.astype(F32)
        reach = jnp.sqrt(jnp.sum(qf * qf, axis=0, keepdims=True)) * cmax[:, 0:1]
        slack = jnp.maximum(slack, 2.0 * reach + (brng_ref[0, h][:, 0:1] - brng_ref[1, h][:, 0:1]))
        bound_rows = jnp.zeros((LANES, tq), F32)
        for j, part in enumerate(_split_bf16(-(reach + brng_ref[0, h][:, 0:1]), CUM_PARTS)):
            bound_rows = jnp.where(row_id == j, part, bound_rows)
        ql_aug.append(jnp.concatenate([ql_h, bound_rows.astype(BF16)], axis=0))
    ql_g = [jnp.concatenate(ql_aug[g * DSA_HEAD_GROUP:(g + 1) * DSA_HEAD_GROUP], axis=1)
            for g in range(n_groups)]
    fast = jnp.max(slack) < FAST_GAP
    l_s[...] = jnp.zeros(l_s.shape, F32)
    acc_s[...] = jnp.zeros(acc_s.shape, F32)

    def attend(k, near, online):
        ct_aug = c_s[tile_rows(k), :]
        ct = c_s[tile_rows(k), 0:LANES]
        mask = jnp.where(keys[k] >= thr, 0.0, NEG)
        s_next = s0[...]
        for g in range(n_groups):
            s_g = s_next
            if g + 1 < n_groups:
                s_next = _dot(ct_aug, ql_g[g + 1])
            else:
                s0[...] = _dot(c_s[tile_rows(jnp.minimum(k + 1, n_tiles - 1)), :], ql_g[0])
            parts = []
            for hh in range(DSA_HEAD_GROUP):
                s = s_g[:, hh * tq:(hh + 1) * tq]
                if near:
                    s = jnp.concatenate([s[:tk - DSA_NEAR_KEYS], s[tk - DSA_NEAR_KEYS:]
                                         + bias_ref[g * DSA_HEAD_GROUP + hh]], axis=0)
                parts.append(s + mask)
            s = jnp.concatenate(parts, axis=1)
            if online:
                m_old = m_s[g]
                m_new = jnp.maximum(m_old, jnp.max(s, axis=0, keepdims=True))
                alpha = jnp.exp2(m_old - m_new)
                pr = jnp.exp2(s - m_new)
                l_s[g] = alpha * l_s[g] + jnp.sum(pr, axis=0, keepdims=True)
                m_s[g] = m_new
                acc_s[g] = alpha * acc_s[g] + _dot_tn(ct, pr.astype(BF16))
            else:
                pr = jnp.exp2(s)
                l_s[g] = l_s[g] + jnp.sum(pr, axis=0, keepdims=True)
                acc_s[g] = acc_s[g] + _dot_tn(ct, pr.astype(BF16))

    s0[...] = _dot(c_s[tile_rows(0), :], ql_g[0])

    def run(online):
        attend(0, near=True, online=online)

        def far(k, _):
            attend(k, near=False, online=online)
            return 0

        lax.fori_loop(1, n_tiles, far, 0)

    @pl.when(fast)
    def _():
        run(online=False)

    @pl.when(jnp.logical_not(fast))
    def _():
        m_s[...] = jnp.full(m_s.shape, NEG, F32)
        run(online=True)

    o_parts = []
    for g in range(n_groups):
        o_g = acc_s[g] / l_s[g]
        o_parts += [o_g[:, hh * tq:(hh + 1) * tq] for hh in range(DSA_HEAD_GROUP)]
    o_lat_t = jnp.concatenate(o_parts, axis=0).astype(BF16)
    o_ref[...] = _dot(wuv_ref[...], o_lat_t).T.astype(o_ref.dtype)


def _dsa(qb, qi, small, ckv, ki2, kvn, wuk_bd, wuv_bd, bias_near, batch):
    n, w = qb.shape
    seq = n // batch
    tq = TQ_DSA
    nq = seq // tq
    heads = bias_near.shape[0]
    n_groups = heads // DSA_HEAD_GROUP
    gw = DSA_HEAD_GROUP * tq
    k_sel = min(TOPK_MAX, seq // 4)
    blk = lambda width: pl.BlockSpec((tq, width), lambda b, i: (b * nq + i, 0))
    per_batch = pl.BlockSpec((seq, LANES), lambda b, i: (b, 0))
    bias_rng = jnp.stack([jnp.maximum(jnp.max(bias_near, axis=(1, 2)), 0.0),
                          jnp.minimum(jnp.min(bias_near, axis=(1, 2)), 0.0)])
    bias_rng = jnp.broadcast_to(bias_rng[:, :, None, None], (2, heads, 1, LANES))
    return pl.pallas_call(
        functools.partial(_dsa_kernel, k_sel=k_sel),
        grid=(batch, nq),
        in_specs=[blk(w), blk(w), blk(LANES), per_batch, per_batch, _const_spec(kvn.shape),
                  _const_spec(wuk_bd.shape), _const_spec(wuv_bd.shape),
                  _const_spec(bias_near.shape), _const_spec(bias_rng.shape)],
        out_specs=blk(w),
        out_shape=jax.ShapeDtypeStruct((n, w), BF16),
        scratch_shapes=[pltpu.VMEM((seq + DSA_PAD, 2 * LANES), BF16),
                        pltpu.VMEM((seq + DSA_PAD, LANES), BF16),
                        pltpu.VMEM((pl.cdiv(seq, TK_DSA), TK_DSA, tq), I32),
                        pltpu.VMEM((pl.cdiv(seq, TK_DSA), TK_DSA, tq), I16),
                        pltpu.VMEM((IDX_HEADS * tq, LANES), BF16),
                        pltpu.VMEM((TK_DSA, 2 * tq), F32),
                        pltpu.VMEM((TK_DSA, gw), F32),
                        pltpu.VMEM((1, LANES), F32),
                        pltpu.VMEM((n_groups, 1, gw), F32),
                        pltpu.VMEM((n_groups, 1, gw), F32),
                        pltpu.VMEM((n_groups, LANES, gw), F32)],
        compiler_params=_params("parallel", "arbitrary"),
        name="dsa",
    )(qb, qi, small, ckv, ki2, kvn, wuk_bd, wuv_bd, bias_near, bias_rng)


def _merge_kernel(x_ref, oa_ref, ob_ref, oc_ref, gate_ref, wbr_ref, wout_ref, o_ref):
    d = x_ref.shape[1]
    merged = jnp.zeros(x_ref.shape, F32)
    for j, br_ref in enumerate((oa_ref, ob_ref, oc_ref)):
        gate = jax.nn.sigmoid(gate_ref[:, j * d:(j + 1) * d].astype(F32))
        merged = merged + gate * _dot(br_ref[...], wbr_ref[j])
    o_ref[...] = x_ref[...] + _dot(merged.astype(BF16), wout_ref[...])


def _merge(x, oa, ob, oc, gates, wbr, wout):
    n, d = x.shape
    tm = min(TM_TOKENS, n)
    bw = oa.shape[1]
    blk = lambda width: pl.BlockSpec((tm, width), lambda i: (i, 0))
    return pl.pallas_call(
        _merge_kernel,
        grid=(n // tm,),
        in_specs=[blk(d), blk(bw), blk(bw), blk(bw), blk(3 * d), _const_spec(wbr.shape),
                  _const_spec(wout.shape)],
        out_specs=blk(d),
        out_shape=jax.ShapeDtypeStruct((n, d), F32),
        compiler_params=_params("parallel"),
        name="merge",
    )(x, oa, ob, oc, gates, wbr, wout)


def _mem_attn_kernel(x_ref, g_ref, wq_ref, kv_ref, wo_ref, o_ref):
    x = x_ref[...]
    hw = MEM_HEADS * MEM_HEAD_DIM
    q = _dot(_rms(x, g_ref[...]).astype(BF16), wq_ref[...]).astype(BF16)
    outs = []
    for h in range(MEM_HEADS):
        lanes = slice(h * MEM_HEAD_DIM, (h + 1) * MEM_HEAD_DIM)
        s = _dot_nt(q[:, lanes], kv_ref[:, lanes]) * (MEM_HEAD_DIM ** -0.5)
        pr = jnp.exp(s - jnp.max(s, axis=1, keepdims=True))
        pr = pr / jnp.sum(pr, axis=1, keepdims=True)
        outs.append(_dot(pr.astype(BF16), kv_ref[:, hw + h * MEM_HEAD_DIM:hw + (h + 1) * MEM_HEAD_DIM]))
    o = jnp.concatenate(outs, axis=1).astype(BF16)
    o_ref[...] = x + _dot(o, wo_ref[...])


def _mem_attn(x, g, wq, kv, wo, batch):
    n, d = x.shape
    seq = n // batch
    tm = min(TM_TOKENS, seq)
    ns = seq // tm
    mem_len = kv.shape[0] // batch
    return pl.pallas_call(
        _mem_attn_kernel,
        grid=(batch, ns),
        in_specs=[pl.BlockSpec((tm, d), lambda b, s: (b * ns + s, 0)), _const_spec((1, d)),
                  _const_spec(wq.shape),
                  pl.BlockSpec((mem_len, kv.shape[1]), lambda b, s: (b, 0)),
                  _const_spec(wo.shape)],
        out_specs=pl.BlockSpec((tm, d), lambda b, s: (b * ns + s, 0)),
        out_shape=jax.ShapeDtypeStruct((n, d), F32),
        compiler_params=_params("parallel", "arbitrary"),
        name="mem_attn",
    )(x, g, wq, kv, wo)


def _t5_bucket(rel):
    nb = NUM_BUCKETS // 2
    max_exact = nb // 2
    base = jnp.where(rel > 0, nb, 0)
    n = jnp.abs(rel)
    n_f = jnp.maximum(n, 1).astype(jnp.float32)
    large = max_exact + (jnp.log(n_f / max_exact) / math.log(MAX_DISTANCE / max_exact)
                         * (nb - max_exact)).astype(jnp.int32)
    large = jnp.minimum(large, nb - 1)
    return base + jnp.where(n < max_exact, n, large)


def _near_bias(t5_bias):
    t = jnp.arange(TQ_DSA)[:, None]
    s = jnp.arange(DSA_NEAR_KEYS)[None, :] - MAX_DISTANCE
    onehot = jax.nn.one_hot(_t5_bucket(s - t), NUM_BUCKETS, dtype=F32)
    tbl = jnp.einsum("tsb,bh->hst", onehot, (t5_bias - t5_bias[NUM_BUCKETS // 2 - 1]) * LOG2E,
                     precision=lax.Precision.HIGHEST)
    return tbl.astype(F32)


def _block_diag(w):
    nb, a, b = w.shape
    eye = jnp.eye(nb, dtype=w.dtype)
    return (eye[:, None, :, None] * w[:, :, None, :]).reshape(nb * a, nb * b)


def _pad_cols(w, width):
    return jnp.pad(w, ((0, 0), (0, width - w.shape[1])))


def kernel(x, mem, ffn1_norm, ffn1_w_gu, ffn1_w_down, mix_norm, w_in, conv_w, conv_b, rg_wa, rg_ba, rg_wx, rg_bx, rg_lambda, kv_norm, w_uk, w_uv, forget_bias, w_branch, w_out, xattn_norm, mem_norm, w_mq, w_mkv, w_mo, ffn2_norm, ffn2_w_gu, ffn2_w_down, t5_bias, final_norm):
    batch, seq, d = x.shape
    depth = w_in.shape[0]
    d_ff = ffn1_w_down.shape[1]
    dr = conv_w.shape[2]
    heads = w_uk.shape[1]
    d_lat = w_uk.shape[3]
    n = batch * seq
    assert seq % TK_DSA == 0 and d_lat == LANES and 2 * IDX_DIM == LANES and 2 * HEAD_DIM == LANES

    widths = (dr, dr, heads * HEAD_DIM, d_lat, IDX_HEADS * IDX_DIM, IDX_DIM, IDX_HEADS,
              heads * HEAD_DIM, heads * HEAD_DIM, heads * HEAD_DIM, heads, d, d, d)
    offs = np.concatenate([[0], np.cumsum(widths)])
    col = lambda w, j: w[:, int(offs[j]):int(offs[j + 1])]

    bias_near = _near_bias(t5_bias)
    cum_sel = _cum_selector(heads * HEAD_DIM // LANES, IDX_HEADS)
    row = lambda v: v.reshape(1, -1).astype(F32)

    xf = x.reshape(n, d)
    memf = mem.reshape(batch * mem.shape[1], d)
    for l in range(depth):
        wl = w_in[l]
        small_w = _pad_cols(jnp.concatenate([col(wl, 6), col(wl, 10)], axis=1), LANES)
        w_all = jnp.concatenate(
            [col(wl, 0), col(wl, 1), col(wl, 2), col(wl, 3), col(wl, 4), col(wl, 5), col(wl, 5),
             col(wl, 7) * (HEAD_DIM ** -0.5 * LOG2E),
             col(wl, 8), col(wl, 9), col(wl, 11), col(wl, 12), col(wl, 13)],
            axis=1).astype(BF16)
        w_all = jnp.concatenate([w_all, small_w.astype(BF16)], axis=1)
        out_widths = (2 * dr, heads * HEAD_DIM, d_lat, IDX_HEADS * IDX_DIM, 2 * IDX_DIM,
                      heads * HEAD_DIM, heads * HEAD_DIM, heads * HEAD_DIM, 3 * d, LANES)
        out_dtypes = (BF16,) * 9 + (F32,)

        xf = _ffn(xf, row(ffn1_norm[l]), ffn1_w_gu[l][:, :d_ff].astype(BF16),
                  ffn1_w_gu[l][:, d_ff:].astype(BF16), ffn1_w_down[l].astype(BF16),
                  row(final_norm), norm_out=False)

        rg, qb, ckv, qi, ki2, qc, kc, vc, gates, small = _norm_proj(
            xf, row(mix_norm[l]), w_all, out_widths, out_dtypes)

        wax = jnp.concatenate([_block_diag(rg_wa[l]), _block_diag(rg_wx[l])], axis=1).astype(BF16)
        bax = jnp.concatenate([rg_ba[l], rg_bx[l]]).reshape(1, -1)
        o_a = _rglru(rg, batch, conv_w[l], row(conv_b[l]), wax, bax, row(rg_lambda[l]))

        fbias = _pad_cols(jnp.concatenate([jnp.zeros((IDX_HEADS,), F32), forget_bias[l]])[None], LANES)
        o_c = _fox(qc, kc, vc, _forget_cumsum(small, batch, fbias), cum_sel, batch)

        o_b = _dsa(qb, qi, small, ckv, ki2, row(kv_norm[l]), _block_diag(w_uk[l]).T.astype(BF16),
                   _block_diag(w_uv[l]).T.astype(BF16), bias_near, batch)

        xf = _merge(xf, o_a, o_b, o_c, gates, w_branch[l].astype(BF16), w_out[l].astype(BF16))

        (kv,) = _norm_proj(memf, row(mem_norm[l]), w_mkv[l].astype(BF16),
                           (w_mkv.shape[2],), (BF16,))
        xf = _mem_attn(xf, row(xattn_norm[l]), w_mq[l].astype(BF16), kv, w_mo[l].astype(BF16), batch)

        xf = _ffn(xf, row(ffn2_norm[l]), ffn2_w_gu[l][:, :d_ff].astype(BF16),
                  ffn2_w_gu[l][:, d_ff:].astype(BF16), ffn2_w_down[l].astype(BF16),
                  row(final_norm), norm_out=(l == depth - 1))
    return xf.reshape(batch, seq, d)
```

```python
import functools
import math

import numpy as np
import jax
import jax.numpy as jnp
from jax import lax
from jax.experimental import pallas as pl
from jax.experimental.pallas import tpu as pltpu

F32 = jnp.float32
BF16 = jnp.bfloat16
I32 = jnp.int32
I16 = jnp.int16

LANES = 128
SUBLANES = 8
VMEM_LIMIT_BYTES = 56 * 1024 * 1024

EPS = 1e-6
NEG = -1e30
LOG2E = math.log2(math.e)
FAST_GAP = 100.0
CHUNK = 64
CHUNK_SHIFT = CHUNK.bit_length() - 1
HEAD_DIM = 64
RG_C = 8.0
CONV_W = 4
IDX_HEADS = 8
IDX_DIM = 64
TOPK_MAX = 256
MEM_HEADS = 4
MEM_HEAD_DIM = 128
NUM_BUCKETS = 32
MAX_DISTANCE = 128

TM_TOKENS = 512
FFN_CHUNK = 256
PROJ_CHUNK = 1024
TS_SCAN = 256
TQ_FOX = 512
CUM_PARTS = 3
TQ_DSA = 256
DSA_NEAR_KEYS = TQ_DSA + MAX_DISTANCE
TK_DSA = 512
DSA_HEAD_GROUP = 2
DSA_PAD = TK_DSA - TQ_DSA


def _key_of_float(v):
    b = int(np.float32(v).view(np.int32))
    return b ^ ((b >> 31) & 0x7FFFFFFF)


KEY_NEG = _key_of_float(NEG)


def _params(*semantics):
    return pltpu.CompilerParams(dimension_semantics=semantics, vmem_limit_bytes=VMEM_LIMIT_BYTES)


def _const_spec(shape):
    nd = len(shape)
    return pl.BlockSpec(shape, lambda *_: (0,) * nd, pipeline_mode=pl.Buffered(1))


def _rms(x32, g):
    ms = jnp.mean(x32 * x32, axis=-1, keepdims=True)
    return x32 * lax.rsqrt(ms + EPS) * g


def _dot(a, b):
    return jnp.dot(a, b, preferred_element_type=F32)


def _dot_nt(a, b):
    return lax.dot_general(a, b, (((1,), (1,)), ((), ())), preferred_element_type=F32)


def _tree_sum(xs):
    while len(xs) > 1:
        xs = [xs[j] + xs[j + 1] for j in range(0, len(xs) - 1, 2)] + ([xs[-1]] if len(xs) % 2 else [])
    return xs[0]


def _dot_tn(a, b):
    return lax.dot_general(a, b, (((0,), (0,)), ((), ())), preferred_element_type=F32)


def _ffn_kernel(x_ref, g_ref, wg_ref, wu_ref, wd_ref, gout_ref, o_ref, a_ref, *, norm_out):
    x = x_ref[...]
    h = _rms(x, g_ref[...]).astype(BF16)
    d_ff = wg_ref.shape[1]
    for f0 in range(0, d_ff, FFN_CHUNK):
        g = _dot(h, wg_ref[:, f0:f0 + FFN_CHUNK])
        u = _dot(h, wu_ref[:, f0:f0 + FFN_CHUNK])
        a_ref[:, f0:f0 + FFN_CHUNK] = (g * jax.nn.sigmoid(g) * u).astype(BF16)
    y = x + 0.5 * _dot(a_ref[...], wd_ref[...])
    o_ref[...] = _rms(y, gout_ref[...]) if norm_out else y


def _ffn(x, g, wg, wu, wd, g_out, norm_out):
    n, d = x.shape
    d_ff = wg.shape[1]
    tm = min(TM_TOKENS, n)
    return pl.pallas_call(
        functools.partial(_ffn_kernel, norm_out=norm_out),
        grid=(n // tm,),
        in_specs=[pl.BlockSpec((tm, d), lambda i: (i, 0)),
                  _const_spec((1, d)), _const_spec((d, d_ff)), _const_spec((d, d_ff)),
                  _const_spec((d_ff, d)), _const_spec((1, d))],
        out_specs=pl.BlockSpec((tm, d), lambda i: (i, 0)),
        out_shape=jax.ShapeDtypeStruct((n, d), F32),
        scratch_shapes=[pltpu.VMEM((tm, d_ff), BF16)],
        compiler_params=_params("parallel"),
        name="ffn",
    )(x, g, wg, wu, wd, g_out)


def _norm_proj_kernel(x_ref, g_ref, w_ref, *o_refs):
    h = _rms(x_ref[...].astype(F32), g_ref[...]).astype(BF16)
    starts = np.cumsum([0] + [o_ref.shape[1] for o_ref in o_refs])
    for c0 in range(0, w_ref.shape[1], PROJ_CHUNK):
        c1 = min(c0 + PROJ_CHUNK, w_ref.shape[1])
        y = _dot(h, w_ref[:, c0:c1])
        for o_ref, o0 in zip(o_refs, starts[:-1]):
            lo, hi = max(c0, int(o0)), min(c1, int(o0) + o_ref.shape[1])
            if lo < hi:
                o_ref[:, lo - int(o0):hi - int(o0)] = y[:, lo - c0:hi - c0].astype(o_ref.dtype)


def _norm_proj(x, g, w, widths, dtypes):
    n, d = x.shape
    tm = min(TM_TOKENS, n)
    assert sum(widths) == w.shape[1]
    return pl.pallas_call(
        _norm_proj_kernel,
        grid=(n // tm,),
        in_specs=[pl.BlockSpec((tm, d), lambda i: (i, 0)), _const_spec((1, d)),
                  _const_spec(w.shape)],
        out_specs=[pl.BlockSpec((tm, wd), lambda i: (i, 0)) for wd in widths],
        out_shape=[jax.ShapeDtypeStruct((n, wd), dt) for wd, dt in zip(widths, dtypes)],
        compiler_params=_params("parallel"),
        name="norm_proj",
    )(x, g, w)


def _gelu_tanh(x):
    return 0.5 * x * (1.0 + jnp.tanh(math.sqrt(2.0 / math.pi) * (x + 0.044715 * (x * x * x))))


def _softplus(x):
    return jnp.maximum(x, 0.0) + jnp.log1p(jnp.exp(-jnp.abs(x)))


def _rglru_kernel(rg_ref, cw_ref, cb_ref, wax_ref, bax_ref, lam_ref, o_ref, xbuf, hc):
    ts = rg_ref.shape[0]
    dr = o_ref.shape[1]

    @pl.when(pl.program_id(1) == 0)
    def _():
        xbuf[0:8, :] = jnp.zeros((8, dr), F32)
        hc[...] = jnp.zeros_like(hc)

    xr = rg_ref[:, :dr].astype(F32)
    gr = rg_ref[:, dr:].astype(F32)
    xbuf[8:, :] = xr
    xc = cb_ref[...] + cw_ref[3:4, :] * xr
    for j in range(CONV_W - 1):
        xc = xc + cw_ref[j:j + 1, :] * xbuf[pl.ds(5 + j, ts), :]
    xbuf[0:8, :] = xbuf[ts:ts + 8, :]

    ax = _dot(xc.astype(BF16), wax_ref[...]) + bax_ref[...]
    r = jax.nn.sigmoid(ax[:, :dr])
    gi = jax.nn.sigmoid(ax[:, dr:])
    log_a = (-RG_C) * r * _softplus(-lam_ref[...])
    a = jnp.exp(log_a)
    th = jnp.tanh(log_a)
    num = -2.0 * th
    u = num * lax.rsqrt(jnp.maximum(num * (1.0 - th), jnp.finfo(F32).tiny)) * (gi * xc)

    row = lax.broadcasted_iota(I32, (ts, dr), 0) & (SUBLANES - 1)
    d = 1
    while d < SUBLANES:
        a_sh = pltpu.roll(a, d, 0)
        u_sh = pltpu.roll(u, d, 0)
        keep = row >= d
        u = jnp.where(keep, a * u_sh + u, u)
        a = jnp.where(keep, a * a_sh, a)
        d *= 2
    carry = hc[0:1, :]
    groups = []
    for g in range(ts // SUBLANES):
        rows = slice(g * SUBLANES, (g + 1) * SUBLANES)
        h_g = u[rows] + a[rows] * carry
        groups.append(h_g)
        carry = h_g[SUBLANES - 1:SUBLANES, :]
    h = jnp.concatenate(groups, axis=0)
    hc[...] = jnp.broadcast_to(carry, hc.shape)
    o_ref[...] = (h * _gelu_tanh(gr)).astype(o_ref.dtype)


def _rglru(rg, batch, conv_w, conv_b, wax, bax, lam):
    n, two_dr = rg.shape
    dr = two_dr // 2
    seq = n // batch
    ts = min(TS_SCAN, seq)
    ns = seq // ts
    return pl.pallas_call(
        _rglru_kernel,
        grid=(batch, ns),
        in_specs=[pl.BlockSpec((ts, two_dr), lambda b, s: (b * ns + s, 0)),
                  _const_spec(conv_w.shape), _const_spec(conv_b.shape), _const_spec(wax.shape),
                  _const_spec(bax.shape), _const_spec(lam.shape)],
        out_specs=pl.BlockSpec((ts, dr), lambda b, s: (b * ns + s, 0)),
        out_shape=jax.ShapeDtypeStruct((n, dr), BF16),
        scratch_shapes=[pltpu.VMEM((ts + 8, dr), F32), pltpu.VMEM((8, dr), F32)],
        compiler_params=_params("parallel", "arbitrary"),
        name="rglru",
    )(rg, conv_w, conv_b, wax, bax, lam)


def _forget_cumsum_kernel(f_ref, b_ref, o_ref):
    z = f_ref[...] + b_ref[...]
    x = -_softplus(-z)
    seq = x.shape[0]
    row = lax.broadcasted_iota(I32, x.shape, 0)
    d = 1
    while d < seq:
        x = jnp.where(row >= d, x + pltpu.roll(x, d, 0), x)
        d *= 2
    x = x * LOG2E
    for j in range(CUM_PARTS):
        part = x.astype(BF16)
        o_ref[j] = part
        x = x - part.astype(F32)


def _forget_cumsum(small, batch, fbias):
    n, w = small.shape
    seq = n // batch
    return pl.pallas_call(
        _forget_cumsum_kernel,
        grid=(batch,),
        in_specs=[pl.BlockSpec((seq, w), lambda b: (b, 0)), _const_spec((1, w))],
        out_specs=pl.BlockSpec((CUM_PARTS, seq, w), lambda b: (0, b, 0)),
        out_shape=jax.ShapeDtypeStruct((CUM_PARTS, n, w), BF16),
        compiler_params=_params("parallel"),
        name="forget_cumsum",
    )(small, fbias)


def _split_bf16(x, n):
    parts = []
    for _ in range(n):
        part = x.astype(BF16).astype(F32)
        parts.append(part)
        x = x - part
    return parts


def _fox_kernel(q_ref, k_ref, v_ref, cp_ref, sel_ref, o_ref, kaug, qaug, s0, kmax, m_s, l_s, acc_s):
    tq = q_ref.shape[0]
    n_pairs = q_ref.shape[1] // LANES
    i = pl.program_id(1)
    lane = lax.broadcasted_iota(I32, (1, LANES), 1)
    bound_lanes = 2 * CUM_PARTS

    @pl.when(i == 0)
    def _():
        ones = jnp.where((lane >= bound_lanes) & (lane < bound_lanes + CUM_PARTS), 1.0, 0.0)
        for p in range(n_pairs):
            kp = k_ref[:, p * LANES:(p + 1) * LANES]
            kaug[:, 2 * p * LANES:(2 * p + 1) * LANES] = kp
            extra = _dot(cp_ref[0], sel_ref[p, 0])
            for j in range(1, CUM_PARTS):
                extra = extra + _dot(cp_ref[j], sel_ref[p, j])
            kaug[:, (2 * p + 1) * LANES:(2 * p + 2) * LANES] = (extra + ones).astype(BF16)
            k2 = kp.astype(F32) * kp.astype(F32)
            for e in range(2):
                in_head = (lane >= e * HEAD_DIM) & (lane < (e + 1) * HEAD_DIM)
                n2 = jnp.sum(jnp.where(in_head, k2, 0.0), axis=1, keepdims=True)
                kmax[2 * p + e] = jnp.broadcast_to(jnp.sqrt(jnp.max(n2, axis=0, keepdims=True)),
                                                   (1, LANES))

    q_rows = pl.ds(pl.multiple_of(i * tq, tq), tq)
    f_t = cp_ref[0, q_rows, :].astype(F32)
    for j in range(1, CUM_PARTS):
        f_t = f_t + cp_ref[j, q_rows, :].astype(F32)

    slack = jnp.zeros((tq, 1), F32)
    for p in range(n_pairs):
        qp = q_ref[:, p * LANES:(p + 1) * LANES]
        for e in range(2):
            h = 2 * p + e
            rows = slice(e * tq, (e + 1) * tq)
            in_head = (lane >= e * HEAD_DIM) & (lane < (e + 1) * HEAD_DIM)
            qh = jnp.where(in_head, qp, jnp.zeros_like(qp))
            qaug[p, rows, 0:LANES] = qh
            qf = qh.astype(F32)
            reach = jnp.sqrt(jnp.sum(qf * qf, axis=1, keepdims=True)) * kmax[h][:, 0:1]
            slack = jnp.maximum(slack, reach)
            bound = reach - f_t[:, IDX_HEADS + h:IDX_HEADS + h + 1]
            extra = jnp.where((lane >= CUM_PARTS * e) & (lane < CUM_PARTS * (e + 1)), 1.0, 0.0)
            for j, part in enumerate(_split_bf16(-bound, CUM_PARTS)):
                extra = jnp.where(lane == bound_lanes + j, part, extra)
            qaug[p, rows, LANES:2 * LANES] = extra.astype(BF16)
    fast = 2.0 * jnp.max(slack) < FAST_GAP

    l_s[...] = jnp.zeros(l_s.shape, F32)
    acc_s[...] = jnp.zeros(acc_s.shape, F32)
    key_row = lax.broadcasted_iota(I32, (tq, 2 * tq), 0)
    query = lax.broadcasted_iota(I32, (tq, 2 * tq), 1) & (tq - 1)
    causal = key_row <= query

    def logits(j, p):
        rows = pl.ds(pl.multiple_of(j * tq, tq), tq)
        return _dot_nt(kaug[rows, 2 * p * LANES:(2 * p + 2) * LANES], qaug[p])

    s0[...] = logits(0, 0)

    def step(j, masked, online):
        rows = pl.ds(pl.multiple_of(j * tq, tq), tq)
        s_next = s0[...]
        for p in range(n_pairs):
            s = s_next
            if p + 1 < n_pairs:
                s_next = logits(j, p + 1)
            else:
                s0[...] = logits(jnp.minimum(j + 1, i), 0)
            if masked:
                s = jnp.where(causal, s, NEG)
            vt = v_ref[rows, p * LANES:(p + 1) * LANES]
            if online:
                m_old = m_s[p]
                m_new = jnp.maximum(m_old, jnp.max(s, axis=0, keepdims=True))
                alpha = jnp.exp2(m_old - m_new)
                pr = jnp.exp2(s - m_new)
                l_s[p] = alpha * l_s[p] + jnp.sum(pr, axis=0, keepdims=True)
                m_s[p] = m_new
                acc_s[p] = alpha * acc_s[p] + _dot_tn(vt, pr.astype(BF16))
            else:
                pr = jnp.exp2(s)
                l_s[p] = l_s[p] + jnp.sum(pr, axis=0, keepdims=True)
                acc_s[p] = acc_s[p] + _dot_tn(vt, pr.astype(BF16))

    def run(online):
        def body(j, _):
            step(j, masked=False, online=online)
            return 0
        lax.fori_loop(0, i, body, 0)
        step(i, masked=True, online=online)

    @pl.when(fast)
    def _():
        run(online=False)

    @pl.when(jnp.logical_not(fast))
    def _():
        m_s[...] = jnp.full(m_s.shape, NEG, F32)
        run(online=True)

    feat = lax.broadcasted_iota(I32, (LANES, tq), 0)
    for p in range(n_pairs):
        o_t = acc_s[p] / l_s[p]
        o_t = jnp.where(feat < HEAD_DIM, o_t[:, :tq], o_t[:, tq:])
        o_ref[:, p * LANES:(p + 1) * LANES] = o_t.T.astype(o_ref.dtype)


def _fox(q, k, v, cum_parts, sel, batch):
    n, w = q.shape
    seq = n // batch
    tq = min(TQ_FOX, seq)
    nq = seq // tq
    n_pairs = w // LANES
    return pl.pallas_call(
        _fox_kernel,
        grid=(batch, nq),
        in_specs=[pl.BlockSpec((tq, w), lambda b, i: (b * nq + i, 0)),
                  pl.BlockSpec((seq, w), lambda b, i: (b, 0)),
                  pl.BlockSpec((seq, w), lambda b, i: (b, 0)),
                  pl.BlockSpec((CUM_PARTS, seq, LANES), lambda b, i: (0, b, 0)),
                  _const_spec(sel.shape)],
        out_specs=pl.BlockSpec((tq, w), lambda b, i: (b * nq + i, 0)),
        out_shape=jax.ShapeDtypeStruct((n, w), BF16),
        scratch_shapes=[pltpu.VMEM((seq, 2 * w), BF16),
                        pltpu.VMEM((n_pairs, 2 * tq, 2 * LANES), BF16),
                        pltpu.VMEM((tq, 2 * tq), F32),
                        pltpu.VMEM((2 * n_pairs, 1, LANES), F32),
                        pltpu.VMEM((n_pairs, 1, 2 * tq), F32),
                        pltpu.VMEM((n_pairs, 1, 2 * tq), F32),
                        pltpu.VMEM((n_pairs, LANES, 2 * tq), F32)],
        compiler_params=_params("parallel", "arbitrary"),
        name="fox",
    )(q, k, v, cum_parts, sel)


def _cum_selector(n_pairs, first_lane):
    sel = np.zeros((n_pairs, CUM_PARTS, LANES, LANES), np.float32)
    for p in range(n_pairs):
        for j in range(CUM_PARTS):
            for e in range(2):
                sel[p, j, first_lane + 2 * p + e, CUM_PARTS * e + j] = -1.0
    return jnp.asarray(sel, BF16)


def _float_key(x):
    b = lax.bitcast_convert_type(x, I32)
    return b ^ ((b >> 31) & 0x7FFFFFFF)


def _dsa_kernel(qb_ref, qi_ref, sm_ref, ckv_ref, ki_ref, kvn_ref, wuk_ref, wuv_ref, bias_ref, brng_ref,
                o_ref, c_s, ki_s, keys, keys16, qim, d0, s0, cmax, m_s, l_s, acc_s, *, k_sel):
    tq = qb_ref.shape[0]
    seq = ckv_ref.shape[0]
    heads = bias_ref.shape[0]
    tk = TK_DSA
    i = pl.program_id(1)
    lane = lax.broadcasted_iota(I32, (1, LANES), 1)

    @pl.when(i == 0)
    def _():
        c_s[0:DSA_PAD, :] = jnp.zeros((DSA_PAD, 2 * LANES), BF16)
        ki_s[0:DSA_PAD, :] = jnp.zeros((DSA_PAD, LANES), BF16)
        c = _rms(ckv_ref[...].astype(F32), kvn_ref[...]).astype(BF16)
        c_s[DSA_PAD:, 0:LANES] = c
        c_s[DSA_PAD:, LANES:] = jnp.broadcast_to(jnp.where(lane < CUM_PARTS, 1.0, 0.0),
                                                 (seq, LANES)).astype(BF16)
        ki_s[DSA_PAD:, :] = ki_ref[...]
        cf = c.astype(F32)
        n2 = jnp.max(jnp.sum(cf * cf, axis=1, keepdims=True), axis=0, keepdims=True)
        cmax[...] = jnp.broadcast_to(jnp.sqrt(n2), cmax.shape)

    n_valid = (i + 1) * tq
    n_tiles = (n_valid + tk - 1) // tk
    max_tiles = pl.cdiv(seq, tk)

    def tile_rows(k):
        return pl.ds(pl.multiple_of(n_valid - (k + 1) * tk + DSA_PAD, tq), tk)

    lane = lax.broadcasted_iota(I32, (1, LANES), 1)
    for h in range(IDX_HEADS):
        qp = qi_ref[:, (h // 2) * LANES:(h // 2 + 1) * LANES]
        e = h % 2
        in_head = (lane >= e * IDX_DIM) & (lane < (e + 1) * IDX_DIM)
        qim[h * tq:(h + 1) * tq, :] = jnp.where(in_head, qp, jnp.zeros_like(qp))
    w_t = sm_ref[...].T[0:IDX_HEADS, :] * (IDX_HEADS ** -0.5 * IDX_DIM ** -0.5)
    q_chunk = (i * tq + lax.broadcasted_iota(I32, (tk, tq), 1)) >> CHUNK_SHIFT
    key_local = lax.broadcasted_iota(I32, (tk, tq), 0)

    def pair_dots(k, p):
        return _dot_nt(ki_s[tile_rows(k), :], qim[2 * p * tq:(2 * p + 2) * tq, :])

    d0[...] = pair_dots(0, 0)

    def score_tile(k, _):
        d_next = d0[...]
        sc = jnp.zeros((tk, tq), F32)
        for p in range(IDX_HEADS // 2):
            d = d_next
            if p + 1 < IDX_HEADS // 2:
                d_next = pair_dots(k, p + 1)
            else:
                d0[...] = pair_dots(jnp.minimum(k + 1, n_tiles - 1), 0)
            for e in range(2):
                sc = sc + jnp.maximum(d[:, e * tq:(e + 1) * tq], 0.0) * w_t[2 * p + e:2 * p + e + 1, :]
        pos = key_local + (n_valid - (k + 1) * tk)
        sc = jnp.where(pos >= 0, jnp.where((pos >> CHUNK_SHIFT) <= q_chunk, sc, NEG), NEG)
        key = _float_key(sc)
        keys[k] = key
        keys16[k] = (key >> 16).astype(I16)
        return 0

    lax.fori_loop(0, n_tiles, score_tile, 0)

    def count_rows(hit_fn):
        def body(k, cnt):
            return cnt + jnp.sum(hit_fn(k).reshape(tk // SUBLANES, SUBLANES, tq), axis=0)
        cnt = lax.fori_loop(0, n_tiles, body, jnp.zeros((SUBLANES, tq), I32))
        return jnp.sum(cnt, axis=0, keepdims=True)

    def count_ge(thr):
        return count_rows(lambda k: jnp.where(keys[k] >= thr, 1, 0))

    def count_ge16(thr16):
        rows16 = 2 * SUBLANES
        def tile(k):
            ge = jnp.where(keys16[k] >= thr16, jnp.ones((), I16), jnp.zeros((), I16))
            return _tree_sum([ge[c * rows16:(c + 1) * rows16, :] for c in range(tk // rows16)])
        branches = [functools.partial(lambda t: _tree_sum([tile(k) for k in range(t)]), t)
                    for t in range(1, max_tiles + 1)]
        cnt = lax.switch(n_tiles - 1, branches)
        return jnp.sum(cnt.astype(I32), axis=0, keepdims=True)

    half = 2 ** 15

    def kth_largest16(k_need):
        def step(b, u):
            cand = u | lax.shift_left(jnp.int32(1), 15 - b)
            return jnp.where(count_ge16((cand - half).astype(I16)) >= k_need, cand, u)
        return lax.fori_loop(0, 16, step, jnp.zeros((1, tq), I32)) - half

    t_hi = kth_largest16(k_sel)
    n_above = jnp.where(t_hi >= half - 1, 0,
                        count_ge16((jnp.minimum(t_hi, half - 2) + 1).astype(I16)))

    def low_halves(k, _):
        key = keys[k]
        keys16[k] = jnp.where((key >> 16) == t_hi, (key & 0xFFFF) - half, -half).astype(I16)
        return 0

    lax.fori_loop(0, n_tiles, low_halves, 0)
    t_lo = kth_largest16(k_sel - n_above)
    thr = jnp.maximum(lax.shift_left(t_hi, 16) + (t_lo + half), KEY_NEG + 1)

    excess = count_ge(thr) - k_sel

    @pl.when(jnp.max(excess) > 0)
    def _():
        def count_eq_below(cut):
            def hit(k):
                pos = key_local + (n_valid - (k + 1) * tk)
                return jnp.where(keys[k] == thr, jnp.where(pos < cut, 1, 0), 0)
            return count_rows(hit)

        n_bits = max(1, (2 * seq - 1).bit_length())
        keep = count_eq_below(jnp.full((1, tq), 2 ** n_bits, I32)) - jnp.maximum(excess, 0)

        def cut_step(b, cut):
            cand = cut | lax.shift_left(jnp.int32(1), n_bits - 1 - b)
            return jnp.where(count_eq_below(cand) <= keep, cand, cut)

        cut = lax.fori_loop(0, n_bits, cut_step, jnp.zeros((1, tq), I32))

        def drop(k, _):
            pos = key_local + (n_valid - (k + 1) * tk)
            kk = keys[k]
            keys[k] = jnp.where(kk == thr, jnp.where(pos >= cut, KEY_NEG, kk), kk)
            return 0

        lax.fori_loop(0, n_tiles, drop, 0)

    ql_t = (_dot_nt(wuk_ref[...], qb_ref[...]) * (HEAD_DIM ** -0.5 * LOG2E)).astype(BF16)
    n_groups = heads // DSA_HEAD_GROUP
    gw = DSA_HEAD_GROUP * tq
    slack = jnp.zeros((1, tq), F32)
    row_id = lax.broadcasted_iota(I32, (LANES, tq), 0)
    ql_aug = []
    for h in range(heads):
        ql_h = ql_t[h * LANES:(h + 1) * LANES, :]
        qf = ql_h.astype(F32)
        reach = jnp.sqrt(jnp.sum(qf * qf, axis=0, keepdims=True)) * cmax[:, 0:1]
        slack = jnp.maximum(slack, 2.0 * reach + (brng_ref[0, h][:, 0:1] - brng_ref[1, h][:, 0:1]))
        bound_rows = jnp.zeros((LANES, tq), F32)
        for j, part in enumerate(_split_bf16(-(reach + brng_ref[0, h][:, 0:1]), CUM_PARTS)):
            bound_rows = jnp.where(row_id == j, part, bound_rows)
        ql_aug.append(jnp.concatenate([ql_h, bound_rows.astype(BF16)], axis=0))
    ql_g = [jnp.concatenate(ql_aug[g * DSA_HEAD_GROUP:(g + 1) * DSA_HEAD_GROUP], axis=1)
            for g in range(n_groups)]
    fast = jnp.max(slack) < FAST_GAP
    l_s[...] = jnp.zeros(l_s.shape, F32)
    acc_s[...] = jnp.zeros(acc_s.shape, F32)

    def attend(k, near, online):
        ct_aug = c_s[tile_rows(k), :]
        ct = c_s[tile_rows(k), 0:LANES]
        mask = jnp.where(keys[k] >= thr, 0.0, NEG)
        s_next = s0[...]
        for g in range(n_groups):
            s_g = s_next
            if g + 1 < n_groups:
                s_next = _dot(ct_aug, ql_g[g + 1])
            else:
                s0[...] = _dot(c_s[tile_rows(jnp.minimum(k + 1, n_tiles - 1)), :], ql_g[0])
            parts = []
            for hh in range(DSA_HEAD_GROUP):
                s = s_g[:, hh * tq:(hh + 1) * tq]
                if near:
                    s = jnp.concatenate([s[:tk - DSA_NEAR_KEYS], s[tk - DSA_NEAR_KEYS:]
                                         + bias_ref[g * DSA_HEAD_GROUP + hh]], axis=0)
                parts.append(s + mask)
            s = jnp.concatenate(parts, axis=1)
            if online:
                m_old = m_s[g]
                m_new = jnp.maximum(m_old, jnp.max(s, axis=0, keepdims=True))
                alpha = jnp.exp2(m_old - m_new)
                pr = jnp.exp2(s - m_new)
                l_s[g] = alpha * l_s[g] + jnp.sum(pr, axis=0, keepdims=True)
                m_s[g] = m_new
                acc_s[g] = alpha * acc_s[g] + _dot_tn(ct, pr.astype(BF16))
            else:
                pr = jnp.exp2(s)
                l_s[g] = l_s[g] + jnp.sum(pr, axis=0, keepdims=True)
                acc_s[g] = acc_s[g] + _dot_tn(ct, pr.astype(BF16))

    s0[...] = _dot(c_s[tile_rows(0), :], ql_g[0])

    def run(online):
        attend(0, near=True, online=online)

        def far(k, _):
            attend(k, near=False, online=online)
            return 0

        lax.fori_loop(1, n_tiles, far, 0)

    @pl.when(fast)
    def _():
        run(online=False)

    @pl.when(jnp.logical_not(fast))
    def _():
        m_s[...] = jnp.full(m_s.shape, NEG, F32)
        run(online=True)

    o_parts = []
    for g in range(n_groups):
        o_g = acc_s[g] / l_s[g]
        o_parts += [o_g[:, hh * tq:(hh + 1) * tq] for hh in range(DSA_HEAD_GROUP)]
    o_lat_t = jnp.concatenate(o_parts, axis=0).astype(BF16)
    o_ref[...] = _dot(wuv_ref[...], o_lat_t).T.astype(o_ref.dtype)


def _dsa(qb, qi, small, ckv, ki2, kvn, wuk_bd, wuv_bd, bias_near, batch):
    n, w = qb.shape
    seq = n // batch
    tq = TQ_DSA
    nq = seq // tq
    heads = bias_near.shape[0]
    n_groups = heads // DSA_HEAD_GROUP
    gw = DSA_HEAD_GROUP * tq
    k_sel = min(TOPK_MAX, seq // 4)
    blk = lambda width: pl.BlockSpec((tq, width), lambda b, i: (b * nq + i, 0))
    per_batch = pl.BlockSpec((seq, LANES), lambda b, i: (b, 0))
    bias_rng = jnp.stack([jnp.maximum(jnp.max(bias_near, axis=(1, 2)), 0.0),
                          jnp.minimum(jnp.min(bias_near, axis=(1, 2)), 0.0)])
    bias_rng = jnp.broadcast_to(bias_rng[:, :, None, None], (2, heads, 1, LANES))
    return pl.pallas_call(
        functools.partial(_dsa_kernel, k_sel=k_sel),
        grid=(batch, nq),
        in_specs=[blk(w), blk(w), blk(LANES), per_batch, per_batch, _const_spec(kvn.shape),
                  _const_spec(wuk_bd.shape), _const_spec(wuv_bd.shape),
                  _const_spec(bias_near.shape), _const_spec(bias_rng.shape)],
        out_specs=blk(w),
        out_shape=jax.ShapeDtypeStruct((n, w), BF16),
        scratch_shapes=[pltpu.VMEM((seq + DSA_PAD, 2 * LANES), BF16),
                        pltpu.VMEM((seq + DSA_PAD, LANES), BF16),
                        pltpu.VMEM((pl.cdiv(seq, TK_DSA), TK_DSA, tq), I32),
                        pltpu.VMEM((pl.cdiv(seq, TK_DSA), TK_DSA, tq), I16),
                        pltpu.VMEM((IDX_HEADS * tq, LANES), BF16),
                        pltpu.VMEM((TK_DSA, 2 * tq), F32),
                        pltpu.VMEM((TK_DSA, gw), F32),
                        pltpu.VMEM((1, LANES), F32),
                        pltpu.VMEM((n_groups, 1, gw), F32),
                        pltpu.VMEM((n_groups, 1, gw), F32),
                        pltpu.VMEM((n_groups, LANES, gw), F32)],
        compiler_params=_params("parallel", "arbitrary"),
        name="dsa",
    )(qb, qi, small, ckv, ki2, kvn, wuk_bd, wuv_bd, bias_near, bias_rng)


def _merge_mem_kernel(x_ref, oa_ref, ob_ref, oc_ref, gate_ref, wbr_ref, wout_ref,
                      g_ref, wq_ref, kv_ref, wo_ref, o_ref):
    d = x_ref.shape[1]
    merged = jnp.zeros(x_ref.shape, F32)
    for j, br_ref in enumerate((oa_ref, ob_ref, oc_ref)):
        gate = jax.nn.sigmoid(gate_ref[:, j * d:(j + 1) * d].astype(F32))
        merged = merged + gate * _dot(br_ref[...], wbr_ref[j])
    x = x_ref[...] + _dot(merged.astype(BF16), wout_ref[...])
    hw = MEM_HEADS * MEM_HEAD_DIM
    q = _dot(_rms(x, g_ref[...]).astype(BF16), wq_ref[...]).astype(BF16)
    outs = []
    for h in range(MEM_HEADS):
        lanes = slice(h * MEM_HEAD_DIM, (h + 1) * MEM_HEAD_DIM)
        s = _dot_nt(q[:, lanes], kv_ref[:, lanes]) * (MEM_HEAD_DIM ** -0.5)
        pr = jnp.exp(s - jnp.max(s, axis=1, keepdims=True))
        pr = pr / jnp.sum(pr, axis=1, keepdims=True)
        outs.append(_dot(pr.astype(BF16), kv_ref[:, hw + h * MEM_HEAD_DIM:hw + (h + 1) * MEM_HEAD_DIM]))
    o = jnp.concatenate(outs, axis=1).astype(BF16)
    o_ref[...] = x + _dot(o, wo_ref[...])


def _merge_mem(x, oa, ob, oc, gates, wbr, wout, g, wq, kv, wo, batch):
    n, d = x.shape
    seq = n // batch
    tm = min(TM_TOKENS, seq)
    ns = seq // tm
    bw = oa.shape[1]
    mem_len = kv.shape[0] // batch
    blk = lambda width: pl.BlockSpec((tm, width), lambda b, s: (b * ns + s, 0))
    return pl.pallas_call(
        _merge_mem_kernel,
        grid=(batch, ns),
        in_specs=[blk(d), blk(bw), blk(bw), blk(bw), blk(3 * d), _const_spec(wbr.shape),
                  _const_spec(wout.shape), _const_spec((1, d)), _const_spec(wq.shape),
                  pl.BlockSpec((mem_len, kv.shape[1]), lambda b, s: (b, 0)),
                  _const_spec(wo.shape)],
        out_specs=blk(d),
        out_shape=jax.ShapeDtypeStruct((n, d), F32),
        compiler_params=_params("parallel", "arbitrary"),
        name="merge_mem",
    )(x, oa, ob, oc, gates, wbr, wout, g, wq, kv, wo)


def _t5_bucket(rel):
    nb = NUM_BUCKETS // 2
    max_exact = nb // 2
    base = jnp.where(rel > 0, nb, 0)
    n = jnp.abs(rel)
    n_f = jnp.maximum(n, 1).astype(jnp.float32)
    large = max_exact + (jnp.log(n_f / max_exact) / math.log(MAX_DISTANCE / max_exact)
                         * (nb - max_exact)).astype(jnp.int32)
    large = jnp.minimum(large, nb - 1)
    return base + jnp.where(n < max_exact, n, large)


def _near_bias(t5_bias):
    t = jnp.arange(TQ_DSA)[:, None]
    s = jnp.arange(DSA_NEAR_KEYS)[None, :] - MAX_DISTANCE
    onehot = jax.nn.one_hot(_t5_bucket(s - t), NUM_BUCKETS, dtype=F32)
    tbl = jnp.einsum("tsb,bh->hst", onehot, (t5_bias - t5_bias[NUM_BUCKETS // 2 - 1]) * LOG2E,
                     precision=lax.Precision.HIGHEST)
    return tbl.astype(F32)


def _block_diag(w):
    nb, a, b = w.shape
    eye = jnp.eye(nb, dtype=w.dtype)
    return (eye[:, None, :, None] * w[:, :, None, :]).reshape(nb * a, nb * b)


def _pad_cols(w, width):
    return jnp.pad(w, ((0, 0), (0, width - w.shape[1])))


def kernel(x, mem, ffn1_norm, ffn1_w_gu, ffn1_w_down, mix_norm, w_in, conv_w, conv_b, rg_wa, rg_ba, rg_wx, rg_bx, rg_lambda, kv_norm, w_uk, w_uv, forget_bias, w_branch, w_out, xattn_norm, mem_norm, w_mq, w_mkv, w_mo, ffn2_norm, ffn2_w_gu, ffn2_w_down, t5_bias, final_norm):
    batch, seq, d = x.shape
    depth = w_in.shape[0]
    d_ff = ffn1_w_down.shape[1]
    dr = conv_w.shape[2]
    heads = w_uk.shape[1]
    d_lat = w_uk.shape[3]
    n = batch * seq
    assert seq % TK_DSA == 0 and d_lat == LANES and 2 * IDX_DIM == LANES and 2 * HEAD_DIM == LANES

    widths = (dr, dr, heads * HEAD_DIM, d_lat, IDX_HEADS * IDX_DIM, IDX_DIM, IDX_HEADS,
              heads * HEAD_DIM, heads * HEAD_DIM, heads * HEAD_DIM, heads, d, d, d)
    offs = np.concatenate([[0], np.cumsum(widths)])
    col = lambda w, j: w[:, int(offs[j]):int(offs[j + 1])]

    bias_near = _near_bias(t5_bias)
    cum_sel = _cum_selector(heads * HEAD_DIM // LANES, IDX_HEADS)
    row = lambda v: v.reshape(1, -1).astype(F32)

    xf = x.reshape(n, d)
    memf = mem.reshape(batch * mem.shape[1], d)
    for l in range(depth):
        wl = w_in[l]
        small_w = _pad_cols(jnp.concatenate([col(wl, 6), col(wl, 10)], axis=1), LANES)
        w_all = jnp.concatenate(
            [col(wl, 0), col(wl, 1), col(wl, 2), col(wl, 3), col(wl, 4), col(wl, 5), col(wl, 5),
             col(wl, 7) * (HEAD_DIM ** -0.5 * LOG2E),
             col(wl, 8), col(wl, 9), col(wl, 11), col(wl, 12), col(wl, 13)],
            axis=1).astype(BF16)
        w_all = jnp.concatenate([w_all, small_w.astype(BF16)], axis=1)
        out_widths = (2 * dr, heads * HEAD_DIM, d_lat, IDX_HEADS * IDX_DIM, 2 * IDX_DIM,
                      heads * HEAD_DIM, heads * HEAD_DIM, heads * HEAD_DIM, 3 * d, LANES)
        out_dtypes = (BF16,) * 9 + (F32,)

        xf = _ffn(xf, row(ffn1_norm[l]), ffn1_w_gu[l][:, :d_ff].astype(BF16),
                  ffn1_w_gu[l][:, d_ff:].astype(BF16), ffn1_w_down[l].astype(BF16),
                  row(final_norm), norm_out=False)

        rg, qb, ckv, qi, ki2, qc, kc, vc, gates, small = _norm_proj(
            xf, row(mix_norm[l]), w_all, out_widths, out_dtypes)

        wax = jnp.concatenate([_block_diag(rg_wa[l]), _block_diag(rg_wx[l])], axis=1).astype(BF16)
        bax = jnp.concatenate([rg_ba[l], rg_bx[l]]).reshape(1, -1)
        o_a = _rglru(rg, batch, conv_w[l], row(conv_b[l]), wax, bax, row(rg_lambda[l]))

        fbias = _pad_cols(jnp.concatenate([jnp.zeros((IDX_HEADS,), F32), forget_bias[l]])[None], LANES)
        o_c = _fox(qc, kc, vc, _forget_cumsum(small, batch, fbias), cum_sel, batch)

        o_b = _dsa(qb, qi, small, ckv, ki2, row(kv_norm[l]), _block_diag(w_uk[l]).T.astype(BF16),
                   _block_diag(w_uv[l]).T.astype(BF16), bias_near, batch)

        (kv,) = _norm_proj(memf, row(mem_norm[l]), w_mkv[l].astype(BF16),
                           (w_mkv.shape[2],), (BF16,))
        xf = _merge_mem(xf, o_a, o_b, o_c, gates, w_branch[l].astype(BF16), w_out[l].astype(BF16),
                        row(xattn_norm[l]), w_mq[l].astype(BF16), kv, w_mo[l].astype(BF16), batch)

        xf = _ffn(xf, row(ffn2_norm[l]), ffn2_w_gu[l][:, :d_ff].astype(BF16),
                  ffn2_w_gu[l][:, d_ff:].astype(BF16), ffn2_w_down[l].astype(BF16),
                  row(final_norm), norm_out=(l == depth - 1))
    return xf.reshape(batch, seq, d)
```

```python
import functools
import math

import numpy as np
import jax
import jax.numpy as jnp
from jax import lax
from jax.experimental import pallas as pl
from jax.experimental.pallas import tpu as pltpu

F32 = jnp.float32
BF16 = jnp.bfloat16
I32 = jnp.int32
I16 = jnp.int16

LANES = 128
SUBLANES = 8
VMEM_LIMIT_BYTES = 56 * 1024 * 1024

EPS = 1e-6
NEG = -1e30
LOG2E = math.log2(math.e)
FAST_GAP = 100.0
CHUNK = 64
CHUNK_SHIFT = CHUNK.bit_length() - 1
HEAD_DIM = 64
RG_C = 8.0
CONV_W = 4
IDX_HEADS = 8
IDX_DIM = 64
TOPK_MAX = 256
MEM_HEADS = 4
MEM_HEAD_DIM = 128
NUM_BUCKETS = 32
MAX_DISTANCE = 128

TM_TOKENS = 512
FFN_CHUNK = 256
PROJ_CHUNK = 1024
TS_SCAN = 256
TQ_FOX = 512
CUM_PARTS = 3
TQ_DSA = 256
DSA_NEAR_KEYS = TQ_DSA + MAX_DISTANCE
TK_DSA = 512
DSA_HEAD_GROUP = 2
DSA_PAD = TK_DSA - TQ_DSA


def _key_of_float(v):
    b = int(np.float32(v).view(np.int32))
    return b ^ ((b >> 31) & 0x7FFFFFFF)


KEY_NEG = _key_of_float(NEG)


def _params(*semantics):
    return pltpu.CompilerParams(dimension_semantics=semantics, vmem_limit_bytes=VMEM_LIMIT_BYTES)


def _const_spec(shape):
    nd = len(shape)
    return pl.BlockSpec(shape, lambda *_: (0,) * nd, pipeline_mode=pl.Buffered(1))


def _rms(x32, g):
    ms = jnp.mean(x32 * x32, axis=-1, keepdims=True)
    return x32 * lax.rsqrt(ms + EPS) * g


def _dot(a, b):
    return jnp.dot(a, b, preferred_element_type=F32)


def _dot_nt(a, b):
    return lax.dot_general(a, b, (((1,), (1,)), ((), ())), preferred_element_type=F32)


def _tree_sum(xs):
    while len(xs) > 1:
        xs = [xs[j] + xs[j + 1] for j in range(0, len(xs) - 1, 2)] + ([xs[-1]] if len(xs) % 2 else [])
    return xs[0]


def _dot_tn(a, b):
    return lax.dot_general(a, b, (((0,), (0,)), ((), ())), preferred_element_type=F32)


def _ffn_kernel(x_ref, g_ref, wg_ref, wu_ref, wd_ref, gout_ref, o_ref, a_ref, *, norm_out):
    x = x_ref[...]
    h = _rms(x, g_ref[...]).astype(BF16)
    d_ff = wg_ref.shape[1]
    for f0 in range(0, d_ff, FFN_CHUNK):
        g = _dot(h, wg_ref[:, f0:f0 + FFN_CHUNK])
        u = _dot(h, wu_ref[:, f0:f0 + FFN_CHUNK])
        a_ref[:, f0:f0 + FFN_CHUNK] = (g * jax.nn.sigmoid(g) * u).astype(BF16)
    y = x + 0.5 * _dot(a_ref[...], wd_ref[...])
    o_ref[...] = _rms(y, gout_ref[...]) if norm_out else y


def _ffn(x, g, wg, wu, wd, g_out, norm_out):
    n, d = x.shape
    d_ff = wg.shape[1]
    tm = min(TM_TOKENS, n)
    return pl.pallas_call(
        functools.partial(_ffn_kernel, norm_out=norm_out),
        grid=(n // tm,),
        in_specs=[pl.BlockSpec((tm, d), lambda i: (i, 0)),
                  _const_spec((1, d)), _const_spec((d, d_ff)), _const_spec((d, d_ff)),
                  _const_spec((d_ff, d)), _const_spec((1, d))],
        out_specs=pl.BlockSpec((tm, d), lambda i: (i, 0)),
        out_shape=jax.ShapeDtypeStruct((n, d), F32),
        scratch_shapes=[pltpu.VMEM((tm, d_ff), BF16)],
        compiler_params=_params("parallel"),
        name="ffn",
    )(x, g, wg, wu, wd, g_out)


def _norm_proj_kernel(x_ref, g_ref, w_ref, *o_refs):
    h = _rms(x_ref[...].astype(F32), g_ref[...]).astype(BF16)
    starts = np.cumsum([0] + [o_ref.shape[1] for o_ref in o_refs])
    for c0 in range(0, w_ref.shape[1], PROJ_CHUNK):
        c1 = min(c0 + PROJ_CHUNK, w_ref.shape[1])
        y = _dot(h, w_ref[:, c0:c1])
        for o_ref, o0 in zip(o_refs, starts[:-1]):
            lo, hi = max(c0, int(o0)), min(c1, int(o0) + o_ref.shape[1])
            if lo < hi:
                o_ref[:, lo - int(o0):hi - int(o0)] = y[:, lo - c0:hi - c0].astype(o_ref.dtype)


def _norm_proj(x, g, w, widths, dtypes):
    n, d = x.shape
    tm = min(TM_TOKENS, n)
    assert sum(widths) == w.shape[1]
    return pl.pallas_call(
        _norm_proj_kernel,
        grid=(n // tm,),
        in_specs=[pl.BlockSpec((tm, d), lambda i: (i, 0)), _const_spec((1, d)),
                  _const_spec(w.shape)],
        out_specs=[pl.BlockSpec((tm, wd), lambda i: (i, 0)) for wd in widths],
        out_shape=[jax.ShapeDtypeStruct((n, wd), dt) for wd, dt in zip(widths, dtypes)],
        compiler_params=_params("parallel"),
        name="norm_proj",
    )(x, g, w)


def _gelu_tanh(x):
    return 0.5 * x * (1.0 + jnp.tanh(math.sqrt(2.0 / math.pi) * (x + 0.044715 * (x * x * x))))


def _softplus(x):
    return jnp.maximum(x, 0.0) + jnp.log1p(jnp.exp(-jnp.abs(x)))


def _rglru_kernel(rg_ref, cw_ref, cb_ref, wax_ref, bax_ref, lam_ref, o_ref, xbuf, hc):
    ts = rg_ref.shape[0]
    dr = o_ref.shape[1]

    @pl.when(pl.program_id(1) == 0)
    def _():
        xbuf[0:8, :] = jnp.zeros((8, dr), F32)
        hc[...] = jnp.zeros_like(hc)

    xr = rg_ref[:, :dr].astype(F32)
    gr = rg_ref[:, dr:].astype(F32)
    xbuf[8:, :] = xr
    xc = cb_ref[...] + cw_ref[3:4, :] * xr
    for j in range(CONV_W - 1):
        xc = xc + cw_ref[j:j + 1, :] * xbuf[pl.ds(5 + j, ts), :]
    xbuf[0:8, :] = xbuf[ts:ts + 8, :]

    ax = _dot(xc.astype(BF16), wax_ref[...]) + bax_ref[...]
    r = jax.nn.sigmoid(ax[:, :dr])
    gi = jax.nn.sigmoid(ax[:, dr:])
    log_a = (-RG_C) * r * _softplus(-lam_ref[...])
    a = jnp.exp(log_a)
    th = jnp.tanh(log_a)
    num = -2.0 * th
    u = num * lax.rsqrt(jnp.maximum(num * (1.0 - th), jnp.finfo(F32).tiny)) * (gi * xc)

    row = lax.broadcasted_iota(I32, (ts, dr), 0) & (SUBLANES - 1)
    d = 1
    while d < SUBLANES:
        a_sh = pltpu.roll(a, d, 0)
        u_sh = pltpu.roll(u, d, 0)
        keep = row >= d
        u = jnp.where(keep, a * u_sh + u, u)
        a = jnp.where(keep, a * a_sh, a)
        d *= 2
    carry = hc[0:1, :]
    groups = []
    for g in range(ts // SUBLANES):
        rows = slice(g * SUBLANES, (g + 1) * SUBLANES)
        h_g = u[rows] + a[rows] * carry
        groups.append(h_g)
        carry = h_g[SUBLANES - 1:SUBLANES, :]
    h = jnp.concatenate(groups, axis=0)
    hc[...] = jnp.broadcast_to(carry, hc.shape)
    o_ref[...] = (h * _gelu_tanh(gr)).astype(o_ref.dtype)


def _rglru(rg, batch, conv_w, conv_b, wax, bax, lam):
    n, two_dr = rg.shape
    dr = two_dr // 2
    seq = n // batch
    ts = min(TS_SCAN, seq)
    ns = seq // ts
    return pl.pallas_call(
        _rglru_kernel,
        grid=(batch, ns),
        in_specs=[pl.BlockSpec((ts, two_dr), lambda b, s: (b * ns + s, 0)),
                  _const_spec(conv_w.shape), _const_spec(conv_b.shape), _const_spec(wax.shape),
                  _const_spec(bax.shape), _const_spec(lam.shape)],
        out_specs=pl.BlockSpec((ts, dr), lambda b, s: (b * ns + s, 0)),
        out_shape=jax.ShapeDtypeStruct((n, dr), BF16),
        scratch_shapes=[pltpu.VMEM((ts + 8, dr), F32), pltpu.VMEM((8, dr), F32)],
        compiler_params=_params("parallel", "arbitrary"),
        name="rglru",
    )(rg, conv_w, conv_b, wax, bax, lam)


def _forget_cumsum_kernel(f_ref, b_ref, o_ref):
    z = f_ref[...] + b_ref[...]
    x = -_softplus(-z)
    seq = x.shape[0]
    row = lax.broadcasted_iota(I32, x.shape, 0)
    d = 1
    while d < seq:
        x = jnp.where(row >= d, x + pltpu.roll(x, d, 0), x)
        d *= 2
    x = x * LOG2E
    for j in range(CUM_PARTS):
        part = x.astype(BF16)
        o_ref[j] = part
        x = x - part.astype(F32)


def _forget_cumsum(small, batch, fbias):
    n, w = small.shape
    seq = n // batch
    return pl.pallas_call(
        _forget_cumsum_kernel,
        grid=(batch,),
        in_specs=[pl.BlockSpec((seq, w), lambda b: (b, 0)), _const_spec((1, w))],
        out_specs=pl.BlockSpec((CUM_PARTS, seq, w), lambda b: (0, b, 0)),
        out_shape=jax.ShapeDtypeStruct((CUM_PARTS, n, w), BF16),
        compiler_params=_params("parallel"),
        name="forget_cumsum",
    )(small, fbias)


def _split_bf16(x, n):
    parts = []
    for _ in range(n):
        part = x.astype(BF16).astype(F32)
        parts.append(part)
        x = x - part
    return parts


def _fox_kernel(q_ref, k_ref, v_ref, cp_ref, sel_ref, o_ref, kaug, qaug, s0, kmax, m_s, l_s, acc_s):
    tq = q_ref.shape[0]
    n_pairs = q_ref.shape[1] // LANES
    i = pl.program_id(1)
    lane = lax.broadcasted_iota(I32, (1, LANES), 1)
    bound_lanes = 2 * CUM_PARTS

    @pl.when(i == 0)
    def _():
        ones = jnp.where((lane >= bound_lanes) & (lane < bound_lanes + CUM_PARTS), 1.0, 0.0)
        for p in range(n_pairs):
            kp = k_ref[:, p * LANES:(p + 1) * LANES]
            kaug[:, 2 * p * LANES:(2 * p + 1) * LANES] = kp
            extra = _dot(cp_ref[0], sel_ref[p, 0])
            for j in range(1, CUM_PARTS):
                extra = extra + _dot(cp_ref[j], sel_ref[p, j])
            kaug[:, (2 * p + 1) * LANES:(2 * p + 2) * LANES] = (extra + ones).astype(BF16)
            k2 = kp.astype(F32) * kp.astype(F32)
            for e in range(2):
                in_head = (lane >= e * HEAD_DIM) & (lane < (e + 1) * HEAD_DIM)
                n2 = jnp.sum(jnp.where(in_head, k2, 0.0), axis=1, keepdims=True)
                kmax[2 * p + e] = jnp.broadcast_to(jnp.sqrt(jnp.max(n2, axis=0, keepdims=True)),
                                                   (1, LANES))

    q_rows = pl.ds(pl.multiple_of(i * tq, tq), tq)
    f_t = cp_ref[0, q_rows, :].astype(F32)
    for j in range(1, CUM_PARTS):
        f_t = f_t + cp_ref[j, q_rows, :].astype(F32)

    slack = jnp.zeros((tq, 1), F32)
    for p in range(n_pairs):
        qp = q_ref[:, p * LANES:(p + 1) * LANES]
        for e in range(2):
            h = 2 * p + e
            rows = slice(e * tq, (e + 1) * tq)
            in_head = (lane >= e * HEAD_DIM) & (lane < (e + 1) * HEAD_DIM)
            qh = jnp.where(in_head, qp, jnp.zeros_like(qp))
            qaug[p, rows, 0:LANES] = qh
            qf = qh.astype(F32)
            reach = jnp.sqrt(jnp.sum(qf * qf, axis=1, keepdims=True)) * kmax[h][:, 0:1]
            slack = jnp.maximum(slack, reach)
            bound = reach - f_t[:, IDX_HEADS + h:IDX_HEADS + h + 1]
            extra = jnp.where((lane >= CUM_PARTS * e) & (lane < CUM_PARTS * (e + 1)), 1.0, 0.0)
            for j, part in enumerate(_split_bf16(-bound, CUM_PARTS)):
                extra = jnp.where(lane == bound_lanes + j, part, extra)
            qaug[p, rows, LANES:2 * LANES] = extra.astype(BF16)
    fast = 2.0 * jnp.max(slack) < FAST_GAP

    l_s[...] = jnp.zeros(l_s.shape, F32)
    acc_s[...] = jnp.zeros(acc_s.shape, F32)
    key_row = lax.broadcasted_iota(I32, (tq, 2 * tq), 0)
    query = lax.broadcasted_iota(I32, (tq, 2 * tq), 1) & (tq - 1)
    causal = key_row <= query

    def logits(j, p):
        rows = pl.ds(pl.multiple_of(j * tq, tq), tq)
        return _dot_nt(kaug[rows, 2 * p * LANES:(2 * p + 2) * LANES], qaug[p])

    s0[...] = logits(0, 0)

    def step(j, masked, online):
        rows = pl.ds(pl.multiple_of(j * tq, tq), tq)
        s_next = s0[...]
        for p in range(n_pairs):
            s = s_next
            if p + 1 < n_pairs:
                s_next = logits(j, p + 1)
            else:
                s0[...] = logits(jnp.minimum(j + 1, i), 0)
            if masked:
                s = jnp.where(causal, s, NEG)
            vt = v_ref[rows, p * LANES:(p + 1) * LANES]
            if online:
                m_old = m_s[p]
                m_new = jnp.maximum(m_old, jnp.max(s, axis=0, keepdims=True))
                alpha = jnp.exp2(m_old - m_new)
                pr = jnp.exp2(s - m_new)
                l_s[p] = alpha * l_s[p] + jnp.sum(pr, axis=0, keepdims=True)
                m_s[p] = m_new
                acc_s[p] = alpha * acc_s[p] + _dot_tn(vt, pr.astype(BF16))
            else:
                pr = jnp.exp2(s)
                l_s[p] = l_s[p] + jnp.sum(pr, axis=0, keepdims=True)
                acc_s[p] = acc_s[p] + _dot_tn(vt, pr.astype(BF16))

    def run(online):
        def body(j, _):
            step(j, masked=False, online=online)
            return 0
        lax.fori_loop(0, i, body, 0)
        step(i, masked=True, online=online)

    @pl.when(fast)
    def _():
        run(online=False)

    @pl.when(jnp.logical_not(fast))
    def _():
        m_s[...] = jnp.full(m_s.shape, NEG, F32)
        run(online=True)

    feat = lax.broadcasted_iota(I32, (LANES, tq), 0)
    for p in range(n_pairs):
        o_t = acc_s[p] / l_s[p]
        o_t = jnp.where(feat < HEAD_DIM, o_t[:, :tq], o_t[:, tq:])
        o_ref[:, p * LANES:(p + 1) * LANES] = o_t.T.astype(o_ref.dtype)


def _fox(q, k, v, cum_parts, sel, batch):
    n, w = q.shape
    seq = n // batch
    tq = min(TQ_FOX, seq)
    nq = seq // tq
    n_pairs = w // LANES
    return pl.pallas_call(
        _fox_kernel,
        grid=(batch, nq),
        in_specs=[pl.BlockSpec((tq, w), lambda b, i: (b * nq + i, 0)),
                  pl.BlockSpec((seq, w), lambda b, i: (b, 0)),
                  pl.BlockSpec((seq, w), lambda b, i: (b, 0)),
                  pl.BlockSpec((CUM_PARTS, seq, LANES), lambda b, i: (0, b, 0)),
                  _const_spec(sel.shape)],
        out_specs=pl.BlockSpec((tq, w), lambda b, i: (b * nq + i, 0)),
        out_shape=jax.ShapeDtypeStruct((n, w), BF16),
        scratch_shapes=[pltpu.VMEM((seq, 2 * w), BF16),
                        pltpu.VMEM((n_pairs, 2 * tq, 2 * LANES), BF16),
                        pltpu.VMEM((tq, 2 * tq), F32),
                        pltpu.VMEM((2 * n_pairs, 1, LANES), F32),
                        pltpu.VMEM((n_pairs, 1, 2 * tq), F32),
                        pltpu.VMEM((n_pairs, 1, 2 * tq), F32),
                        pltpu.VMEM((n_pairs, LANES, 2 * tq), F32)],
        compiler_params=_params("parallel", "arbitrary"),
        name="fox",
    )(q, k, v, cum_parts, sel)


def _cum_selector(n_pairs, first_lane):
    sel = np.zeros((n_pairs, CUM_PARTS, LANES, LANES), np.float32)
    for p in range(n_pairs):
        for j in range(CUM_PARTS):
            for e in range(2):
                sel[p, j, first_lane + 2 * p + e, CUM_PARTS * e + j] = -1.0
    return jnp.asarray(sel, BF16)


def _float_key(x):
    b = lax.bitcast_convert_type(x, I32)
    return b ^ ((b >> 31) & 0x7FFFFFFF)


def _dsa_kernel(qb_ref, qi_ref, sm_ref, ckv_ref, ki_ref, kvn_ref, wuk_ref, wuv_ref, bias_ref, brng_ref,
                o_ref, c_s, ki_s, keys, keys16, thr_s, qim, d0, s0, cmax, m_s, l_s, acc_s, *, k_sel):
    tq = TQ_DSA
    seq = ckv_ref.shape[0]
    heads = bias_ref.shape[0]
    tk = TK_DSA
    pair = pl.program_id(1)
    lane = lax.broadcasted_iota(I32, (1, LANES), 1)

    @pl.when(pair == 0)
    def _():
        c_s[0:DSA_PAD, :] = jnp.zeros((DSA_PAD, 2 * LANES), BF16)
        ki_s[0:DSA_PAD, :] = jnp.zeros((DSA_PAD, LANES), BF16)
        c = _rms(ckv_ref[...].astype(F32), kvn_ref[...]).astype(BF16)
        c_s[DSA_PAD:, 0:LANES] = c
        c_s[DSA_PAD:, LANES:] = jnp.broadcast_to(jnp.where(lane < CUM_PARTS, 1.0, 0.0),
                                                 (seq, LANES)).astype(BF16)
        ki_s[DSA_PAD:, :] = ki_ref[...]
        cf = c.astype(F32)
        n2 = jnp.max(jnp.sum(cf * cf, axis=1, keepdims=True), axis=0, keepdims=True)
        cmax[...] = jnp.broadcast_to(jnp.sqrt(n2), cmax.shape)

    n_tiles = pair + 1
    max_tiles = pl.cdiv(seq, tk)
    key_local = lax.broadcasted_iota(I32, (tk, tq), 0)

    def n_valid(blk):
        return (2 * pair + blk + 1) * tq

    def tile_rows(blk, k):
        return pl.ds(pl.multiple_of(n_valid(blk) - (k + 1) * tk + DSA_PAD, tq), tk)

    def q_rows(blk):
        return pl.ds(pl.multiple_of(blk * tq, tq), tq)

    def score_block(blk, _):
        for h in range(IDX_HEADS):
            qp = qi_ref[q_rows(blk), (h // 2) * LANES:(h // 2 + 1) * LANES]
            e = h % 2
            in_head = (lane >= e * IDX_DIM) & (lane < (e + 1) * IDX_DIM)
            qim[h * tq:(h + 1) * tq, :] = jnp.where(in_head, qp, jnp.zeros_like(qp))
        w_t = sm_ref[q_rows(blk), :].T[0:IDX_HEADS, :] * (IDX_HEADS ** -0.5 * IDX_DIM ** -0.5)
        q_chunk = ((2 * pair + blk) * tq + lax.broadcasted_iota(I32, (tk, tq), 1)) >> CHUNK_SHIFT

        def pair_dots(k, p):
            return _dot_nt(ki_s[tile_rows(blk, k), :], qim[2 * p * tq:(2 * p + 2) * tq, :])

        d0[...] = pair_dots(0, 0)

        def score_tile(k, _):
            d_next = d0[...]
            sc = jnp.zeros((tk, tq), F32)
            for p in range(IDX_HEADS // 2):
                d = d_next
                if p + 1 < IDX_HEADS // 2:
                    d_next = pair_dots(k, p + 1)
                else:
                    d0[...] = pair_dots(jnp.minimum(k + 1, n_tiles - 1), 0)
                for e in range(2):
                    sc = sc + jnp.maximum(d[:, e * tq:(e + 1) * tq], 0.0) * w_t[2 * p + e:2 * p + e + 1, :]
            pos = key_local + (n_valid(blk) - (k + 1) * tk)
            sc = jnp.where(pos >= 0, jnp.where((pos >> CHUNK_SHIFT) <= q_chunk, sc, NEG), NEG)
            key = _float_key(sc)
            keys[blk, k] = key
            keys16[blk, k] = (key >> 16).astype(I16)
            return 0

        lax.fori_loop(0, n_tiles, score_tile, 0)
        return 0

    lax.fori_loop(0, 2, score_block, 0)

    def count_ge16(thr16):
        rows16 = 2 * SUBLANES
        def tile(blk, k):
            ge = jnp.where(keys16[blk, k] >= thr16[blk], jnp.ones((), I16), jnp.zeros((), I16))
            return _tree_sum([ge[c * rows16:(c + 1) * rows16, :] for c in range(tk // rows16)])
        branches = [functools.partial(
            lambda t: tuple(_tree_sum([tile(blk, k) for k in range(t)]) for blk in range(2)), t)
            for t in range(1, max_tiles + 1)]
        cnts = lax.switch(n_tiles - 1, branches)
        return tuple(jnp.sum(c.astype(I32), axis=0, keepdims=True) for c in cnts)

    half = 2 ** 15

    def kth_largest16(k_need):
        def step(b, us):
            cands = tuple(u | lax.shift_left(jnp.int32(1), 15 - b) for u in us)
            cnts = count_ge16(tuple((c - half).astype(I16) for c in cands))
            return tuple(jnp.where(n >= need, c, u) for n, need, c, u in zip(cnts, k_need, cands, us))
        zero = jnp.zeros((1, tq), I32)
        return tuple(u - half for u in lax.fori_loop(0, 16, step, (zero, zero)))

    t_hi = kth_largest16((k_sel, k_sel))
    n_above = count_ge16(tuple((jnp.minimum(t, half - 2) + 1).astype(I16) for t in t_hi))
    n_above = tuple(jnp.where(t >= half - 1, 0, n) for t, n in zip(t_hi, n_above))

    def low_halves(k, _):
        for blk in range(2):
            key = keys[blk, k]
            keys16[blk, k] = jnp.where((key >> 16) == t_hi[blk], (key & 0xFFFF) - half, -half).astype(I16)
        return 0

    lax.fori_loop(0, n_tiles, low_halves, 0)
    t_lo = kth_largest16(tuple(k_sel - n for n in n_above))
    for blk in range(2):
        thr_s[blk] = jnp.maximum(lax.shift_left(t_hi[blk], 16) + (t_lo[blk] + half), KEY_NEG + 1)

    def finish_block(blk, _):
        thr = thr_s[blk]

        def count_rows(hit_fn):
            def body(k, cnt):
                return cnt + jnp.sum(hit_fn(k).reshape(tk // SUBLANES, SUBLANES, tq), axis=0)
            cnt = lax.fori_loop(0, n_tiles, body, jnp.zeros((SUBLANES, tq), I32))
            return jnp.sum(cnt, axis=0, keepdims=True)

        excess = count_rows(lambda k: jnp.where(keys[blk, k] >= thr, 1, 0)) - k_sel

        @pl.when(jnp.max(excess) > 0)
        def _():
            def count_eq_below(cut):
                def hit(k):
                    pos = key_local + (n_valid(blk) - (k + 1) * tk)
                    return jnp.where(keys[blk, k] == thr, jnp.where(pos < cut, 1, 0), 0)
                return count_rows(hit)

            n_bits = max(1, (2 * seq - 1).bit_length())
            keep = count_eq_below(jnp.full((1, tq), 2 ** n_bits, I32)) - jnp.maximum(excess, 0)

            def cut_step(b, cut):
                cand = cut | lax.shift_left(jnp.int32(1), n_bits - 1 - b)
                return jnp.where(count_eq_below(cand) <= keep, cand, cut)

            cut = lax.fori_loop(0, n_bits, cut_step, jnp.zeros((1, tq), I32))

            def drop(k, _):
                pos = key_local + (n_valid(blk) - (k + 1) * tk)
                kk = keys[blk, k]
                keys[blk, k] = jnp.where(kk == thr, jnp.where(pos >= cut, KEY_NEG, kk), kk)
                return 0

            lax.fori_loop(0, n_tiles, drop, 0)

        attend_block(blk, thr)
        return 0

    n_groups = heads // DSA_HEAD_GROUP
    gw = DSA_HEAD_GROUP * tq
    row_id = lax.broadcasted_iota(I32, (LANES, tq), 0)

    def attend_block(blk, thr):
        ql_t = (_dot_nt(wuk_ref[...], qb_ref[q_rows(blk), :]) * (HEAD_DIM ** -0.5 * LOG2E)).astype(BF16)
        slack = jnp.zeros((1, tq), F32)
        ql_aug = []
        for h in range(heads):
            ql_h = ql_t[h * LANES:(h + 1) * LANES, :]
            qf = ql_h.astype(F32)
            reach = jnp.sqrt(jnp.sum(qf * qf, axis=0, keepdims=True)) * cmax[:, 0:1]
            slack = jnp.maximum(slack, 2.0 * reach + (brng_ref[0, h][:, 0:1] - brng_ref[1, h][:, 0:1]))
            bound_rows = jnp.zeros((LANES, tq), F32)
            for j, part in enumerate(_split_bf16(-(reach + brng_ref[0, h][:, 0:1]), CUM_PARTS)):
                bound_rows = jnp.where(row_id == j, part, bound_rows)
            ql_aug.append(jnp.concatenate([ql_h, bound_rows.astype(BF16)], axis=0))
        ql_g = [jnp.concatenate(ql_aug[g * DSA_HEAD_GROUP:(g + 1) * DSA_HEAD_GROUP], axis=1)
                for g in range(n_groups)]
        fast = jnp.max(slack) < FAST_GAP
        l_s[...] = jnp.zeros(l_s.shape, F32)
        acc_s[...] = jnp.zeros(acc_s.shape, F32)

        def attend(k, near, online):
            ct_aug = c_s[tile_rows(blk, k), :]
            ct = c_s[tile_rows(blk, k), 0:LANES]
            mask = jnp.where(keys[blk, k] >= thr, 0.0, NEG)
            s_next = s0[...]
            for g in range(n_groups):
                s_g = s_next
                if g + 1 < n_groups:
                    s_next = _dot(ct_aug, ql_g[g + 1])
                else:
                    s0[...] = _dot(c_s[tile_rows(blk, jnp.minimum(k + 1, n_tiles - 1)), :], ql_g[0])
                parts = []
                for hh in range(DSA_HEAD_GROUP):
                    s = s_g[:, hh * tq:(hh + 1) * tq]
                    if near:
                        s = jnp.concatenate([s[:tk - DSA_NEAR_KEYS], s[tk - DSA_NEAR_KEYS:]
                                             + bias_ref[g * DSA_HEAD_GROUP + hh]], axis=0)
                    parts.append(s + mask)
                s = jnp.concatenate(parts, axis=1)
                if online:
                    m_old = m_s[g]
                    m_new = jnp.maximum(m_old, jnp.max(s, axis=0, keepdims=True))
                    alpha = jnp.exp2(m_old - m_new)
                    pr = jnp.exp2(s - m_new)
                    l_s[g] = alpha * l_s[g] + jnp.sum(pr, axis=0, keepdims=True)
                    m_s[g] = m_new
                    acc_s[g] = alpha * acc_s[g] + _dot_tn(ct, pr.astype(BF16))
                else:
                    pr = jnp.exp2(s)
                    l_s[g] = l_s[g] + jnp.sum(pr, axis=0, keepdims=True)
                    acc_s[g] = acc_s[g] + _dot_tn(ct, pr.astype(BF16))

        s0[...] = _dot(c_s[tile_rows(blk, 0), :], ql_g[0])

        def run(online):
            attend(0, near=True, online=online)

            def far(k, _):
                attend(k, near=False, online=online)
                return 0

            lax.fori_loop(1, n_tiles, far, 0)

        @pl.when(fast)
        def _():
            run(online=False)

        @pl.when(jnp.logical_not(fast))
        def _():
            m_s[...] = jnp.full(m_s.shape, NEG, F32)
            run(online=True)

        o_parts = []
        for g in range(n_groups):
            o_g = acc_s[g] / l_s[g]
            o_parts += [o_g[:, hh * tq:(hh + 1) * tq] for hh in range(DSA_HEAD_GROUP)]
        o_lat_t = jnp.concatenate(o_parts, axis=0).astype(BF16)
        o_ref[q_rows(blk), :] = _dot(wuv_ref[...], o_lat_t).T.astype(o_ref.dtype)

    lax.fori_loop(0, 2, finish_block, 0)


def _dsa(qb, qi, small, ckv, ki2, kvn, wuk_bd, wuv_bd, bias_near, batch):
    n, w = qb.shape
    seq = n // batch
    tq = TQ_DSA
    assert TK_DSA == 2 * tq and seq % TK_DSA == 0
    n_pairs = seq // (2 * tq)
    n_tiles = pl.cdiv(seq, TK_DSA)
    heads = bias_near.shape[0]
    n_groups = heads // DSA_HEAD_GROUP
    gw = DSA_HEAD_GROUP * tq
    k_sel = min(TOPK_MAX, seq // 4)
    blk = lambda width: pl.BlockSpec((2 * tq, width), lambda b, i: (b * n_pairs + i, 0))
    per_batch = pl.BlockSpec((seq, LANES), lambda b, i: (b, 0))
    bias_rng = jnp.stack([jnp.maximum(jnp.max(bias_near, axis=(1, 2)), 0.0),
                          jnp.minimum(jnp.min(bias_near, axis=(1, 2)), 0.0)])
    bias_rng = jnp.broadcast_to(bias_rng[:, :, None, None], (2, heads, 1, LANES))
    return pl.pallas_call(
        functools.partial(_dsa_kernel, k_sel=k_sel),
        grid=(batch, n_pairs),
        in_specs=[blk(w), blk(w), blk(LANES), per_batch, per_batch, _const_spec(kvn.shape),
                  _const_spec(wuk_bd.shape), _const_spec(wuv_bd.shape),
                  _const_spec(bias_near.shape), _const_spec(bias_rng.shape)],
        out_specs=blk(w),
        out_shape=jax.ShapeDtypeStruct((n, w), BF16),
        scratch_shapes=[pltpu.VMEM((seq + DSA_PAD, 2 * LANES), BF16),
                        pltpu.VMEM((seq + DSA_PAD, LANES), BF16),
                        pltpu.VMEM((2, n_tiles, TK_DSA, tq), I32),
                        pltpu.VMEM((2, n_tiles, TK_DSA, tq), I16),
                        pltpu.VMEM((2, 1, tq), I32),
                        pltpu.VMEM((IDX_HEADS * tq, LANES), BF16),
                        pltpu.VMEM((TK_DSA, 2 * tq), F32),
                        pltpu.VMEM((TK_DSA, gw), F32),
                        pltpu.VMEM((1, LANES), F32),
                        pltpu.VMEM((n_groups, 1, gw), F32),
                        pltpu.VMEM((n_groups, 1, gw), F32),
                        pltpu.VMEM((n_groups, LANES, gw), F32)],
        compiler_params=_params("parallel", "arbitrary"),
        name="dsa",
    )(qb, qi, small, ckv, ki2, kvn, wuk_bd, wuv_bd, bias_near, bias_rng)


def _merge_mem_kernel(x_ref, oa_ref, ob_ref, oc_ref, gate_ref, wbr_ref, wout_ref,
                      g_ref, wq_ref, kv_ref, wo_ref, o_ref):
    d = x_ref.shape[1]
    merged = jnp.zeros(x_ref.shape, F32)
    for j, br_ref in enumerate((oa_ref, ob_ref, oc_ref)):
        gate = jax.nn.sigmoid(gate_ref[:, j * d:(j + 1) * d].astype(F32))
        merged = merged + gate * _dot(br_ref[...], wbr_ref[j])
    x = x_ref[...] + _dot(merged.astype(BF16), wout_ref[...])
    hw = MEM_HEADS * MEM_HEAD_DIM
    q = _dot(_rms(x, g_ref[...]).astype(BF16), wq_ref[...]).astype(BF16)
    outs = []
    for h in range(MEM_HEADS):
        lanes = slice(h * MEM_HEAD_DIM, (h + 1) * MEM_HEAD_DIM)
        s = _dot_nt(q[:, lanes], kv_ref[:, lanes]) * (MEM_HEAD_DIM ** -0.5)
        pr = jnp.exp(s - jnp.max(s, axis=1, keepdims=True))
        pr = pr / jnp.sum(pr, axis=1, keepdims=True)
        outs.append(_dot(pr.astype(BF16), kv_ref[:, hw + h * MEM_HEAD_DIM:hw + (h + 1) * MEM_HEAD_DIM]))
    o = jnp.concatenate(outs, axis=1).astype(BF16)
    o_ref[...] = x + _dot(o, wo_ref[...])


def _merge_mem(x, oa, ob, oc, gates, wbr, wout, g, wq, kv, wo, batch):
    n, d = x.shape
    seq = n // batch
    tm = min(TM_TOKENS, seq)
    ns = seq // tm
    bw = oa.shape[1]
    mem_len = kv.shape[0] // batch
    blk = lambda width: pl.BlockSpec((tm, width), lambda b, s: (b * ns + s, 0))
    return pl.pallas_call(
        _merge_mem_kernel,
        grid=(batch, ns),
        in_specs=[blk(d), blk(bw), blk(bw), blk(bw), blk(3 * d), _const_spec(wbr.shape),
                  _const_spec(wout.shape), _const_spec((1, d)), _const_spec(wq.shape),
                  pl.BlockSpec((mem_len, kv.shape[1]), lambda b, s: (b, 0)),
                  _const_spec(wo.shape)],
        out_specs=blk(d),
        out_shape=jax.ShapeDtypeStruct((n, d), F32),
        compiler_params=_params("parallel", "arbitrary"),
        name="merge_mem",
    )(x, oa, ob, oc, gates, wbr, wout, g, wq, kv, wo)


def _t5_bucket(rel):
    nb = NUM_BUCKETS // 2
    max_exact = nb // 2
    base = jnp.where(rel > 0, nb, 0)
    n = jnp.abs(rel)
    n_f = jnp.maximum(n, 1).astype(jnp.float32)
    large = max_exact + (jnp.log(n_f / max_exact) / math.log(MAX_DISTANCE / max_exact)
                         * (nb - max_exact)).astype(jnp.int32)
    large = jnp.minimum(large, nb - 1)
    return base + jnp.where(n < max_exact, n, large)


def _near_bias(t5_bias):
    t = jnp.arange(TQ_DSA)[:, None]
    s = jnp.arange(DSA_NEAR_KEYS)[None, :] - MAX_DISTANCE
    onehot = jax.nn.one_hot(_t5_bucket(s - t), NUM_BUCKETS, dtype=F32)
    tbl = jnp.einsum("tsb,bh->hst", onehot, (t5_bias - t5_bias[NUM_BUCKETS // 2 - 1]) * LOG2E,
                     precision=lax.Precision.HIGHEST)
    return tbl.astype(F32)


def _block_diag(w):
    nb, a, b = w.shape
    eye = jnp.eye(nb, dtype=w.dtype)
    return (eye[:, None, :, None] * w[:, :, None, :]).reshape(nb * a, nb * b)


def _pad_cols(w, width):
    return jnp.pad(w, ((0, 0), (0, width - w.shape[1])))


def kernel(x, mem, ffn1_norm, ffn1_w_gu, ffn1_w_down, mix_norm, w_in, conv_w, conv_b, rg_wa, rg_ba, rg_wx, rg_bx, rg_lambda, kv_norm, w_uk, w_uv, forget_bias, w_branch, w_out, xattn_norm, mem_norm, w_mq, w_mkv, w_mo, ffn2_norm, ffn2_w_gu, ffn2_w_down, t5_bias, final_norm):
    batch, seq, d = x.shape
    depth = w_in.shape[0]
    d_ff = ffn1_w_down.shape[1]
    dr = conv_w.shape[2]
    heads = w_uk.shape[1]
    d_lat = w_uk.shape[3]
    n = batch * seq
    assert seq % TK_DSA == 0 and d_lat == LANES and 2 * IDX_DIM == LANES and 2 * HEAD_DIM == LANES

    widths = (dr, dr, heads * HEAD_DIM, d_lat, IDX_HEADS * IDX_DIM, IDX_DIM, IDX_HEADS,
              heads * HEAD_DIM, heads * HEAD_DIM, heads * HEAD_DIM, heads, d, d, d)
    offs = np.concatenate([[0], np.cumsum(widths)])
    col = lambda w, j: w[:, int(offs[j]):int(offs[j + 1])]

    bias_near = _near_bias(t5_bias)
    cum_sel = _cum_selector(heads * HEAD_DIM // LANES, IDX_HEADS)
    row = lambda v: v.reshape(1, -1).astype(F32)

    xf = x.reshape(n, d)
    memf = mem.reshape(batch * mem.shape[1], d)
    for l in range(depth):
        wl = w_in[l]
        small_w = _pad_cols(jnp.concatenate([col(wl, 6), col(wl, 10)], axis=1), LANES)
        w_all = jnp.concatenate(
            [col(wl, 0), col(wl, 1), col(wl, 2), col(wl, 3), col(wl, 4), col(wl, 5), col(wl, 5),
             col(wl, 7) * (HEAD_DIM ** -0.5 * LOG2E),
             col(wl, 8), col(wl, 9), col(wl, 11), col(wl, 12), col(wl, 13)],
            axis=1).astype(BF16)
        w_all = jnp.concatenate([w_all, small_w.astype(BF16)], axis=1)
        out_widths = (2 * dr, heads * HEAD_DIM, d_lat, IDX_HEADS * IDX_DIM, 2 * IDX_DIM,
                      heads * HEAD_DIM, heads * HEAD_DIM, heads * HEAD_DIM, 3 * d, LANES)
        out_dtypes = (BF16,) * 9 + (F32,)

        xf = _ffn(xf, row(ffn1_norm[l]), ffn1_w_gu[l][:, :d_ff].astype(BF16),
                  ffn1_w_gu[l][:, d_ff:].astype(BF16), ffn1_w_down[l].astype(BF16),
                  row(final_norm), norm_out=False)

        rg, qb, ckv, qi, ki2, qc, kc, vc, gates, small = _norm_proj(
            xf, row(mix_norm[l]), w_all, out_widths, out_dtypes)

        wax = jnp.concatenate([_block_diag(rg_wa[l]), _block_diag(rg_wx[l])], axis=1).astype(BF16)
        bax = jnp.concatenate([rg_ba[l], rg_bx[l]]).reshape(1, -1)
        o_a = _rglru(rg, batch, conv_w[l], row(conv_b[l]), wax, bax, row(rg_lambda[l]))

        fbias = _pad_cols(jnp.concatenate([jnp.zeros((IDX_HEADS,), F32), forget_bias[l]])[None], LANES)
        o_c = _fox(qc, kc, vc, _forget_cumsum(small, batch, fbias), cum_sel, batch)

        o_b = _dsa(qb, qi, small, ckv, ki2, row(kv_norm[l]), _block_diag(w_uk[l]).T.astype(BF16),
                   _block_diag(w_uv[l]).T.astype(BF16), bias_near, batch)

        (kv,) = _norm_proj(memf, row(mem_norm[l]), w_mkv[l].astype(BF16),
                           (w_mkv.shape[2],), (BF16,))
        xf = _merge_mem(xf, o_a, o_b, o_c, gates, w_branch[l].astype(BF16), w_out[l].astype(BF16),
                        row(xattn_norm[l]), w_mq[l].astype(BF16), kv, w_mo[l].astype(BF16), batch)

        xf = _ffn(xf, row(ffn2_norm[l]), ffn2_w_gu[l][:, :d_ff].astype(BF16),
                  ffn2_w_gu[l][:, d_ff:].astype(BF16), ffn2_w_down[l].astype(BF16),
                  row(final_norm), norm_out=(l == depth - 1))
    return xf.reshape(batch, seq, d)
```

```python
import functools
import math

import numpy as np
import jax
import jax.numpy as jnp
from jax import lax
from jax.experimental import pallas as pl
from jax.experimental.pallas import tpu as pltpu

F32 = jnp.float32
BF16 = jnp.bfloat16
I32 = jnp.int32
I16 = jnp.int16

LANES = 128
SUBLANES = 8
VMEM_LIMIT_BYTES = 56 * 1024 * 1024

EPS = 1e-6
NEG = -1e30
LOG2E = math.log2(math.e)
FAST_GAP = 100.0
CHUNK = 64
CHUNK_SHIFT = CHUNK.bit_length() - 1
HEAD_DIM = 64
RG_C = 8.0
CONV_W = 4
IDX_HEADS = 8
IDX_DIM = 64
TOPK_MAX = 256
MEM_HEADS = 4
MEM_HEAD_DIM = 128
NUM_BUCKETS = 32
MAX_DISTANCE = 128

TM_TOKENS = 512
FFN_CHUNK = 256
PROJ_CHUNK = 1024
TS_SCAN = 256
TQ_FOX = 512
CUM_PARTS = 3
TQ_DSA = 256
DSA_NEAR_KEYS = TQ_DSA + MAX_DISTANCE
TK_DSA = 512
DSA_HEAD_GROUP = 2
SEARCH_ACCUMULATORS = 4
DSA_PAD = TK_DSA - TQ_DSA


def _key_of_float(v):
    b = int(np.float32(v).view(np.int32))
    return b ^ ((b >> 31) & 0x7FFFFFFF)


KEY_NEG = _key_of_float(NEG)


def _params(*semantics):
    return pltpu.CompilerParams(dimension_semantics=semantics, vmem_limit_bytes=VMEM_LIMIT_BYTES)


def _const_spec(shape):
    nd = len(shape)
    return pl.BlockSpec(shape, lambda *_: (0,) * nd, pipeline_mode=pl.Buffered(1))


def _rms(x32, g):
    ms = jnp.mean(x32 * x32, axis=-1, keepdims=True)
    return x32 * lax.rsqrt(ms + EPS) * g


def _dot(a, b):
    return jnp.dot(a, b, preferred_element_type=F32)


def _dot_nt(a, b):
    return lax.dot_general(a, b, (((1,), (1,)), ((), ())), preferred_element_type=F32)


def _tree_sum(xs):
    while len(xs) > 1:
        xs = [xs[j] + xs[j + 1] for j in range(0, len(xs) - 1, 2)] + ([xs[-1]] if len(xs) % 2 else [])
    return xs[0]


def _dot_tn(a, b):
    return lax.dot_general(a, b, (((0,), (0,)), ((), ())), preferred_element_type=F32)


def _ffn_kernel(x_ref, g_ref, wg_ref, wu_ref, wd_ref, gout_ref, o_ref, a_ref, *, norm_out):
    x = x_ref[...]
    h = _rms(x, g_ref[...]).astype(BF16)
    d_ff = wg_ref.shape[1]
    for f0 in range(0, d_ff, FFN_CHUNK):
        g = _dot(h, wg_ref[:, f0:f0 + FFN_CHUNK])
        u = _dot(h, wu_ref[:, f0:f0 + FFN_CHUNK])
        a_ref[:, f0:f0 + FFN_CHUNK] = (g * jax.nn.sigmoid(g) * u).astype(BF16)
    y = x + 0.5 * _dot(a_ref[...], wd_ref[...])
    o_ref[...] = _rms(y, gout_ref[...]) if norm_out else y


def _ffn(x, g, wg, wu, wd, g_out, norm_out):
    n, d = x.shape
    d_ff = wg.shape[1]
    tm = min(TM_TOKENS, n)
    return pl.pallas_call(
        functools.partial(_ffn_kernel, norm_out=norm_out),
        grid=(n // tm,),
        in_specs=[pl.BlockSpec((tm, d), lambda i: (i, 0)),
                  _const_spec((1, d)), _const_spec((d, d_ff)), _const_spec((d, d_ff)),
                  _const_spec((d_ff, d)), _const_spec((1, d))],
        out_specs=pl.BlockSpec((tm, d), lambda i: (i, 0)),
        out_shape=jax.ShapeDtypeStruct((n, d), F32),
        scratch_shapes=[pltpu.VMEM((tm, d_ff), BF16)],
        compiler_params=_params("parallel"),
        name="ffn",
    )(x, g, wg, wu, wd, g_out)


def _norm_proj_kernel(x_ref, g_ref, w_ref, *o_refs):
    h = _rms(x_ref[...].astype(F32), g_ref[...]).astype(BF16)
    starts = np.cumsum([0] + [o_ref.shape[1] for o_ref in o_refs])
    for c0 in range(0, w_ref.shape[1], PROJ_CHUNK):
        c1 = min(c0 + PROJ_CHUNK, w_ref.shape[1])
        y = _dot(h, w_ref[:, c0:c1])
        for o_ref, o0 in zip(o_refs, starts[:-1]):
            lo, hi = max(c0, int(o0)), min(c1, int(o0) + o_ref.shape[1])
            if lo < hi:
                o_ref[:, lo - int(o0):hi - int(o0)] = y[:, lo - c0:hi - c0].astype(o_ref.dtype)


def _norm_proj(x, g, w, widths, dtypes):
    n, d = x.shape
    tm = min(TM_TOKENS, n)
    assert sum(widths) == w.shape[1]
    return pl.pallas_call(
        _norm_proj_kernel,
        grid=(n // tm,),
        in_specs=[pl.BlockSpec((tm, d), lambda i: (i, 0)), _const_spec((1, d)),
                  _const_spec(w.shape)],
        out_specs=[pl.BlockSpec((tm, wd), lambda i: (i, 0)) for wd in widths],
        out_shape=[jax.ShapeDtypeStruct((n, wd), dt) for wd, dt in zip(widths, dtypes)],
        compiler_params=_params("parallel"),
        name="norm_proj",
    )(x, g, w)


def _gelu_tanh(x):
    return 0.5 * x * (1.0 + jnp.tanh(math.sqrt(2.0 / math.pi) * (x + 0.044715 * (x * x * x))))


def _softplus(x):
    return jnp.maximum(x, 0.0) + jnp.log1p(jnp.exp(-jnp.abs(x)))


def _rglru_kernel(rg_ref, cw_ref, cb_ref, wax_ref, bax_ref, lam_ref, o_ref, xbuf, hc):
    ts = rg_ref.shape[0]
    dr = o_ref.shape[1]

    @pl.when(pl.program_id(1) == 0)
    def _():
        xbuf[0:8, :] = jnp.zeros((8, dr), F32)
        hc[...] = jnp.zeros_like(hc)

    xr = rg_ref[:, :dr].astype(F32)
    gr = rg_ref[:, dr:].astype(F32)
    xbuf[8:, :] = xr
    xc = cb_ref[...] + cw_ref[3:4, :] * xr
    for j in range(CONV_W - 1):
        xc = xc + cw_ref[j:j + 1, :] * xbuf[pl.ds(5 + j, ts), :]
    xbuf[0:8, :] = xbuf[ts:ts + 8, :]

    ax = _dot(xc.astype(BF16), wax_ref[...]) + bax_ref[...]
    r = jax.nn.sigmoid(ax[:, :dr])
    gi = jax.nn.sigmoid(ax[:, dr:])
    log_a = (-RG_C) * r * _softplus(-lam_ref[...])
    a = jnp.exp(log_a)
    th = jnp.tanh(log_a)
    num = -2.0 * th
    u = num * lax.rsqrt(jnp.maximum(num * (1.0 - th), jnp.finfo(F32).tiny)) * (gi * xc)

    row = lax.broadcasted_iota(I32, (ts, dr), 0) & (SUBLANES - 1)
    d = 1
    while d < SUBLANES:
        a_sh = pltpu.roll(a, d, 0)
        u_sh = pltpu.roll(u, d, 0)
        keep = row >= d
        u = jnp.where(keep, a * u_sh + u, u)
        a = jnp.where(keep, a * a_sh, a)
        d *= 2
    carry = hc[0:1, :]
    groups = []
    for g in range(ts // SUBLANES):
        rows = slice(g * SUBLANES, (g + 1) * SUBLANES)
        h_g = u[rows] + a[rows] * carry
        groups.append(h_g)
        carry = h_g[SUBLANES - 1:SUBLANES, :]
    h = jnp.concatenate(groups, axis=0)
    hc[...] = jnp.broadcast_to(carry, hc.shape)
    o_ref[...] = (h * _gelu_tanh(gr)).astype(o_ref.dtype)


def _rglru(rg, batch, conv_w, conv_b, wax, bax, lam):
    n, two_dr = rg.shape
    dr = two_dr // 2
    seq = n // batch
    ts = min(TS_SCAN, seq)
    ns = seq // ts
    return pl.pallas_call(
        _rglru_kernel,
        grid=(batch, ns),
        in_specs=[pl.BlockSpec((ts, two_dr), lambda b, s: (b * ns + s, 0)),
                  _const_spec(conv_w.shape), _const_spec(conv_b.shape), _const_spec(wax.shape),
                  _const_spec(bax.shape), _const_spec(lam.shape)],
        out_specs=pl.BlockSpec((ts, dr), lambda b, s: (b * ns + s, 0)),
        out_shape=jax.ShapeDtypeStruct((n, dr), BF16),
        scratch_shapes=[pltpu.VMEM((ts + 8, dr), F32), pltpu.VMEM((8, dr), F32)],
        compiler_params=_params("parallel", "arbitrary"),
        name="rglru",
    )(rg, conv_w, conv_b, wax, bax, lam)


def _forget_cumsum_kernel(f_ref, b_ref, o_ref):
    z = f_ref[...] + b_ref[...]
    x = -_softplus(-z)
    seq = x.shape[0]
    row = lax.broadcasted_iota(I32, x.shape, 0)
    d = 1
    while d < seq:
        x = jnp.where(row >= d, x + pltpu.roll(x, d, 0), x)
        d *= 2
    x = x * LOG2E
    for j in range(CUM_PARTS):
        part = x.astype(BF16)
        o_ref[j] = part
        x = x - part.astype(F32)


def _forget_cumsum(small, batch, fbias):
    n, w = small.shape
    seq = n // batch
    return pl.pallas_call(
        _forget_cumsum_kernel,
        grid=(batch,),
        in_specs=[pl.BlockSpec((seq, w), lambda b: (b, 0)), _const_spec((1, w))],
        out_specs=pl.BlockSpec((CUM_PARTS, seq, w), lambda b: (0, b, 0)),
        out_shape=jax.ShapeDtypeStruct((CUM_PARTS, n, w), BF16),
        compiler_params=_params("parallel"),
        name="forget_cumsum",
    )(small, fbias)


def _split_bf16(x, n):
    parts = []
    for _ in range(n):
        part = x.astype(BF16).astype(F32)
        parts.append(part)
        x = x - part
    return parts


def _fox_kernel(q_ref, k_ref, v_ref, cp_ref, sel_ref, o_ref, kaug, qaug, s0, kmax, m_s, l_s, acc_s):
    tq = q_ref.shape[0]
    n_pairs = q_ref.shape[1] // LANES
    i = pl.program_id(1)
    lane = lax.broadcasted_iota(I32, (1, LANES), 1)
    bound_lanes = 2 * CUM_PARTS

    @pl.when(i == 0)
    def _():
        ones = jnp.where((lane >= bound_lanes) & (lane < bound_lanes + CUM_PARTS), 1.0, 0.0)
        for p in range(n_pairs):
            kp = k_ref[:, p * LANES:(p + 1) * LANES]
            kaug[:, 2 * p * LANES:(2 * p + 1) * LANES] = kp
            extra = _dot(cp_ref[0], sel_ref[p, 0])
            for j in range(1, CUM_PARTS):
                extra = extra + _dot(cp_ref[j], sel_ref[p, j])
            kaug[:, (2 * p + 1) * LANES:(2 * p + 2) * LANES] = (extra + ones).astype(BF16)
            k2 = kp.astype(F32) * kp.astype(F32)
            for e in range(2):
                in_head = (lane >= e * HEAD_DIM) & (lane < (e + 1) * HEAD_DIM)
                n2 = jnp.sum(jnp.where(in_head, k2, 0.0), axis=1, keepdims=True)
                kmax[2 * p + e] = jnp.broadcast_to(jnp.sqrt(jnp.max(n2, axis=0, keepdims=True)),
                                                   (1, LANES))

    q_rows = pl.ds(pl.multiple_of(i * tq, tq), tq)
    f_t = cp_ref[0, q_rows, :].astype(F32)
    for j in range(1, CUM_PARTS):
        f_t = f_t + cp_ref[j, q_rows, :].astype(F32)

    slack = jnp.zeros((tq, 1), F32)
    for p in range(n_pairs):
        qp = q_ref[:, p * LANES:(p + 1) * LANES]
        for e in range(2):
            h = 2 * p + e
            rows = slice(e * tq, (e + 1) * tq)
            in_head = (lane >= e * HEAD_DIM) & (lane < (e + 1) * HEAD_DIM)
            qh = jnp.where(in_head, qp, jnp.zeros_like(qp))
            qaug[p, rows, 0:LANES] = qh
            qf = qh.astype(F32)
            reach = jnp.sqrt(jnp.sum(qf * qf, axis=1, keepdims=True)) * kmax[h][:, 0:1]
            slack = jnp.maximum(slack, reach)
            bound = reach - f_t[:, IDX_HEADS + h:IDX_HEADS + h + 1]
            extra = jnp.where((lane >= CUM_PARTS * e) & (lane < CUM_PARTS * (e + 1)), 1.0, 0.0)
            for j, part in enumerate(_split_bf16(-bound, CUM_PARTS)):
                extra = jnp.where(lane == bound_lanes + j, part, extra)
            qaug[p, rows, LANES:2 * LANES] = extra.astype(BF16)
    fast = 2.0 * jnp.max(slack) < FAST_GAP

    l_s[...] = jnp.zeros(l_s.shape, F32)
    acc_s[...] = jnp.zeros(acc_s.shape, F32)
    key_row = lax.broadcasted_iota(I32, (tq, 2 * tq), 0)
    query = lax.broadcasted_iota(I32, (tq, 2 * tq), 1) & (tq - 1)
    causal = key_row <= query

    def logits(j, p):
        rows = pl.ds(pl.multiple_of(j * tq, tq), tq)
        return _dot_nt(kaug[rows, 2 * p * LANES:(2 * p + 2) * LANES], qaug[p])

    s0[...] = logits(0, 0)

    def step(j, masked, online):
        rows = pl.ds(pl.multiple_of(j * tq, tq), tq)
        s_next = s0[...]
        for p in range(n_pairs):
            s = s_next
            if p + 1 < n_pairs:
                s_next = logits(j, p + 1)
            else:
                s0[...] = logits(jnp.minimum(j + 1, i), 0)
            if masked:
                s = jnp.where(causal, s, NEG)
            vt = v_ref[rows, p * LANES:(p + 1) * LANES]
            if online:
                m_old = m_s[p]
                m_new = jnp.maximum(m_old, jnp.max(s, axis=0, keepdims=True))
                alpha = jnp.exp2(m_old - m_new)
                pr = jnp.exp2(s - m_new)
                l_s[p] = alpha * l_s[p] + jnp.sum(pr, axis=0, keepdims=True)
                m_s[p] = m_new
                acc_s[p] = alpha * acc_s[p] + _dot_tn(vt, pr.astype(BF16))
            else:
                pr = jnp.exp2(s)
                l_s[p] = l_s[p] + jnp.sum(pr, axis=0, keepdims=True)
                acc_s[p] = acc_s[p] + _dot_tn(vt, pr.astype(BF16))

    def run(online):
        def body(j, _):
            step(j, masked=False, online=online)
            return 0
        lax.fori_loop(0, i, body, 0)
        step(i, masked=True, online=online)

    @pl.when(fast)
    def _():
        run(online=False)

    @pl.when(jnp.logical_not(fast))
    def _():
        m_s[...] = jnp.full(m_s.shape, NEG, F32)
        run(online=True)

    feat = lax.broadcasted_iota(I32, (LANES, tq), 0)
    for p in range(n_pairs):
        o_t = acc_s[p] / l_s[p]
        o_t = jnp.where(feat < HEAD_DIM, o_t[:, :tq], o_t[:, tq:])
        o_ref[:, p * LANES:(p + 1) * LANES] = o_t.T.astype(o_ref.dtype)


def _fox(q, k, v, cum_parts, sel, batch):
    n, w = q.shape
    seq = n // batch
    tq = min(TQ_FOX, seq)
    nq = seq // tq
    n_pairs = w // LANES
    return pl.pallas_call(
        _fox_kernel,
        grid=(batch, nq),
        in_specs=[pl.BlockSpec((tq, w), lambda b, i: (b * nq + i, 0)),
                  pl.BlockSpec((seq, w), lambda b, i: (b, 0)),
                  pl.BlockSpec((seq, w), lambda b, i: (b, 0)),
                  pl.BlockSpec((CUM_PARTS, seq, LANES), lambda b, i: (0, b, 0)),
                  _const_spec(sel.shape)],
        out_specs=pl.BlockSpec((tq, w), lambda b, i: (b * nq + i, 0)),
        out_shape=jax.ShapeDtypeStruct((n, w), BF16),
        scratch_shapes=[pltpu.VMEM((seq, 2 * w), BF16),
                        pltpu.VMEM((n_pairs, 2 * tq, 2 * LANES), BF16),
                        pltpu.VMEM((tq, 2 * tq), F32),
                        pltpu.VMEM((2 * n_pairs, 1, LANES), F32),
                        pltpu.VMEM((n_pairs, 1, 2 * tq), F32),
                        pltpu.VMEM((n_pairs, 1, 2 * tq), F32),
                        pltpu.VMEM((n_pairs, LANES, 2 * tq), F32)],
        compiler_params=_params("parallel", "arbitrary"),
        name="fox",
    )(q, k, v, cum_parts, sel)


def _cum_selector(n_pairs, first_lane):
    sel = np.zeros((n_pairs, CUM_PARTS, LANES, LANES), np.float32)
    for p in range(n_pairs):
        for j in range(CUM_PARTS):
            for e in range(2):
                sel[p, j, first_lane + 2 * p + e, CUM_PARTS * e + j] = -1.0
    return jnp.asarray(sel, BF16)


def _float_key(x):
    b = lax.bitcast_convert_type(x, I32)
    return b ^ ((b >> 31) & 0x7FFFFFFF)


def _dsa_kernel(qb_ref, qi_ref, sm_ref, ckv_ref, ki_ref, kvn_ref, wuk_ref, wuv_ref, bias_ref, brng_ref,
                o_ref, c_s, ki_s, keys, keys16, thr_s, qim, d0, s0, cmax, m_s, l_s, acc_s, *, k_sel):
    tq = TQ_DSA
    seq = ckv_ref.shape[0]
    heads = bias_ref.shape[0]
    tk = TK_DSA
    pair = pl.program_id(1)
    lane = lax.broadcasted_iota(I32, (1, LANES), 1)

    @pl.when(pair == 0)
    def _():
        c_s[0:DSA_PAD, :] = jnp.zeros((DSA_PAD, 2 * LANES), BF16)
        ki_s[0:DSA_PAD, :] = jnp.zeros((DSA_PAD, LANES), BF16)
        c = _rms(ckv_ref[...].astype(F32), kvn_ref[...]).astype(BF16)
        c_s[DSA_PAD:, 0:LANES] = c
        c_s[DSA_PAD:, LANES:] = jnp.broadcast_to(jnp.where(lane < CUM_PARTS, 1.0, 0.0),
                                                 (seq, LANES)).astype(BF16)
        ki_s[DSA_PAD:, :] = ki_ref[...]
        cf = c.astype(F32)
        n2 = jnp.max(jnp.sum(cf * cf, axis=1, keepdims=True), axis=0, keepdims=True)
        cmax[...] = jnp.broadcast_to(jnp.sqrt(n2), cmax.shape)

    n_tiles = pair + 1
    max_tiles = pl.cdiv(seq, tk)
    key_local = lax.broadcasted_iota(I32, (tk, tq), 0)

    def n_valid(blk):
        return (2 * pair + blk + 1) * tq

    def tile_rows(blk, k):
        return pl.ds(pl.multiple_of(n_valid(blk) - (k + 1) * tk + DSA_PAD, tq), tk)

    def q_rows(blk):
        return pl.ds(pl.multiple_of(blk * tq, tq), tq)

    def score_block(blk, _):
        for h in range(IDX_HEADS):
            qp = qi_ref[q_rows(blk), (h // 2) * LANES:(h // 2 + 1) * LANES]
            e = h % 2
            in_head = (lane >= e * IDX_DIM) & (lane < (e + 1) * IDX_DIM)
            qim[h * tq:(h + 1) * tq, :] = jnp.where(in_head, qp, jnp.zeros_like(qp))
        w_t = sm_ref[q_rows(blk), :].T[0:IDX_HEADS, :] * (IDX_HEADS ** -0.5 * IDX_DIM ** -0.5)
        q_chunk = ((2 * pair + blk) * tq + lax.broadcasted_iota(I32, (tk, tq), 1)) >> CHUNK_SHIFT

        def pair_dots(k, p):
            return _dot_nt(ki_s[tile_rows(blk, k), :], qim[2 * p * tq:(2 * p + 2) * tq, :])

        d0[...] = pair_dots(0, 0)

        def score_tile(k, _):
            d_next = d0[...]
            sc = jnp.zeros((tk, tq), F32)
            for p in range(IDX_HEADS // 2):
                d = d_next
                if p + 1 < IDX_HEADS // 2:
                    d_next = pair_dots(k, p + 1)
                else:
                    d0[...] = pair_dots(jnp.minimum(k + 1, n_tiles - 1), 0)
                for e in range(2):
                    sc = sc + jnp.maximum(d[:, e * tq:(e + 1) * tq], 0.0) * w_t[2 * p + e:2 * p + e + 1, :]
            pos = key_local + (n_valid(blk) - (k + 1) * tk)
            sc = jnp.where(pos >= 0, jnp.where((pos >> CHUNK_SHIFT) <= q_chunk, sc, NEG), NEG)
            key = _float_key(sc)
            keys[blk, k] = key
            keys16[blk, k] = (key >> 16).astype(I16)
            return 0

        lax.fori_loop(0, n_tiles, score_tile, 0)
        return 0

    lax.fori_loop(0, 2, score_block, 0)

    def count_ge16(thr16):
        rows16 = 2 * SUBLANES

        def block_count(blk, t):
            accs = [jnp.zeros((rows16, tq), I16)] * SEARCH_ACCUMULATORS
            for k in range(t):
                for c in range(tk // rows16):
                    x = keys16[blk, k, c * rows16:(c + 1) * rows16, :]
                    ge = jnp.where(x >= thr16[blk], jnp.ones((), I16), jnp.zeros((), I16))
                    accs[c % SEARCH_ACCUMULATORS] = accs[c % SEARCH_ACCUMULATORS] + ge
            return _tree_sum(accs)

        branches = [functools.partial(lambda t: tuple(block_count(blk, t) for blk in range(2)), t)
                    for t in range(1, max_tiles + 1)]
        cnts = lax.switch(n_tiles - 1, branches)
        return tuple(jnp.sum(c.astype(I32), axis=0, keepdims=True) for c in cnts)

    half = 2 ** 15

    def kth_largest16(k_need):
        def step(b, us):
            cands = tuple(u | lax.shift_left(jnp.int32(1), 15 - b) for u in us)
            cnts = count_ge16(tuple((c - half).astype(I16) for c in cands))
            return tuple(jnp.where(n >= need, c, u) for n, need, c, u in zip(cnts, k_need, cands, us))
        zero = jnp.zeros((1, tq), I32)
        return tuple(u - half for u in lax.fori_loop(0, 16, step, (zero, zero)))

    t_hi = kth_largest16((k_sel, k_sel))
    n_above = count_ge16(tuple((jnp.minimum(t, half - 2) + 1).astype(I16) for t in t_hi))
    n_above = tuple(jnp.where(t >= half - 1, 0, n) for t, n in zip(t_hi, n_above))

    def low_halves(k, _):
        for blk in range(2):
            key = keys[blk, k]
            keys16[blk, k] = jnp.where((key >> 16) == t_hi[blk], (key & 0xFFFF) - half, -half).astype(I16)
        return 0

    lax.fori_loop(0, n_tiles, low_halves, 0)
    t_lo = kth_largest16(tuple(k_sel - n for n in n_above))
    for blk in range(2):
        thr_s[blk] = jnp.maximum(lax.shift_left(t_hi[blk], 16) + (t_lo[blk] + half), KEY_NEG + 1)

    def finish_block(blk, _):
        thr = thr_s[blk]

        def count_rows(hit_fn):
            def body(k, cnt):
                return cnt + jnp.sum(hit_fn(k).reshape(tk // SUBLANES, SUBLANES, tq), axis=0)
            cnt = lax.fori_loop(0, n_tiles, body, jnp.zeros((SUBLANES, tq), I32))
            return jnp.sum(cnt, axis=0, keepdims=True)

        excess = count_rows(lambda k: jnp.where(keys[blk, k] >= thr, 1, 0)) - k_sel

        @pl.when(jnp.max(excess) > 0)
        def _():
            def count_eq_below(cut):
                def hit(k):
                    pos = key_local + (n_valid(blk) - (k + 1) * tk)
                    return jnp.where(keys[blk, k] == thr, jnp.where(pos < cut, 1, 0), 0)
                return count_rows(hit)

            n_bits = max(1, (2 * seq - 1).bit_length())
            keep = count_eq_below(jnp.full((1, tq), 2 ** n_bits, I32)) - jnp.maximum(excess, 0)

            def cut_step(b, cut):
                cand = cut | lax.shift_left(jnp.int32(1), n_bits - 1 - b)
                return jnp.where(count_eq_below(cand) <= keep, cand, cut)

            cut = lax.fori_loop(0, n_bits, cut_step, jnp.zeros((1, tq), I32))

            def drop(k, _):
                pos = key_local + (n_valid(blk) - (k + 1) * tk)
                kk = keys[blk, k]
                keys[blk, k] = jnp.where(kk == thr, jnp.where(pos >= cut, KEY_NEG, kk), kk)
                return 0

            lax.fori_loop(0, n_tiles, drop, 0)

        attend_block(blk, thr)
        return 0

    n_groups = heads // DSA_HEAD_GROUP
    gw = DSA_HEAD_GROUP * tq
    row_id = lax.broadcasted_iota(I32, (LANES, tq), 0)

    def attend_block(blk, thr):
        ql_t = (_dot_nt(wuk_ref[...], qb_ref[q_rows(blk), :]) * (HEAD_DIM ** -0.5 * LOG2E)).astype(BF16)
        slack = jnp.zeros((1, tq), F32)
        ql_aug = []
        for h in range(heads):
            ql_h = ql_t[h * LANES:(h + 1) * LANES, :]
            qf = ql_h.astype(F32)
            reach = jnp.sqrt(jnp.sum(qf * qf, axis=0, keepdims=True)) * cmax[:, 0:1]
            slack = jnp.maximum(slack, 2.0 * reach + (brng_ref[0, h][:, 0:1] - brng_ref[1, h][:, 0:1]))
            bound_rows = jnp.zeros((LANES, tq), F32)
            for j, part in enumerate(_split_bf16(-(reach + brng_ref[0, h][:, 0:1]), CUM_PARTS)):
                bound_rows = jnp.where(row_id == j, part, bound_rows)
            ql_aug.append(jnp.concatenate([ql_h, bound_rows.astype(BF16)], axis=0))
        ql_g = [jnp.concatenate(ql_aug[g * DSA_HEAD_GROUP:(g + 1) * DSA_HEAD_GROUP], axis=1)
                for g in range(n_groups)]
        fast = jnp.max(slack) < FAST_GAP
        l_s[...] = jnp.zeros(l_s.shape, F32)
        acc_s[...] = jnp.zeros(acc_s.shape, F32)

        def attend(k, near, online):
            ct_aug = c_s[tile_rows(blk, k), :]
            ct = c_s[tile_rows(blk, k), 0:LANES]
            mask = jnp.where(keys[blk, k] >= thr, 0.0, NEG)
            s_next = s0[...]
            for g in range(n_groups):
                s_g = s_next
                if g + 1 < n_groups:
                    s_next = _dot(ct_aug, ql_g[g + 1])
                else:
                    s0[...] = _dot(c_s[tile_rows(blk, jnp.minimum(k + 1, n_tiles - 1)), :], ql_g[0])
                parts = []
                for hh in range(DSA_HEAD_GROUP):
                    s = s_g[:, hh * tq:(hh + 1) * tq]
                    if near:
                        s = jnp.concatenate([s[:tk - DSA_NEAR_KEYS], s[tk - DSA_NEAR_KEYS:]
                                             + bias_ref[g * DSA_HEAD_GROUP + hh]], axis=0)
                    parts.append(s + mask)
                s = jnp.concatenate(parts, axis=1)
                if online:
                    m_old = m_s[g]
                    m_new = jnp.maximum(m_old, jnp.max(s, axis=0, keepdims=True))
                    alpha = jnp.exp2(m_old - m_new)
                    pr = jnp.exp2(s - m_new)
                    l_s[g] = alpha * l_s[g] + jnp.sum(pr, axis=0, keepdims=True)
                    m_s[g] = m_new
                    acc_s[g] = alpha * acc_s[g] + _dot_tn(ct, pr.astype(BF16))
                else:
                    pr = jnp.exp2(s)
                    l_s[g] = l_s[g] + jnp.sum(pr, axis=0, keepdims=True)
                    acc_s[g] = acc_s[g] + _dot_tn(ct, pr.astype(BF16))

        s0[...] = _dot(c_s[tile_rows(blk, 0), :], ql_g[0])

        def run(online):
            attend(0, near=True, online=online)

            def far(k, _):
                attend(k, near=False, online=online)
                return 0

            lax.fori_loop(1, n_tiles, far, 0)

        @pl.when(fast)
        def _():
            run(online=False)

        @pl.when(jnp.logical_not(fast))
        def _():
            m_s[...] = jnp.full(m_s.shape, NEG, F32)
            run(online=True)

        o_parts = []
        for g in range(n_groups):
            o_g = acc_s[g] / l_s[g]
            o_parts += [o_g[:, hh * tq:(hh + 1) * tq] for hh in range(DSA_HEAD_GROUP)]
        o_lat_t = jnp.concatenate(o_parts, axis=0).astype(BF16)
        o_ref[q_rows(blk), :] = _dot(wuv_ref[...], o_lat_t).T.astype(o_ref.dtype)

    lax.fori_loop(0, 2, finish_block, 0)


def _dsa(qb, qi, small, ckv, ki2, kvn, wuk_bd, wuv_bd, bias_near, batch):
    n, w = qb.shape
    seq = n // batch
    tq = TQ_DSA
    assert TK_DSA == 2 * tq and seq % TK_DSA == 0
    n_pairs = seq // (2 * tq)
    n_tiles = pl.cdiv(seq, TK_DSA)
    heads = bias_near.shape[0]
    n_groups = heads // DSA_HEAD_GROUP
    gw = DSA_HEAD_GROUP * tq
    k_sel = min(TOPK_MAX, seq // 4)
    blk = lambda width: pl.BlockSpec((2 * tq, width), lambda b, i: (b * n_pairs + i, 0))
    per_batch = pl.BlockSpec((seq, LANES), lambda b, i: (b, 0))
    bias_rng = jnp.stack([jnp.maximum(jnp.max(bias_near, axis=(1, 2)), 0.0),
                          jnp.minimum(jnp.min(bias_near, axis=(1, 2)), 0.0)])
    bias_rng = jnp.broadcast_to(bias_rng[:, :, None, None], (2, heads, 1, LANES))
    return pl.pallas_call(
        functools.partial(_dsa_kernel, k_sel=k_sel),
        grid=(batch, n_pairs),
        in_specs=[blk(w), blk(w), blk(LANES), per_batch, per_batch, _const_spec(kvn.shape),
                  _const_spec(wuk_bd.shape), _const_spec(wuv_bd.shape),
                  _const_spec(bias_near.shape), _const_spec(bias_rng.shape)],
        out_specs=blk(w),
        out_shape=jax.ShapeDtypeStruct((n, w), BF16),
        scratch_shapes=[pltpu.VMEM((seq + DSA_PAD, 2 * LANES), BF16),
                        pltpu.VMEM((seq + DSA_PAD, LANES), BF16),
                        pltpu.VMEM((2, n_tiles, TK_DSA, tq), I32),
                        pltpu.VMEM((2, n_tiles, TK_DSA, tq), I16),
                        pltpu.VMEM((2, 1, tq), I32),
                        pltpu.VMEM((IDX_HEADS * tq, LANES), BF16),
                        pltpu.VMEM((TK_DSA, 2 * tq), F32),
                        pltpu.VMEM((TK_DSA, gw), F32),
                        pltpu.VMEM((1, LANES), F32),
                        pltpu.VMEM((n_groups, 1, gw), F32),
                        pltpu.VMEM((n_groups, 1, gw), F32),
                        pltpu.VMEM((n_groups, LANES, gw), F32)],
        compiler_params=_params("parallel", "arbitrary"),
        name="dsa",
    )(qb, qi, small, ckv, ki2, kvn, wuk_bd, wuv_bd, bias_near, bias_rng)


def _merge_mem_kernel(x_ref, oa_ref, ob_ref, oc_ref, gate_ref, wbr_ref, wout_ref,
                      g_ref, wq_ref, kv_ref, wo_ref, o_ref):
    d = x_ref.shape[1]
    merged = jnp.zeros(x_ref.shape, F32)
    for j, br_ref in enumerate((oa_ref, ob_ref, oc_ref)):
        gate = jax.nn.sigmoid(gate_ref[:, j * d:(j + 1) * d].astype(F32))
        merged = merged + gate * _dot(br_ref[...], wbr_ref[j])
    x = x_ref[...] + _dot(merged.astype(BF16), wout_ref[...])
    hw = MEM_HEADS * MEM_HEAD_DIM
    q = _dot(_rms(x, g_ref[...]).astype(BF16), wq_ref[...]).astype(BF16)
    outs = []
    for h in range(MEM_HEADS):
        lanes = slice(h * MEM_HEAD_DIM, (h + 1) * MEM_HEAD_DIM)
        s = _dot_nt(q[:, lanes], kv_ref[:, lanes]) * (MEM_HEAD_DIM ** -0.5)
        pr = jnp.exp(s - jnp.max(s, axis=1, keepdims=True))
        pr = pr / jnp.sum(pr, axis=1, keepdims=True)
        outs.append(_dot(pr.astype(BF16), kv_ref[:, hw + h * MEM_HEAD_DIM:hw + (h + 1) * MEM_HEAD_DIM]))
    o = jnp.concatenate(outs, axis=1).astype(BF16)
    o_ref[...] = x + _dot(o, wo_ref[...])


def _merge_mem(x, oa, ob, oc, gates, wbr, wout, g, wq, kv, wo, batch):
    n, d = x.shape
    seq = n // batch
    tm = min(TM_TOKENS, seq)
    ns = seq // tm
    bw = oa.shape[1]
    mem_len = kv.shape[0] // batch
    blk = lambda width: pl.BlockSpec((tm, width), lambda b, s: (b * ns + s, 0))
    return pl.pallas_call(
        _merge_mem_kernel,
        grid=(batch, ns),
        in_specs=[blk(d), blk(bw), blk(bw), blk(bw), blk(3 * d), _const_spec(wbr.shape),
                  _const_spec(wout.shape), _const_spec((1, d)), _const_spec(wq.shape),
                  pl.BlockSpec((mem_len, kv.shape[1]), lambda b, s: (b, 0)),
                  _const_spec(wo.shape)],
        out_specs=blk(d),
        out_shape=jax.ShapeDtypeStruct((n, d), F32),
        compiler_params=_params("parallel", "arbitrary"),
        name="merge_mem",
    )(x, oa, ob, oc, gates, wbr, wout, g, wq, kv, wo)


def _t5_bucket(rel):
    nb = NUM_BUCKETS // 2
    max_exact = nb // 2
    base = jnp.where(rel > 0, nb, 0)
    n = jnp.abs(rel)
    n_f = jnp.maximum(n, 1).astype(jnp.float32)
    large = max_exact + (jnp.log(n_f / max_exact) / math.log(MAX_DISTANCE / max_exact)
                         * (nb - max_exact)).astype(jnp.int32)
    large = jnp.minimum(large, nb - 1)
    return base + jnp.where(n < max_exact, n, large)


def _near_bias(t5_bias):
    t = jnp.arange(TQ_DSA)[:, None]
    s = jnp.arange(DSA_NEAR_KEYS)[None, :] - MAX_DISTANCE
    onehot = jax.nn.one_hot(_t5_bucket(s - t), NUM_BUCKETS, dtype=F32)
    tbl = jnp.einsum("tsb,bh->hst", onehot, (t5_bias - t5_bias[NUM_BUCKETS // 2 - 1]) * LOG2E,
                     precision=lax.Precision.HIGHEST)
    return tbl.astype(F32)


def _block_diag(w):
    nb, a, b = w.shape
    eye = jnp.eye(nb, dtype=w.dtype)
    return (eye[:, None, :, None] * w[:, :, None, :]).reshape(nb * a, nb * b)


def _pad_cols(w, width):
    return jnp.pad(w, ((0, 0), (0, width - w.shape[1])))


def kernel(x, mem, ffn1_norm, ffn1_w_gu, ffn1_w_down, mix_norm, w_in, conv_w, conv_b, rg_wa, rg_ba, rg_wx, rg_bx, rg_lambda, kv_norm, w_uk, w_uv, forget_bias, w_branch, w_out, xattn_norm, mem_norm, w_mq, w_mkv, w_mo, ffn2_norm, ffn2_w_gu, ffn2_w_down, t5_bias, final_norm):
    batch, seq, d = x.shape
    depth = w_in.shape[0]
    d_ff = ffn1_w_down.shape[1]
    dr = conv_w.shape[2]
    heads = w_uk.shape[1]
    d_lat = w_uk.shape[3]
    n = batch * seq
    assert seq % TK_DSA == 0 and d_lat == LANES and 2 * IDX_DIM == LANES and 2 * HEAD_DIM == LANES

    widths = (dr, dr, heads * HEAD_DIM, d_lat, IDX_HEADS * IDX_DIM, IDX_DIM, IDX_HEADS,
              heads * HEAD_DIM, heads * HEAD_DIM, heads * HEAD_DIM, heads, d, d, d)
    offs = np.concatenate([[0], np.cumsum(widths)])
    col = lambda w, j: w[:, int(offs[j]):int(offs[j + 1])]

    bias_near = _near_bias(t5_bias)
    cum_sel = _cum_selector(heads * HEAD_DIM // LANES, IDX_HEADS)
    row = lambda v: v.reshape(1, -1).astype(F32)

    xf = x.reshape(n, d)
    memf = mem.reshape(batch * mem.shape[1], d)
    for l in range(depth):
        wl = w_in[l]
        small_w = _pad_cols(jnp.concatenate([col(wl, 6), col(wl, 10)], axis=1), LANES)
        w_all = jnp.concatenate(
            [col(wl, 0), col(wl, 1), col(wl, 2), col(wl, 3), col(wl, 4), col(wl, 5), col(wl, 5),
             col(wl, 7) * (HEAD_DIM ** -0.5 * LOG2E),
             col(wl, 8), col(wl, 9), col(wl, 11), col(wl, 12), col(wl, 13)],
            axis=1).astype(BF16)
        w_all = jnp.concatenate([w_all, small_w.astype(BF16)], axis=1)
        out_widths = (2 * dr, heads * HEAD_DIM, d_lat, IDX_HEADS * IDX_DIM, 2 * IDX_DIM,
                      heads * HEAD_DIM, heads * HEAD_DIM, heads * HEAD_DIM, 3 * d, LANES)
        out_dtypes = (BF16,) * 9 + (F32,)

        xf = _ffn(xf, row(ffn1_norm[l]), ffn1_w_gu[l][:, :d_ff].astype(BF16),
                  ffn1_w_gu[l][:, d_ff:].astype(BF16), ffn1_w_down[l].astype(BF16),
                  row(final_norm), norm_out=False)

        rg, qb, ckv, qi, ki2, qc, kc, vc, gates, small = _norm_proj(
            xf, row(mix_norm[l]), w_all, out_widths, out_dtypes)

        wax = jnp.concatenate([_block_diag(rg_wa[l]), _block_diag(rg_wx[l])], axis=1).astype(BF16)
        bax = jnp.concatenate([rg_ba[l], rg_bx[l]]).reshape(1, -1)
        o_a = _rglru(rg, batch, conv_w[l], row(conv_b[l]), wax, bax, row(rg_lambda[l]))

        fbias = _pad_cols(jnp.concatenate([jnp.zeros((IDX_HEADS,), F32), forget_bias[l]])[None], LANES)
        o_c = _fox(qc, kc, vc, _forget_cumsum(small, batch, fbias), cum_sel, batch)

        o_b = _dsa(qb, qi, small, ckv, ki2, row(kv_norm[l]), _block_diag(w_uk[l]).T.astype(BF16),
                   _block_diag(w_uv[l]).T.astype(BF16), bias_near, batch)

        (kv,) = _norm_proj(memf, row(mem_norm[l]), w_mkv[l].astype(BF16),
                           (w_mkv.shape[2],), (BF16,))
        xf = _merge_mem(xf, o_a, o_b, o_c, gates, w_branch[l].astype(BF16), w_out[l].astype(BF16),
                        row(xattn_norm[l]), w_mq[l].astype(BF16), kv, w_mo[l].astype(BF16), batch)

        xf = _ffn(xf, row(ffn2_norm[l]), ffn2_w_gu[l][:, :d_ff].astype(BF16),
                  ffn2_w_gu[l][:, d_ff:].astype(BF16), ffn2_w_down[l].astype(BF16),
                  row(final_norm), norm_out=(l == depth - 1))
    return xf.reshape(batch, seq, d)
```

```python
import functools
import math

import numpy as np
import jax
import jax.numpy as jnp
from jax import lax
from jax.experimental import pallas as pl
from jax.experimental.pallas import tpu as pltpu

F32 = jnp.float32
BF16 = jnp.bfloat16
I32 = jnp.int32
I16 = jnp.int16

LANES = 128
SUBLANES = 8
VMEM_LIMIT_BYTES = 56 * 1024 * 1024

EPS = 1e-6
NEG = -1e30
LOG2E = math.log2(math.e)
FAST_GAP = 100.0
CHUNK = 64
CHUNK_SHIFT = CHUNK.bit_length() - 1
HEAD_DIM = 64
RG_C = 8.0
CONV_W = 4
IDX_HEADS = 8
IDX_DIM = 64
TOPK_MAX = 256
MEM_HEADS = 4
MEM_HEAD_DIM = 128
NUM_BUCKETS = 32
MAX_DISTANCE = 128

TM_TOKENS = 512
FFN_CHUNK = 256
PROJ_CHUNK = 1024
TS_SCAN = 256
TQ_FOX = 512
CUM_PARTS = 3
TQ_DSA = 256
DSA_NEAR_KEYS = TQ_DSA + MAX_DISTANCE
TK_DSA = 512
DSA_HEAD_GROUP = 2
SEARCH_ACCUMULATORS = 4
DSA_PAD = TK_DSA - TQ_DSA


def _key_of_float(v):
    b = int(np.float32(v).view(np.int32))
    return b ^ ((b >> 31) & 0x7FFFFFFF)


KEY_NEG = _key_of_float(NEG)


def _params(*semantics):
    return pltpu.CompilerParams(dimension_semantics=semantics, vmem_limit_bytes=VMEM_LIMIT_BYTES)


def _const_spec(shape):
    nd = len(shape)
    return pl.BlockSpec(shape, lambda *_: (0,) * nd, pipeline_mode=pl.Buffered(1))


def _rms(x32, g):
    ms = jnp.mean(x32 * x32, axis=-1, keepdims=True)
    return x32 * lax.rsqrt(ms + EPS) * g


def _dot(a, b):
    return jnp.dot(a, b, preferred_element_type=F32)


def _dot_nt(a, b):
    return lax.dot_general(a, b, (((1,), (1,)), ((), ())), preferred_element_type=F32)


def _tree_sum(xs):
    while len(xs) > 1:
        xs = [xs[j] + xs[j + 1] for j in range(0, len(xs) - 1, 2)] + ([xs[-1]] if len(xs) % 2 else [])
    return xs[0]


def _dot_tn(a, b):
    return lax.dot_general(a, b, (((0,), (0,)), ((), ())), preferred_element_type=F32)


def _ffn_kernel(x_ref, g_ref, wg_ref, wu_ref, wd_ref, gout_ref, o_ref, a_ref, *, norm_out):
    x = x_ref[...]
    h = _rms(x, g_ref[...]).astype(BF16)
    d_ff = wg_ref.shape[1]
    for f0 in range(0, d_ff, FFN_CHUNK):
        g = _dot(h, wg_ref[:, f0:f0 + FFN_CHUNK])
        u = _dot(h, wu_ref[:, f0:f0 + FFN_CHUNK])
        a_ref[:, f0:f0 + FFN_CHUNK] = (g * jax.nn.sigmoid(g) * u).astype(BF16)
    y = x + 0.5 * _dot(a_ref[...], wd_ref[...])
    o_ref[...] = _rms(y, gout_ref[...]) if norm_out else y


def _ffn(x, g, wg, wu, wd, g_out, norm_out):
    n, d = x.shape
    d_ff = wg.shape[1]
    tm = min(TM_TOKENS, n)
    return pl.pallas_call(
        functools.partial(_ffn_kernel, norm_out=norm_out),
        grid=(n // tm,),
        in_specs=[pl.BlockSpec((tm, d), lambda i: (i, 0)),
                  _const_spec((1, d)), _const_spec((d, d_ff)), _const_spec((d, d_ff)),
                  _const_spec((d_ff, d)), _const_spec((1, d))],
        out_specs=pl.BlockSpec((tm, d), lambda i: (i, 0)),
        out_shape=jax.ShapeDtypeStruct((n, d), F32),
        scratch_shapes=[pltpu.VMEM((tm, d_ff), BF16)],
        compiler_params=_params("parallel"),
        name="ffn",
    )(x, g, wg, wu, wd, g_out)


def _norm_proj_kernel(x_ref, g_ref, w_ref, *o_refs):
    h = _rms(x_ref[...].astype(F32), g_ref[...]).astype(BF16)
    starts = np.cumsum([0] + [o_ref.shape[1] for o_ref in o_refs])
    for c0 in range(0, w_ref.shape[1], PROJ_CHUNK):
        c1 = min(c0 + PROJ_CHUNK, w_ref.shape[1])
        y = _dot(h, w_ref[:, c0:c1])
        for o_ref, o0 in zip(o_refs, starts[:-1]):
            lo, hi = max(c0, int(o0)), min(c1, int(o0) + o_ref.shape[1])
            if lo < hi:
                o_ref[:, lo - int(o0):hi - int(o0)] = y[:, lo - c0:hi - c0].astype(o_ref.dtype)


def _norm_proj(x, g, w, widths, dtypes):
    n, d = x.shape
    tm = min(TM_TOKENS, n)
    assert sum(widths) == w.shape[1]
    return pl.pallas_call(
        _norm_proj_kernel,
        grid=(n // tm,),
        in_specs=[pl.BlockSpec((tm, d), lambda i: (i, 0)), _const_spec((1, d)),
                  _const_spec(w.shape)],
        out_specs=[pl.BlockSpec((tm, wd), lambda i: (i, 0)) for wd in widths],
        out_shape=[jax.ShapeDtypeStruct((n, wd), dt) for wd, dt in zip(widths, dtypes)],
        compiler_params=_params("parallel"),
        name="norm_proj",
    )(x, g, w)


def _gelu_tanh(x):
    return 0.5 * x * (1.0 + jnp.tanh(math.sqrt(2.0 / math.pi) * (x + 0.044715 * (x * x * x))))


def _softplus(x):
    return jnp.maximum(x, 0.0) + jnp.log1p(jnp.exp(-jnp.abs(x)))


def _rglru_kernel(rg_ref, cw_ref, cb_ref, wax_ref, bax_ref, lam_ref, o_ref, xbuf, hc):
    ts = rg_ref.shape[0]
    dr = o_ref.shape[1]

    @pl.when(pl.program_id(1) == 0)
    def _():
        xbuf[0:SUBLANES, :] = jnp.zeros((SUBLANES, dr), F32)
        hc[...] = jnp.zeros_like(hc)

    xr = rg_ref[:, :dr].astype(F32)
    gr = rg_ref[:, dr:].astype(F32)
    xbuf[SUBLANES:, :] = xr
    xc = cb_ref[...] + cw_ref[CONV_W - 1:CONV_W, :] * xr
    for j in range(CONV_W - 1):
        xc = xc + cw_ref[j:j + 1, :] * xbuf[pl.ds(SUBLANES - (CONV_W - 1) + j, ts), :]
    xbuf[0:SUBLANES, :] = xbuf[ts:ts + SUBLANES, :]

    ax = _dot(xc.astype(BF16), wax_ref[...]) + bax_ref[...]
    r = jax.nn.sigmoid(ax[:, :dr])
    gi = jax.nn.sigmoid(ax[:, dr:])
    log_a = (-RG_C) * r * _softplus(-lam_ref[...])
    a = jnp.exp(log_a)
    th = jnp.tanh(log_a)
    num = -2.0 * th
    u = num * lax.rsqrt(jnp.maximum(num * (1.0 - th), jnp.finfo(F32).tiny)) * (gi * xc)

    row = lax.broadcasted_iota(I32, (ts, dr), 0) & (SUBLANES - 1)
    d = 1
    while d < SUBLANES:
        a_sh = pltpu.roll(a, d, 0)
        u_sh = pltpu.roll(u, d, 0)
        keep = row >= d
        u = jnp.where(keep, a * u_sh + u, u)
        a = jnp.where(keep, a * a_sh, a)
        d *= 2
    carry = hc[0:1, :]
    groups = []
    for g in range(ts // SUBLANES):
        rows = slice(g * SUBLANES, (g + 1) * SUBLANES)
        h_g = u[rows] + a[rows] * carry
        groups.append(h_g)
        carry = h_g[SUBLANES - 1:SUBLANES, :]
    h = jnp.concatenate(groups, axis=0)
    hc[...] = jnp.broadcast_to(carry, hc.shape)
    o_ref[...] = (h * _gelu_tanh(gr)).astype(o_ref.dtype)


def _rglru(rg, batch, conv_w, conv_b, wax, bax, lam):
    n, two_dr = rg.shape
    dr = two_dr // 2
    seq = n // batch
    ts = min(TS_SCAN, seq)
    ns = seq // ts
    return pl.pallas_call(
        _rglru_kernel,
        grid=(batch, ns),
        in_specs=[pl.BlockSpec((ts, two_dr), lambda b, s: (b * ns + s, 0)),
                  _const_spec(conv_w.shape), _const_spec(conv_b.shape), _const_spec(wax.shape),
                  _const_spec(bax.shape), _const_spec(lam.shape)],
        out_specs=pl.BlockSpec((ts, dr), lambda b, s: (b * ns + s, 0)),
        out_shape=jax.ShapeDtypeStruct((n, dr), BF16),
        scratch_shapes=[pltpu.VMEM((ts + SUBLANES, dr), F32), pltpu.VMEM((SUBLANES, dr), F32)],
        compiler_params=_params("parallel", "arbitrary"),
        name="rglru",
    )(rg, conv_w, conv_b, wax, bax, lam)


def _forget_cumsum_kernel(f_ref, b_ref, o_ref):
    z = f_ref[...] + b_ref[...]
    x = -_softplus(-z)
    seq = x.shape[0]
    row = lax.broadcasted_iota(I32, x.shape, 0)
    d = 1
    while d < seq:
        x = jnp.where(row >= d, x + pltpu.roll(x, d, 0), x)
        d *= 2
    x = x * LOG2E
    for j in range(CUM_PARTS):
        part = x.astype(BF16)
        o_ref[j] = part
        x = x - part.astype(F32)


def _forget_cumsum(small, batch, fbias):
    n, w = small.shape
    seq = n // batch
    return pl.pallas_call(
        _forget_cumsum_kernel,
        grid=(batch,),
        in_specs=[pl.BlockSpec((seq, w), lambda b: (b, 0)), _const_spec((1, w))],
        out_specs=pl.BlockSpec((CUM_PARTS, seq, w), lambda b: (0, b, 0)),
        out_shape=jax.ShapeDtypeStruct((CUM_PARTS, n, w), BF16),
        compiler_params=_params("parallel"),
        name="forget_cumsum",
    )(small, fbias)


def _split_bf16(x, n):
    parts = []
    for _ in range(n):
        part = x.astype(BF16).astype(F32)
        parts.append(part)
        x = x - part
    return parts


def _fox_kernel(q_ref, k_ref, v_ref, cp_ref, sel_ref, o_ref, kaug, qaug, s0, kmax, m_s, l_s, acc_s):
    tq = q_ref.shape[0]
    n_pairs = q_ref.shape[1] // LANES
    i = pl.program_id(1)
    lane = lax.broadcasted_iota(I32, (1, LANES), 1)
    bound_lanes = 2 * CUM_PARTS

    @pl.when(i == 0)
    def _():
        ones = jnp.where((lane >= bound_lanes) & (lane < bound_lanes + CUM_PARTS), 1.0, 0.0)
        for p in range(n_pairs):
            kp = k_ref[:, p * LANES:(p + 1) * LANES]
            kaug[:, 2 * p * LANES:(2 * p + 1) * LANES] = kp
            extra = _dot(cp_ref[0], sel_ref[p, 0])
            for j in range(1, CUM_PARTS):
                extra = extra + _dot(cp_ref[j], sel_ref[p, j])
            kaug[:, (2 * p + 1) * LANES:(2 * p + 2) * LANES] = (extra + ones).astype(BF16)
            k2 = kp.astype(F32) * kp.astype(F32)
            for e in range(2):
                in_head = (lane >= e * HEAD_DIM) & (lane < (e + 1) * HEAD_DIM)
                n2 = jnp.sum(jnp.where(in_head, k2, 0.0), axis=1, keepdims=True)
                kmax[2 * p + e] = jnp.broadcast_to(jnp.sqrt(jnp.max(n2, axis=0, keepdims=True)),
                                                   (1, LANES))

    q_rows = pl.ds(pl.multiple_of(i * tq, tq), tq)
    f_t = cp_ref[0, q_rows, :].astype(F32)
    for j in range(1, CUM_PARTS):
        f_t = f_t + cp_ref[j, q_rows, :].astype(F32)

    slack = jnp.zeros((tq, 1), F32)
    for p in range(n_pairs):
        qp = q_ref[:, p * LANES:(p + 1) * LANES]
        for e in range(2):
            h = 2 * p + e
            rows = slice(e * tq, (e + 1) * tq)
            in_head = (lane >= e * HEAD_DIM) & (lane < (e + 1) * HEAD_DIM)
            qh = jnp.where(in_head, qp, jnp.zeros_like(qp))
            qaug[p, rows, 0:LANES] = qh
            qf = qh.astype(F32)
            reach = jnp.sqrt(jnp.sum(qf * qf, axis=1, keepdims=True)) * kmax[h][:, 0:1]
            slack = jnp.maximum(slack, reach)
            bound = reach - f_t[:, IDX_HEADS + h:IDX_HEADS + h + 1]
            extra = jnp.where((lane >= CUM_PARTS * e) & (lane < CUM_PARTS * (e + 1)), 1.0, 0.0)
            for j, part in enumerate(_split_bf16(-bound, CUM_PARTS)):
                extra = jnp.where(lane == bound_lanes + j, part, extra)
            qaug[p, rows, LANES:2 * LANES] = extra.astype(BF16)
    fast = 2.0 * jnp.max(slack) < FAST_GAP

    l_s[...] = jnp.zeros(l_s.shape, F32)
    acc_s[...] = jnp.zeros(acc_s.shape, F32)
    key_row = lax.broadcasted_iota(I32, (tq, 2 * tq), 0)
    query = lax.broadcasted_iota(I32, (tq, 2 * tq), 1) & (tq - 1)
    causal = key_row <= query

    def logits(j, p):
        rows = pl.ds(pl.multiple_of(j * tq, tq), tq)
        return _dot_nt(kaug[rows, 2 * p * LANES:(2 * p + 2) * LANES], qaug[p])

    s0[...] = logits(0, 0)

    def step(j, masked, online):
        rows = pl.ds(pl.multiple_of(j * tq, tq), tq)
        s_next = s0[...]
        for p in range(n_pairs):
            s = s_next
            if p + 1 < n_pairs:
                s_next = logits(j, p + 1)
            else:
                s0[...] = logits(jnp.minimum(j + 1, i), 0)
            if masked:
                s = jnp.where(causal, s, NEG)
            vt = v_ref[rows, p * LANES:(p + 1) * LANES]
            if online:
                m_old = m_s[p]
                m_new = jnp.maximum(m_old, jnp.max(s, axis=0, keepdims=True))
                alpha = jnp.exp2(m_old - m_new)
                pr = jnp.exp2(s - m_new)
                l_s[p] = alpha * l_s[p] + jnp.sum(pr, axis=0, keepdims=True)
                m_s[p] = m_new
                acc_s[p] = alpha * acc_s[p] + _dot_tn(vt, pr.astype(BF16))
            else:
                pr = jnp.exp2(s)
                l_s[p] = l_s[p] + jnp.sum(pr, axis=0, keepdims=True)
                acc_s[p] = acc_s[p] + _dot_tn(vt, pr.astype(BF16))

    def run(online):
        def body(j, _):
            step(j, masked=False, online=online)
            return 0
        lax.fori_loop(0, i, body, 0)
        step(i, masked=True, online=online)

    @pl.when(fast)
    def _():
        run(online=False)

    @pl.when(jnp.logical_not(fast))
    def _():
        m_s[...] = jnp.full(m_s.shape, NEG, F32)
        run(online=True)

    feat = lax.broadcasted_iota(I32, (LANES, tq), 0)
    for p in range(n_pairs):
        o_t = acc_s[p] / l_s[p]
        o_t = jnp.where(feat < HEAD_DIM, o_t[:, :tq], o_t[:, tq:])
        o_ref[:, p * LANES:(p + 1) * LANES] = o_t.T.astype(o_ref.dtype)


def _fox(q, k, v, cum_parts, sel, batch):
    n, w = q.shape
    seq = n // batch
    tq = min(TQ_FOX, seq)
    nq = seq // tq
    n_pairs = w // LANES
    return pl.pallas_call(
        _fox_kernel,
        grid=(batch, nq),
        in_specs=[pl.BlockSpec((tq, w), lambda b, i: (b * nq + i, 0)),
                  pl.BlockSpec((seq, w), lambda b, i: (b, 0)),
                  pl.BlockSpec((seq, w), lambda b, i: (b, 0)),
                  pl.BlockSpec((CUM_PARTS, seq, LANES), lambda b, i: (0, b, 0)),
                  _const_spec(sel.shape)],
        out_specs=pl.BlockSpec((tq, w), lambda b, i: (b * nq + i, 0)),
        out_shape=jax.ShapeDtypeStruct((n, w), BF16),
        scratch_shapes=[pltpu.VMEM((seq, 2 * w), BF16),
                        pltpu.VMEM((n_pairs, 2 * tq, 2 * LANES), BF16),
                        pltpu.VMEM((tq, 2 * tq), F32),
                        pltpu.VMEM((2 * n_pairs, 1, LANES), F32),
                        pltpu.VMEM((n_pairs, 1, 2 * tq), F32),
                        pltpu.VMEM((n_pairs, 1, 2 * tq), F32),
                        pltpu.VMEM((n_pairs, LANES, 2 * tq), F32)],
        compiler_params=_params("parallel", "arbitrary"),
        name="fox",
    )(q, k, v, cum_parts, sel)


def _cum_selector(n_pairs, first_lane):
    sel = np.zeros((n_pairs, CUM_PARTS, LANES, LANES), np.float32)
    for p in range(n_pairs):
        for j in range(CUM_PARTS):
            for e in range(2):
                sel[p, j, first_lane + 2 * p + e, CUM_PARTS * e + j] = -1.0
    return jnp.asarray(sel, BF16)


def _float_key(x):
    b = lax.bitcast_convert_type(x, I32)
    return b ^ ((b >> 31) & 0x7FFFFFFF)


def _dsa_kernel(qb_ref, qi_ref, sm_ref, ckv_ref, ki_ref, kvn_ref, wuk_ref, wuv_ref, bias_ref, brng_ref,
                o_ref, c_s, ki_s, keys, keys16, thr_s, qim, d0, s0, cmax, m_s, l_s, acc_s, *, k_sel):
    tq = TQ_DSA
    seq = ckv_ref.shape[0]
    heads = bias_ref.shape[0]
    tk = TK_DSA
    pair = pl.program_id(1)
    lane = lax.broadcasted_iota(I32, (1, LANES), 1)

    @pl.when(pair == 0)
    def _():
        c_s[0:DSA_PAD, :] = jnp.zeros((DSA_PAD, 2 * LANES), BF16)
        ki_s[0:DSA_PAD, :] = jnp.zeros((DSA_PAD, LANES), BF16)
        c = _rms(ckv_ref[...].astype(F32), kvn_ref[...]).astype(BF16)
        c_s[DSA_PAD:, 0:LANES] = c
        c_s[DSA_PAD:, LANES:] = jnp.broadcast_to(jnp.where(lane < CUM_PARTS, 1.0, 0.0),
                                                 (seq, LANES)).astype(BF16)
        ki_s[DSA_PAD:, :] = ki_ref[...]
        cf = c.astype(F32)
        n2 = jnp.max(jnp.sum(cf * cf, axis=1, keepdims=True), axis=0, keepdims=True)
        cmax[...] = jnp.broadcast_to(jnp.sqrt(n2), cmax.shape)

    n_tiles = pair + 1
    max_tiles = pl.cdiv(seq, tk)
    key_local = lax.broadcasted_iota(I32, (tk, tq), 0)

    def n_valid(blk):
        return (2 * pair + blk + 1) * tq

    def tile_rows(blk, k):
        return pl.ds(pl.multiple_of(n_valid(blk) - (k + 1) * tk + DSA_PAD, tq), tk)

    def q_rows(blk):
        return pl.ds(pl.multiple_of(blk * tq, tq), tq)

    def score_block(blk, _):
        for h in range(IDX_HEADS):
            qp = qi_ref[q_rows(blk), (h // 2) * LANES:(h // 2 + 1) * LANES]
            e = h % 2
            in_head = (lane >= e * IDX_DIM) & (lane < (e + 1) * IDX_DIM)
            qim[h * tq:(h + 1) * tq, :] = jnp.where(in_head, qp, jnp.zeros_like(qp))
        w_t = sm_ref[q_rows(blk), :].T[0:IDX_HEADS, :] * (IDX_HEADS ** -0.5 * IDX_DIM ** -0.5)
        q_chunk = ((2 * pair + blk) * tq + lax.broadcasted_iota(I32, (tk, tq), 1)) >> CHUNK_SHIFT

        def pair_dots(k, p):
            return _dot_nt(ki_s[tile_rows(blk, k), :], qim[2 * p * tq:(2 * p + 2) * tq, :])

        d0[...] = pair_dots(0, 0)

        def score_tile(k, _):
            d_next = d0[...]
            sc = jnp.zeros((tk, tq), F32)
            for p in range(IDX_HEADS // 2):
                d = d_next
                if p + 1 < IDX_HEADS // 2:
                    d_next = pair_dots(k, p + 1)
                else:
                    d0[...] = pair_dots(jnp.minimum(k + 1, n_tiles - 1), 0)
                for e in range(2):
                    sc = sc + jnp.maximum(d[:, e * tq:(e + 1) * tq], 0.0) * w_t[2 * p + e:2 * p + e + 1, :]
            pos = key_local + (n_valid(blk) - (k + 1) * tk)
            sc = jnp.where(pos >= 0, jnp.where((pos >> CHUNK_SHIFT) <= q_chunk, sc, NEG), NEG)
            key = _float_key(sc)
            keys[blk, k] = key
            keys16[blk, k] = (key >> 16).astype(I16)
            return 0

        lax.fori_loop(0, n_tiles, score_tile, 0)
        return 0

    lax.fori_loop(0, 2, score_block, 0)

    def count_ge16(thr16):
        rows16 = 2 * SUBLANES

        def block_count(blk, t):
            accs = [jnp.zeros((rows16, tq), I16)] * SEARCH_ACCUMULATORS
            for k in range(t):
                for c in range(tk // rows16):
                    x = keys16[blk, k, c * rows16:(c + 1) * rows16, :]
                    ge = jnp.where(x >= thr16[blk], jnp.ones((), I16), jnp.zeros((), I16))
                    accs[c % SEARCH_ACCUMULATORS] = accs[c % SEARCH_ACCUMULATORS] + ge
            return _tree_sum(accs)

        branches = [functools.partial(lambda t: tuple(block_count(blk, t) for blk in range(2)), t)
                    for t in range(1, max_tiles + 1)]
        cnts = lax.switch(n_tiles - 1, branches)
        return tuple(jnp.sum(c.astype(I32), axis=0, keepdims=True) for c in cnts)

    half = 2 ** 15

    def kth_largest16(k_need):
        def step(b, us):
            cands = tuple(u | lax.shift_left(jnp.int32(1), 15 - b) for u in us)
            cnts = count_ge16(tuple((c - half).astype(I16) for c in cands))
            return tuple(jnp.where(n >= need, c, u) for n, need, c, u in zip(cnts, k_need, cands, us))
        zero = jnp.zeros((1, tq), I32)
        return tuple(u - half for u in lax.fori_loop(0, 16, step, (zero, zero)))

    t_hi = kth_largest16((k_sel, k_sel))
    n_above = count_ge16(tuple((jnp.minimum(t, half - 2) + 1).astype(I16) for t in t_hi))
    n_above = tuple(jnp.where(t >= half - 1, 0, n) for t, n in zip(t_hi, n_above))

    def low_halves(k, _):
        for blk in range(2):
            key = keys[blk, k]
            keys16[blk, k] = jnp.where((key >> 16) == t_hi[blk], (key & 0xFFFF) - half, -half).astype(I16)
        return 0

    lax.fori_loop(0, n_tiles, low_halves, 0)
    t_lo = kth_largest16(tuple(k_sel - n for n in n_above))
    for blk in range(2):
        thr_s[blk] = jnp.maximum(lax.shift_left(t_hi[blk], 16) + (t_lo[blk] + half), KEY_NEG + 1)

    def finish_block(blk, _):
        thr = thr_s[blk]

        def count_rows(hit_fn):
            def body(k, cnt):
                return cnt + jnp.sum(hit_fn(k).reshape(tk // SUBLANES, SUBLANES, tq), axis=0)
            cnt = lax.fori_loop(0, n_tiles, body, jnp.zeros((SUBLANES, tq), I32))
            return jnp.sum(cnt, axis=0, keepdims=True)

        excess = count_rows(lambda k: jnp.where(keys[blk, k] >= thr, 1, 0)) - k_sel

        @pl.when(jnp.max(excess) > 0)
        def _():
            def count_eq_below(cut):
                def hit(k):
                    pos = key_local + (n_valid(blk) - (k + 1) * tk)
                    return jnp.where(keys[blk, k] == thr, jnp.where(pos < cut, 1, 0), 0)
                return count_rows(hit)

            n_bits = max(1, (2 * seq - 1).bit_length())
            keep = count_eq_below(jnp.full((1, tq), 2 ** n_bits, I32)) - jnp.maximum(excess, 0)

            def cut_step(b, cut):
                cand = cut | lax.shift_left(jnp.int32(1), n_bits - 1 - b)
                return jnp.where(count_eq_below(cand) <= keep, cand, cut)

            cut = lax.fori_loop(0, n_bits, cut_step, jnp.zeros((1, tq), I32))

            def drop(k, _):
                pos = key_local + (n_valid(blk) - (k + 1) * tk)
                kk = keys[blk, k]
                keys[blk, k] = jnp.where(kk == thr, jnp.where(pos >= cut, KEY_NEG, kk), kk)
                return 0

            lax.fori_loop(0, n_tiles, drop, 0)

        attend_block(blk, thr)
        return 0

    n_groups = heads // DSA_HEAD_GROUP
    gw = DSA_HEAD_GROUP * tq
    row_id = lax.broadcasted_iota(I32, (LANES, tq), 0)

    def attend_block(blk, thr):
        ql_t = (_dot_nt(wuk_ref[...], qb_ref[q_rows(blk), :]) * (HEAD_DIM ** -0.5 * LOG2E)).astype(BF16)
        slack = jnp.zeros((1, tq), F32)
        ql_aug = []
        for h in range(heads):
            ql_h = ql_t[h * LANES:(h + 1) * LANES, :]
            qf = ql_h.astype(F32)
            reach = jnp.sqrt(jnp.sum(qf * qf, axis=0, keepdims=True)) * cmax[:, 0:1]
            slack = jnp.maximum(slack, 2.0 * reach + (brng_ref[0, h][:, 0:1] - brng_ref[1, h][:, 0:1]))
            bound_rows = jnp.zeros((LANES, tq), F32)
            for j, part in enumerate(_split_bf16(-(reach + brng_ref[0, h][:, 0:1]), CUM_PARTS)):
                bound_rows = jnp.where(row_id == j, part, bound_rows)
            ql_aug.append(jnp.concatenate([ql_h, bound_rows.astype(BF16)], axis=0))
        ql_g = [jnp.concatenate(ql_aug[g * DSA_HEAD_GROUP:(g + 1) * DSA_HEAD_GROUP], axis=1)
                for g in range(n_groups)]
        fast = jnp.max(slack) < FAST_GAP
        l_s[...] = jnp.zeros(l_s.shape, F32)
        acc_s[...] = jnp.zeros(acc_s.shape, F32)

        def attend(k, near, online):
            ct_aug = c_s[tile_rows(blk, k), :]
            ct = c_s[tile_rows(blk, k), 0:LANES]
            mask = jnp.where(keys[blk, k] >= thr, 0.0, NEG)
            s_next = s0[...]
            for g in range(n_groups):
                s_g = s_next
                if g + 1 < n_groups:
                    s_next = _dot(ct_aug, ql_g[g + 1])
                else:
                    s0[...] = _dot(c_s[tile_rows(blk, jnp.minimum(k + 1, n_tiles - 1)), :], ql_g[0])
                parts = []
                for hh in range(DSA_HEAD_GROUP):
                    s = s_g[:, hh * tq:(hh + 1) * tq]
                    if near:
                        s = jnp.concatenate([s[:tk - DSA_NEAR_KEYS], s[tk - DSA_NEAR_KEYS:]
                                             + bias_ref[g * DSA_HEAD_GROUP + hh]], axis=0)
                    parts.append(s + mask)
                s = jnp.concatenate(parts, axis=1)
                if online:
                    m_old = m_s[g]
                    m_new = jnp.maximum(m_old, jnp.max(s, axis=0, keepdims=True))
                    alpha = jnp.exp2(m_old - m_new)
                    pr = jnp.exp2(s - m_new)
                    l_s[g] = alpha * l_s[g] + jnp.sum(pr, axis=0, keepdims=True)
                    m_s[g] = m_new
                    acc_s[g] = alpha * acc_s[g] + _dot_tn(ct, pr.astype(BF16))
                else:
                    pr = jnp.exp2(s)
                    l_s[g] = l_s[g] + jnp.sum(pr, axis=0, keepdims=True)
                    acc_s[g] = acc_s[g] + _dot_tn(ct, pr.astype(BF16))

        s0[...] = _dot(c_s[tile_rows(blk, 0), :], ql_g[0])

        def run(online):
            attend(0, near=True, online=online)

            def far(k, _):
                attend(k, near=False, online=online)
                return 0

            lax.fori_loop(1, n_tiles, far, 0)

        @pl.when(fast)
        def _():
            run(online=False)

        @pl.when(jnp.logical_not(fast))
        def _():
            m_s[...] = jnp.full(m_s.shape, NEG, F32)
            run(online=True)

        o_parts = []
        for g in range(n_groups):
            o_g = acc_s[g] / l_s[g]
            o_parts += [o_g[:, hh * tq:(hh + 1) * tq] for hh in range(DSA_HEAD_GROUP)]
        o_lat_t = jnp.concatenate(o_parts, axis=0).astype(BF16)
        o_ref[q_rows(blk), :] = _dot(wuv_ref[...], o_lat_t).T.astype(o_ref.dtype)

    lax.fori_loop(0, 2, finish_block, 0)


def _dsa(qb, qi, small, ckv, ki2, kvn, wuk_bd, wuv_bd, bias_near, batch):
    n, w = qb.shape
    seq = n // batch
    tq = TQ_DSA
    assert TK_DSA == 2 * tq and seq % TK_DSA == 0
    n_pairs = seq // (2 * tq)
    n_tiles = pl.cdiv(seq, TK_DSA)
    heads = bias_near.shape[0]
    n_groups = heads // DSA_HEAD_GROUP
    gw = DSA_HEAD_GROUP * tq
    k_sel = min(TOPK_MAX, seq // 4)
    blk = lambda width: pl.BlockSpec((2 * tq, width), lambda b, i: (b * n_pairs + i, 0))
    per_batch = pl.BlockSpec((seq, LANES), lambda b, i: (b, 0))
    bias_rng = jnp.stack([jnp.maximum(jnp.max(bias_near, axis=(1, 2)), 0.0),
                          jnp.minimum(jnp.min(bias_near, axis=(1, 2)), 0.0)])
    bias_rng = jnp.broadcast_to(bias_rng[:, :, None, None], (2, heads, 1, LANES))
    return pl.pallas_call(
        functools.partial(_dsa_kernel, k_sel=k_sel),
        grid=(batch, n_pairs),
        in_specs=[blk(w), blk(w), blk(LANES), per_batch, per_batch, _const_spec(kvn.shape),
                  _const_spec(wuk_bd.shape), _const_spec(wuv_bd.shape),
                  _const_spec(bias_near.shape), _const_spec(bias_rng.shape)],
        out_specs=blk(w),
        out_shape=jax.ShapeDtypeStruct((n, w), BF16),
        scratch_shapes=[pltpu.VMEM((seq + DSA_PAD, 2 * LANES), BF16),
                        pltpu.VMEM((seq + DSA_PAD, LANES), BF16),
                        pltpu.VMEM((2, n_tiles, TK_DSA, tq), I32),
                        pltpu.VMEM((2, n_tiles, TK_DSA, tq), I16),
                        pltpu.VMEM((2, 1, tq), I32),
                        pltpu.VMEM((IDX_HEADS * tq, LANES), BF16),
                        pltpu.VMEM((TK_DSA, 2 * tq), F32),
                        pltpu.VMEM((TK_DSA, gw), F32),
                        pltpu.VMEM((1, LANES), F32),
                        pltpu.VMEM((n_groups, 1, gw), F32),
                        pltpu.VMEM((n_groups, 1, gw), F32),
                        pltpu.VMEM((n_groups, LANES, gw), F32)],
        compiler_params=_params("parallel", "arbitrary"),
        name="dsa",
    )(qb, qi, small, ckv, ki2, kvn, wuk_bd, wuv_bd, bias_near, bias_rng)


def _merge_mem_kernel(x_ref, oa_ref, ob_ref, oc_ref, gate_ref, wbr_ref, wout_ref,
                      g_ref, wq_ref, kv_ref, wo_ref, o_ref):
    d = x_ref.shape[1]
    merged = jnp.zeros(x_ref.shape, F32)
    for j, br_ref in enumerate((oa_ref, ob_ref, oc_ref)):
        gate = jax.nn.sigmoid(gate_ref[:, j * d:(j + 1) * d].astype(F32))
        merged = merged + gate * _dot(br_ref[...], wbr_ref[j])
    x = x_ref[...] + _dot(merged.astype(BF16), wout_ref[...])
    hw = MEM_HEADS * MEM_HEAD_DIM
    q = _dot(_rms(x, g_ref[...]).astype(BF16), wq_ref[...]).astype(BF16)
    outs = []
    for h in range(MEM_HEADS):
        lanes = slice(h * MEM_HEAD_DIM, (h + 1) * MEM_HEAD_DIM)
        s = _dot_nt(q[:, lanes], kv_ref[:, lanes]) * (MEM_HEAD_DIM ** -0.5)
        pr = jnp.exp(s - jnp.max(s, axis=1, keepdims=True))
        pr = pr / jnp.sum(pr, axis=1, keepdims=True)
        outs.append(_dot(pr.astype(BF16), kv_ref[:, hw + h * MEM_HEAD_DIM:hw + (h + 1) * MEM_HEAD_DIM]))
    o = jnp.concatenate(outs, axis=1).astype(BF16)
    o_ref[...] = x + _dot(o, wo_ref[...])


def _merge_mem(x, oa, ob, oc, gates, wbr, wout, g, wq, kv, wo, batch):
    n, d = x.shape
    seq = n // batch
    tm = min(TM_TOKENS, seq)
    ns = seq // tm
    bw = oa.shape[1]
    mem_len = kv.shape[0] // batch
    blk = lambda width: pl.BlockSpec((tm, width), lambda b, s: (b * ns + s, 0))
    return pl.pallas_call(
        _merge_mem_kernel,
        grid=(batch, ns),
        in_specs=[blk(d), blk(bw), blk(bw), blk(bw), blk(3 * d), _const_spec(wbr.shape),
                  _const_spec(wout.shape), _const_spec((1, d)), _const_spec(wq.shape),
                  pl.BlockSpec((mem_len, kv.shape[1]), lambda b, s: (b, 0)),
                  _const_spec(wo.shape)],
        out_specs=blk(d),
        out_shape=jax.ShapeDtypeStruct((n, d), F32),
        compiler_params=_params("parallel", "arbitrary"),
        name="merge_mem",
    )(x, oa, ob, oc, gates, wbr, wout, g, wq, kv, wo)


def _t5_bucket(rel):
    nb = NUM_BUCKETS // 2
    max_exact = nb // 2
    base = jnp.where(rel > 0, nb, 0)
    n = jnp.abs(rel)
    n_f = jnp.maximum(n, 1).astype(jnp.float32)
    large = max_exact + (jnp.log(n_f / max_exact) / math.log(MAX_DISTANCE / max_exact)
                         * (nb - max_exact)).astype(jnp.int32)
    large = jnp.minimum(large, nb - 1)
    return base + jnp.where(n < max_exact, n, large)


def _near_bias(t5_bias):
    t = jnp.arange(TQ_DSA)[:, None]
    s = jnp.arange(DSA_NEAR_KEYS)[None, :] - MAX_DISTANCE
    onehot = jax.nn.one_hot(_t5_bucket(s - t), NUM_BUCKETS, dtype=F32)
    tbl = jnp.einsum("tsb,bh->hst", onehot, (t5_bias - t5_bias[NUM_BUCKETS // 2 - 1]) * LOG2E,
                     precision=lax.Precision.HIGHEST)
    return tbl.astype(F32)


def _block_diag(w):
    nb, a, b = w.shape
    eye = jnp.eye(nb, dtype=w.dtype)
    return (eye[:, None, :, None] * w[:, :, None, :]).reshape(nb * a, nb * b)


def _pad_cols(w, width):
    return jnp.pad(w, ((0, 0), (0, width - w.shape[1])))


def kernel(x, mem, ffn1_norm, ffn1_w_gu, ffn1_w_down, mix_norm, w_in, conv_w, conv_b, rg_wa, rg_ba, rg_wx, rg_bx, rg_lambda, kv_norm, w_uk, w_uv, forget_bias, w_branch, w_out, xattn_norm, mem_norm, w_mq, w_mkv, w_mo, ffn2_norm, ffn2_w_gu, ffn2_w_down, t5_bias, final_norm):
    batch, seq, d = x.shape
    depth = w_in.shape[0]
    d_ff = ffn1_w_down.shape[1]
    dr = conv_w.shape[2]
    heads = w_uk.shape[1]
    d_lat = w_uk.shape[3]
    n = batch * seq
    assert seq % TK_DSA == 0 and d_lat == LANES and 2 * IDX_DIM == LANES and 2 * HEAD_DIM == LANES

    widths = (dr, dr, heads * HEAD_DIM, d_lat, IDX_HEADS * IDX_DIM, IDX_DIM, IDX_HEADS,
              heads * HEAD_DIM, heads * HEAD_DIM, heads * HEAD_DIM, heads, d, d, d)
    offs = np.concatenate([[0], np.cumsum(widths)])
    col = lambda w, j: w[:, int(offs[j]):int(offs[j + 1])]

    bias_near = _near_bias(t5_bias)
    cum_sel = _cum_selector(heads * HEAD_DIM // LANES, IDX_HEADS)
    row = lambda v: v.reshape(1, -1).astype(F32)

    xf = x.reshape(n, d)
    memf = mem.reshape(batch * mem.shape[1], d)
    for l in range(depth):
        wl = w_in[l]
        small_w = _pad_cols(jnp.concatenate([col(wl, 6), col(wl, 10)], axis=1), LANES)
        w_all = jnp.concatenate(
            [col(wl, 0), col(wl, 1), col(wl, 2), col(wl, 3), col(wl, 4), col(wl, 5), col(wl, 5),
             col(wl, 7) * (HEAD_DIM ** -0.5 * LOG2E),
             col(wl, 8), col(wl, 9), col(wl, 11), col(wl, 12), col(wl, 13)],
            axis=1).astype(BF16)
        w_all = jnp.concatenate([w_all, small_w.astype(BF16)], axis=1)
        out_widths = (2 * dr, heads * HEAD_DIM, d_lat, IDX_HEADS * IDX_DIM, 2 * IDX_DIM,
                      heads * HEAD_DIM, heads * HEAD_DIM, heads * HEAD_DIM, 3 * d, LANES)
        out_dtypes = (BF16,) * 9 + (F32,)

        xf = _ffn(xf, row(ffn1_norm[l]), ffn1_w_gu[l][:, :d_ff].astype(BF16),
                  ffn1_w_gu[l][:, d_ff:].astype(BF16), ffn1_w_down[l].astype(BF16),
                  row(final_norm), norm_out=False)

        rg, qb, ckv, qi, ki2, qc, kc, vc, gates, small = _norm_proj(
            xf, row(mix_norm[l]), w_all, out_widths, out_dtypes)

        wax = jnp.concatenate([_block_diag(rg_wa[l]), _block_diag(rg_wx[l])], axis=1).astype(BF16)
        bax = jnp.concatenate([rg_ba[l], rg_bx[l]]).reshape(1, -1)
        o_a = _rglru(rg, batch, conv_w[l], row(conv_b[l]), wax, bax, row(rg_lambda[l]))

        fbias = _pad_cols(jnp.concatenate([jnp.zeros((IDX_HEADS,), F32), forget_bias[l]])[None], LANES)
        o_c = _fox(qc, kc, vc, _forget_cumsum(small, batch, fbias), cum_sel, batch)

        o_b = _dsa(qb, qi, small, ckv, ki2, row(kv_norm[l]), _block_diag(w_uk[l]).T.astype(BF16),
                   _block_diag(w_uv[l]).T.astype(BF16), bias_near, batch)

        (kv,) = _norm_proj(memf, row(mem_norm[l]), w_mkv[l].astype(BF16),
                           (w_mkv.shape[2],), (BF16,))
        xf = _merge_mem(xf, o_a, o_b, o_c, gates, w_branch[l].astype(BF16), w_out[l].astype(BF16),
                        row(xattn_norm[l]), w_mq[l].astype(BF16), kv, w_mo[l].astype(BF16), batch)

        xf = _ffn(xf, row(ffn2_norm[l]), ffn2_w_gu[l][:, :d_ff].astype(BF16),
                  ffn2_w_gu[l][:, d_ff:].astype(BF16), ffn2_w_down[l].astype(BF16),
                  row(final_norm), norm_out=(l == depth - 1))
    return xf.reshape(batch, seq, d)
```

```python
import functools
import math

import numpy as np
import jax
import jax.numpy as jnp
from jax import lax
from jax.experimental import pallas as pl
from jax.experimental.pallas import tpu as pltpu

F32 = jnp.float32
BF16 = jnp.bfloat16
I32 = jnp.int32
I16 = jnp.int16

LANES = 128
SUBLANES = 8
VMEM_LIMIT_BYTES = 56 * 1024 * 1024

EPS = 1e-6
NEG = -1e30
LOG2E = math.log2(math.e)
FAST_GAP = 100.0
CHUNK = 64
CHUNK_SHIFT = CHUNK.bit_length() - 1
HEAD_DIM = 64
RG_C = 8.0
CONV_W = 4
IDX_HEADS = 8
IDX_DIM = 64
TOPK_MAX = 256
MEM_HEADS = 4
MEM_HEAD_DIM = 128
NUM_BUCKETS = 32
MAX_DISTANCE = 128

TM_TOKENS = 512
FFN_CHUNK = 256
PROJ_CHUNK = 1024
TS_SCAN = 256
TQ_FOX = 512
CUM_PARTS = 3
TQ_DSA = 256
DSA_NEAR_KEYS = TQ_DSA + MAX_DISTANCE
TK_DSA = 512
DSA_HEAD_GROUP = 2
SEARCH_ACCUMULATORS = 4
DSA_PAD = TK_DSA - TQ_DSA


def _key_of_float(v):
    b = int(np.float32(v).view(np.int32))
    return b ^ ((b >> 31) & 0x7FFFFFFF)


KEY_NEG = _key_of_float(NEG)


def _params(*semantics):
    return pltpu.CompilerParams(dimension_semantics=semantics, vmem_limit_bytes=VMEM_LIMIT_BYTES)


def _const_spec(shape):
    nd = len(shape)
    return pl.BlockSpec(shape, lambda *_: (0,) * nd, pipeline_mode=pl.Buffered(1))


def _rms(x32, g):
    ms = jnp.mean(x32 * x32, axis=-1, keepdims=True)
    return x32 * lax.rsqrt(ms + EPS) * g


def _dot(a, b):
    return jnp.dot(a, b, preferred_element_type=F32)


def _dot_nt(a, b):
    return lax.dot_general(a, b, (((1,), (1,)), ((), ())), preferred_element_type=F32)


def _tree_sum(xs):
    while len(xs) > 1:
        xs = [xs[j] + xs[j + 1] for j in range(0, len(xs) - 1, 2)] + ([xs[-1]] if len(xs) % 2 else [])
    return xs[0]


def _dot_tn(a, b):
    return lax.dot_general(a, b, (((0,), (0,)), ((), ())), preferred_element_type=F32)


def _ffn_kernel(x_ref, g_ref, wg_ref, wu_ref, wd_ref, gout_ref, o_ref, a_ref, *, norm_out):
    x = x_ref[...]
    h = _rms(x, g_ref[...]).astype(BF16)
    d_ff = wg_ref.shape[1]
    for f0 in range(0, d_ff, FFN_CHUNK):
        g = _dot(h, wg_ref[:, f0:f0 + FFN_CHUNK])
        u = _dot(h, wu_ref[:, f0:f0 + FFN_CHUNK])
        a_ref[:, f0:f0 + FFN_CHUNK] = (g * jax.nn.sigmoid(g) * u).astype(BF16)
    y = x + 0.5 * _dot(a_ref[...], wd_ref[...])
    o_ref[...] = _rms(y, gout_ref[...]) if norm_out else y


def _ffn(x, g, wg, wu, wd, g_out, norm_out):
    n, d = x.shape
    d_ff = wg.shape[1]
    tm = min(TM_TOKENS, n)
    return pl.pallas_call(
        functools.partial(_ffn_kernel, norm_out=norm_out),
        grid=(n // tm,),
        in_specs=[pl.BlockSpec((tm, d), lambda i: (i, 0)),
                  _const_spec((1, d)), _const_spec((d, d_ff)), _const_spec((d, d_ff)),
                  _const_spec((d_ff, d)), _const_spec((1, d))],
        out_specs=pl.BlockSpec((tm, d), lambda i: (i, 0)),
        out_shape=jax.ShapeDtypeStruct((n, d), F32),
        scratch_shapes=[pltpu.VMEM((tm, d_ff), BF16)],
        compiler_params=_params("parallel"),
        name="ffn",
    )(x, g, wg, wu, wd, g_out)


def _norm_proj_kernel(x_ref, g_ref, w_ref, *o_refs):
    h = _rms(x_ref[...].astype(F32), g_ref[...]).astype(BF16)
    starts = np.cumsum([0] + [o_ref.shape[1] for o_ref in o_refs])
    for c0 in range(0, w_ref.shape[1], PROJ_CHUNK):
        c1 = min(c0 + PROJ_CHUNK, w_ref.shape[1])
        y = _dot(h, w_ref[:, c0:c1])
        for o_ref, o0 in zip(o_refs, starts[:-1]):
            lo, hi = max(c0, int(o0)), min(c1, int(o0) + o_ref.shape[1])
            if lo < hi:
                o_ref[:, lo - int(o0):hi - int(o0)] = y[:, lo - c0:hi - c0].astype(o_ref.dtype)


def _norm_proj(x, g, w, widths, dtypes):
    n, d = x.shape
    tm = min(TM_TOKENS, n)
    assert sum(widths) == w.shape[1]
    return pl.pallas_call(
        _norm_proj_kernel,
        grid=(n // tm,),
        in_specs=[pl.BlockSpec((tm, d), lambda i: (i, 0)), _const_spec((1, d)),
                  _const_spec(w.shape)],
        out_specs=[pl.BlockSpec((tm, wd), lambda i: (i, 0)) for wd in widths],
        out_shape=[jax.ShapeDtypeStruct((n, wd), dt) for wd, dt in zip(widths, dtypes)],
        compiler_params=_params("parallel"),
        name="norm_proj",
    )(x, g, w)


def _gelu_tanh(x):
    return 0.5 * x * (1.0 + jnp.tanh(math.sqrt(2.0 / math.pi) * (x + 0.044715 * (x * x * x))))


def _softplus(x):
    return jnp.maximum(x, 0.0) + jnp.log1p(jnp.exp(-jnp.abs(x)))


def _rglru_kernel(rg_ref, cw_ref, cb_ref, wax_ref, bax_ref, lam_ref, o_ref, xbuf, hc):
    ts = rg_ref.shape[0]
    dr = o_ref.shape[1]

    @pl.when(pl.program_id(1) == 0)
    def _():
        xbuf[0:SUBLANES, :] = jnp.zeros((SUBLANES, dr), F32)
        hc[...] = jnp.zeros_like(hc)

    xr = rg_ref[:, :dr].astype(F32)
    gr = rg_ref[:, dr:].astype(F32)
    xbuf[SUBLANES:, :] = xr
    xc = cb_ref[...] + cw_ref[CONV_W - 1:CONV_W, :] * xr
    for j in range(CONV_W - 1):
        xc = xc + cw_ref[j:j + 1, :] * xbuf[pl.ds(SUBLANES - (CONV_W - 1) + j, ts), :]
    xbuf[0:SUBLANES, :] = xbuf[ts:ts + SUBLANES, :]

    ax = _dot(xc.astype(BF16), wax_ref[...]) + bax_ref[...]
    r = jax.nn.sigmoid(ax[:, :dr])
    gi = jax.nn.sigmoid(ax[:, dr:])
    log_a = (-RG_C) * r * _softplus(-lam_ref[...])
    a = jnp.exp(log_a)
    th = jnp.tanh(log_a)
    num = -2.0 * th
    u = num * lax.rsqrt(jnp.maximum(num * (1.0 - th), jnp.finfo(F32).tiny)) * (gi * xc)

    row = lax.broadcasted_iota(I32, (ts, dr), 0) & (SUBLANES - 1)
    d = 1
    while d < SUBLANES:
        a_sh = pltpu.roll(a, d, 0)
        u_sh = pltpu.roll(u, d, 0)
        keep = row >= d
        u = jnp.where(keep, a * u_sh + u, u)
        a = jnp.where(keep, a * a_sh, a)
        d *= 2
    carry = hc[0:1, :]
    groups = []
    for g in range(ts // SUBLANES):
        rows = slice(g * SUBLANES, (g + 1) * SUBLANES)
        h_g = u[rows] + a[rows] * carry
        groups.append(h_g)
        carry = h_g[SUBLANES - 1:SUBLANES, :]
    h = jnp.concatenate(groups, axis=0)
    hc[...] = jnp.broadcast_to(carry, hc.shape)
    o_ref[...] = (h * _gelu_tanh(gr)).astype(o_ref.dtype)


def _rglru(rg, batch, conv_w, conv_b, wax, bax, lam):
    n, two_dr = rg.shape
    dr = two_dr // 2
    seq = n // batch
    ts = min(TS_SCAN, seq)
    ns = seq // ts
    return pl.pallas_call(
        _rglru_kernel,
        grid=(batch, ns),
        in_specs=[pl.BlockSpec((ts, two_dr), lambda b, s: (b * ns + s, 0)),
                  _const_spec(conv_w.shape), _const_spec(conv_b.shape), _const_spec(wax.shape),
                  _const_spec(bax.shape), _const_spec(lam.shape)],
        out_specs=pl.BlockSpec((ts, dr), lambda b, s: (b * ns + s, 0)),
        out_shape=jax.ShapeDtypeStruct((n, dr), BF16),
        scratch_shapes=[pltpu.VMEM((ts + SUBLANES, dr), F32), pltpu.VMEM((SUBLANES, dr), F32)],
        compiler_params=_params("parallel", "arbitrary"),
        name="rglru",
    )(rg, conv_w, conv_b, wax, bax, lam)


def _forget_cumsum_kernel(f_ref, b_ref, o_ref):
    z = f_ref[...] + b_ref[...]
    x = -_softplus(-z)
    seq = x.shape[0]
    row = lax.broadcasted_iota(I32, x.shape, 0)
    d = 1
    while d < seq:
        x = jnp.where(row >= d, x + pltpu.roll(x, d, 0), x)
        d *= 2
    x = x * LOG2E
    for j in range(CUM_PARTS):
        part = x.astype(BF16)
        o_ref[j] = part
        x = x - part.astype(F32)


def _forget_cumsum(small, batch, fbias):
    n, w = small.shape
    seq = n // batch
    return pl.pallas_call(
        _forget_cumsum_kernel,
        grid=(batch,),
        in_specs=[pl.BlockSpec((seq, w), lambda b: (b, 0)), _const_spec((1, w))],
        out_specs=pl.BlockSpec((CUM_PARTS, seq, w), lambda b: (0, b, 0)),
        out_shape=jax.ShapeDtypeStruct((CUM_PARTS, n, w), BF16),
        compiler_params=_params("parallel"),
        name="forget_cumsum",
    )(small, fbias)


def _split_bf16(x, n):
    parts = []
    for _ in range(n):
        part = x.astype(BF16).astype(F32)
        parts.append(part)
        x = x - part
    return parts


def _fox_kernel(q_ref, k_ref, v_ref, cp_ref, sel_ref, o_ref, kaug, qaug, s0, kmax, m_s, l_s, acc_s):
    tq = q_ref.shape[0]
    n_pairs = q_ref.shape[1] // LANES
    i = pl.program_id(1)
    lane = lax.broadcasted_iota(I32, (1, LANES), 1)
    bound_lanes = 2 * CUM_PARTS

    @pl.when(i == 0)
    def _():
        ones = jnp.where((lane >= bound_lanes) & (lane < bound_lanes + CUM_PARTS), 1.0, 0.0)
        for p in range(n_pairs):
            kp = k_ref[:, p * LANES:(p + 1) * LANES]
            kaug[:, 2 * p * LANES:(2 * p + 1) * LANES] = kp
            extra = _dot(cp_ref[0], sel_ref[p, 0])
            for j in range(1, CUM_PARTS):
                extra = extra + _dot(cp_ref[j], sel_ref[p, j])
            kaug[:, (2 * p + 1) * LANES:(2 * p + 2) * LANES] = (extra + ones).astype(BF16)
            k2 = kp.astype(F32) * kp.astype(F32)
            for e in range(2):
                in_head = (lane >= e * HEAD_DIM) & (lane < (e + 1) * HEAD_DIM)
                n2 = jnp.sum(jnp.where(in_head, k2, 0.0), axis=1, keepdims=True)
                kmax[2 * p + e] = jnp.broadcast_to(jnp.sqrt(jnp.max(n2, axis=0, keepdims=True)),
                                                   (1, LANES))

    q_rows = pl.ds(pl.multiple_of(i * tq, tq), tq)
    f_t = cp_ref[0, q_rows, :].astype(F32)
    for j in range(1, CUM_PARTS):
        f_t = f_t + cp_ref[j, q_rows, :].astype(F32)

    slack = jnp.zeros((tq, 1), F32)
    for p in range(n_pairs):
        qp = q_ref[:, p * LANES:(p + 1) * LANES]
        for e in range(2):
            h = 2 * p + e
            rows = slice(e * tq, (e + 1) * tq)
            in_head = (lane >= e * HEAD_DIM) & (lane < (e + 1) * HEAD_DIM)
            qh = jnp.where(in_head, qp, jnp.zeros_like(qp))
            qaug[p, rows, 0:LANES] = qh
            qf = qh.astype(F32)
            reach = jnp.sqrt(jnp.sum(qf * qf, axis=1, keepdims=True)) * kmax[h][:, 0:1]
            slack = jnp.maximum(slack, reach)
            bound = reach - f_t[:, IDX_HEADS + h:IDX_HEADS + h + 1]
            extra = jnp.where((lane >= CUM_PARTS * e) & (lane < CUM_PARTS * (e + 1)), 1.0, 0.0)
            for j, part in enumerate(_split_bf16(-bound, CUM_PARTS)):
                extra = jnp.where(lane == bound_lanes + j, part, extra)
            qaug[p, rows, LANES:2 * LANES] = extra.astype(BF16)
    fast = 2.0 * jnp.max(slack) < FAST_GAP

    l_s[...] = jnp.zeros(l_s.shape, F32)
    acc_s[...] = jnp.zeros(acc_s.shape, F32)
    key_row = lax.broadcasted_iota(I32, (tq, 2 * tq), 0)
    query = lax.broadcasted_iota(I32, (tq, 2 * tq), 1) & (tq - 1)
    causal = key_row <= query

    def logits(j, p):
        rows = pl.ds(pl.multiple_of(j * tq, tq), tq)
        return _dot_nt(kaug[rows, 2 * p * LANES:(2 * p + 2) * LANES], qaug[p])

    s0[...] = logits(0, 0)

    def step(j, masked, online):
        rows = pl.ds(pl.multiple_of(j * tq, tq), tq)
        s_next = s0[...]
        for p in range(n_pairs):
            s = s_next
            if p + 1 < n_pairs:
                s_next = logits(j, p + 1)
            else:
                s0[...] = logits(jnp.minimum(j + 1, i), 0)
            if masked:
                s = jnp.where(causal, s, NEG)
            vt = v_ref[rows, p * LANES:(p + 1) * LANES]
            if online:
                m_old = m_s[p]
                m_new = jnp.maximum(m_old, jnp.max(s, axis=0, keepdims=True))
                alpha = jnp.exp2(m_old - m_new)
                pr = jnp.exp2(s - m_new)
                l_s[p] = alpha * l_s[p] + jnp.sum(pr, axis=0, keepdims=True)
                m_s[p] = m_new
                acc_s[p] = alpha * acc_s[p] + _dot_tn(vt, pr.astype(BF16))
            else:
                pr = jnp.exp2(s)
                l_s[p] = l_s[p] + jnp.sum(pr, axis=0, keepdims=True)
                acc_s[p] = acc_s[p] + _dot_tn(vt, pr.astype(BF16))

    def run(online):
        def body(j, _):
            step(j, masked=False, online=online)
            return 0
        lax.fori_loop(0, i, body, 0)
        step(i, masked=True, online=online)

    @pl.when(fast)
    def _():
        run(online=False)

    @pl.when(jnp.logical_not(fast))
    def _():
        m_s[...] = jnp.full(m_s.shape, NEG, F32)
        run(online=True)

    feat = lax.broadcasted_iota(I32, (LANES, tq), 0)
    for p in range(n_pairs):
        o_t = acc_s[p] / l_s[p]
        o_t = jnp.where(feat < HEAD_DIM, o_t[:, :tq], o_t[:, tq:])
        o_ref[:, p * LANES:(p + 1) * LANES] = o_t.T.astype(o_ref.dtype)


def _fox(q, k, v, cum_parts, sel, batch):
    n, w = q.shape
    seq = n // batch
    tq = min(TQ_FOX, seq)
    nq = seq // tq
    n_pairs = w // LANES
    return pl.pallas_call(
        _fox_kernel,
        grid=(batch, nq),
        in_specs=[pl.BlockSpec((tq, w), lambda b, i: (b * nq + i, 0)),
                  pl.BlockSpec((seq, w), lambda b, i: (b, 0)),
                  pl.BlockSpec((seq, w), lambda b, i: (b, 0)),
                  pl.BlockSpec((CUM_PARTS, seq, LANES), lambda b, i: (0, b, 0)),
                  _const_spec(sel.shape)],
        out_specs=pl.BlockSpec((tq, w), lambda b, i: (b * nq + i, 0)),
        out_shape=jax.ShapeDtypeStruct((n, w), BF16),
        scratch_shapes=[pltpu.VMEM((seq, 2 * w), BF16),
                        pltpu.VMEM((n_pairs, 2 * tq, 2 * LANES), BF16),
                        pltpu.VMEM((tq, 2 * tq), F32),
                        pltpu.VMEM((2 * n_pairs, 1, LANES), F32),
                        pltpu.VMEM((n_pairs, 1, 2 * tq), F32),
                        pltpu.VMEM((n_pairs, 1, 2 * tq), F32),
                        pltpu.VMEM((n_pairs, LANES, 2 * tq), F32)],
        compiler_params=_params("parallel", "arbitrary"),
        name="fox",
    )(q, k, v, cum_parts, sel)


def _cum_selector(n_pairs, first_lane):
    sel = np.zeros((n_pairs, CUM_PARTS, LANES, LANES), np.float32)
    for p in range(n_pairs):
        for j in range(CUM_PARTS):
            for e in range(2):
                sel[p, j, first_lane + 2 * p + e, CUM_PARTS * e + j] = -1.0
    return jnp.asarray(sel, BF16)


def _float_key(x):
    b = lax.bitcast_convert_type(x, I32)
    return b ^ ((b >> 31) & 0x7FFFFFFF)


def _dsa_kernel(qb_ref, qi_ref, sm_ref, ckv_ref, ki_ref, kvn_ref, wuk_ref, wuv_ref, bias_ref, brng_ref,
                o_ref, c_s, ki_s, keys, keys16, thr_s, qim, d0, s0, cmax, m_s, l_s, acc_s, *, k_sel):
    tq = TQ_DSA
    seq = ckv_ref.shape[0]
    heads = bias_ref.shape[0]
    tk = TK_DSA
    pair = pl.program_id(1)
    lane = lax.broadcasted_iota(I32, (1, LANES), 1)

    @pl.when(pair == 0)
    def _():
        c_s[0:DSA_PAD, :] = jnp.zeros((DSA_PAD, 2 * LANES), BF16)
        ki_s[0:DSA_PAD, :] = jnp.zeros((DSA_PAD, LANES), BF16)
        c = _rms(ckv_ref[...].astype(F32), kvn_ref[...]).astype(BF16)
        c_s[DSA_PAD:, 0:LANES] = c
        c_s[DSA_PAD:, LANES:] = jnp.broadcast_to(jnp.where(lane < CUM_PARTS, 1.0, 0.0),
                                                 (seq, LANES)).astype(BF16)
        ki_s[DSA_PAD:, :] = ki_ref[...]
        cf = c.astype(F32)
        n2 = jnp.max(jnp.sum(cf * cf, axis=1, keepdims=True), axis=0, keepdims=True)
        cmax[...] = jnp.broadcast_to(jnp.sqrt(n2), cmax.shape)

    n_tiles = pair + 1
    max_tiles = pl.cdiv(seq, tk)
    key_local = lax.broadcasted_iota(I32, (tk, tq), 0)

    def n_valid(blk):
        return (2 * pair + blk + 1) * tq

    def tile_rows(blk, k):
        return pl.ds(pl.multiple_of(n_valid(blk) - (k + 1) * tk + DSA_PAD, tq), tk)

    def q_rows(blk):
        return pl.ds(pl.multiple_of(blk * tq, tq), tq)

    def score_block(blk, _):
        for h in range(IDX_HEADS):
            qp = qi_ref[q_rows(blk), (h // 2) * LANES:(h // 2 + 1) * LANES]
            e = h % 2
            in_head = (lane >= e * IDX_DIM) & (lane < (e + 1) * IDX_DIM)
            qim[h * tq:(h + 1) * tq, :] = jnp.where(in_head, qp, jnp.zeros_like(qp))
        w_t = sm_ref[q_rows(blk), :].T[0:IDX_HEADS, :] * (IDX_HEADS ** -0.5 * IDX_DIM ** -0.5)
        q_chunk = ((2 * pair + blk) * tq + lax.broadcasted_iota(I32, (tk, tq), 1)) >> CHUNK_SHIFT

        def pair_dots(k, p):
            return _dot_nt(ki_s[tile_rows(blk, k), :], qim[2 * p * tq:(2 * p + 2) * tq, :])

        d0[...] = pair_dots(0, 0)

        def score_tile(k, nearest):
            d_next = d0[...]
            sc = jnp.zeros((tk, tq), F32)
            for p in range(IDX_HEADS // 2):
                d = d_next
                if p + 1 < IDX_HEADS // 2:
                    d_next = pair_dots(k, p + 1)
                else:
                    d0[...] = pair_dots(jnp.minimum(k + 1, n_tiles - 1), 0)
                for e in range(2):
                    sc = sc + jnp.maximum(d[:, e * tq:(e + 1) * tq], 0.0) * w_t[2 * p + e:2 * p + e + 1, :]
            pos = key_local + (n_valid(blk) - (k + 1) * tk)
            if nearest:
                sc = jnp.where((pos >> CHUNK_SHIFT) <= q_chunk, sc, NEG)
            sc = jnp.where(pos >= 0, sc, NEG)
            key = _float_key(sc)
            keys[blk, k] = key
            keys16[blk, k] = (key >> 16).astype(I16)

        score_tile(0, nearest=True)

        def farther(k, _):
            score_tile(k, nearest=False)
            return 0

        lax.fori_loop(1, n_tiles, farther, 0)
        return 0

    lax.fori_loop(0, 2, score_block, 0)

    def count_ge16(thr16):
        rows16 = 2 * SUBLANES

        def block_count(blk, t):
            accs = [jnp.zeros((rows16, tq), I16)] * SEARCH_ACCUMULATORS
            for k in range(t):
                for c in range(tk // rows16):
                    x = keys16[blk, k, c * rows16:(c + 1) * rows16, :]
                    ge = jnp.where(x >= thr16[blk], jnp.ones((), I16), jnp.zeros((), I16))
                    accs[c % SEARCH_ACCUMULATORS] = accs[c % SEARCH_ACCUMULATORS] + ge
            return _tree_sum(accs)

        branches = [functools.partial(lambda t: tuple(block_count(blk, t) for blk in range(2)), t)
                    for t in range(1, max_tiles + 1)]
        cnts = lax.switch(n_tiles - 1, branches)
        return tuple(jnp.sum(c.astype(I32), axis=0, keepdims=True) for c in cnts)

    half = 2 ** 15

    def kth_largest16(k_need):
        def step(b, us):
            cands = tuple(u | lax.shift_left(jnp.int32(1), 15 - b) for u in us)
            cnts = count_ge16(tuple((c - half).astype(I16) for c in cands))
            return tuple(jnp.where(n >= need, c, u) for n, need, c, u in zip(cnts, k_need, cands, us))
        zero = jnp.zeros((1, tq), I32)
        return tuple(u - half for u in lax.fori_loop(0, 16, step, (zero, zero)))

    t_hi = kth_largest16((k_sel, k_sel))
    n_above = count_ge16(tuple((jnp.minimum(t, half - 2) + 1).astype(I16) for t in t_hi))
    n_above = tuple(jnp.where(t >= half - 1, 0, n) for t, n in zip(t_hi, n_above))

    def low_halves(k, _):
        for blk in range(2):
            key = keys[blk, k]
            keys16[blk, k] = jnp.where((key >> 16) == t_hi[blk], (key & 0xFFFF) - half, -half).astype(I16)
        return 0

    lax.fori_loop(0, n_tiles, low_halves, 0)
    t_lo = kth_largest16(tuple(k_sel - n for n in n_above))
    for blk in range(2):
        thr_s[blk] = jnp.maximum(lax.shift_left(t_hi[blk], 16) + (t_lo[blk] + half), KEY_NEG + 1)

    def finish_block(blk, _):
        thr = thr_s[blk]

        def count_rows(hit_fn):
            def body(k, cnt):
                return cnt + jnp.sum(hit_fn(k).reshape(tk // SUBLANES, SUBLANES, tq), axis=0)
            cnt = lax.fori_loop(0, n_tiles, body, jnp.zeros((SUBLANES, tq), I32))
            return jnp.sum(cnt, axis=0, keepdims=True)

        excess = count_rows(lambda k: jnp.where(keys[blk, k] >= thr, 1, 0)) - k_sel

        @pl.when(jnp.max(excess) > 0)
        def _():
            def count_eq_below(cut):
                def hit(k):
                    pos = key_local + (n_valid(blk) - (k + 1) * tk)
                    return jnp.where(keys[blk, k] == thr, jnp.where(pos < cut, 1, 0), 0)
                return count_rows(hit)

            n_bits = max(1, (2 * seq - 1).bit_length())
            keep = count_eq_below(jnp.full((1, tq), 2 ** n_bits, I32)) - jnp.maximum(excess, 0)

            def cut_step(b, cut):
                cand = cut | lax.shift_left(jnp.int32(1), n_bits - 1 - b)
                return jnp.where(count_eq_below(cand) <= keep, cand, cut)

            cut = lax.fori_loop(0, n_bits, cut_step, jnp.zeros((1, tq), I32))

            def drop(k, _):
                pos = key_local + (n_valid(blk) - (k + 1) * tk)
                kk = keys[blk, k]
                keys[blk, k] = jnp.where(kk == thr, jnp.where(pos >= cut, KEY_NEG, kk), kk)
                return 0

            lax.fori_loop(0, n_tiles, drop, 0)

        attend_block(blk, thr)
        return 0

    n_groups = heads // DSA_HEAD_GROUP
    gw = DSA_HEAD_GROUP * tq
    row_id = lax.broadcasted_iota(I32, (LANES, tq), 0)

    def attend_block(blk, thr):
        ql_t = (_dot_nt(wuk_ref[...], qb_ref[q_rows(blk), :]) * (HEAD_DIM ** -0.5 * LOG2E)).astype(BF16)
        slack = jnp.zeros((1, tq), F32)
        ql_aug = []
        for h in range(heads):
            ql_h = ql_t[h * LANES:(h + 1) * LANES, :]
            qf = ql_h.astype(F32)
            reach = jnp.sqrt(jnp.sum(qf * qf, axis=0, keepdims=True)) * cmax[:, 0:1]
            slack = jnp.maximum(slack, 2.0 * reach + (brng_ref[0, h][:, 0:1] - brng_ref[1, h][:, 0:1]))
            bound_rows = jnp.zeros((LANES, tq), F32)
            for j, part in enumerate(_split_bf16(-(reach + brng_ref[0, h][:, 0:1]), CUM_PARTS)):
                bound_rows = jnp.where(row_id == j, part, bound_rows)
            ql_aug.append(jnp.concatenate([ql_h, bound_rows.astype(BF16)], axis=0))
        ql_g = [jnp.concatenate(ql_aug[g * DSA_HEAD_GROUP:(g + 1) * DSA_HEAD_GROUP], axis=1)
                for g in range(n_groups)]
        fast = jnp.max(slack) < FAST_GAP
        l_s[...] = jnp.zeros(l_s.shape, F32)
        acc_s[...] = jnp.zeros(acc_s.shape, F32)

        def attend(k, near, online):
            ct_aug = c_s[tile_rows(blk, k), :]
            ct = c_s[tile_rows(blk, k), 0:LANES]
            mask = jnp.where(keys[blk, k] >= thr, 0.0, NEG)
            s_next = s0[...]
            for g in range(n_groups):
                s_g = s_next
                if g + 1 < n_groups:
                    s_next = _dot(ct_aug, ql_g[g + 1])
                else:
                    s0[...] = _dot(c_s[tile_rows(blk, jnp.minimum(k + 1, n_tiles - 1)), :], ql_g[0])
                parts = []
                for hh in range(DSA_HEAD_GROUP):
                    s = s_g[:, hh * tq:(hh + 1) * tq]
                    if near:
                        s = jnp.concatenate([s[:tk - DSA_NEAR_KEYS], s[tk - DSA_NEAR_KEYS:]
                                             + bias_ref[g * DSA_HEAD_GROUP + hh]], axis=0)
                    parts.append(s + mask)
                s = jnp.concatenate(parts, axis=1)
                if online:
                    m_old = m_s[g]
                    m_new = jnp.maximum(m_old, jnp.max(s, axis=0, keepdims=True))
                    alpha = jnp.exp2(m_old - m_new)
                    pr = jnp.exp2(s - m_new)
                    l_s[g] = alpha * l_s[g] + jnp.sum(pr, axis=0, keepdims=True)
                    m_s[g] = m_new
                    acc_s[g] = alpha * acc_s[g] + _dot_tn(ct, pr.astype(BF16))
                else:
                    pr = jnp.exp2(s)
                    l_s[g] = l_s[g] + jnp.sum(pr, axis=0, keepdims=True)
                    acc_s[g] = acc_s[g] + _dot_tn(ct, pr.astype(BF16))

        s0[...] = _dot(c_s[tile_rows(blk, 0), :], ql_g[0])

        def run(online):
            attend(0, near=True, online=online)

            def far(k, _):
                attend(k, near=False, online=online)
                return 0

            lax.fori_loop(1, n_tiles, far, 0)

        @pl.when(fast)
        def _():
            run(online=False)

        @pl.when(jnp.logical_not(fast))
        def _():
            m_s[...] = jnp.full(m_s.shape, NEG, F32)
            run(online=True)

        o_parts = []
        for g in range(n_groups):
            o_g = acc_s[g] / l_s[g]
            o_parts += [o_g[:, hh * tq:(hh + 1) * tq] for hh in range(DSA_HEAD_GROUP)]
        o_lat_t = jnp.concatenate(o_parts, axis=0).astype(BF16)
        o_ref[q_rows(blk), :] = _dot(wuv_ref[...], o_lat_t).T.astype(o_ref.dtype)

    lax.fori_loop(0, 2, finish_block, 0)


def _dsa(qb, qi, small, ckv, ki2, kvn, wuk_bd, wuv_bd, bias_near, batch):
    n, w = qb.shape
    seq = n // batch
    tq = TQ_DSA
    assert TK_DSA == 2 * tq and seq % TK_DSA == 0
    n_pairs = seq // (2 * tq)
    n_tiles = pl.cdiv(seq, TK_DSA)
    heads = bias_near.shape[0]
    n_groups = heads // DSA_HEAD_GROUP
    gw = DSA_HEAD_GROUP * tq
    k_sel = min(TOPK_MAX, seq // 4)
    blk = lambda width: pl.BlockSpec((2 * tq, width), lambda b, i: (b * n_pairs + i, 0))
    per_batch = pl.BlockSpec((seq, LANES), lambda b, i: (b, 0))
    bias_rng = jnp.stack([jnp.maximum(jnp.max(bias_near, axis=(1, 2)), 0.0),
                          jnp.minimum(jnp.min(bias_near, axis=(1, 2)), 0.0)])
    bias_rng = jnp.broadcast_to(bias_rng[:, :, None, None], (2, heads, 1, LANES))
    return pl.pallas_call(
        functools.partial(_dsa_kernel, k_sel=k_sel),
        grid=(batch, n_pairs),
        in_specs=[blk(w), blk(w), blk(LANES), per_batch, per_batch, _const_spec(kvn.shape),
                  _const_spec(wuk_bd.shape), _const_spec(wuv_bd.shape),
                  _const_spec(bias_near.shape), _const_spec(bias_rng.shape)],
        out_specs=blk(w),
        out_shape=jax.ShapeDtypeStruct((n, w), BF16),
        scratch_shapes=[pltpu.VMEM((seq + DSA_PAD, 2 * LANES), BF16),
                        pltpu.VMEM((seq + DSA_PAD, LANES), BF16),
                        pltpu.VMEM((2, n_tiles, TK_DSA, tq), I32),
                        pltpu.VMEM((2, n_tiles, TK_DSA, tq), I16),
                        pltpu.VMEM((2, 1, tq), I32),
                        pltpu.VMEM((IDX_HEADS * tq, LANES), BF16),
                        pltpu.VMEM((TK_DSA, 2 * tq), F32),
                        pltpu.VMEM((TK_DSA, gw), F32),
                        pltpu.VMEM((1, LANES), F32),
                        pltpu.VMEM((n_groups, 1, gw), F32),
                        pltpu.VMEM((n_groups, 1, gw), F32),
                        pltpu.VMEM((n_groups, LANES, gw), F32)],
        compiler_params=_params("parallel", "arbitrary"),
        name="dsa",
    )(qb, qi, small, ckv, ki2, kvn, wuk_bd, wuv_bd, bias_near, bias_rng)


def _merge_mem_kernel(x_ref, oa_ref, ob_ref, oc_ref, gate_ref, wbr_ref, wout_ref,
                      g_ref, wq_ref, kv_ref, wo_ref, o_ref):
    d = x_ref.shape[1]
    merged = jnp.zeros(x_ref.shape, F32)
    for j, br_ref in enumerate((oa_ref, ob_ref, oc_ref)):
        gate = jax.nn.sigmoid(gate_ref[:, j * d:(j + 1) * d].astype(F32))
        merged = merged + gate * _dot(br_ref[...], wbr_ref[j])
    x = x_ref[...] + _dot(merged.astype(BF16), wout_ref[...])
    hw = MEM_HEADS * MEM_HEAD_DIM
    q = _dot(_rms(x, g_ref[...]).astype(BF16), wq_ref[...]).astype(BF16)
    outs = []
    for h in range(MEM_HEADS):
        lanes = slice(h * MEM_HEAD_DIM, (h + 1) * MEM_HEAD_DIM)
        s = _dot_nt(q[:, lanes], kv_ref[:, lanes]) * (MEM_HEAD_DIM ** -0.5)
        pr = jnp.exp(s - jnp.max(s, axis=1, keepdims=True))
        pr = pr / jnp.sum(pr, axis=1, keepdims=True)
        outs.append(_dot(pr.astype(BF16), kv_ref[:, hw + h * MEM_HEAD_DIM:hw + (h + 1) * MEM_HEAD_DIM]))
    o = jnp.concatenate(outs, axis=1).astype(BF16)
    o_ref[...] = x + _dot(o, wo_ref[...])


def _merge_mem(x, oa, ob, oc, gates, wbr, wout, g, wq, kv, wo, batch):
    n, d = x.shape
    seq = n // batch
    tm = min(TM_TOKENS, seq)
    ns = seq // tm
    bw = oa.shape[1]
    mem_len = kv.shape[0] // batch
    blk = lambda width: pl.BlockSpec((tm, width), lambda b, s: (b * ns + s, 0))
    return pl.pallas_call(
        _merge_mem_kernel,
        grid=(batch, ns),
        in_specs=[blk(d), blk(bw), blk(bw), blk(bw), blk(3 * d), _const_spec(wbr.shape),
                  _const_spec(wout.shape), _const_spec((1, d)), _const_spec(wq.shape),
                  pl.BlockSpec((mem_len, kv.shape[1]), lambda b, s: (b, 0)),
                  _const_spec(wo.shape)],
        out_specs=blk(d),
        out_shape=jax.ShapeDtypeStruct((n, d), F32),
        compiler_params=_params("parallel", "arbitrary"),
        name="merge_mem",
    )(x, oa, ob, oc, gates, wbr, wout, g, wq, kv, wo)


def _t5_bucket(rel):
    nb = NUM_BUCKETS // 2
    max_exact = nb // 2
    base = jnp.where(rel > 0, nb, 0)
    n = jnp.abs(rel)
    n_f = jnp.maximum(n, 1).astype(jnp.float32)
    large = max_exact + (jnp.log(n_f / max_exact) / math.log(MAX_DISTANCE / max_exact)
                         * (nb - max_exact)).astype(jnp.int32)
    large = jnp.minimum(large, nb - 1)
    return base + jnp.where(n < max_exact, n, large)


def _near_bias(t5_bias):
    t = jnp.arange(TQ_DSA)[:, None]
    s = jnp.arange(DSA_NEAR_KEYS)[None, :] - MAX_DISTANCE
    onehot = jax.nn.one_hot(_t5_bucket(s - t), NUM_BUCKETS, dtype=F32)
    tbl = jnp.einsum("tsb,bh->hst", onehot, (t5_bias - t5_bias[NUM_BUCKETS // 2 - 1]) * LOG2E,
                     precision=lax.Precision.HIGHEST)
    return tbl.astype(F32)


def _block_diag(w):
    nb, a, b = w.shape
    eye = jnp.eye(nb, dtype=w.dtype)
    return (eye[:, None, :, None] * w[:, :, None, :]).reshape(nb * a, nb * b)


def _pad_cols(w, width):
    return jnp.pad(w, ((0, 0), (0, width - w.shape[1])))


def kernel(x, mem, ffn1_norm, ffn1_w_gu, ffn1_w_down, mix_norm, w_in, conv_w, conv_b, rg_wa, rg_ba, rg_wx, rg_bx, rg_lambda, kv_norm, w_uk, w_uv, forget_bias, w_branch, w_out, xattn_norm, mem_norm, w_mq, w_mkv, w_mo, ffn2_norm, ffn2_w_gu, ffn2_w_down, t5_bias, final_norm):
    batch, seq, d = x.shape
    depth = w_in.shape[0]
    d_ff = ffn1_w_down.shape[1]
    dr = conv_w.shape[2]
    heads = w_uk.shape[1]
    d_lat = w_uk.shape[3]
    n = batch * seq
    assert seq % TK_DSA == 0 and d_lat == LANES and 2 * IDX_DIM == LANES and 2 * HEAD_DIM == LANES

    widths = (dr, dr, heads * HEAD_DIM, d_lat, IDX_HEADS * IDX_DIM, IDX_DIM, IDX_HEADS,
              heads * HEAD_DIM, heads * HEAD_DIM, heads * HEAD_DIM, heads, d, d, d)
    offs = np.concatenate([[0], np.cumsum(widths)])
    col = lambda w, j: w[:, int(offs[j]):int(offs[j + 1])]

    bias_near = _near_bias(t5_bias)
    cum_sel = _cum_selector(heads * HEAD_DIM // LANES, IDX_HEADS)
    row = lambda v: v.reshape(1, -1).astype(F32)

    xf = x.reshape(n, d)
    memf = mem.reshape(batch * mem.shape[1], d)
    for l in range(depth):
        wl = w_in[l]
        small_w = _pad_cols(jnp.concatenate([col(wl, 6), col(wl, 10)], axis=1), LANES)
        w_all = jnp.concatenate(
            [col(wl, 0), col(wl, 1), col(wl, 2), col(wl, 3), col(wl, 4), col(wl, 5), col(wl, 5),
             col(wl, 7) * (HEAD_DIM ** -0.5 * LOG2E),
             col(wl, 8), col(wl, 9), col(wl, 11), col(wl, 12), col(wl, 13)],
            axis=1).astype(BF16)
        w_all = jnp.concatenate([w_all, small_w.astype(BF16)], axis=1)
        out_widths = (2 * dr, heads * HEAD_DIM, d_lat, IDX_HEADS * IDX_DIM, 2 * IDX_DIM,
                      heads * HEAD_DIM, heads * HEAD_DIM, heads * HEAD_DIM, 3 * d, LANES)
        out_dtypes = (BF16,) * 9 + (F32,)

        xf = _ffn(xf, row(ffn1_norm[l]), ffn1_w_gu[l][:, :d_ff].astype(BF16),
                  ffn1_w_gu[l][:, d_ff:].astype(BF16), ffn1_w_down[l].astype(BF16),
                  row(final_norm), norm_out=False)

        rg, qb, ckv, qi, ki2, qc, kc, vc, gates, small = _norm_proj(
            xf, row(mix_norm[l]), w_all, out_widths, out_dtypes)

        wax = jnp.concatenate([_block_diag(rg_wa[l]), _block_diag(rg_wx[l])], axis=1).astype(BF16)
        bax = jnp.concatenate([rg_ba[l], rg_bx[l]]).reshape(1, -1)
        o_a = _rglru(rg, batch, conv_w[l], row(conv_b[l]), wax, bax, row(rg_lambda[l]))

        fbias = _pad_cols(jnp.concatenate([jnp.zeros((IDX_HEADS,), F32), forget_bias[l]])[None], LANES)
        o_c = _fox(qc, kc, vc, _forget_cumsum(small, batch, fbias), cum_sel, batch)

        o_b = _dsa(qb, qi, small, ckv, ki2, row(kv_norm[l]), _block_diag(w_uk[l]).T.astype(BF16),
                   _block_diag(w_uv[l]).T.astype(BF16), bias_near, batch)

        (kv,) = _norm_proj(memf, row(mem_norm[l]), w_mkv[l].astype(BF16),
                           (w_mkv.shape[2],), (BF16,))
        xf = _merge_mem(xf, o_a, o_b, o_c, gates, w_branch[l].astype(BF16), w_out[l].astype(BF16),
                        row(xattn_norm[l]), w_mq[l].astype(BF16), kv, w_mo[l].astype(BF16), batch)

        xf = _ffn(xf, row(ffn2_norm[l]), ffn2_w_gu[l][:, :d_ff].astype(BF16),
                  ffn2_w_gu[l][:, d_ff:].astype(BF16), ffn2_w_down[l].astype(BF16),
                  row(final_norm), norm_out=(l == depth - 1))
    return xf.reshape(batch, seq, d)
```

```python
import functools
import math

import numpy as np
import jax
import jax.numpy as jnp
from jax import lax
from jax.experimental import pallas as pl
from jax.experimental.pallas import tpu as pltpu

F32 = jnp.float32
BF16 = jnp.bfloat16
I32 = jnp.int32
I16 = jnp.int16

LANES = 128
SUBLANES = 8
VMEM_LIMIT_BYTES = 56 * 1024 * 1024

EPS = 1e-6
NEG = -1e30
LOG2E = math.log2(math.e)
FAST_GAP = 100.0
CHUNK = 64
CHUNK_SHIFT = CHUNK.bit_length() - 1
HEAD_DIM = 64
RG_C = 8.0
CONV_W = 4
IDX_HEADS = 8
IDX_DIM = 64
TOPK_MAX = 256
MEM_HEADS = 4
MEM_HEAD_DIM = 128
NUM_BUCKETS = 32
MAX_DISTANCE = 128

TM_TOKENS = 512
FFN_CHUNK = 256
PROJ_CHUNK = 1024
TS_SCAN = 256
TQ_FOX = 512
CUM_PARTS = 3
TQ_DSA = 256
DSA_NEAR_KEYS = TQ_DSA + MAX_DISTANCE
TK_DSA = 512
DSA_HEAD_GROUP = 2
SEARCH_ACCUMULATORS = 4
DSA_PAD = TK_DSA - TQ_DSA


def _key_of_float(v):
    b = int(np.float32(v).view(np.int32))
    return b ^ ((b >> 31) & 0x7FFFFFFF)


KEY_NEG = _key_of_float(NEG)


def _params(*semantics):
    return pltpu.CompilerParams(dimension_semantics=semantics, vmem_limit_bytes=VMEM_LIMIT_BYTES)


def _const_spec(shape):
    nd = len(shape)
    return pl.BlockSpec(shape, lambda *_: (0,) * nd, pipeline_mode=pl.Buffered(1))


def _rms(x32, g):
    ms = jnp.mean(x32 * x32, axis=-1, keepdims=True)
    return x32 * lax.rsqrt(ms + EPS) * g


def _dot(a, b):
    return jnp.dot(a, b, preferred_element_type=F32)


def _dot_nt(a, b):
    return lax.dot_general(a, b, (((1,), (1,)), ((), ())), preferred_element_type=F32)


def _tree_sum(xs):
    while len(xs) > 1:
        xs = [xs[j] + xs[j + 1] for j in range(0, len(xs) - 1, 2)] + ([xs[-1]] if len(xs) % 2 else [])
    return xs[0]


def _dot_tn(a, b):
    return lax.dot_general(a, b, (((0,), (0,)), ((), ())), preferred_element_type=F32)


def _ffn_kernel(x_ref, g_ref, wg_ref, wu_ref, wd_ref, gout_ref, o_ref, a_ref, *, norm_out):
    x = x_ref[...]
    h = _rms(x, g_ref[...]).astype(BF16)
    d_ff = wg_ref.shape[1]
    for f0 in range(0, d_ff, FFN_CHUNK):
        g = _dot(h, wg_ref[:, f0:f0 + FFN_CHUNK])
        u = _dot(h, wu_ref[:, f0:f0 + FFN_CHUNK])
        a_ref[:, f0:f0 + FFN_CHUNK] = (g * jax.nn.sigmoid(g) * u).astype(BF16)
    y = x + 0.5 * _dot(a_ref[...], wd_ref[...])
    o_ref[...] = _rms(y, gout_ref[...]) if norm_out else y


def _ffn(x, g, wg, wu, wd, g_out, norm_out):
    n, d = x.shape
    d_ff = wg.shape[1]
    tm = min(TM_TOKENS, n)
    return pl.pallas_call(
        functools.partial(_ffn_kernel, norm_out=norm_out),
        grid=(n // tm,),
        in_specs=[pl.BlockSpec((tm, d), lambda i: (i, 0)),
                  _const_spec((1, d)), _const_spec((d, d_ff)), _const_spec((d, d_ff)),
                  _const_spec((d_ff, d)), _const_spec((1, d))],
        out_specs=pl.BlockSpec((tm, d), lambda i: (i, 0)),
        out_shape=jax.ShapeDtypeStruct((n, d), F32),
        scratch_shapes=[pltpu.VMEM((tm, d_ff), BF16)],
        compiler_params=_params("parallel"),
        name="ffn",
    )(x, g, wg, wu, wd, g_out)


def _norm_proj_kernel(x_ref, g_ref, w_ref, *o_refs):
    h = _rms(x_ref[...].astype(F32), g_ref[...]).astype(BF16)
    starts = np.cumsum([0] + [o_ref.shape[1] for o_ref in o_refs])
    for c0 in range(0, w_ref.shape[1], PROJ_CHUNK):
        c1 = min(c0 + PROJ_CHUNK, w_ref.shape[1])
        y = _dot(h, w_ref[:, c0:c1])
        for o_ref, o0 in zip(o_refs, starts[:-1]):
            lo, hi = max(c0, int(o0)), min(c1, int(o0) + o_ref.shape[1])
            if lo < hi:
                o_ref[:, lo - int(o0):hi - int(o0)] = y[:, lo - c0:hi - c0].astype(o_ref.dtype)


def _norm_proj(x, g, w, widths, dtypes):
    n, d = x.shape
    tm = min(TM_TOKENS, n)
    assert sum(widths) == w.shape[1]
    return pl.pallas_call(
        _norm_proj_kernel,
        grid=(n // tm,),
        in_specs=[pl.BlockSpec((tm, d), lambda i: (i, 0)), _const_spec((1, d)),
                  _const_spec(w.shape)],
        out_specs=[pl.BlockSpec((tm, wd), lambda i: (i, 0)) for wd in widths],
        out_shape=[jax.ShapeDtypeStruct((n, wd), dt) for wd, dt in zip(widths, dtypes)],
        compiler_params=_params("parallel"),
        name="norm_proj",
    )(x, g, w)


def _gelu_tanh(x):
    return 0.5 * x * (1.0 + jnp.tanh(math.sqrt(2.0 / math.pi) * (x + 0.044715 * (x * x * x))))


def _softplus(x):
    return jnp.maximum(x, 0.0) + jnp.log1p(jnp.exp(-jnp.abs(x)))


def _rglru_kernel(rg_ref, cw_ref, cb_ref, wax_ref, bax_ref, lam_ref, o_ref, xbuf, hc):
    ts = rg_ref.shape[0]
    dr = o_ref.shape[1]

    @pl.when(pl.program_id(1) == 0)
    def _():
        xbuf[0:SUBLANES, :] = jnp.zeros((SUBLANES, dr), F32)
        hc[...] = jnp.zeros_like(hc)

    xr = rg_ref[:, :dr].astype(F32)
    gr = rg_ref[:, dr:].astype(F32)
    xbuf[SUBLANES:, :] = xr
    xc = cb_ref[...] + cw_ref[CONV_W - 1:CONV_W, :] * xr
    for j in range(CONV_W - 1):
        xc = xc + cw_ref[j:j + 1, :] * xbuf[pl.ds(SUBLANES - (CONV_W - 1) + j, ts), :]
    xbuf[0:SUBLANES, :] = xbuf[ts:ts + SUBLANES, :]

    ax = _dot(xc.astype(BF16), wax_ref[...]) + bax_ref[...]
    r = jax.nn.sigmoid(ax[:, :dr])
    gi = jax.nn.sigmoid(ax[:, dr:])
    log_a = (-RG_C) * r * _softplus(-lam_ref[...])
    a = jnp.exp(log_a)
    th = jnp.tanh(log_a)
    num = -2.0 * th
    u = num * lax.rsqrt(jnp.maximum(num * (1.0 - th), jnp.finfo(F32).tiny)) * (gi * xc)

    row = lax.broadcasted_iota(I32, (ts, dr), 0) & (SUBLANES - 1)
    d = 1
    while d < SUBLANES:
        a_sh = pltpu.roll(a, d, 0)
        u_sh = pltpu.roll(u, d, 0)
        keep = row >= d
        u = jnp.where(keep, a * u_sh + u, u)
        a = jnp.where(keep, a * a_sh, a)
        d *= 2
    carry = hc[0:1, :]
    groups = []
    for g in range(ts // SUBLANES):
        rows = slice(g * SUBLANES, (g + 1) * SUBLANES)
        h_g = u[rows] + a[rows] * carry
        groups.append(h_g)
        carry = h_g[SUBLANES - 1:SUBLANES, :]
    h = jnp.concatenate(groups, axis=0)
    hc[...] = jnp.broadcast_to(carry, hc.shape)
    o_ref[...] = (h * _gelu_tanh(gr)).astype(o_ref.dtype)


def _rglru(rg, batch, conv_w, conv_b, wax, bax, lam):
    n, two_dr = rg.shape
    dr = two_dr // 2
    seq = n // batch
    ts = min(TS_SCAN, seq)
    ns = seq // ts
    return pl.pallas_call(
        _rglru_kernel,
        grid=(batch, ns),
        in_specs=[pl.BlockSpec((ts, two_dr), lambda b, s: (b * ns + s, 0)),
                  _const_spec(conv_w.shape), _const_spec(conv_b.shape), _const_spec(wax.shape),
                  _const_spec(bax.shape), _const_spec(lam.shape)],
        out_specs=pl.BlockSpec((ts, dr), lambda b, s: (b * ns + s, 0)),
        out_shape=jax.ShapeDtypeStruct((n, dr), BF16),
        scratch_shapes=[pltpu.VMEM((ts + SUBLANES, dr), F32), pltpu.VMEM((SUBLANES, dr), F32)],
        compiler_params=_params("parallel", "arbitrary"),
        name="rglru",
    )(rg, conv_w, conv_b, wax, bax, lam)


def _forget_cumsum_kernel(f_ref, b_ref, o_ref):
    z = f_ref[...] + b_ref[...]
    x = -_softplus(-z)
    seq = x.shape[0]
    row = lax.broadcasted_iota(I32, x.shape, 0)
    d = 1
    while d < seq:
        x = jnp.where(row >= d, x + pltpu.roll(x, d, 0), x)
        d *= 2
    x = x * LOG2E
    for j in range(CUM_PARTS):
        part = x.astype(BF16)
        o_ref[j] = part
        x = x - part.astype(F32)


def _forget_cumsum(small, batch, fbias):
    n, w = small.shape
    seq = n // batch
    return pl.pallas_call(
        _forget_cumsum_kernel,
        grid=(batch,),
        in_specs=[pl.BlockSpec((seq, w), lambda b: (b, 0)), _const_spec((1, w))],
        out_specs=pl.BlockSpec((CUM_PARTS, seq, w), lambda b: (0, b, 0)),
        out_shape=jax.ShapeDtypeStruct((CUM_PARTS, n, w), BF16),
        compiler_params=_params("parallel"),
        name="forget_cumsum",
    )(small, fbias)


def _split_bf16(x, n):
    parts = []
    for _ in range(n):
        part = x.astype(BF16).astype(F32)
        parts.append(part)
        x = x - part
    return parts


def _fox_kernel(q_ref, k_ref, v_ref, cp_ref, sel_ref, o_ref, kaug, qaug, s0, kmax, m_s, l_s, acc_s):
    tq = q_ref.shape[0]
    n_pairs = q_ref.shape[1] // LANES
    i = pl.program_id(1)
    lane = lax.broadcasted_iota(I32, (1, LANES), 1)
    bound_lanes = 2 * CUM_PARTS

    @pl.when(i == 0)
    def _():
        ones = jnp.where((lane >= bound_lanes) & (lane < bound_lanes + CUM_PARTS), 1.0, 0.0)
        for p in range(n_pairs):
            kp = k_ref[:, p * LANES:(p + 1) * LANES]
            kaug[:, 2 * p * LANES:(2 * p + 1) * LANES] = kp
            extra = _dot(cp_ref[0], sel_ref[p, 0])
            for j in range(1, CUM_PARTS):
                extra = extra + _dot(cp_ref[j], sel_ref[p, j])
            kaug[:, (2 * p + 1) * LANES:(2 * p + 2) * LANES] = (extra + ones).astype(BF16)
            k2 = kp.astype(F32) * kp.astype(F32)
            for e in range(2):
                in_head = (lane >= e * HEAD_DIM) & (lane < (e + 1) * HEAD_DIM)
                n2 = jnp.sum(jnp.where(in_head, k2, 0.0), axis=1, keepdims=True)
                kmax[2 * p + e] = jnp.broadcast_to(jnp.sqrt(jnp.max(n2, axis=0, keepdims=True)),
                                                   (1, LANES))

    q_rows = pl.ds(pl.multiple_of(i * tq, tq), tq)
    f_t = cp_ref[0, q_rows, :].astype(F32)
    for j in range(1, CUM_PARTS):
        f_t = f_t + cp_ref[j, q_rows, :].astype(F32)

    slack = jnp.zeros((tq, 1), F32)
    for p in range(n_pairs):
        qp = q_ref[:, p * LANES:(p + 1) * LANES]
        for e in range(2):
            h = 2 * p + e
            rows = slice(e * tq, (e + 1) * tq)
            in_head = (lane >= e * HEAD_DIM) & (lane < (e + 1) * HEAD_DIM)
            qh = jnp.where(in_head, qp, jnp.zeros_like(qp))
            qaug[p, rows, 0:LANES] = qh
            qf = qh.astype(F32)
            reach = jnp.sqrt(jnp.sum(qf * qf, axis=1, keepdims=True)) * kmax[h][:, 0:1]
            slack = jnp.maximum(slack, reach)
            bound = reach - f_t[:, IDX_HEADS + h:IDX_HEADS + h + 1]
            extra = jnp.where((lane >= CUM_PARTS * e) & (lane < CUM_PARTS * (e + 1)), 1.0, 0.0)
            for j, part in enumerate(_split_bf16(-bound, CUM_PARTS)):
                extra = jnp.where(lane == bound_lanes + j, part, extra)
            qaug[p, rows, LANES:2 * LANES] = extra.astype(BF16)
    fast = 2.0 * jnp.max(slack) < FAST_GAP

    l_s[...] = jnp.zeros(l_s.shape, F32)
    acc_s[...] = jnp.zeros(acc_s.shape, F32)
    key_row = lax.broadcasted_iota(I32, (tq, 2 * tq), 0)
    query = lax.broadcasted_iota(I32, (tq, 2 * tq), 1) & (tq - 1)
    causal = key_row <= query

    def logits(j, p):
        rows = pl.ds(pl.multiple_of(j * tq, tq), tq)
        return _dot_nt(kaug[rows, 2 * p * LANES:(2 * p + 2) * LANES], qaug[p])

    s0[...] = logits(0, 0)

    def step(j, masked, online):
        rows = pl.ds(pl.multiple_of(j * tq, tq), tq)
        s_next = s0[...]
        for p in range(n_pairs):
            s = s_next
            if p + 1 < n_pairs:
                s_next = logits(j, p + 1)
            else:
                s0[...] = logits(jnp.minimum(j + 1, i), 0)
            if masked:
                s = jnp.where(causal, s, NEG)
            vt = v_ref[rows, p * LANES:(p + 1) * LANES]
            if online:
                m_old = m_s[p]
                m_new = jnp.maximum(m_old, jnp.max(s, axis=0, keepdims=True))
                alpha = jnp.exp2(m_old - m_new)
                pr = jnp.exp2(s - m_new)
                l_s[p] = alpha * l_s[p] + jnp.sum(pr, axis=0, keepdims=True)
                m_s[p] = m_new
                acc_s[p] = alpha * acc_s[p] + _dot_tn(vt, pr.astype(BF16))
            else:
                pr = jnp.exp2(s)
                l_s[p] = l_s[p] + jnp.sum(pr, axis=0, keepdims=True)
                acc_s[p] = acc_s[p] + _dot_tn(vt, pr.astype(BF16))

    def run(online):
        def body(j, _):
            step(j, masked=False, online=online)
            return 0
        lax.fori_loop(0, i, body, 0)
        step(i, masked=True, online=online)

    @pl.when(fast)
    def _():
        run(online=False)

    @pl.when(jnp.logical_not(fast))
    def _():
        m_s[...] = jnp.full(m_s.shape, NEG, F32)
        run(online=True)

    feat = lax.broadcasted_iota(I32, (LANES, tq), 0)
    for p in range(n_pairs):
        o_t = acc_s[p] / l_s[p]
        o_t = jnp.where(feat < HEAD_DIM, o_t[:, :tq], o_t[:, tq:])
        o_ref[:, p * LANES:(p + 1) * LANES] = o_t.T.astype(o_ref.dtype)


def _fox(q, k, v, cum_parts, sel, batch):
    n, w = q.shape
    seq = n // batch
    tq = min(TQ_FOX, seq)
    nq = seq // tq
    n_pairs = w // LANES
    return pl.pallas_call(
        _fox_kernel,
        grid=(batch, nq),
        in_specs=[pl.BlockSpec((tq, w), lambda b, i: (b * nq + i, 0)),
                  pl.BlockSpec((seq, w), lambda b, i: (b, 0)),
                  pl.BlockSpec((seq, w), lambda b, i: (b, 0)),
                  pl.BlockSpec((CUM_PARTS, seq, LANES), lambda b, i: (0, b, 0)),
                  _const_spec(sel.shape)],
        out_specs=pl.BlockSpec((tq, w), lambda b, i: (b * nq + i, 0)),
        out_shape=jax.ShapeDtypeStruct((n, w), BF16),
        scratch_shapes=[pltpu.VMEM((seq, 2 * w), BF16),
                        pltpu.VMEM((n_pairs, 2 * tq, 2 * LANES), BF16),
                        pltpu.VMEM((tq, 2 * tq), F32),
                        pltpu.VMEM((2 * n_pairs, 1, LANES), F32),
                        pltpu.VMEM((n_pairs, 1, 2 * tq), F32),
                        pltpu.VMEM((n_pairs, 1, 2 * tq), F32),
                        pltpu.VMEM((n_pairs, LANES, 2 * tq), F32)],
        compiler_params=_params("parallel", "arbitrary"),
        name="fox",
    )(q, k, v, cum_parts, sel)


def _cum_selector(n_pairs, first_lane):
    sel = np.zeros((n_pairs, CUM_PARTS, LANES, LANES), np.float32)
    for p in range(n_pairs):
        for j in range(CUM_PARTS):
            for e in range(2):
                sel[p, j, first_lane + 2 * p + e, CUM_PARTS * e + j] = -1.0
    return jnp.asarray(sel, BF16)


def _float_key(x):
    b = lax.bitcast_convert_type(x, I32)
    return b ^ ((b >> 31) & 0x7FFFFFFF)


def _dsa_kernel(qb_ref, qi_ref, sm_ref, ckv_ref, ki_ref, kvn_ref, wuk_ref, wuv_ref, bias_ref, brng_ref,
                o_ref, c_s, ki_s, keys, keys16, thr_s, qim, d0, s0, cmax, m_s, l_s, acc_s, *, k_sel):
    tq = TQ_DSA
    seq = ckv_ref.shape[0]
    heads = bias_ref.shape[0]
    tk = TK_DSA
    pair = pl.program_id(1)
    lane = lax.broadcasted_iota(I32, (1, LANES), 1)

    @pl.when(pair == 0)
    def _():
        c_s[0:DSA_PAD, :] = jnp.zeros((DSA_PAD, 2 * LANES), BF16)
        ki_s[0:DSA_PAD, :] = jnp.zeros((DSA_PAD, LANES), BF16)
        c = _rms(ckv_ref[...].astype(F32), kvn_ref[...]).astype(BF16)
        c_s[DSA_PAD:, 0:LANES] = c
        c_s[DSA_PAD:, LANES:] = jnp.broadcast_to(jnp.where(lane < CUM_PARTS, 1.0, 0.0),
                                                 (seq, LANES)).astype(BF16)
        ki_s[DSA_PAD:, :] = ki_ref[...]
        cf = c.astype(F32)
        n2 = jnp.max(jnp.sum(cf * cf, axis=1, keepdims=True), axis=0, keepdims=True)
        cmax[...] = jnp.broadcast_to(jnp.sqrt(n2), cmax.shape)

    n_tiles = pair + 1
    max_tiles = pl.cdiv(seq, tk)
    key_local = lax.broadcasted_iota(I32, (tk, tq), 0)

    def n_valid(blk):
        return (2 * pair + blk + 1) * tq

    def tile_rows(blk, k):
        return pl.ds(pl.multiple_of(n_valid(blk) - (k + 1) * tk + DSA_PAD, tq), tk)

    def q_rows(blk):
        return pl.ds(pl.multiple_of(blk * tq, tq), tq)

    def score_block(blk, _):
        for h in range(IDX_HEADS):
            qp = qi_ref[q_rows(blk), (h // 2) * LANES:(h // 2 + 1) * LANES]
            e = h % 2
            in_head = (lane >= e * IDX_DIM) & (lane < (e + 1) * IDX_DIM)
            qim[h * tq:(h + 1) * tq, :] = jnp.where(in_head, qp, jnp.zeros_like(qp))
        w_t = sm_ref[q_rows(blk), :].T[0:IDX_HEADS, :] * (IDX_HEADS ** -0.5 * IDX_DIM ** -0.5)
        q_chunk = ((2 * pair + blk) * tq + lax.broadcasted_iota(I32, (tk, tq), 1)) >> CHUNK_SHIFT

        def pair_dots(k, p):
            return _dot_nt(ki_s[tile_rows(blk, k), :], qim[2 * p * tq:(2 * p + 2) * tq, :])

        d0[...] = pair_dots(0, 0)

        def score_tile(k, nearest):
            d_next = d0[...]
            sc = jnp.zeros((tk, tq), F32)
            for p in range(IDX_HEADS // 2):
                d = d_next
                if p + 1 < IDX_HEADS // 2:
                    d_next = pair_dots(k, p + 1)
                else:
                    d0[...] = pair_dots(jnp.minimum(k + 1, n_tiles - 1), 0)
                for e in range(2):
                    sc = sc + jnp.maximum(d[:, e * tq:(e + 1) * tq], 0.0) * w_t[2 * p + e:2 * p + e + 1, :]
            pos = key_local + (n_valid(blk) - (k + 1) * tk)
            if nearest:
                sc = jnp.where((pos >> CHUNK_SHIFT) <= q_chunk, sc, NEG)
            sc = jnp.where(pos >= 0, sc, NEG)
            key = _float_key(sc)
            keys[blk, k] = key
            keys16[blk, k] = (key >> 16).astype(I16)

        score_tile(0, nearest=True)

        def farther(k, _):
            score_tile(k, nearest=False)
            return 0

        lax.fori_loop(1, n_tiles, farther, 0)
        return 0

    lax.fori_loop(0, 2, score_block, 0)

    def count_ge16(thr16, t):
        rows16 = 2 * SUBLANES
        cnts = []
        for blk in range(2):
            accs = [jnp.zeros((rows16, tq), I16)] * SEARCH_ACCUMULATORS
            for k in range(t):
                for c in range(tk // rows16):
                    x = keys16[blk, k, c * rows16:(c + 1) * rows16, :]
                    ge = jnp.where(x >= thr16[blk], jnp.ones((), I16), jnp.zeros((), I16))
                    accs[c % SEARCH_ACCUMULATORS] = accs[c % SEARCH_ACCUMULATORS] + ge
            cnts.append(jnp.sum(_tree_sum(accs).astype(I32), axis=0, keepdims=True))
        return tuple(cnts)

    half = 2 ** 15

    def per_tile_count(fn, *args):
        return lax.switch(n_tiles - 1, [functools.partial(fn, t) for t in range(1, max_tiles + 1)], *args)

    def kth_largest16(k_need):
        def search(t, k_need):
            def step(b, us):
                cands = tuple(u | lax.shift_left(jnp.int32(1), 15 - b) for u in us)
                cnts = count_ge16(tuple((c - half).astype(I16) for c in cands), t)
                return tuple(jnp.where(n >= need, c, u) for n, need, c, u in zip(cnts, k_need, cands, us))
            zero = jnp.zeros((1, tq), I32)
            return tuple(u - half for u in lax.fori_loop(0, 16, step, (zero, zero)))
        return per_tile_count(search, k_need)

    full = jnp.full((1, tq), k_sel, I32)
    t_hi = kth_largest16((full, full))
    n_above = per_tile_count(
        lambda t, ts: count_ge16(tuple((jnp.minimum(x, half - 2) + 1).astype(I16) for x in ts), t), t_hi)
    n_above = tuple(jnp.where(t >= half - 1, 0, n) for t, n in zip(t_hi, n_above))

    def low_halves(k, _):
        for blk in range(2):
            key = keys[blk, k]
            keys16[blk, k] = jnp.where((key >> 16) == t_hi[blk], (key & 0xFFFF) - half, -half).astype(I16)
        return 0

    lax.fori_loop(0, n_tiles, low_halves, 0)
    t_lo = kth_largest16(tuple(k_sel - n for n in n_above))
    for blk in range(2):
        thr_s[blk] = jnp.maximum(lax.shift_left(t_hi[blk], 16) + (t_lo[blk] + half), KEY_NEG + 1)

    def finish_block(blk, _):
        thr = thr_s[blk]

        def count_rows(hit_fn):
            def body(k, cnt):
                return cnt + jnp.sum(hit_fn(k).reshape(tk // SUBLANES, SUBLANES, tq), axis=0)
            cnt = lax.fori_loop(0, n_tiles, body, jnp.zeros((SUBLANES, tq), I32))
            return jnp.sum(cnt, axis=0, keepdims=True)

        excess = count_rows(lambda k: jnp.where(keys[blk, k] >= thr, 1, 0)) - k_sel

        @pl.when(jnp.max(excess) > 0)
        def _():
            def count_eq_below(cut):
                def hit(k):
                    pos = key_local + (n_valid(blk) - (k + 1) * tk)
                    return jnp.where(keys[blk, k] == thr, jnp.where(pos < cut, 1, 0), 0)
                return count_rows(hit)

            n_bits = max(1, (2 * seq - 1).bit_length())
            keep = count_eq_below(jnp.full((1, tq), 2 ** n_bits, I32)) - jnp.maximum(excess, 0)

            def cut_step(b, cut):
                cand = cut | lax.shift_left(jnp.int32(1), n_bits - 1 - b)
                return jnp.where(count_eq_below(cand) <= keep, cand, cut)

            cut = lax.fori_loop(0, n_bits, cut_step, jnp.zeros((1, tq), I32))

            def drop(k, _):
                pos = key_local + (n_valid(blk) - (k + 1) * tk)
                kk = keys[blk, k]
                keys[blk, k] = jnp.where(kk == thr, jnp.where(pos >= cut, KEY_NEG, kk), kk)
                return 0

            lax.fori_loop(0, n_tiles, drop, 0)

        attend_block(blk, thr)
        return 0

    n_groups = heads // DSA_HEAD_GROUP
    gw = DSA_HEAD_GROUP * tq
    row_id = lax.broadcasted_iota(I32, (LANES, tq), 0)

    def attend_block(blk, thr):
        ql_t = (_dot_nt(wuk_ref[...], qb_ref[q_rows(blk), :]) * (HEAD_DIM ** -0.5 * LOG2E)).astype(BF16)
        slack = jnp.zeros((1, tq), F32)
        ql_aug = []
        for h in range(heads):
            ql_h = ql_t[h * LANES:(h + 1) * LANES, :]
            qf = ql_h.astype(F32)
            reach = jnp.sqrt(jnp.sum(qf * qf, axis=0, keepdims=True)) * cmax[:, 0:1]
            slack = jnp.maximum(slack, 2.0 * reach + (brng_ref[0, h][:, 0:1] - brng_ref[1, h][:, 0:1]))
            bound_rows = jnp.zeros((LANES, tq), F32)
            for j, part in enumerate(_split_bf16(-(reach + brng_ref[0, h][:, 0:1]), CUM_PARTS)):
                bound_rows = jnp.where(row_id == j, part, bound_rows)
            ql_aug.append(jnp.concatenate([ql_h, bound_rows.astype(BF16)], axis=0))
        ql_g = [jnp.concatenate(ql_aug[g * DSA_HEAD_GROUP:(g + 1) * DSA_HEAD_GROUP], axis=1)
                for g in range(n_groups)]
        fast = jnp.max(slack) < FAST_GAP
        l_s[...] = jnp.zeros(l_s.shape, F32)
        acc_s[...] = jnp.zeros(acc_s.shape, F32)

        def attend(k, near, online):
            ct_aug = c_s[tile_rows(blk, k), :]
            ct = c_s[tile_rows(blk, k), 0:LANES]
            mask = jnp.where(keys[blk, k] >= thr, 0.0, NEG)
            s_next = s0[...]
            for g in range(n_groups):
                s_g = s_next
                if g + 1 < n_groups:
                    s_next = _dot(ct_aug, ql_g[g + 1])
                else:
                    s0[...] = _dot(c_s[tile_rows(blk, jnp.minimum(k + 1, n_tiles - 1)), :], ql_g[0])
                parts = []
                for hh in range(DSA_HEAD_GROUP):
                    s = s_g[:, hh * tq:(hh + 1) * tq]
                    if near:
                        s = jnp.concatenate([s[:tk - DSA_NEAR_KEYS], s[tk - DSA_NEAR_KEYS:]
                                             + bias_ref[g * DSA_HEAD_GROUP + hh]], axis=0)
                    parts.append(s + mask)
                s = jnp.concatenate(parts, axis=1)
                if online:
                    m_old = m_s[g]
                    m_new = jnp.maximum(m_old, jnp.max(s, axis=0, keepdims=True))
                    alpha = jnp.exp2(m_old - m_new)
                    pr = jnp.exp2(s - m_new)
                    l_s[g] = alpha * l_s[g] + jnp.sum(pr, axis=0, keepdims=True)
                    m_s[g] = m_new
                    acc_s[g] = alpha * acc_s[g] + _dot_tn(ct, pr.astype(BF16))
                else:
                    pr = jnp.exp2(s)
                    l_s[g] = l_s[g] + jnp.sum(pr, axis=0, keepdims=True)
                    acc_s[g] = acc_s[g] + _dot_tn(ct, pr.astype(BF16))

        s0[...] = _dot(c_s[tile_rows(blk, 0), :], ql_g[0])

        def run(online):
            attend(0, near=True, online=online)

            def far(k, _):
                attend(k, near=False, online=online)
                return 0

            lax.fori_loop(1, n_tiles, far, 0)

        @pl.when(fast)
        def _():
            run(online=False)

        @pl.when(jnp.logical_not(fast))
        def _():
            m_s[...] = jnp.full(m_s.shape, NEG, F32)
            run(online=True)

        o_parts = []
        for g in range(n_groups):
            o_g = acc_s[g] / l_s[g]
            o_parts += [o_g[:, hh * tq:(hh + 1) * tq] for hh in range(DSA_HEAD_GROUP)]
        o_lat_t = jnp.concatenate(o_parts, axis=0).astype(BF16)
        o_ref[q_rows(blk), :] = _dot(wuv_ref[...], o_lat_t).T.astype(o_ref.dtype)

    lax.fori_loop(0, 2, finish_block, 0)


def _dsa(qb, qi, small, ckv, ki2, kvn, wuk_bd, wuv_bd, bias_near, batch):
    n, w = qb.shape
    seq = n // batch
    tq = TQ_DSA
    assert TK_DSA == 2 * tq and seq % TK_DSA == 0
    n_pairs = seq // (2 * tq)
    n_tiles = pl.cdiv(seq, TK_DSA)
    heads = bias_near.shape[0]
    n_groups = heads // DSA_HEAD_GROUP
    gw = DSA_HEAD_GROUP * tq
    k_sel = min(TOPK_MAX, seq // 4)
    blk = lambda width: pl.BlockSpec((2 * tq, width), lambda b, i: (b * n_pairs + i, 0))
    per_batch = pl.BlockSpec((seq, LANES), lambda b, i: (b, 0))
    bias_rng = jnp.stack([jnp.maximum(jnp.max(bias_near, axis=(1, 2)), 0.0),
                          jnp.minimum(jnp.min(bias_near, axis=(1, 2)), 0.0)])
    bias_rng = jnp.broadcast_to(bias_rng[:, :, None, None], (2, heads, 1, LANES))
    return pl.pallas_call(
        functools.partial(_dsa_kernel, k_sel=k_sel),
        grid=(batch, n_pairs),
        in_specs=[blk(w), blk(w), blk(LANES), per_batch, per_batch, _const_spec(kvn.shape),
                  _const_spec(wuk_bd.shape), _const_spec(wuv_bd.shape),
                  _const_spec(bias_near.shape), _const_spec(bias_rng.shape)],
        out_specs=blk(w),
        out_shape=jax.ShapeDtypeStruct((n, w), BF16),
        scratch_shapes=[pltpu.VMEM((seq + DSA_PAD, 2 * LANES), BF16),
                        pltpu.VMEM((seq + DSA_PAD, LANES), BF16),
                        pltpu.VMEM((2, n_tiles, TK_DSA, tq), I32),
                        pltpu.VMEM((2, n_tiles, TK_DSA, tq), I16),
                        pltpu.VMEM((2, 1, tq), I32),
                        pltpu.VMEM((IDX_HEADS * tq, LANES), BF16),
                        pltpu.VMEM((TK_DSA, 2 * tq), F32),
                        pltpu.VMEM((TK_DSA, gw), F32),
                        pltpu.VMEM((1, LANES), F32),
                        pltpu.VMEM((n_groups, 1, gw), F32),
                        pltpu.VMEM((n_groups, 1, gw), F32),
                        pltpu.VMEM((n_groups, LANES, gw), F32)],
        compiler_params=_params("parallel", "arbitrary"),
        name="dsa",
    )(qb, qi, small, ckv, ki2, kvn, wuk_bd, wuv_bd, bias_near, bias_rng)


def _merge_mem_kernel(x_ref, oa_ref, ob_ref, oc_ref, gate_ref, wbr_ref, wout_ref,
                      g_ref, wq_ref, kv_ref, wo_ref, o_ref):
    d = x_ref.shape[1]
    merged = jnp.zeros(x_ref.shape, F32)
    for j, br_ref in enumerate((oa_ref, ob_ref, oc_ref)):
        gate = jax.nn.sigmoid(gate_ref[:, j * d:(j + 1) * d].astype(F32))
        merged = merged + gate * _dot(br_ref[...], wbr_ref[j])
    x = x_ref[...] + _dot(merged.astype(BF16), wout_ref[...])
    hw = MEM_HEADS * MEM_HEAD_DIM
    q = _dot(_rms(x, g_ref[...]).astype(BF16), wq_ref[...]).astype(BF16)
    outs = []
    for h in range(MEM_HEADS):
        lanes = slice(h * MEM_HEAD_DIM, (h + 1) * MEM_HEAD_DIM)
        s = _dot_nt(q[:, lanes], kv_ref[:, lanes]) * (MEM_HEAD_DIM ** -0.5)
        pr = jnp.exp(s - jnp.max(s, axis=1, keepdims=True))
        pr = pr / jnp.sum(pr, axis=1, keepdims=True)
        outs.append(_dot(pr.astype(BF16), kv_ref[:, hw + h * MEM_HEAD_DIM:hw + (h + 1) * MEM_HEAD_DIM]))
    o = jnp.concatenate(outs, axis=1).astype(BF16)
    o_ref[...] = x + _dot(o, wo_ref[...])


def _merge_mem(x, oa, ob, oc, gates, wbr, wout, g, wq, kv, wo, batch):
    n, d = x.shape
    seq = n // batch
    tm = min(TM_TOKENS, seq)
    ns = seq // tm
    bw = oa.shape[1]
    mem_len = kv.shape[0] // batch
    blk = lambda width: pl.BlockSpec((tm, width), lambda b, s: (b * ns + s, 0))
    return pl.pallas_call(
        _merge_mem_kernel,
        grid=(batch, ns),
        in_specs=[blk(d), blk(bw), blk(bw), blk(bw), blk(3 * d), _const_spec(wbr.shape),
                  _const_spec(wout.shape), _const_spec((1, d)), _const_spec(wq.shape),
                  pl.BlockSpec((mem_len, kv.shape[1]), lambda b, s: (b, 0)),
                  _const_spec(wo.shape)],
        out_specs=blk(d),
        out_shape=jax.ShapeDtypeStruct((n, d), F32),
        compiler_params=_params("parallel", "arbitrary"),
        name="merge_mem",
    )(x, oa, ob, oc, gates, wbr, wout, g, wq, kv, wo)


def _t5_bucket(rel):
    nb = NUM_BUCKETS // 2
    max_exact = nb // 2
    base = jnp.where(rel > 0, nb, 0)
    n = jnp.abs(rel)
    n_f = jnp.maximum(n, 1).astype(jnp.float32)
    large = max_exact + (jnp.log(n_f / max_exact) / math.log(MAX_DISTANCE / max_exact)
                         * (nb - max_exact)).astype(jnp.int32)
    large = jnp.minimum(large, nb - 1)
    return base + jnp.where(n < max_exact, n, large)


def _near_bias(t5_bias):
    t = jnp.arange(TQ_DSA)[:, None]
    s = jnp.arange(DSA_NEAR_KEYS)[None, :] - MAX_DISTANCE
    onehot = jax.nn.one_hot(_t5_bucket(s - t), NUM_BUCKETS, dtype=F32)
    tbl = jnp.einsum("tsb,bh->hst", onehot, (t5_bias - t5_bias[NUM_BUCKETS // 2 - 1]) * LOG2E,
                     precision=lax.Precision.HIGHEST)
    return tbl.astype(F32)


def _block_diag(w):
    nb, a, b = w.shape
    eye = jnp.eye(nb, dtype=w.dtype)
    return (eye[:, None, :, None] * w[:, :, None, :]).reshape(nb * a, nb * b)


def _pad_cols(w, width):
    return jnp.pad(w, ((0, 0), (0, width - w.shape[1])))


def kernel(x, mem, ffn1_norm, ffn1_w_gu, ffn1_w_down, mix_norm, w_in, conv_w, conv_b, rg_wa, rg_ba, rg_wx, rg_bx, rg_lambda, kv_norm, w_uk, w_uv, forget_bias, w_branch, w_out, xattn_norm, mem_norm, w_mq, w_mkv, w_mo, ffn2_norm, ffn2_w_gu, ffn2_w_down, t5_bias, final_norm):
    batch, seq, d = x.shape
    depth = w_in.shape[0]
    d_ff = ffn1_w_down.shape[1]
    dr = conv_w.shape[2]
    heads = w_uk.shape[1]
    d_lat = w_uk.shape[3]
    n = batch * seq
    assert seq % TK_DSA == 0 and d_lat == LANES and 2 * IDX_DIM == LANES and 2 * HEAD_DIM == LANES

    widths = (dr, dr, heads * HEAD_DIM, d_lat, IDX_HEADS * IDX_DIM, IDX_DIM, IDX_HEADS,
              heads * HEAD_DIM, heads * HEAD_DIM, heads * HEAD_DIM, heads, d, d, d)
    offs = np.concatenate([[0], np.cumsum(widths)])
    col = lambda w, j: w[:, int(offs[j]):int(offs[j + 1])]

    bias_near = _near_bias(t5_bias)
    cum_sel = _cum_selector(heads * HEAD_DIM // LANES, IDX_HEADS)
    row = lambda v: v.reshape(1, -1).astype(F32)

    xf = x.reshape(n, d)
    memf = mem.reshape(batch * mem.shape[1], d)
    for l in range(depth):
        wl = w_in[l]
        small_w = _pad_cols(jnp.concatenate([col(wl, 6), col(wl, 10)], axis=1), LANES)
        w_all = jnp.concatenate(
            [col(wl, 0), col(wl, 1), col(wl, 2), col(wl, 3), col(wl, 4), col(wl, 5), col(wl, 5),
             col(wl, 7) * (HEAD_DIM ** -0.5 * LOG2E),
             col(wl, 8), col(wl, 9), col(wl, 11), col(wl, 12), col(wl, 13)],
            axis=1).astype(BF16)
        w_all = jnp.concatenate([w_all, small_w.astype(BF16)], axis=1)
        out_widths = (2 * dr, heads * HEAD_DIM, d_lat, IDX_HEADS * IDX_DIM, 2 * IDX_DIM,
                      heads * HEAD_DIM, heads * HEAD_DIM, heads * HEAD_DIM, 3 * d, LANES)
        out_dtypes = (BF16,) * 9 + (F32,)

        xf = _ffn(xf, row(ffn1_norm[l]), ffn1_w_gu[l][:, :d_ff].astype(BF16),
                  ffn1_w_gu[l][:, d_ff:].astype(BF16), ffn1_w_down[l].astype(BF16),
                  row(final_norm), norm_out=False)

        rg, qb, ckv, qi, ki2, qc, kc, vc, gates, small = _norm_proj(
            xf, row(mix_norm[l]), w_all, out_widths, out_dtypes)

        wax = jnp.concatenate([_block_diag(rg_wa[l]), _block_diag(rg_wx[l])], axis=1).astype(BF16)
        bax = jnp.concatenate([rg_ba[l], rg_bx[l]]).reshape(1, -1)
        o_a = _rglru(rg, batch, conv_w[l], row(conv_b[l]), wax, bax, row(rg_lambda[l]))

        fbias = _pad_cols(jnp.concatenate([jnp.zeros((IDX_HEADS,), F32), forget_bias[l]])[None], LANES)
        o_c = _fox(qc, kc, vc, _forget_cumsum(small, batch, fbias), cum_sel, batch)

        o_b = _dsa(qb, qi, small, ckv, ki2, row(kv_norm[l]), _block_diag(w_uk[l]).T.astype(BF16),
                   _block_diag(w_uv[l]).T.astype(BF16), bias_near, batch)

        (kv,) = _norm_proj(memf, row(mem_norm[l]), w_mkv[l].astype(BF16),
                           (w_mkv.shape[2],), (BF16,))
        xf = _merge_mem(xf, o_a, o_b, o_c, gates, w_branch[l].astype(BF16), w_out[l].astype(BF16),
                        row(xattn_norm[l]), w_mq[l].astype(BF16), kv, w_mo[l].astype(BF16), batch)

        xf = _ffn(xf, row(ffn2_norm[l]), ffn2_w_gu[l][:, :d_ff].astype(BF16),
                  ffn2_w_gu[l][:, d_ff:].astype(BF16), ffn2_w_down[l].astype(BF16),
                  row(final_norm), norm_out=(l == depth - 1))
    return xf.reshape(batch, seq, d)
```

```python
import functools
import math

import numpy as np
import jax
import jax.numpy as jnp
from jax import lax
from jax.experimental import pallas as pl
from jax.experimental.pallas import tpu as pltpu

F32 = jnp.float32
BF16 = jnp.bfloat16
I32 = jnp.int32
I16 = jnp.int16

LANES = 128
SUBLANES = 8
VMEM_LIMIT_BYTES = 56 * 1024 * 1024

EPS = 1e-6
NEG = -1e30
LOG2E = math.log2(math.e)
FAST_GAP = 100.0
CHUNK = 64
CHUNK_SHIFT = CHUNK.bit_length() - 1
HEAD_DIM = 64
RG_C = 8.0
CONV_W = 4
IDX_HEADS = 8
IDX_DIM = 64
TOPK_MAX = 256
MEM_HEADS = 4
MEM_HEAD_DIM = 128
NUM_BUCKETS = 32
MAX_DISTANCE = 128

TM_TOKENS = 512
FFN_CHUNK = 256
PROJ_CHUNK = 1024
TS_SCAN = 256
TQ_FOX = 512
CUM_PARTS = 3
TQ_DSA = 256
DSA_NEAR_KEYS = TQ_DSA + MAX_DISTANCE
TK_DSA = 512
DSA_HEAD_GROUP = 2
SEARCH_ACCUMULATORS = 4
DSA_PAD = TK_DSA - TQ_DSA


def _key_of_float(v):
    b = int(np.float32(v).view(np.int32))
    return b ^ ((b >> 31) & 0x7FFFFFFF)


KEY_NEG = _key_of_float(NEG)


def _params(*semantics):
    return pltpu.CompilerParams(dimension_semantics=semantics, vmem_limit_bytes=VMEM_LIMIT_BYTES)


def _const_spec(shape):
    nd = len(shape)
    return pl.BlockSpec(shape, lambda *_: (0,) * nd, pipeline_mode=pl.Buffered(1))


def _rms(x32, g):
    ms = jnp.mean(x32 * x32, axis=-1, keepdims=True)
    return x32 * lax.rsqrt(ms + EPS) * g


def _dot(a, b):
    return jnp.dot(a, b, preferred_element_type=F32)


def _dot_nt(a, b):
    return lax.dot_general(a, b, (((1,), (1,)), ((), ())), preferred_element_type=F32)


def _tree_sum(xs):
    while len(xs) > 1:
        xs = [xs[j] + xs[j + 1] for j in range(0, len(xs) - 1, 2)] + ([xs[-1]] if len(xs) % 2 else [])
    return xs[0]


def _dot_tn(a, b):
    return lax.dot_general(a, b, (((0,), (0,)), ((), ())), preferred_element_type=F32)


def _ffn_kernel(x_ref, g_ref, wg_ref, wu_ref, wd_ref, gout_ref, o_ref, a_ref, *, norm_out):
    x = x_ref[...]
    h = _rms(x, g_ref[...]).astype(BF16)
    d_ff = wg_ref.shape[1]
    for f0 in range(0, d_ff, FFN_CHUNK):
        g = _dot(h, wg_ref[:, f0:f0 + FFN_CHUNK])
        u = _dot(h, wu_ref[:, f0:f0 + FFN_CHUNK])
        a_ref[:, f0:f0 + FFN_CHUNK] = (g * jax.nn.sigmoid(g) * u).astype(BF16)
    y = x + 0.5 * _dot(a_ref[...], wd_ref[...])
    o_ref[...] = _rms(y, gout_ref[...]) if norm_out else y


def _ffn(x, g, wg, wu, wd, g_out, norm_out):
    n, d = x.shape
    d_ff = wg.shape[1]
    tm = min(TM_TOKENS, n)
    return pl.pallas_call(
        functools.partial(_ffn_kernel, norm_out=norm_out),
        grid=(n // tm,),
        in_specs=[pl.BlockSpec((tm, d), lambda i: (i, 0)),
                  _const_spec((1, d)), _const_spec((d, d_ff)), _const_spec((d, d_ff)),
                  _const_spec((d_ff, d)), _const_spec((1, d))],
        out_specs=pl.BlockSpec((tm, d), lambda i: (i, 0)),
        out_shape=jax.ShapeDtypeStruct((n, d), F32),
        scratch_shapes=[pltpu.VMEM((tm, d_ff), BF16)],
        compiler_params=_params("parallel"),
        name="ffn",
    )(x, g, wg, wu, wd, g_out)


def _norm_proj_kernel(x_ref, g_ref, w_ref, *o_refs):
    h = _rms(x_ref[...].astype(F32), g_ref[...]).astype(BF16)
    starts = np.cumsum([0] + [o_ref.shape[1] for o_ref in o_refs])
    for c0 in range(0, w_ref.shape[1], PROJ_CHUNK):
        c1 = min(c0 + PROJ_CHUNK, w_ref.shape[1])
        y = _dot(h, w_ref[:, c0:c1])
        for o_ref, o0 in zip(o_refs, starts[:-1]):
            lo, hi = max(c0, int(o0)), min(c1, int(o0) + o_ref.shape[1])
            if lo < hi:
                o_ref[:, lo - int(o0):hi - int(o0)] = y[:, lo - c0:hi - c0].astype(o_ref.dtype)


def _norm_proj(x, g, w, widths, dtypes):
    n, d = x.shape
    tm = min(TM_TOKENS, n)
    assert sum(widths) == w.shape[1]
    return pl.pallas_call(
        _norm_proj_kernel,
        grid=(n // tm,),
        in_specs=[pl.BlockSpec((tm, d), lambda i: (i, 0)), _const_spec((1, d)),
                  _const_spec(w.shape)],
        out_specs=[pl.BlockSpec((tm, wd), lambda i: (i, 0)) for wd in widths],
        out_shape=[jax.ShapeDtypeStruct((n, wd), dt) for wd, dt in zip(widths, dtypes)],
        compiler_params=_params("parallel"),
        name="norm_proj",
    )(x, g, w)


def _gelu_tanh(x):
    return 0.5 * x * (1.0 + jnp.tanh(math.sqrt(2.0 / math.pi) * (x + 0.044715 * (x * x * x))))


def _softplus(x):
    return jnp.maximum(x, 0.0) + jnp.log1p(jnp.exp(-jnp.abs(x)))


def _rglru_kernel(rg_ref, cw_ref, cb_ref, wax_ref, bax_ref, lam_ref, o_ref, xbuf, hc):
    ts = rg_ref.shape[0]
    dr = o_ref.shape[1]

    @pl.when(pl.program_id(1) == 0)
    def _():
        xbuf[0:SUBLANES, :] = jnp.zeros((SUBLANES, dr), F32)
        hc[...] = jnp.zeros_like(hc)

    xr = rg_ref[:, :dr].astype(F32)
    gr = rg_ref[:, dr:].astype(F32)
    xbuf[SUBLANES:, :] = xr
    xc = cb_ref[...] + cw_ref[CONV_W - 1:CONV_W, :] * xr
    for j in range(CONV_W - 1):
        xc = xc + cw_ref[j:j + 1, :] * xbuf[pl.ds(SUBLANES - (CONV_W - 1) + j, ts), :]
    xbuf[0:SUBLANES, :] = xbuf[ts:ts + SUBLANES, :]

    ax = _dot(xc.astype(BF16), wax_ref[...]) + bax_ref[...]
    r = jax.nn.sigmoid(ax[:, :dr])
    gi = jax.nn.sigmoid(ax[:, dr:])
    log_a = (-RG_C) * r * _softplus(-lam_ref[...])
    a = jnp.exp(log_a)
    th = jnp.tanh(log_a)
    num = -2.0 * th
    u = num * lax.rsqrt(jnp.maximum(num * (1.0 - th), jnp.finfo(F32).tiny)) * (gi * xc)

    row = lax.broadcasted_iota(I32, (ts, dr), 0) & (SUBLANES - 1)
    d = 1
    while d < SUBLANES:
        a_sh = pltpu.roll(a, d, 0)
        u_sh = pltpu.roll(u, d, 0)
        keep = row >= d
        u = jnp.where(keep, a * u_sh + u, u)
        a = jnp.where(keep, a * a_sh, a)
        d *= 2
    carry = hc[0:1, :]
    groups = []
    for g in range(ts // SUBLANES):
        rows = slice(g * SUBLANES, (g + 1) * SUBLANES)
        h_g = u[rows] + a[rows] * carry
        groups.append(h_g)
        carry = h_g[SUBLANES - 1:SUBLANES, :]
    h = jnp.concatenate(groups, axis=0)
    hc[...] = jnp.broadcast_to(carry, hc.shape)
    o_ref[...] = (h * _gelu_tanh(gr)).astype(o_ref.dtype)


def _rglru(rg, batch, conv_w, conv_b, wax, bax, lam):
    n, two_dr = rg.shape
    dr = two_dr // 2
    seq = n // batch
    ts = min(TS_SCAN, seq)
    ns = seq // ts
    return pl.pallas_call(
        _rglru_kernel,
        grid=(batch, ns),
        in_specs=[pl.BlockSpec((ts, two_dr), lambda b, s: (b * ns + s, 0)),
                  _const_spec(conv_w.shape), _const_spec(conv_b.shape), _const_spec(wax.shape),
                  _const_spec(bax.shape), _const_spec(lam.shape)],
        out_specs=pl.BlockSpec((ts, dr), lambda b, s: (b * ns + s, 0)),
        out_shape=jax.ShapeDtypeStruct((n, dr), BF16),
        scratch_shapes=[pltpu.VMEM((ts + SUBLANES, dr), F32), pltpu.VMEM((SUBLANES, dr), F32)],
        compiler_params=_params("parallel", "arbitrary"),
        name="rglru",
    )(rg, conv_w, conv_b, wax, bax, lam)


def _forget_cumsum_kernel(f_ref, b_ref, o_ref):
    z = f_ref[...] + b_ref[...]
    x = -_softplus(-z)
    seq = x.shape[0]
    row = lax.broadcasted_iota(I32, x.shape, 0)
    d = 1
    while d < seq:
        x = jnp.where(row >= d, x + pltpu.roll(x, d, 0), x)
        d *= 2
    x = x * LOG2E
    for j in range(CUM_PARTS):
        part = x.astype(BF16)
        o_ref[j] = part
        x = x - part.astype(F32)


def _forget_cumsum(small, batch, fbias):
    n, w = small.shape
    seq = n // batch
    return pl.pallas_call(
        _forget_cumsum_kernel,
        grid=(batch,),
        in_specs=[pl.BlockSpec((seq, w), lambda b: (b, 0)), _const_spec((1, w))],
        out_specs=pl.BlockSpec((CUM_PARTS, seq, w), lambda b: (0, b, 0)),
        out_shape=jax.ShapeDtypeStruct((CUM_PARTS, n, w), BF16),
        compiler_params=_params("parallel"),
        name="forget_cumsum",
    )(small, fbias)


def _split_bf16(x, n):
    parts = []
    for _ in range(n):
        part = x.astype(BF16).astype(F32)
        parts.append(part)
        x = x - part
    return parts


def _fox_kernel(q_ref, k_ref, v_ref, cp_ref, sel_ref, o_ref, kaug, qaug, s0, kmax, m_s, l_s, acc_s):
    tq = q_ref.shape[0]
    n_pairs = q_ref.shape[1] // LANES
    i = pl.program_id(1)
    lane = lax.broadcasted_iota(I32, (1, LANES), 1)
    bound_lanes = 2 * CUM_PARTS

    @pl.when(i == 0)
    def _():
        ones = jnp.where((lane >= bound_lanes) & (lane < bound_lanes + CUM_PARTS), 1.0, 0.0)
        for p in range(n_pairs):
            kp = k_ref[:, p * LANES:(p + 1) * LANES]
            kaug[:, 2 * p * LANES:(2 * p + 1) * LANES] = kp
            extra = _dot(cp_ref[0], sel_ref[p, 0])
            for j in range(1, CUM_PARTS):
                extra = extra + _dot(cp_ref[j], sel_ref[p, j])
            kaug[:, (2 * p + 1) * LANES:(2 * p + 2) * LANES] = (extra + ones).astype(BF16)
            k2 = kp.astype(F32) * kp.astype(F32)
            for e in range(2):
                in_head = (lane >= e * HEAD_DIM) & (lane < (e + 1) * HEAD_DIM)
                n2 = jnp.sum(jnp.where(in_head, k2, 0.0), axis=1, keepdims=True)
                kmax[2 * p + e] = jnp.broadcast_to(jnp.sqrt(jnp.max(n2, axis=0, keepdims=True)),
                                                   (1, LANES))

    q_rows = pl.ds(pl.multiple_of(i * tq, tq), tq)
    f_t = cp_ref[0, q_rows, :].astype(F32)
    for j in range(1, CUM_PARTS):
        f_t = f_t + cp_ref[j, q_rows, :].astype(F32)

    slack = jnp.zeros((tq, 1), F32)
    for p in range(n_pairs):
        qp = q_ref[:, p * LANES:(p + 1) * LANES]
        for e in range(2):
            h = 2 * p + e
            rows = slice(e * tq, (e + 1) * tq)
            in_head = (lane >= e * HEAD_DIM) & (lane < (e + 1) * HEAD_DIM)
            qh = jnp.where(in_head, qp, jnp.zeros_like(qp))
            qaug[p, rows, 0:LANES] = qh
            qf = qh.astype(F32)
            reach = jnp.sqrt(jnp.sum(qf * qf, axis=1, keepdims=True)) * kmax[h][:, 0:1]
            slack = jnp.maximum(slack, reach)
            bound = reach - f_t[:, IDX_HEADS + h:IDX_HEADS + h + 1]
            extra = jnp.where((lane >= CUM_PARTS * e) & (lane < CUM_PARTS * (e + 1)), 1.0, 0.0)
            for j, part in enumerate(_split_bf16(-bound, CUM_PARTS)):
                extra = jnp.where(lane == bound_lanes + j, part, extra)
            qaug[p, rows, LANES:2 * LANES] = extra.astype(BF16)
    fast = 2.0 * jnp.max(slack) < FAST_GAP

    l_s[...] = jnp.zeros(l_s.shape, F32)
    acc_s[...] = jnp.zeros(acc_s.shape, F32)
    key_row = lax.broadcasted_iota(I32, (tq, 2 * tq), 0)
    query = lax.broadcasted_iota(I32, (tq, 2 * tq), 1) & (tq - 1)
    causal = key_row <= query

    def logits(j, p):
        rows = pl.ds(pl.multiple_of(j * tq, tq), tq)
        return _dot_nt(kaug[rows, 2 * p * LANES:(2 * p + 2) * LANES], qaug[p])

    s0[...] = logits(0, 0)

    def step(j, masked, online):
        rows = pl.ds(pl.multiple_of(j * tq, tq), tq)
        s_next = s0[...]
        for p in range(n_pairs):
            s = s_next
            if p + 1 < n_pairs:
                s_next = logits(j, p + 1)
            else:
                s0[...] = logits(jnp.minimum(j + 1, i), 0)
            if masked:
                s = jnp.where(causal, s, NEG)
            vt = v_ref[rows, p * LANES:(p + 1) * LANES]
            if online:
                m_old = m_s[p]
                m_new = jnp.maximum(m_old, jnp.max(s, axis=0, keepdims=True))
                alpha = jnp.exp2(m_old - m_new)
                pr = jnp.exp2(s - m_new)
                l_s[p] = alpha * l_s[p] + jnp.sum(pr, axis=0, keepdims=True)
                m_s[p] = m_new
                acc_s[p] = alpha * acc_s[p] + _dot_tn(vt, pr.astype(BF16))
            else:
                pr = jnp.exp2(s)
                l_s[p] = l_s[p] + jnp.sum(pr, axis=0, keepdims=True)
                acc_s[p] = acc_s[p] + _dot_tn(vt, pr.astype(BF16))

    def run(online):
        def body(j, _):
            step(j, masked=False, online=online)
            return 0
        lax.fori_loop(0, i, body, 0)
        step(i, masked=True, online=online)

    @pl.when(fast)
    def _():
        run(online=False)

    @pl.when(jnp.logical_not(fast))
    def _():
        m_s[...] = jnp.full(m_s.shape, NEG, F32)
        run(online=True)

    feat = lax.broadcasted_iota(I32, (LANES, tq), 0)
    for p in range(n_pairs):
        o_t = acc_s[p] / l_s[p]
        o_t = jnp.where(feat < HEAD_DIM, o_t[:, :tq], o_t[:, tq:])
        o_ref[:, p * LANES:(p + 1) * LANES] = o_t.T.astype(o_ref.dtype)


def _fox(q, k, v, cum_parts, sel, batch):
    n, w = q.shape
    seq = n // batch
    tq = min(TQ_FOX, seq)
    nq = seq // tq
    n_pairs = w // LANES
    return pl.pallas_call(
        _fox_kernel,
        grid=(batch, nq),
        in_specs=[pl.BlockSpec((tq, w), lambda b, i: (b * nq + i, 0)),
                  pl.BlockSpec((seq, w), lambda b, i: (b, 0)),
                  pl.BlockSpec((seq, w), lambda b, i: (b, 0)),
                  pl.BlockSpec((CUM_PARTS, seq, LANES), lambda b, i: (0, b, 0)),
                  _const_spec(sel.shape)],
        out_specs=pl.BlockSpec((tq, w), lambda b, i: (b * nq + i, 0)),
        out_shape=jax.ShapeDtypeStruct((n, w), BF16),
        scratch_shapes=[pltpu.VMEM((seq, 2 * w), BF16),
                        pltpu.VMEM((n_pairs, 2 * tq, 2 * LANES), BF16),
                        pltpu.VMEM((tq, 2 * tq), F32),
                        pltpu.VMEM((2 * n_pairs, 1, LANES), F32),
                        pltpu.VMEM((n_pairs, 1, 2 * tq), F32),
                        pltpu.VMEM((n_pairs, 1, 2 * tq), F32),
                        pltpu.VMEM((n_pairs, LANES, 2 * tq), F32)],
        compiler_params=_params("parallel", "arbitrary"),
        name="fox",
    )(q, k, v, cum_parts, sel)


def _cum_selector(n_pairs, first_lane):
    sel = np.zeros((n_pairs, CUM_PARTS, LANES, LANES), np.float32)
    for p in range(n_pairs):
        for j in range(CUM_PARTS):
            for e in range(2):
                sel[p, j, first_lane + 2 * p + e, CUM_PARTS * e + j] = -1.0
    return jnp.asarray(sel, BF16)


def _float_key(x):
    b = lax.bitcast_convert_type(x, I32)
    return b ^ ((b >> 31) & 0x7FFFFFFF)


def _dsa_kernel(qb_ref, qi_ref, sm_ref, ckv_ref, ki_ref, kvn_ref, wuk_ref, wuv_ref, bias_ref, brng_ref,
                o_ref, c_s, ki_s, keys, keys16, thr_s, qim, d0, s0, cmax, m_s, l_s, acc_s, *, k_sel):
    tq = TQ_DSA
    seq = ckv_ref.shape[0]
    heads = bias_ref.shape[0]
    tk = TK_DSA
    pair = pl.program_id(1)
    lane = lax.broadcasted_iota(I32, (1, LANES), 1)

    @pl.when(pair == 0)
    def _():
        c_s[0:DSA_PAD, :] = jnp.zeros((DSA_PAD, 2 * LANES), BF16)
        ki_s[0:DSA_PAD, :] = jnp.zeros((DSA_PAD, LANES), BF16)
        c = _rms(ckv_ref[...].astype(F32), kvn_ref[...]).astype(BF16)
        c_s[DSA_PAD:, 0:LANES] = c
        c_s[DSA_PAD:, LANES:] = jnp.broadcast_to(jnp.where(lane < CUM_PARTS, 1.0, 0.0),
                                                 (seq, LANES)).astype(BF16)
        ki_s[DSA_PAD:, :] = ki_ref[...]
        cf = c.astype(F32)
        n2 = jnp.max(jnp.sum(cf * cf, axis=1, keepdims=True), axis=0, keepdims=True)
        cmax[...] = jnp.broadcast_to(jnp.sqrt(n2), cmax.shape)

    n_tiles = pair + 1
    max_tiles = pl.cdiv(seq, tk)
    key_local = lax.broadcasted_iota(I32, (tk, tq), 0)

    def n_valid(blk):
        return (2 * pair + blk + 1) * tq

    def tile_rows(blk, k):
        return pl.ds(pl.multiple_of(n_valid(blk) - (k + 1) * tk + DSA_PAD, tq), tk)

    def q_rows(blk):
        return pl.ds(pl.multiple_of(blk * tq, tq), tq)

    def score_block(blk, _):
        for h in range(IDX_HEADS):
            qp = qi_ref[q_rows(blk), (h // 2) * LANES:(h // 2 + 1) * LANES]
            e = h % 2
            in_head = (lane >= e * IDX_DIM) & (lane < (e + 1) * IDX_DIM)
            qim[h * tq:(h + 1) * tq, :] = jnp.where(in_head, qp, jnp.zeros_like(qp))
        w_t = sm_ref[q_rows(blk), :].T[0:IDX_HEADS, :] * (IDX_HEADS ** -0.5 * IDX_DIM ** -0.5)
        q_chunk = ((2 * pair + blk) * tq + lax.broadcasted_iota(I32, (tk, tq), 1)) >> CHUNK_SHIFT

        def pair_dots(k, p):
            return _dot_nt(ki_s[tile_rows(blk, k), :], qim[2 * p * tq:(2 * p + 2) * tq, :])

        d0[...] = pair_dots(0, 0)

        def score_tile(k, nearest):
            d_next = d0[...]
            sc = jnp.zeros((tk, tq), F32)
            for p in range(IDX_HEADS // 2):
                d = d_next
                if p + 1 < IDX_HEADS // 2:
                    d_next = pair_dots(k, p + 1)
                else:
                    d0[...] = pair_dots(jnp.minimum(k + 1, n_tiles - 1), 0)
                for e in range(2):
                    sc = sc + jnp.maximum(d[:, e * tq:(e + 1) * tq], 0.0) * w_t[2 * p + e:2 * p + e + 1, :]
            pos = key_local + (n_valid(blk) - (k + 1) * tk)
            if nearest:
                sc = jnp.where((pos >> CHUNK_SHIFT) <= q_chunk, sc, NEG)
            sc = jnp.where(pos >= 0, sc, NEG)
            key = _float_key(sc)
            keys[blk, k] = key
            keys16[blk, k] = (key >> 16).astype(I16)

        score_tile(0, nearest=True)

        def farther(k, _):
            score_tile(k, nearest=False)
            return 0

        lax.fori_loop(1, n_tiles, farther, 0)
        return 0

    lax.fori_loop(0, 2, score_block, 0)

    def count_ge16(thr16, t):
        rows16 = 2 * SUBLANES
        cnts = []
        for blk in range(2):
            accs = [jnp.zeros((rows16, tq), I16)] * SEARCH_ACCUMULATORS
            for k in range(t):
                for c in range(tk // rows16):
                    x = keys16[blk, k, c * rows16:(c + 1) * rows16, :]
                    ge = jnp.where(x >= thr16[blk], jnp.ones((), I16), jnp.zeros((), I16))
                    accs[c % SEARCH_ACCUMULATORS] = accs[c % SEARCH_ACCUMULATORS] + ge
            cnts.append(jnp.sum(_tree_sum(accs).astype(I32), axis=0, keepdims=True))
        return tuple(cnts)

    half = 2 ** 15

    def per_tile_count(fn, *args):
        return lax.switch(n_tiles - 1, [functools.partial(fn, t) for t in range(1, max_tiles + 1)], *args)

    def kth_largest16(k_need):
        def search(t, k_need):
            def step(b, us):
                cands = tuple(u | lax.shift_left(jnp.int32(1), 15 - b) for u in us)
                cnts = count_ge16(tuple((c - half).astype(I16) for c in cands), t)
                return tuple(jnp.where(n >= need, c, u) for n, need, c, u in zip(cnts, k_need, cands, us))
            zero = jnp.zeros((1, tq), I32)
            return tuple(u - half for u in lax.fori_loop(0, 16, step, (zero, zero)))
        return per_tile_count(search, k_need)

    full = jnp.full((1, tq), k_sel, I32)
    t_hi = kth_largest16((full, full))
    n_above = per_tile_count(
        lambda t, ts: count_ge16(tuple((jnp.minimum(x, half - 2) + 1).astype(I16) for x in ts), t), t_hi)
    n_above = tuple(jnp.where(t >= half - 1, 0, n) for t, n in zip(t_hi, n_above))

    def low_halves(k, _):
        for blk in range(2):
            key = keys[blk, k]
            keys16[blk, k] = jnp.where((key >> 16) == t_hi[blk], (key & 0xFFFF) - half, -half).astype(I16)
        return 0

    lax.fori_loop(0, n_tiles, low_halves, 0)
    t_lo = kth_largest16(tuple(k_sel - n for n in n_above))
    for blk in range(2):
        thr_s[blk] = jnp.maximum(lax.shift_left(t_hi[blk], 16) + (t_lo[blk] + half), KEY_NEG + 1)

    def finish_block(blk, _):
        thr = thr_s[blk]

        def count_rows(hit_fn):
            def body(k, cnt):
                return cnt + jnp.sum(hit_fn(k).reshape(tk // SUBLANES, SUBLANES, tq), axis=0)
            cnt = lax.fori_loop(0, n_tiles, body, jnp.zeros((SUBLANES, tq), I32))
            return jnp.sum(cnt, axis=0, keepdims=True)

        excess = count_rows(lambda k: jnp.where(keys[blk, k] >= thr, 1, 0)) - k_sel

        @pl.when(jnp.max(excess) > 0)
        def _():
            def count_eq_below(cut):
                def hit(k):
                    pos = key_local + (n_valid(blk) - (k + 1) * tk)
                    return jnp.where(keys[blk, k] == thr, jnp.where(pos < cut, 1, 0), 0)
                return count_rows(hit)

            n_bits = max(1, (2 * seq - 1).bit_length())
            keep = count_eq_below(jnp.full((1, tq), 2 ** n_bits, I32)) - jnp.maximum(excess, 0)

            def cut_step(b, cut):
                cand = cut | lax.shift_left(jnp.int32(1), n_bits - 1 - b)
                return jnp.where(count_eq_below(cand) <= keep, cand, cut)

            cut = lax.fori_loop(0, n_bits, cut_step, jnp.zeros((1, tq), I32))

            def drop(k, _):
                pos = key_local + (n_valid(blk) - (k + 1) * tk)
                kk = keys[blk, k]
                keys[blk, k] = jnp.where(kk == thr, jnp.where(pos >= cut, KEY_NEG, kk), kk)
                return 0

            lax.fori_loop(0, n_tiles, drop, 0)

        attend_block(blk, thr)
        return 0

    n_groups = heads // DSA_HEAD_GROUP
    gw = DSA_HEAD_GROUP * tq
    row_id = lax.broadcasted_iota(I32, (LANES, tq), 0)

    def attend_block(blk, thr):
        ql_t = (_dot_nt(wuk_ref[...], qb_ref[q_rows(blk), :]) * (HEAD_DIM ** -0.5 * LOG2E)).astype(BF16)
        slack = jnp.zeros((1, tq), F32)
        ql_aug = []
        for h in range(heads):
            ql_h = ql_t[h * LANES:(h + 1) * LANES, :]
            qf = ql_h.astype(F32)
            reach = jnp.sqrt(jnp.sum(qf * qf, axis=0, keepdims=True)) * cmax[:, 0:1]
            slack = jnp.maximum(slack, 2.0 * reach + (brng_ref[0, h][:, 0:1] - brng_ref[1, h][:, 0:1]))
            bound_rows = jnp.zeros((LANES, tq), F32)
            for j, part in enumerate(_split_bf16(-(reach + brng_ref[0, h][:, 0:1]), CUM_PARTS)):
                bound_rows = jnp.where(row_id == j, part, bound_rows)
            ql_aug.append(jnp.concatenate([ql_h, bound_rows.astype(BF16)], axis=0))
        ql_g = [jnp.concatenate(ql_aug[g * DSA_HEAD_GROUP:(g + 1) * DSA_HEAD_GROUP], axis=1)
                for g in range(n_groups)]
        fast = jnp.max(slack) < FAST_GAP
        l_s[...] = jnp.zeros(l_s.shape, F32)
        acc_s[...] = jnp.zeros(acc_s.shape, F32)

        def attend(k, near, online):
            ct_aug = c_s[tile_rows(blk, k), :]
            ct = c_s[tile_rows(blk, k), 0:LANES]
            mask = jnp.where(keys[blk, k] >= thr, 0.0, NEG)
            s_next = s0[...]
            for g in range(n_groups):
                s_g = s_next
                if g + 1 < n_groups:
                    s_next = _dot(ct_aug, ql_g[g + 1])
                else:
                    s0[...] = _dot(c_s[tile_rows(blk, jnp.minimum(k + 1, n_tiles - 1)), :], ql_g[0])
                parts = []
                for hh in range(DSA_HEAD_GROUP):
                    s = s_g[:, hh * tq:(hh + 1) * tq]
                    if near:
                        s = jnp.concatenate([s[:tk - DSA_NEAR_KEYS], s[tk - DSA_NEAR_KEYS:]
                                             + bias_ref[g * DSA_HEAD_GROUP + hh]], axis=0)
                    parts.append(s + mask)
                s = jnp.concatenate(parts, axis=1)
                if online:
                    m_old = m_s[g]
                    m_new = jnp.maximum(m_old, jnp.max(s, axis=0, keepdims=True))
                    alpha = jnp.exp2(m_old - m_new)
                    pr = jnp.exp2(s - m_new)
                    l_s[g] = alpha * l_s[g] + jnp.sum(pr, axis=0, keepdims=True)
                    m_s[g] = m_new
                    acc_s[g] = alpha * acc_s[g] + _dot_tn(ct, pr.astype(BF16))
                else:
                    pr = jnp.exp2(s)
                    l_s[g] = l_s[g] + jnp.sum(pr, axis=0, keepdims=True)
                    acc_s[g] = acc_s[g] + _dot_tn(ct, pr.astype(BF16))

        s0[...] = _dot(c_s[tile_rows(blk, 0), :], ql_g[0])

        def run(online):
            attend(0, near=True, online=online)

            def far(k, _):
                attend(k, near=False, online=online)
                return 0

            lax.fori_loop(1, n_tiles, far, 0)

        @pl.when(fast)
        def _():
            run(online=False)

        @pl.when(jnp.logical_not(fast))
        def _():
            m_s[...] = jnp.full(m_s.shape, NEG, F32)
            run(online=True)

        o_parts = []
        for g in range(n_groups):
            o_g = acc_s[g] / l_s[g]
            o_parts += [o_g[:, hh * tq:(hh + 1) * tq] for hh in range(DSA_HEAD_GROUP)]
        o_lat_t = jnp.concatenate(o_parts, axis=0).astype(BF16)
        o_ref[q_rows(blk), :] = _dot(wuv_ref[...], o_lat_t).T.astype(o_ref.dtype)

    lax.fori_loop(0, 2, finish_block, 0)


def _dsa(qb, qi, small, ckv, ki2, kvn, wuk_bd, wuv_bd, bias_near, batch):
    n, w = qb.shape
    seq = n // batch
    tq = TQ_DSA
    assert TK_DSA == 2 * tq and seq % TK_DSA == 0
    n_pairs = seq // (2 * tq)
    n_tiles = pl.cdiv(seq, TK_DSA)
    heads = bias_near.shape[0]
    n_groups = heads // DSA_HEAD_GROUP
    gw = DSA_HEAD_GROUP * tq
    k_sel = min(TOPK_MAX, seq // 4)
    blk = lambda width: pl.BlockSpec((2 * tq, width), lambda b, i: (b * n_pairs + i, 0))
    per_batch = pl.BlockSpec((seq, LANES), lambda b, i: (b, 0))
    bias_rng = jnp.stack([jnp.maximum(jnp.max(bias_near, axis=(1, 2)), 0.0),
                          jnp.minimum(jnp.min(bias_near, axis=(1, 2)), 0.0)])
    bias_rng = jnp.broadcast_to(bias_rng[:, :, None, None], (2, heads, 1, LANES))
    return pl.pallas_call(
        functools.partial(_dsa_kernel, k_sel=k_sel),
        grid=(batch, n_pairs),
        in_specs=[blk(w), blk(w), blk(LANES), per_batch, per_batch, _const_spec(kvn.shape),
                  _const_spec(wuk_bd.shape), _const_spec(wuv_bd.shape),
                  _const_spec(bias_near.shape), _const_spec(bias_rng.shape)],
        out_specs=blk(w),
        out_shape=jax.ShapeDtypeStruct((n, w), BF16),
        scratch_shapes=[pltpu.VMEM((seq + DSA_PAD, 2 * LANES), BF16),
                        pltpu.VMEM((seq + DSA_PAD, LANES), BF16),
                        pltpu.VMEM((2, n_tiles, TK_DSA, tq), I32),
                        pltpu.VMEM((2, n_tiles, TK_DSA, tq), I16),
                        pltpu.VMEM((2, 1, tq), I32),
                        pltpu.VMEM((IDX_HEADS * tq, LANES), BF16),
                        pltpu.VMEM((TK_DSA, 2 * tq), F32),
                        pltpu.VMEM((TK_DSA, gw), F32),
                        pltpu.VMEM((1, LANES), F32),
                        pltpu.VMEM((n_groups, 1, gw), F32),
                        pltpu.VMEM((n_groups, 1, gw), F32),
                        pltpu.VMEM((n_groups, LANES, gw), F32)],
        compiler_params=_params("parallel", "arbitrary"),
        name="dsa",
    )(qb, qi, small, ckv, ki2, kvn, wuk_bd, wuv_bd, bias_near, bias_rng)


def _merge_mem_kernel(x_ref, oa_ref, ob_ref, oc_ref, gate_ref, wbr_ref, wout_ref,
                      g_ref, wq_ref, kv_ref, wo_ref, o_ref):
    d = x_ref.shape[1]
    merged = jnp.zeros(x_ref.shape, F32)
    for j, br_ref in enumerate((oa_ref, ob_ref, oc_ref)):
        gate = jax.nn.sigmoid(gate_ref[:, j * d:(j + 1) * d].astype(F32))
        merged = merged + gate * _dot(br_ref[...], wbr_ref[j])
    x = x_ref[...] + _dot(merged.astype(BF16), wout_ref[...])
    hw = MEM_HEADS * MEM_HEAD_DIM
    q = _dot(_rms(x, g_ref[...]).astype(BF16), wq_ref[...]).astype(BF16)
    lane = lax.broadcasted_iota(I32, (1, LANES), 1)
    ones = jnp.broadcast_to(jnp.where(lane == 0, 1.0, 0.0), (kv_ref.shape[0], LANES)).astype(BF16)
    outs = []
    for h in range(MEM_HEADS):
        lanes = slice(h * MEM_HEAD_DIM, (h + 1) * MEM_HEAD_DIM)
        s = _dot_nt(q[:, lanes], kv_ref[:, lanes]) * (MEM_HEAD_DIM ** -0.5 * LOG2E)
        pr = jnp.exp2(s - jnp.max(s, axis=1, keepdims=True)).astype(BF16)
        v_aug = jnp.concatenate([kv_ref[:, hw + h * MEM_HEAD_DIM:hw + (h + 1) * MEM_HEAD_DIM], ones], axis=1)
        o_aug = _dot(pr, v_aug)
        outs.append(o_aug[:, :MEM_HEAD_DIM] / o_aug[:, MEM_HEAD_DIM:MEM_HEAD_DIM + 1])
    o = jnp.concatenate(outs, axis=1).astype(BF16)
    o_ref[...] = x + _dot(o, wo_ref[...])


def _merge_mem(x, oa, ob, oc, gates, wbr, wout, g, wq, kv, wo, batch):
    n, d = x.shape
    seq = n // batch
    tm = min(TM_TOKENS, seq)
    ns = seq // tm
    bw = oa.shape[1]
    mem_len = kv.shape[0] // batch
    blk = lambda width: pl.BlockSpec((tm, width), lambda b, s: (b * ns + s, 0))
    return pl.pallas_call(
        _merge_mem_kernel,
        grid=(batch, ns),
        in_specs=[blk(d), blk(bw), blk(bw), blk(bw), blk(3 * d), _const_spec(wbr.shape),
                  _const_spec(wout.shape), _const_spec((1, d)), _const_spec(wq.shape),
                  pl.BlockSpec((mem_len, kv.shape[1]), lambda b, s: (b, 0)),
                  _const_spec(wo.shape)],
        out_specs=blk(d),
        out_shape=jax.ShapeDtypeStruct((n, d), F32),
        compiler_params=_params("parallel", "arbitrary"),
        name="merge_mem",
    )(x, oa, ob, oc, gates, wbr, wout, g, wq, kv, wo)


def _t5_bucket(rel):
    nb = NUM_BUCKETS // 2
    max_exact = nb // 2
    base = jnp.where(rel > 0, nb, 0)
    n = jnp.abs(rel)
    n_f = jnp.maximum(n, 1).astype(jnp.float32)
    large = max_exact + (jnp.log(n_f / max_exact) / math.log(MAX_DISTANCE / max_exact)
                         * (nb - max_exact)).astype(jnp.int32)
    large = jnp.minimum(large, nb - 1)
    return base + jnp.where(n < max_exact, n, large)


def _near_bias(t5_bias):
    t = jnp.arange(TQ_DSA)[:, None]
    s = jnp.arange(DSA_NEAR_KEYS)[None, :] - MAX_DISTANCE
    onehot = jax.nn.one_hot(_t5_bucket(s - t), NUM_BUCKETS, dtype=F32)
    tbl = jnp.einsum("tsb,bh->hst", onehot, (t5_bias - t5_bias[NUM_BUCKETS // 2 - 1]) * LOG2E,
                     precision=lax.Precision.HIGHEST)
    return tbl.astype(F32)


def _block_diag(w):
    nb, a, b = w.shape
    eye = jnp.eye(nb, dtype=w.dtype)
    return (eye[:, None, :, None] * w[:, :, None, :]).reshape(nb * a, nb * b)


def _pad_cols(w, width):
    return jnp.pad(w, ((0, 0), (0, width - w.shape[1])))


def kernel(x, mem, ffn1_norm, ffn1_w_gu, ffn1_w_down, mix_norm, w_in, conv_w, conv_b, rg_wa, rg_ba, rg_wx, rg_bx, rg_lambda, kv_norm, w_uk, w_uv, forget_bias, w_branch, w_out, xattn_norm, mem_norm, w_mq, w_mkv, w_mo, ffn2_norm, ffn2_w_gu, ffn2_w_down, t5_bias, final_norm):
    batch, seq, d = x.shape
    depth = w_in.shape[0]
    d_ff = ffn1_w_down.shape[1]
    dr = conv_w.shape[2]
    heads = w_uk.shape[1]
    d_lat = w_uk.shape[3]
    n = batch * seq
    assert seq % TK_DSA == 0 and d_lat == LANES and 2 * IDX_DIM == LANES and 2 * HEAD_DIM == LANES

    widths = (dr, dr, heads * HEAD_DIM, d_lat, IDX_HEADS * IDX_DIM, IDX_DIM, IDX_HEADS,
              heads * HEAD_DIM, heads * HEAD_DIM, heads * HEAD_DIM, heads, d, d, d)
    offs = np.concatenate([[0], np.cumsum(widths)])
    col = lambda w, j: w[:, int(offs[j]):int(offs[j + 1])]

    bias_near = _near_bias(t5_bias)
    cum_sel = _cum_selector(heads * HEAD_DIM // LANES, IDX_HEADS)
    row = lambda v: v.reshape(1, -1).astype(F32)

    xf = x.reshape(n, d)
    memf = mem.reshape(batch * mem.shape[1], d)
    for l in range(depth):
        wl = w_in[l]
        small_w = _pad_cols(jnp.concatenate([col(wl, 6), col(wl, 10)], axis=1), LANES)
        w_all = jnp.concatenate(
            [col(wl, 0), col(wl, 1), col(wl, 2), col(wl, 3), col(wl, 4), col(wl, 5), col(wl, 5),
             col(wl, 7) * (HEAD_DIM ** -0.5 * LOG2E),
             col(wl, 8), col(wl, 9), col(wl, 11), col(wl, 12), col(wl, 13)],
            axis=1).astype(BF16)
        w_all = jnp.concatenate([w_all, small_w.astype(BF16)], axis=1)
        out_widths = (2 * dr, heads * HEAD_DIM, d_lat, IDX_HEADS * IDX_DIM, 2 * IDX_DIM,
                      heads * HEAD_DIM, heads * HEAD_DIM, heads * HEAD_DIM, 3 * d, LANES)
        out_dtypes = (BF16,) * 9 + (F32,)

        xf = _ffn(xf, row(ffn1_norm[l]), ffn1_w_gu[l][:, :d_ff].astype(BF16),
                  ffn1_w_gu[l][:, d_ff:].astype(BF16), ffn1_w_down[l].astype(BF16),
                  row(final_norm), norm_out=False)

        rg, qb, ckv, qi, ki2, qc, kc, vc, gates, small = _norm_proj(
            xf, row(mix_norm[l]), w_all, out_widths, out_dtypes)

        wax = jnp.concatenate([_block_diag(rg_wa[l]), _block_diag(rg_wx[l])], axis=1).astype(BF16)
        bax = jnp.concatenate([rg_ba[l], rg_bx[l]]).reshape(1, -1)
        o_a = _rglru(rg, batch, conv_w[l], row(conv_b[l]), wax, bax, row(rg_lambda[l]))

        fbias = _pad_cols(jnp.concatenate([jnp.zeros((IDX_HEADS,), F32), forget_bias[l]])[None], LANES)
        o_c = _fox(qc, kc, vc, _forget_cumsum(small, batch, fbias), cum_sel, batch)

        o_b = _dsa(qb, qi, small, ckv, ki2, row(kv_norm[l]), _block_diag(w_uk[l]).T.astype(BF16),
                   _block_diag(w_uv[l]).T.astype(BF16), bias_near, batch)

        (kv,) = _norm_proj(memf, row(mem_norm[l]), w_mkv[l].astype(BF16),
                           (w_mkv.shape[2],), (BF16,))
        xf = _merge_mem(xf, o_a, o_b, o_c, gates, w_branch[l].astype(BF16), w_out[l].astype(BF16),
                        row(xattn_norm[l]), w_mq[l].astype(BF16), kv, w_mo[l].astype(BF16), batch)

        xf = _ffn(xf, row(ffn2_norm[l]), ffn2_w_gu[l][:, :d_ff].astype(BF16),
                  ffn2_w_gu[l][:, d_ff:].astype(BF16), ffn2_w_down[l].astype(BF16),
                  row(final_norm), norm_out=(l == depth - 1))
    return xf.reshape(batch, seq, d)
```

```python
import functools
import math

import numpy as np
import jax
import jax.numpy as jnp
from jax import lax
from jax.experimental import pallas as pl
from jax.experimental.pallas import tpu as pltpu

F32 = jnp.float32
BF16 = jnp.bfloat16
I32 = jnp.int32
I16 = jnp.int16

LANES = 128
SUBLANES = 8
VMEM_LIMIT_BYTES = 56 * 1024 * 1024

EPS = 1e-6
NEG = -1e30
LOG2E = math.log2(math.e)
FAST_GAP = 100.0
CHUNK = 64
CHUNK_SHIFT = CHUNK.bit_length() - 1
HEAD_DIM = 64
RG_C = 8.0
CONV_W = 4
IDX_HEADS = 8
IDX_DIM = 64
TOPK_MAX = 256
MEM_HEADS = 4
MEM_HEAD_DIM = 128
NUM_BUCKETS = 32
MAX_DISTANCE = 128

TM_TOKENS = 512
FFN_CHUNK = 256
PROJ_CHUNK = 1024
TS_SCAN = 256
TQ_FOX = 512
CUM_PARTS = 3
TQ_DSA = 256
DSA_NEAR_KEYS = TQ_DSA + MAX_DISTANCE
TK_DSA = 512
DSA_HEAD_GROUP = 2
SEARCH_ACCUMULATORS = 4
DSA_PAD = TK_DSA - TQ_DSA


def _key_of_float(v):
    b = int(np.float32(v).view(np.int32))
    return b ^ ((b >> 31) & 0x7FFFFFFF)


KEY_NEG = _key_of_float(NEG)


def _params(*semantics):
    return pltpu.CompilerParams(dimension_semantics=semantics, vmem_limit_bytes=VMEM_LIMIT_BYTES)


def _const_spec(shape):
    nd = len(shape)
    return pl.BlockSpec(shape, lambda *_: (0,) * nd, pipeline_mode=pl.Buffered(1))


def _rms(x32, g):
    ms = jnp.mean(x32 * x32, axis=-1, keepdims=True)
    return x32 * lax.rsqrt(ms + EPS) * g


def _dot(a, b):
    return jnp.dot(a, b, preferred_element_type=F32)


def _dot_nt(a, b):
    return lax.dot_general(a, b, (((1,), (1,)), ((), ())), preferred_element_type=F32)


def _tree_sum(xs):
    while len(xs) > 1:
        xs = [xs[j] + xs[j + 1] for j in range(0, len(xs) - 1, 2)] + ([xs[-1]] if len(xs) % 2 else [])
    return xs[0]


def _dot_tn(a, b):
    return lax.dot_general(a, b, (((0,), (0,)), ((), ())), preferred_element_type=F32)


def _ffn_kernel(x_ref, g_ref, wg_ref, wu_ref, wd_ref, gout_ref, o_ref, a_ref, *, norm_out):
    x = x_ref[...]
    h = _rms(x, g_ref[...]).astype(BF16)
    d_ff = wg_ref.shape[1]
    for f0 in range(0, d_ff, FFN_CHUNK):
        g = _dot(h, wg_ref[:, f0:f0 + FFN_CHUNK])
        u = _dot(h, wu_ref[:, f0:f0 + FFN_CHUNK])
        a_ref[:, f0:f0 + FFN_CHUNK] = (g * jax.nn.sigmoid(g) * u).astype(BF16)
    y = x + 0.5 * _dot(a_ref[...], wd_ref[...])
    o_ref[...] = _rms(y, gout_ref[...]) if norm_out else y


def _ffn(x, g, wg, wu, wd, g_out, norm_out):
    n, d = x.shape
    d_ff = wg.shape[1]
    tm = min(TM_TOKENS, n)
    return pl.pallas_call(
        functools.partial(_ffn_kernel, norm_out=norm_out),
        grid=(n // tm,),
        in_specs=[pl.BlockSpec((tm, d), lambda i: (i, 0)),
                  _const_spec((1, d)), _const_spec((d, d_ff)), _const_spec((d, d_ff)),
                  _const_spec((d_ff, d)), _const_spec((1, d))],
        out_specs=pl.BlockSpec((tm, d), lambda i: (i, 0)),
        out_shape=jax.ShapeDtypeStruct((n, d), F32),
        scratch_shapes=[pltpu.VMEM((tm, d_ff), BF16)],
        compiler_params=_params("parallel"),
        name="ffn",
    )(x, g, wg, wu, wd, g_out)


def _norm_proj_kernel(x_ref, g_ref, w_ref, *o_refs):
    h = _rms(x_ref[...].astype(F32), g_ref[...]).astype(BF16)
    starts = np.cumsum([0] + [o_ref.shape[1] for o_ref in o_refs])
    for c0 in range(0, w_ref.shape[1], PROJ_CHUNK):
        c1 = min(c0 + PROJ_CHUNK, w_ref.shape[1])
        y = _dot(h, w_ref[:, c0:c1])
        for o_ref, o0 in zip(o_refs, starts[:-1]):
            lo, hi = max(c0, int(o0)), min(c1, int(o0) + o_ref.shape[1])
            if lo < hi:
                o_ref[:, lo - int(o0):hi - int(o0)] = y[:, lo - c0:hi - c0].astype(o_ref.dtype)


def _norm_proj(x, g, w, widths, dtypes):
    n, d = x.shape
    tm = min(TM_TOKENS, n)
    assert sum(widths) == w.shape[1]
    return pl.pallas_call(
        _norm_proj_kernel,
        grid=(n // tm,),
        in_specs=[pl.BlockSpec((tm, d), lambda i: (i, 0)), _const_spec((1, d)),
                  _const_spec(w.shape)],
        out_specs=[pl.BlockSpec((tm, wd), lambda i: (i, 0)) for wd in widths],
        out_shape=[jax.ShapeDtypeStruct((n, wd), dt) for wd, dt in zip(widths, dtypes)],
        compiler_params=_params("parallel"),
        name="norm_proj",
    )(x, g, w)


def _gelu_tanh(x):
    return 0.5 * x * (1.0 + jnp.tanh(math.sqrt(2.0 / math.pi) * (x + 0.044715 * (x * x * x))))


def _softplus(x):
    return jnp.maximum(x, 0.0) + jnp.log1p(jnp.exp(-jnp.abs(x)))


def _rglru_kernel(rg_ref, cw_ref, cb_ref, wax_ref, bax_ref, lam_ref, o_ref, xbuf, hc):
    ts = rg_ref.shape[0]
    dr = o_ref.shape[1]

    @pl.when(pl.program_id(1) == 0)
    def _():
        xbuf[0:SUBLANES, :] = jnp.zeros((SUBLANES, dr), F32)
        hc[...] = jnp.zeros_like(hc)

    xr = rg_ref[:, :dr].astype(F32)
    gr = rg_ref[:, dr:].astype(F32)
    xbuf[SUBLANES:, :] = xr
    xc = cb_ref[...] + cw_ref[CONV_W - 1:CONV_W, :] * xr
    for j in range(CONV_W - 1):
        xc = xc + cw_ref[j:j + 1, :] * xbuf[pl.ds(SUBLANES - (CONV_W - 1) + j, ts), :]
    xbuf[0:SUBLANES, :] = xbuf[ts:ts + SUBLANES, :]

    ax = _dot(xc.astype(BF16), wax_ref[...]) + bax_ref[...]
    r = jax.nn.sigmoid(ax[:, :dr])
    gi = jax.nn.sigmoid(ax[:, dr:])
    log_a = (-RG_C) * r * _softplus(-lam_ref[...])
    a = jnp.exp(log_a)
    th = jnp.tanh(log_a)
    num = -2.0 * th
    u = num * lax.rsqrt(jnp.maximum(num * (1.0 - th), jnp.finfo(F32).tiny)) * (gi * xc)

    row = lax.broadcasted_iota(I32, (ts, dr), 0) & (SUBLANES - 1)
    d = 1
    while d < SUBLANES:
        a_sh = pltpu.roll(a, d, 0)
        u_sh = pltpu.roll(u, d, 0)
        keep = row >= d
        u = jnp.where(keep, a * u_sh + u, u)
        a = jnp.where(keep, a * a_sh, a)
        d *= 2
    carry = hc[0:1, :]
    groups = []
    for g in range(ts // SUBLANES):
        rows = slice(g * SUBLANES, (g + 1) * SUBLANES)
        h_g = u[rows] + a[rows] * carry
        groups.append(h_g)
        carry = h_g[SUBLANES - 1:SUBLANES, :]
    h = jnp.concatenate(groups, axis=0)
    hc[...] = jnp.broadcast_to(carry, hc.shape)
    o_ref[...] = (h * _gelu_tanh(gr)).astype(o_ref.dtype)


def _rglru(rg, batch, conv_w, conv_b, wax, bax, lam):
    n, two_dr = rg.shape
    dr = two_dr // 2
    seq = n // batch
    ts = min(TS_SCAN, seq)
    ns = seq // ts
    return pl.pallas_call(
        _rglru_kernel,
        grid=(batch, ns),
        in_specs=[pl.BlockSpec((ts, two_dr), lambda b, s: (b * ns + s, 0)),
                  _const_spec(conv_w.shape), _const_spec(conv_b.shape), _const_spec(wax.shape),
                  _const_spec(bax.shape), _const_spec(lam.shape)],
        out_specs=pl.BlockSpec((ts, dr), lambda b, s: (b * ns + s, 0)),
        out_shape=jax.ShapeDtypeStruct((n, dr), BF16),
        scratch_shapes=[pltpu.VMEM((ts + SUBLANES, dr), F32), pltpu.VMEM((SUBLANES, dr), F32)],
        compiler_params=_params("parallel", "arbitrary"),
        name="rglru",
    )(rg, conv_w, conv_b, wax, bax, lam)


def _forget_cumsum_kernel(f_ref, b_ref, o_ref):
    z = f_ref[...] + b_ref[...]
    x = -_softplus(-z)
    seq = x.shape[0]
    row = lax.broadcasted_iota(I32, x.shape, 0)
    d = 1
    while d < seq:
        x = jnp.where(row >= d, x + pltpu.roll(x, d, 0), x)
        d *= 2
    x = x * LOG2E
    for j in range(CUM_PARTS):
        part = x.astype(BF16)
        o_ref[j] = part
        x = x - part.astype(F32)


def _forget_cumsum(small, batch, fbias):
    n, w = small.shape
    seq = n // batch
    return pl.pallas_call(
        _forget_cumsum_kernel,
        grid=(batch,),
        in_specs=[pl.BlockSpec((seq, w), lambda b: (b, 0)), _const_spec((1, w))],
        out_specs=pl.BlockSpec((CUM_PARTS, seq, w), lambda b: (0, b, 0)),
        out_shape=jax.ShapeDtypeStruct((CUM_PARTS, n, w), BF16),
        compiler_params=_params("parallel"),
        name="forget_cumsum",
    )(small, fbias)


def _split_bf16(x, n):
    parts = []
    for _ in range(n):
        part = x.astype(BF16).astype(F32)
        parts.append(part)
        x = x - part
    return parts


def _fox_kernel(q_ref, k_ref, v_ref, cp_ref, sel_ref, o_ref, kaug, qaug, s0, kmax, m_s, l_s, acc_s):
    tq = q_ref.shape[0]
    n_pairs = q_ref.shape[1] // LANES
    i = pl.program_id(1)
    lane = lax.broadcasted_iota(I32, (1, LANES), 1)
    bound_lanes = 2 * CUM_PARTS

    @pl.when(i == 0)
    def _():
        ones = jnp.where((lane >= bound_lanes) & (lane < bound_lanes + CUM_PARTS), 1.0, 0.0)
        for p in range(n_pairs):
            kp = k_ref[:, p * LANES:(p + 1) * LANES]
            kaug[:, 2 * p * LANES:(2 * p + 1) * LANES] = kp
            extra = _dot(cp_ref[0], sel_ref[p, 0])
            for j in range(1, CUM_PARTS):
                extra = extra + _dot(cp_ref[j], sel_ref[p, j])
            kaug[:, (2 * p + 1) * LANES:(2 * p + 2) * LANES] = (extra + ones).astype(BF16)
            k2 = kp.astype(F32) * kp.astype(F32)
            for e in range(2):
                in_head = (lane >= e * HEAD_DIM) & (lane < (e + 1) * HEAD_DIM)
                n2 = jnp.sum(jnp.where(in_head, k2, 0.0), axis=1, keepdims=True)
                kmax[2 * p + e] = jnp.broadcast_to(jnp.sqrt(jnp.max(n2, axis=0, keepdims=True)),
                                                   (1, LANES))

    q_rows = pl.ds(pl.multiple_of(i * tq, tq), tq)
    f_t = cp_ref[0, q_rows, :].astype(F32)
    for j in range(1, CUM_PARTS):
        f_t = f_t + cp_ref[j, q_rows, :].astype(F32)

    slack = jnp.zeros((tq, 1), F32)
    for p in range(n_pairs):
        qp = q_ref[:, p * LANES:(p + 1) * LANES]
        for e in range(2):
            h = 2 * p + e
            rows = slice(e * tq, (e + 1) * tq)
            in_head = (lane >= e * HEAD_DIM) & (lane < (e + 1) * HEAD_DIM)
            qh = jnp.where(in_head, qp, jnp.zeros_like(qp))
            qaug[p, rows, 0:LANES] = qh
            qf = qh.astype(F32)
            reach = jnp.sqrt(jnp.sum(qf * qf, axis=1, keepdims=True)) * kmax[h][:, 0:1]
            slack = jnp.maximum(slack, reach)
            bound = reach - f_t[:, IDX_HEADS + h:IDX_HEADS + h + 1]
            extra = jnp.where((lane >= CUM_PARTS * e) & (lane < CUM_PARTS * (e + 1)), 1.0, 0.0)
            for j, part in enumerate(_split_bf16(-bound, CUM_PARTS)):
                extra = jnp.where(lane == bound_lanes + j, part, extra)
            qaug[p, rows, LANES:2 * LANES] = extra.astype(BF16)
    fast = 2.0 * jnp.max(slack) < FAST_GAP

    l_s[...] = jnp.zeros(l_s.shape, F32)
    acc_s[...] = jnp.zeros(acc_s.shape, F32)
    key_row = lax.broadcasted_iota(I32, (tq, 2 * tq), 0)
    query = lax.broadcasted_iota(I32, (tq, 2 * tq), 1) & (tq - 1)
    causal = key_row <= query

    def logits(j, p):
        rows = pl.ds(pl.multiple_of(j * tq, tq), tq)
        return _dot_nt(kaug[rows, 2 * p * LANES:(2 * p + 2) * LANES], qaug[p])

    s0[...] = logits(0, 0)

    def step(j, masked, online):
        rows = pl.ds(pl.multiple_of(j * tq, tq), tq)
        s_next = s0[...]
        for p in range(n_pairs):
            s = s_next
            if p + 1 < n_pairs:
                s_next = logits(j, p + 1)
            else:
                s0[...] = logits(jnp.minimum(j + 1, i), 0)
            if masked:
                s = jnp.where(causal, s, NEG)
            vt = v_ref[rows, p * LANES:(p + 1) * LANES]
            if online:
                m_old = m_s[p]
                m_new = jnp.maximum(m_old, jnp.max(s, axis=0, keepdims=True))
                alpha = jnp.exp2(m_old - m_new)
                pr = jnp.exp2(s - m_new)
                l_s[p] = alpha * l_s[p] + jnp.sum(pr, axis=0, keepdims=True)
                m_s[p] = m_new
                acc_s[p] = alpha * acc_s[p] + _dot_tn(vt, pr.astype(BF16))
            else:
                pr = jnp.exp2(s)
                l_s[p] = l_s[p] + jnp.sum(pr, axis=0, keepdims=True)
                acc_s[p] = acc_s[p] + _dot_tn(vt, pr.astype(BF16))

    def run(online):
        def body(j, _):
            step(j, masked=False, online=online)
            return 0
        lax.fori_loop(0, i, body, 0)
        step(i, masked=True, online=online)

    @pl.when(fast)
    def _():
        run(online=False)

    @pl.when(jnp.logical_not(fast))
    def _():
        m_s[...] = jnp.full(m_s.shape, NEG, F32)
        run(online=True)

    feat = lax.broadcasted_iota(I32, (LANES, tq), 0)
    for p in range(n_pairs):
        o_t = acc_s[p] / l_s[p]
        o_t = jnp.where(feat < HEAD_DIM, o_t[:, :tq], o_t[:, tq:])
        o_ref[:, p * LANES:(p + 1) * LANES] = o_t.T.astype(o_ref.dtype)


def _fox(q, k, v, cum_parts, sel, batch):
    n, w = q.shape
    seq = n // batch
    tq = min(TQ_FOX, seq)
    nq = seq // tq
    n_pairs = w // LANES
    return pl.pallas_call(
        _fox_kernel,
        grid=(batch, nq),
        in_specs=[pl.BlockSpec((tq, w), lambda b, i: (b * nq + i, 0)),
                  pl.BlockSpec((seq, w), lambda b, i: (b, 0)),
                  pl.BlockSpec((seq, w), lambda b, i: (b, 0)),
                  pl.BlockSpec((CUM_PARTS, seq, LANES), lambda b, i: (0, b, 0)),
                  _const_spec(sel.shape)],
        out_specs=pl.BlockSpec((tq, w), lambda b, i: (b * nq + i, 0)),
        out_shape=jax.ShapeDtypeStruct((n, w), BF16),
        scratch_shapes=[pltpu.VMEM((seq, 2 * w), BF16),
                        pltpu.VMEM((n_pairs, 2 * tq, 2 * LANES), BF16),
                        pltpu.VMEM((tq, 2 * tq), F32),
                        pltpu.VMEM((2 * n_pairs, 1, LANES), F32),
                        pltpu.VMEM((n_pairs, 1, 2 * tq), F32),
                        pltpu.VMEM((n_pairs, 1, 2 * tq), F32),
                        pltpu.VMEM((n_pairs, LANES, 2 * tq), F32)],
        compiler_params=_params("parallel", "arbitrary"),
        name="fox",
    )(q, k, v, cum_parts, sel)


def _cum_selector(n_pairs, first_lane):
    sel = np.zeros((n_pairs, CUM_PARTS, LANES, LANES), np.float32)
    for p in range(n_pairs):
        for j in range(CUM_PARTS):
            for e in range(2):
                sel[p, j, first_lane + 2 * p + e, CUM_PARTS * e + j] = -1.0
    return jnp.asarray(sel, BF16)


def _float_key(x):
    b = lax.bitcast_convert_type(x, I32)
    return b ^ ((b >> 31) & 0x7FFFFFFF)


def _dsa_kernel(qb_ref, qi_ref, sm_ref, ckv_ref, ki_ref, kvn_ref, wuk_ref, wuv_ref, bias_ref, brng_ref,
                o_ref, c_s, ki_s, keys, keys16, thr_s, qim, d0, s0, cmax, m_s, l_s, acc_s, *, k_sel):
    tq = TQ_DSA
    seq = ckv_ref.shape[0]
    heads = bias_ref.shape[0]
    tk = TK_DSA
    pair = pl.program_id(1)
    lane = lax.broadcasted_iota(I32, (1, LANES), 1)

    @pl.when(pair == 0)
    def _():
        c_s[0:DSA_PAD, :] = jnp.zeros((DSA_PAD, 2 * LANES), BF16)
        ki_s[0:DSA_PAD, :] = jnp.zeros((DSA_PAD, LANES), BF16)
        c = _rms(ckv_ref[...].astype(F32), kvn_ref[...]).astype(BF16)
        c_s[DSA_PAD:, 0:LANES] = c
        c_s[DSA_PAD:, LANES:] = jnp.broadcast_to(jnp.where(lane < CUM_PARTS, 1.0, 0.0),
                                                 (seq, LANES)).astype(BF16)
        ki_s[DSA_PAD:, :] = ki_ref[...]
        cf = c.astype(F32)
        n2 = jnp.max(jnp.sum(cf * cf, axis=1, keepdims=True), axis=0, keepdims=True)
        cmax[...] = jnp.broadcast_to(jnp.sqrt(n2), cmax.shape)

    n_tiles = pair + 1
    max_tiles = pl.cdiv(seq, tk)
    key_local = lax.broadcasted_iota(I32, (tk, tq), 0)

    def n_valid(blk):
        return (2 * pair + blk + 1) * tq

    def tile_rows(blk, k):
        return pl.ds(pl.multiple_of(n_valid(blk) - (k + 1) * tk + DSA_PAD, tq), tk)

    def q_rows(blk):
        return pl.ds(pl.multiple_of(blk * tq, tq), tq)

    def score_block(blk, _):
        for h in range(IDX_HEADS):
            qp = qi_ref[q_rows(blk), (h // 2) * LANES:(h // 2 + 1) * LANES]
            e = h % 2
            in_head = (lane >= e * IDX_DIM) & (lane < (e + 1) * IDX_DIM)
            qim[h * tq:(h + 1) * tq, :] = jnp.where(in_head, qp, jnp.zeros_like(qp))
        w_t = sm_ref[q_rows(blk), :].T[0:IDX_HEADS, :] * (IDX_HEADS ** -0.5 * IDX_DIM ** -0.5)
        q_chunk = ((2 * pair + blk) * tq + lax.broadcasted_iota(I32, (tk, tq), 1)) >> CHUNK_SHIFT

        def pair_dots(k, p):
            return _dot_nt(ki_s[tile_rows(blk, k), :], qim[2 * p * tq:(2 * p + 2) * tq, :])

        d0[...] = pair_dots(0, 0)

        def score_tile(k, nearest):
            d_next = d0[...]
            sc = jnp.zeros((tk, tq), F32)
            for p in range(IDX_HEADS // 2):
                d = d_next
                if p + 1 < IDX_HEADS // 2:
                    d_next = pair_dots(k, p + 1)
                else:
                    d0[...] = pair_dots(jnp.minimum(k + 1, n_tiles - 1), 0)
                for e in range(2):
                    sc = sc + jnp.maximum(d[:, e * tq:(e + 1) * tq], 0.0) * w_t[2 * p + e:2 * p + e + 1, :]
            pos = key_local + (n_valid(blk) - (k + 1) * tk)
            if nearest:
                sc = jnp.where((pos >> CHUNK_SHIFT) <= q_chunk, sc, NEG)
            sc = jnp.where(pos >= 0, sc, NEG)
            key = _float_key(sc)
            keys[blk, k] = key
            keys16[blk, k] = (key >> 16).astype(I16)

        score_tile(0, nearest=True)

        def farther(k, _):
            score_tile(k, nearest=False)
            return 0

        lax.fori_loop(1, n_tiles, farther, 0)
        return 0

    lax.fori_loop(0, 2, score_block, 0)

    def count_ge16(thr16, t):
        rows16 = 2 * SUBLANES
        cnts = []
        for blk in range(2):
            accs = [jnp.zeros((rows16, tq), I16)] * SEARCH_ACCUMULATORS
            for k in range(t):
                for c in range(tk // rows16):
                    x = keys16[blk, k, c * rows16:(c + 1) * rows16, :]
                    ge = jnp.where(x >= thr16[blk], jnp.ones((), I16), jnp.zeros((), I16))
                    accs[c % SEARCH_ACCUMULATORS] = accs[c % SEARCH_ACCUMULATORS] + ge
            cnts.append(jnp.sum(_tree_sum(accs).astype(I32), axis=0, keepdims=True))
        return tuple(cnts)

    half = 2 ** 15

    def per_tile_count(fn, *args):
        return lax.switch(n_tiles - 1, [functools.partial(fn, t) for t in range(1, max_tiles + 1)], *args)

    def kth_largest16(k_need):
        def search(t, k_need):
            def step(b, us):
                cands = tuple(u | lax.shift_left(jnp.int32(1), 15 - b) for u in us)
                cnts = count_ge16(tuple((c - half).astype(I16) for c in cands), t)
                return tuple(jnp.where(n >= need, c, u) for n, need, c, u in zip(cnts, k_need, cands, us))
            zero = jnp.zeros((1, tq), I32)
            return tuple(u - half for u in lax.fori_loop(0, 16, step, (zero, zero)))
        return per_tile_count(search, k_need)

    full = jnp.full((1, tq), k_sel, I32)
    t_hi = kth_largest16((full, full))
    n_above = per_tile_count(
        lambda t, ts: count_ge16(tuple((jnp.minimum(x, half - 2) + 1).astype(I16) for x in ts), t), t_hi)
    n_above = tuple(jnp.where(t >= half - 1, 0, n) for t, n in zip(t_hi, n_above))

    def low_halves(k, _):
        for blk in range(2):
            key = keys[blk, k]
            keys16[blk, k] = jnp.where((key >> 16) == t_hi[blk], (key & 0xFFFF) - half, -half).astype(I16)
        return 0

    lax.fori_loop(0, n_tiles, low_halves, 0)
    t_lo = kth_largest16(tuple(k_sel - n for n in n_above))
    for blk in range(2):
        thr_s[blk] = jnp.maximum(lax.shift_left(t_hi[blk], 16) + (t_lo[blk] + half), KEY_NEG + 1)

    def finish_block(blk, _):
        thr = thr_s[blk]

        def count_rows(hit_fn):
            def body(k, cnt):
                return cnt + jnp.sum(hit_fn(k).reshape(tk // SUBLANES, SUBLANES, tq), axis=0)
            cnt = lax.fori_loop(0, n_tiles, body, jnp.zeros((SUBLANES, tq), I32))
            return jnp.sum(cnt, axis=0, keepdims=True)

        excess = count_rows(lambda k: jnp.where(keys[blk, k] >= thr, 1, 0)) - k_sel

        @pl.when(jnp.max(excess) > 0)
        def _():
            def count_eq_below(cut):
                def hit(k):
                    pos = key_local + (n_valid(blk) - (k + 1) * tk)
                    return jnp.where(keys[blk, k] == thr, jnp.where(pos < cut, 1, 0), 0)
                return count_rows(hit)

            n_bits = max(1, (2 * seq - 1).bit_length())
            keep = count_eq_below(jnp.full((1, tq), 2 ** n_bits, I32)) - jnp.maximum(excess, 0)

            def cut_step(b, cut):
                cand = cut | lax.shift_left(jnp.int32(1), n_bits - 1 - b)
                return jnp.where(count_eq_below(cand) <= keep, cand, cut)

            cut = lax.fori_loop(0, n_bits, cut_step, jnp.zeros((1, tq), I32))

            def drop(k, _):
                pos = key_local + (n_valid(blk) - (k + 1) * tk)
                kk = keys[blk, k]
                keys[blk, k] = jnp.where(kk == thr, jnp.where(pos >= cut, KEY_NEG, kk), kk)
                return 0

            lax.fori_loop(0, n_tiles, drop, 0)

        attend_block(blk, thr)
        return 0

    n_groups = heads // DSA_HEAD_GROUP
    gw = DSA_HEAD_GROUP * tq
    row_id = lax.broadcasted_iota(I32, (LANES, tq), 0)

    def attend_block(blk, thr):
        ql_t = (_dot_nt(wuk_ref[...], qb_ref[q_rows(blk), :]) * (HEAD_DIM ** -0.5 * LOG2E)).astype(BF16)
        slack = jnp.zeros((1, tq), F32)
        ql_aug = []
        for h in range(heads):
            ql_h = ql_t[h * LANES:(h + 1) * LANES, :]
            qf = ql_h.astype(F32)
            reach = jnp.sqrt(jnp.sum(qf * qf, axis=0, keepdims=True)) * cmax[:, 0:1]
            slack = jnp.maximum(slack, 2.0 * reach + (brng_ref[0, h][:, 0:1] - brng_ref[1, h][:, 0:1]))
            bound_rows = jnp.zeros((LANES, tq), F32)
            for j, part in enumerate(_split_bf16(-(reach + brng_ref[0, h][:, 0:1]), CUM_PARTS)):
                bound_rows = jnp.where(row_id == j, part, bound_rows)
            ql_aug.append(jnp.concatenate([ql_h, bound_rows.astype(BF16)], axis=0))
        ql_g = [jnp.concatenate(ql_aug[g * DSA_HEAD_GROUP:(g + 1) * DSA_HEAD_GROUP], axis=1)
                for g in range(n_groups)]
        fast = jnp.max(slack) < FAST_GAP
        l_s[...] = jnp.zeros(l_s.shape, F32)
        acc_s[...] = jnp.zeros(acc_s.shape, F32)

        def attend(k, near, online):
            ct_aug = c_s[tile_rows(blk, k), :]
            ct = c_s[tile_rows(blk, k), 0:LANES]
            mask = jnp.where(keys[blk, k] >= thr, 0.0, NEG)
            s_next = s0[...]
            for g in range(n_groups):
                s_g = s_next
                if g + 1 < n_groups:
                    s_next = _dot(ct_aug, ql_g[g + 1])
                else:
                    s0[...] = _dot(c_s[tile_rows(blk, jnp.minimum(k + 1, n_tiles - 1)), :], ql_g[0])
                parts = []
                for hh in range(DSA_HEAD_GROUP):
                    s = s_g[:, hh * tq:(hh + 1) * tq]
                    if near:
                        s = s + bias_ref[g * DSA_HEAD_GROUP + hh]
                    parts.append(s + mask)
                s = jnp.concatenate(parts, axis=1)
                if online:
                    m_old = m_s[g]
                    m_new = jnp.maximum(m_old, jnp.max(s, axis=0, keepdims=True))
                    alpha = jnp.exp2(m_old - m_new)
                    pr = jnp.exp2(s - m_new)
                    l_s[g] = alpha * l_s[g] + jnp.sum(pr, axis=0, keepdims=True)
                    m_s[g] = m_new
                    acc_s[g] = alpha * acc_s[g] + _dot_tn(ct, pr.astype(BF16))
                else:
                    pr = jnp.exp2(s)
                    l_s[g] = l_s[g] + jnp.sum(pr, axis=0, keepdims=True)
                    acc_s[g] = acc_s[g] + _dot_tn(ct, pr.astype(BF16))

        s0[...] = _dot(c_s[tile_rows(blk, 0), :], ql_g[0])

        def run(online):
            attend(0, near=True, online=online)

            def far(k, _):
                attend(k, near=False, online=online)
                return 0

            lax.fori_loop(1, n_tiles, far, 0)

        @pl.when(fast)
        def _():
            run(online=False)

        @pl.when(jnp.logical_not(fast))
        def _():
            m_s[...] = jnp.full(m_s.shape, NEG, F32)
            run(online=True)

        o_parts = []
        for g in range(n_groups):
            o_g = acc_s[g] / l_s[g]
            o_parts += [o_g[:, hh * tq:(hh + 1) * tq] for hh in range(DSA_HEAD_GROUP)]
        o_lat_t = jnp.concatenate(o_parts, axis=0).astype(BF16)
        o_ref[q_rows(blk), :] = _dot(wuv_ref[...], o_lat_t).T.astype(o_ref.dtype)

    lax.fori_loop(0, 2, finish_block, 0)


def _dsa(qb, qi, small, ckv, ki2, kvn, wuk_bd, wuv_bd, bias_near, batch):
    n, w = qb.shape
    seq = n // batch
    tq = TQ_DSA
    assert TK_DSA == 2 * tq and seq % TK_DSA == 0
    n_pairs = seq // (2 * tq)
    n_tiles = pl.cdiv(seq, TK_DSA)
    heads = bias_near.shape[0]
    n_groups = heads // DSA_HEAD_GROUP
    gw = DSA_HEAD_GROUP * tq
    k_sel = min(TOPK_MAX, seq // 4)
    blk = lambda width: pl.BlockSpec((2 * tq, width), lambda b, i: (b * n_pairs + i, 0))
    per_batch = pl.BlockSpec((seq, LANES), lambda b, i: (b, 0))
    bias_rng = jnp.stack([jnp.maximum(jnp.max(bias_near, axis=(1, 2)), 0.0),
                          jnp.minimum(jnp.min(bias_near, axis=(1, 2)), 0.0)])
    bias_rng = jnp.broadcast_to(bias_rng[:, :, None, None], (2, heads, 1, LANES))
    return pl.pallas_call(
        functools.partial(_dsa_kernel, k_sel=k_sel),
        grid=(batch, n_pairs),
        in_specs=[blk(w), blk(w), blk(LANES), per_batch, per_batch, _const_spec(kvn.shape),
                  _const_spec(wuk_bd.shape), _const_spec(wuv_bd.shape),
                  _const_spec(bias_near.shape), _const_spec(bias_rng.shape)],
        out_specs=blk(w),
        out_shape=jax.ShapeDtypeStruct((n, w), BF16),
        scratch_shapes=[pltpu.VMEM((seq + DSA_PAD, 2 * LANES), BF16),
                        pltpu.VMEM((seq + DSA_PAD, LANES), BF16),
                        pltpu.VMEM((2, n_tiles, TK_DSA, tq), I32),
                        pltpu.VMEM((2, n_tiles, TK_DSA, tq), I16),
                        pltpu.VMEM((2, 1, tq), I32),
                        pltpu.VMEM((IDX_HEADS * tq, LANES), BF16),
                        pltpu.VMEM((TK_DSA, 2 * tq), F32),
                        pltpu.VMEM((TK_DSA, gw), F32),
                        pltpu.VMEM((1, LANES), F32),
                        pltpu.VMEM((n_groups, 1, gw), F32),
                        pltpu.VMEM((n_groups, 1, gw), F32),
                        pltpu.VMEM((n_groups, LANES, gw), F32)],
        compiler_params=_params("parallel", "arbitrary"),
        name="dsa",
    )(qb, qi, small, ckv, ki2, kvn, wuk_bd, wuv_bd, bias_near, bias_rng)


def _merge_mem_kernel(x_ref, oa_ref, ob_ref, oc_ref, gate_ref, wbr_ref, wout_ref,
                      g_ref, wq_ref, kv_ref, wo_ref, o_ref):
    d = x_ref.shape[1]
    merged = jnp.zeros(x_ref.shape, F32)
    for j, br_ref in enumerate((oa_ref, ob_ref, oc_ref)):
        gate = jax.nn.sigmoid(gate_ref[:, j * d:(j + 1) * d].astype(F32))
        merged = merged + gate * _dot(br_ref[...], wbr_ref[j])
    x = x_ref[...] + _dot(merged.astype(BF16), wout_ref[...])
    hw = MEM_HEADS * MEM_HEAD_DIM
    q = _dot(_rms(x, g_ref[...]).astype(BF16), wq_ref[...]).astype(BF16)
    outs = []
    for h in range(MEM_HEADS):
        lanes = slice(h * MEM_HEAD_DIM, (h + 1) * MEM_HEAD_DIM)
        s = _dot_nt(q[:, lanes], kv_ref[:, lanes]) * (MEM_HEAD_DIM ** -0.5)
        pr = jnp.exp(s - jnp.max(s, axis=1, keepdims=True))
        pr = pr / jnp.sum(pr, axis=1, keepdims=True)
        outs.append(_dot(pr.astype(BF16), kv_ref[:, hw + h * MEM_HEAD_DIM:hw + (h + 1) * MEM_HEAD_DIM]))
    o = jnp.concatenate(outs, axis=1).astype(BF16)
    o_ref[...] = x + _dot(o, wo_ref[...])


def _merge_mem(x, oa, ob, oc, gates, wbr, wout, g, wq, kv, wo, batch):
    n, d = x.shape
    seq = n // batch
    tm = min(TM_TOKENS, seq)
    ns = seq // tm
    bw = oa.shape[1]
    mem_len = kv.shape[0] // batch
    blk = lambda width: pl.BlockSpec((tm, width), lambda b, s: (b * ns + s, 0))
    return pl.pallas_call(
        _merge_mem_kernel,
        grid=(batch, ns),
        in_specs=[blk(d), blk(bw), blk(bw), blk(bw), blk(3 * d), _const_spec(wbr.shape),
                  _const_spec(wout.shape), _const_spec((1, d)), _const_spec(wq.shape),
                  pl.BlockSpec((mem_len, kv.shape[1]), lambda b, s: (b, 0)),
                  _const_spec(wo.shape)],
        out_specs=blk(d),
        out_shape=jax.ShapeDtypeStruct((n, d), F32),
        compiler_params=_params("parallel", "arbitrary"),
        name="merge_mem",
    )(x, oa, ob, oc, gates, wbr, wout, g, wq, kv, wo)


def _t5_bucket(rel):
    nb = NUM_BUCKETS // 2
    max_exact = nb // 2
    base = jnp.where(rel > 0, nb, 0)
    n = jnp.abs(rel)
    n_f = jnp.maximum(n, 1).astype(jnp.float32)
    large = max_exact + (jnp.log(n_f / max_exact) / math.log(MAX_DISTANCE / max_exact)
                         * (nb - max_exact)).astype(jnp.int32)
    large = jnp.minimum(large, nb - 1)
    return base + jnp.where(n < max_exact, n, large)


def _near_bias(t5_bias):
    t = jnp.arange(TQ_DSA)[:, None]
    s = jnp.arange(DSA_NEAR_KEYS)[None, :] - MAX_DISTANCE
    onehot = jax.nn.one_hot(_t5_bucket(s - t), NUM_BUCKETS, dtype=F32)
    tbl = jnp.einsum("tsb,bh->hst", onehot, (t5_bias - t5_bias[NUM_BUCKETS // 2 - 1]) * LOG2E,
                     precision=lax.Precision.HIGHEST)
    return jnp.pad(tbl.astype(F32), ((0, 0), (TK_DSA - DSA_NEAR_KEYS, 0), (0, 0)))


def _block_diag(w):
    nb, a, b = w.shape
    eye = jnp.eye(nb, dtype=w.dtype)
    return (eye[:, None, :, None] * w[:, :, None, :]).reshape(nb * a, nb * b)


def _pad_cols(w, width):
    return jnp.pad(w, ((0, 0), (0, width - w.shape[1])))


def kernel(x, mem, ffn1_norm, ffn1_w_gu, ffn1_w_down, mix_norm, w_in, conv_w, conv_b, rg_wa, rg_ba, rg_wx, rg_bx, rg_lambda, kv_norm, w_uk, w_uv, forget_bias, w_branch, w_out, xattn_norm, mem_norm, w_mq, w_mkv, w_mo, ffn2_norm, ffn2_w_gu, ffn2_w_down, t5_bias, final_norm):
    batch, seq, d = x.shape
    depth = w_in.shape[0]
    d_ff = ffn1_w_down.shape[1]
    dr = conv_w.shape[2]
    heads = w_uk.shape[1]
    d_lat = w_uk.shape[3]
    n = batch * seq
    assert seq % TK_DSA == 0 and d_lat == LANES and 2 * IDX_DIM == LANES and 2 * HEAD_DIM == LANES

    widths = (dr, dr, heads * HEAD_DIM, d_lat, IDX_HEADS * IDX_DIM, IDX_DIM, IDX_HEADS,
              heads * HEAD_DIM, heads * HEAD_DIM, heads * HEAD_DIM, heads, d, d, d)
    offs = np.concatenate([[0], np.cumsum(widths)])
    col = lambda w, j: w[:, int(offs[j]):int(offs[j + 1])]

    bias_near = _near_bias(t5_bias)
    cum_sel = _cum_selector(heads * HEAD_DIM // LANES, IDX_HEADS)
    row = lambda v: v.reshape(1, -1).astype(F32)

    xf = x.reshape(n, d)
    memf = mem.reshape(batch * mem.shape[1], d)
    for l in range(depth):
        wl = w_in[l]
        small_w = _pad_cols(jnp.concatenate([col(wl, 6), col(wl, 10)], axis=1), LANES)
        w_all = jnp.concatenate(
            [col(wl, 0), col(wl, 1), col(wl, 2), col(wl, 3), col(wl, 4), col(wl, 5), col(wl, 5),
             col(wl, 7) * (HEAD_DIM ** -0.5 * LOG2E),
             col(wl, 8), col(wl, 9), col(wl, 11), col(wl, 12), col(wl, 13)],
            axis=1).astype(BF16)
        w_all = jnp.concatenate([w_all, small_w.astype(BF16)], axis=1)
        out_widths = (2 * dr, heads * HEAD_DIM, d_lat, IDX_HEADS * IDX_DIM, 2 * IDX_DIM,
                      heads * HEAD_DIM, heads * HEAD_DIM, heads * HEAD_DIM, 3 * d, LANES)
        out_dtypes = (BF16,) * 9 + (F32,)

        xf = _ffn(xf, row(ffn1_norm[l]), ffn1_w_gu[l][:, :d_ff].astype(BF16),
                  ffn1_w_gu[l][:, d_ff:].astype(BF16), ffn1_w_down[l].astype(BF16),
                  row(final_norm), norm_out=False)

        rg, qb, ckv, qi, ki2, qc, kc, vc, gates, small = _norm_proj(
            xf, row(mix_norm[l]), w_all, out_widths, out_dtypes)

        wax = jnp.concatenate([_block_diag(rg_wa[l]), _block_diag(rg_wx[l])], axis=1).astype(BF16)
        bax = jnp.concatenate([rg_ba[l], rg_bx[l]]).reshape(1, -1)
        o_a = _rglru(rg, batch, conv_w[l], row(conv_b[l]), wax, bax, row(rg_lambda[l]))

        fbias = _pad_cols(jnp.concatenate([jnp.zeros((IDX_HEADS,), F32), forget_bias[l]])[None], LANES)
        o_c = _fox(qc, kc, vc, _forget_cumsum(small, batch, fbias), cum_sel, batch)

        o_b = _dsa(qb, qi, small, ckv, ki2, row(kv_norm[l]), _block_diag(w_uk[l]).T.astype(BF16),
                   _block_diag(w_uv[l]).T.astype(BF16), bias_near, batch)

        (kv,) = _norm_proj(memf, row(mem_norm[l]), w_mkv[l].astype(BF16),
                           (w_mkv.shape[2],), (BF16,))
        xf = _merge_mem(xf, o_a, o_b, o_c, gates, w_branch[l].astype(BF16), w_out[l].astype(BF16),
                        row(xattn_norm[l]), w_mq[l].astype(BF16), kv, w_mo[l].astype(BF16), batch)

        xf = _ffn(xf, row(ffn2_norm[l]), ffn2_w_gu[l][:, :d_ff].astype(BF16),
                  ffn2_w_gu[l][:, d_ff:].astype(BF16), ffn2_w_down[l].astype(BF16),
                  row(final_norm), norm_out=(l == depth - 1))
    return xf.reshape(batch, seq, d)
```
